```python
import math, functools
import jax, jax.numpy as jnp
from jax import lax
import numpy as np

D_MODEL = 2048
BATCH = 4
SEQ = 2048
DEPTH = 1
DEC_BATCH = 32
DEC_SEQ = 8
PAST_LEN = 16384
PAGE_SIZE = 128

N_HEADS = 8
HEAD_DIM = 128
N_KV_HEADS = 2
GQA_GROUP = N_HEADS // N_KV_HEADS
IDX_HEADS = 16
IDX_DIM = 64
TOPK = 256
Q_BLOCK = 128
N_BUCKETS = 32
MAX_DISTANCE = 128
SSD_HEADS = 16
SSD_HEAD_DIM = 64
SSD_GROUPS = 2
D_STATE = 128
CONV_WIDTH = 4
CHUNK = 128
DT_MIN = 0.001
DT_MAX = 0.1
D_ATTN = N_HEADS * HEAD_DIM
D_SSD = SSD_HEADS * SSD_HEAD_DIM
D_MIX = D_ATTN + D_SSD
CONV_DIM = D_SSD + 2 * SSD_GROUPS * D_STATE
D_FF = 5632
EPS = 1e-6
IN_SPLITS = (D_ATTN, N_KV_HEADS * HEAD_DIM, N_KV_HEADS * HEAD_DIM, IDX_HEADS * IDX_DIM, IDX_DIM, IDX_HEADS, D_SSD, CONV_DIM, SSD_HEADS)
D_IN = sum(IN_SPLITS)

kernel_name = 'hymba_dsa_ssd_macaron_step'


def split_cols(t, sizes):
    offsets = [int(o) for o in np.cumsum(sizes)[:-1]]
    return jnp.split(t, offsets, axis=-1)


def rms_norm(x, g):
    xf = x.astype(jnp.float32)
    xf = xf * lax.rsqrt(jnp.mean(xf * xf, axis=-1, keepdims=True) + EPS)
    return (xf * g.astype(jnp.float32)).astype(x.dtype)


def swiglu(x, w1, w3, w2):
    return (jax.nn.silu(x @ w1) * (x @ w3)) @ w2


def t5_bucket(n):
    max_exact = N_BUCKETS // 2
    nf = jnp.maximum(n, 1).astype(jnp.float32)
    large = max_exact + (jnp.log(nf / max_exact) / math.log(MAX_DISTANCE / max_exact) * (N_BUCKETS - max_exact)).astype(jnp.int32)
    return jnp.where(n < max_exact, n, jnp.minimum(large, N_BUCKETS - 1))


def sparse_attend(q, qi, wi, pos_q, ki_all, gather_kv, top, rel_bias):
    b, nq = q.shape[:2]
    n_keys = ki_all.shape[1]
    rel = jnp.einsum('bqhd,bsd->bqhs', qi, ki_all)
    score = jnp.einsum('bqhs,bqh->bqs', jax.nn.relu(rel), wi).astype(jnp.float32)
    admissible = jnp.arange(n_keys)[None, :] <= pos_q[:, None]
    score = jnp.where(admissible[None], score, -jnp.inf)
    _, idx = lax.top_k(score, top)
    valid = idx <= pos_q[None, :, None]
    k_sel, v_sel = gather_kv(idx)
    qg = q.reshape(b, nq, N_KV_HEADS, GQA_GROUP, HEAD_DIM)
    logits = jnp.einsum('bqkgd,bqskd->bqkgs', qg, k_sel).astype(jnp.float32) * HEAD_DIM ** -0.5
    bias = rel_bias[t5_bucket(jnp.maximum(pos_q[None, :, None] - idx, 0))]
    bias = bias.reshape(b, nq, top, N_KV_HEADS, GQA_GROUP).transpose(0, 1, 3, 4, 2)
    logits = jnp.where(valid[:, :, None, None, :], logits + bias.astype(jnp.float32), -jnp.inf)
    p = jax.nn.softmax(logits, axis=-1).astype(v_sel.dtype)
    out = jnp.einsum('bqkgs,bqskd->bqkgd', p, v_sel)
    return out.reshape(b, nq, D_ATTN)


def prompt_attend(q, k, v, qi, ki, wi, rel_bias):
    b, t = q.shape[:2]
    top = min(TOPK, t // 4)
    qb = min(Q_BLOCK, t)
    nb = t // qb
    take = jax.vmap(lambda rows, ii: rows[ii])

    def gather_kv(idx):
        return take(k, idx), take(v, idx)

    def blocks(a):
        return jnp.moveaxis(a.reshape(b, nb, qb, *a.shape[2:]), 1, 0)

    def one_block(args):
        i, qq, qqi, wwi = args
        pos = i * qb + jnp.arange(qb)
        return sparse_attend(qq, qqi, wwi, pos, ki, gather_kv, top, rel_bias)

    out = lax.map(one_block, (jnp.arange(nb), blocks(q), blocks(qi), blocks(wi)))
    return jnp.moveaxis(out, 0, 1).reshape(b, t, D_ATTN)


def sample_attend(q, k, v, qi, ki, wi, rel_bias, pool_k, pool_v, pool_kidx, page_table):
    b, t = q.shape[:2]
    past = page_table.shape[1] * PAGE_SIZE
    ki_past = pool_kidx[page_table].reshape(b, past, IDX_DIM)
    ki_all = jnp.concatenate([ki_past, ki.astype(ki_past.dtype)], axis=1)
    top = min(TOPK, (past + t) // 4)
    take = jax.vmap(lambda rows, ii: rows[ii])

    def gather_kv(idx):
        in_past = (idx < past)[..., None, None]
        pidx = jnp.minimum(idx, past - 1)
        phys = jax.vmap(lambda pt, pg: pt[pg])(page_table, pidx // PAGE_SIZE)
        off = pidx % PAGE_SIZE
        nidx = jnp.clip(idx - past, 0, t - 1)
        k_sel = jnp.where(in_past, pool_k[phys, off], take(k, nidx))
        v_sel = jnp.where(in_past, pool_v[phys, off], take(v, nidx))
        return k_sel, v_sel

    pos = past + jnp.arange(t)
    return sparse_attend(q, qi, wi, pos, ki_all, gather_kv, top, rel_bias)


def ssd_scan(xh, dt, a, bm, cm, h0):
    b, l, H, P = xh.shape
    G, N = bm.shape[2], bm.shape[3]
    E = H // G
    q = min(CHUNK, l)
    nc = -(-l // q)
    pad = nc * q - l
    if pad:
        padf = lambda t: jnp.pad(t, [(0, 0), (0, pad)] + [(0, 0)] * (t.ndim - 2))
        xh, dt, bm, cm = padf(xh), padf(dt), padf(bm), padf(cm)
    xd = (xh * dt[..., None]).reshape(b, nc, q, G, E, P)
    acum = jnp.cumsum((dt * a).reshape(b, nc, q, G, E), axis=2)
    bc = bm.reshape(b, nc, q, G, N)
    cc = cm.reshape(b, nc, q, G, N)
    causal = jnp.tril(jnp.ones((q, q), bool))[:, :, None, None]
    seg = acum[:, :, :, None] - acum[:, :, None, :]
    lmat = jnp.exp(jnp.where(causal, seg, -jnp.inf))
    cb = jnp.einsum('bcign,bcjgn->bcijg', cc, bc)
    y_diag = jnp.einsum('bcijg,bcijge,bcjgep->bcigep', cb, lmat, xd)
    decay = jnp.exp(acum[:, :, -1:] - acum)
    st = jnp.einsum('bcjgn,bcjge,bcjgep->bcgepn', bc, decay, xd)
    cdecay = jnp.exp(acum[:, :, -1])

    def step(h, inp):
        s_c, d_c = inp
        return h * d_c[..., None, None] + s_c, h

    h_last, h_start = lax.scan(step, h0.reshape(b, G, E, P, N), (jnp.moveaxis(st, 1, 0), jnp.moveaxis(cdecay, 1, 0)))
    h_start = jnp.moveaxis(h_start, 0, 1)
    y_off = jnp.einsum('bcign,bcgepn,bcige->bcigep', cc, h_start, jnp.exp(acum))
    y = (y_diag + y_off).reshape(b, nc * q, H, P)[:, :l]
    return y, h_last.reshape(b, H, P, N)


def ssd_mixer(z, xbc, dtr, conv_prev, ssm_prev, conv_w, conv_b, a_log, dt_bias, d_skip, g_ssd):
    b, L = xbc.shape[:2]
    xpad = jnp.concatenate([conv_prev.astype(xbc.dtype), xbc], axis=1)
    conv = sum(xpad[:, k:k + L] * conv_w[k] for k in range(CONV_WIDTH)) + conv_b
    xbc_c = jax.nn.silu(conv)
    xs, bm, cm = split_cols(xbc_c, (D_SSD, SSD_GROUPS * D_STATE, SSD_GROUPS * D_STATE))
    f32 = lambda t: t.astype(jnp.float32)
    xh = f32(xs).reshape(b, L, SSD_HEADS, SSD_HEAD_DIM)
    dt = jax.nn.softplus(f32(dtr) + f32(dt_bias))
    a = -jnp.exp(f32(a_log))
    y, h_new = ssd_scan(xh, dt, a, f32(bm).reshape(b, L, SSD_GROUPS, D_STATE), f32(cm).reshape(b, L, SSD_GROUPS, D_STATE), f32(ssm_prev))
    y = y + f32(d_skip)[:, None] * xh
    y = y.reshape(b, L, D_SSD) * jax.nn.silu(f32(z))
    y = rms_norm(y.reshape(b, L, SSD_GROUPS, D_SSD // SSD_GROUPS), g_ssd.reshape(SSD_GROUPS, D_SSD // SSD_GROUPS))
    return y.reshape(b, L, D_SSD).astype(z.dtype), xpad[:, L:], h_new.astype(z.dtype)


def setup_inputs(seed: int = 0) -> dict:
    key = jax.random.key(seed)
    ks = jax.random.split(key, 32)
    n_pages = PAST_LEN // PAGE_SIZE
    n_used = DEC_BATCH * n_pages
    n_pool = (n_used * 5) // 4
    nrm = lambda k, shape, scale=1.0: jax.random.normal(k, shape, jnp.float32) * scale
    gain = lambda k, shape: 1.0 + 0.01 * jax.random.normal(k, shape, jnp.float32)
    page_table = jax.random.permutation(ks[7], n_pool)[:n_used].reshape(DEC_BATCH, n_pages).astype(jnp.int32)
    dt0 = jnp.exp(jax.random.uniform(ks[17], (DEPTH, SSD_HEADS)) * (math.log(DT_MAX) - math.log(DT_MIN)) + math.log(DT_MIN))
    return {
        'x_prompt': nrm(ks[0], (BATCH, SEQ, D_MODEL)),
        'x_sample': nrm(ks[1], (DEC_BATCH, DEC_SEQ, D_MODEL)),
        'cache_k': nrm(ks[2], (DEPTH, n_pool, PAGE_SIZE, N_KV_HEADS, HEAD_DIM)),
        'cache_v': nrm(ks[3], (DEPTH, n_pool, PAGE_SIZE, N_KV_HEADS, HEAD_DIM)),
        'cache_kidx': nrm(ks[4], (DEPTH, n_pool, PAGE_SIZE, IDX_DIM)),
        'state_conv': nrm(ks[5], (DEPTH, DEC_BATCH, CONV_WIDTH - 1, CONV_DIM)),
        'state_ssm': nrm(ks[6], (DEPTH, DEC_BATCH, SSD_HEADS, SSD_HEAD_DIM, D_STATE), 0.1),
        'page_table': page_table,
        'rel_bias': nrm(ks[8], (N_BUCKETS, N_HEADS), 0.5),
        'g_ffn1': gain(ks[9], (DEPTH, D_MODEL)),
        'w1_ffn1': nrm(ks[10], (DEPTH, D_MODEL, D_FF), D_MODEL ** -0.5),
        'w3_ffn1': nrm(ks[11], (DEPTH, D_MODEL, D_FF), D_MODEL ** -0.5),
        'w2_ffn1': nrm(ks[12], (DEPTH, D_FF, D_MODEL), D_FF ** -0.5),
        'g_mix': gain(ks[13], (DEPTH, D_MODEL)),
        'w_in': nrm(ks[14], (DEPTH, D_MODEL, D_IN), D_MODEL ** -0.5),
        'conv_w': nrm(ks[15], (DEPTH, CONV_WIDTH, CONV_DIM), CONV_WIDTH ** -0.5),
        'conv_b': nrm(ks[16], (DEPTH, CONV_DIM), 0.01),
        'a_log': jnp.log(jax.random.uniform(ks[18], (DEPTH, SSD_HEADS), minval=1.0, maxval=16.0)),
        'dt_bias': dt0 + jnp.log(-jnp.expm1(-dt0)),
        'd_skip': 1.0 + 0.1 * nrm(ks[19], (DEPTH, SSD_HEADS)),
        'g_ssd': gain(ks[20], (DEPTH, D_SSD)),
        'w_out': nrm(ks[21], (DEPTH, D_MIX, D_MODEL), D_MIX ** -0.5),
        'g_ffn2': gain(ks[22], (DEPTH, D_MODEL)),
        'w1_ffn2': nrm(ks[23], (DEPTH, D_MODEL, D_FF), D_MODEL ** -0.5),
        'w3_ffn2': nrm(ks[24], (DEPTH, D_MODEL, D_FF), D_MODEL ** -0.5),
        'w2_ffn2': nrm(ks[25], (DEPTH, D_FF, D_MODEL), D_FF ** -0.5),
        'g_final': gain(ks[26], (D_MODEL,)),
    }


def reference(x_prompt, x_sample, cache_k, cache_v, cache_kidx, state_conv, state_ssm, page_table, rel_bias, g_ffn1, w1_ffn1, w3_ffn1, w2_ffn1, g_mix, w_in, conv_w, conv_b, a_log, dt_bias, d_skip, g_ssd, w_out, g_ffn2, w1_ffn2, w3_ffn2, w2_ffn2, g_final):
    def layer(l, x, attend, conv_prev, ssm_prev):
        b, L, _ = x.shape
        x = x + 0.5 * swiglu(rms_norm(x, g_ffn1[l]), w1_ffn1[l], w3_ffn1[l], w2_ffn1[l])
        h = rms_norm(x, g_mix[l])
        q, k, v, qi, ki, wi, z, xbc, dtr = split_cols(h @ w_in[l], IN_SPLITS)
        q = q.reshape(b, L, N_HEADS, HEAD_DIM)
        k = k.reshape(b, L, N_KV_HEADS, HEAD_DIM)
        v = v.reshape(b, L, N_KV_HEADS, HEAD_DIM)
        qi = qi.reshape(b, L, IDX_HEADS, IDX_DIM)
        wi = wi * (IDX_HEADS * IDX_DIM) ** -0.5
        a_out = attend(q, k, v, qi, ki, wi)
        s_out, conv_new, ssm_new = ssd_mixer(z, xbc, dtr, conv_prev, ssm_prev, conv_w[l], conv_b[l], a_log[l], dt_bias[l], d_skip[l], g_ssd[l])
        x = x + jnp.concatenate([a_out, s_out], axis=-1) @ w_out[l]
        x = x + 0.5 * swiglu(rms_norm(x, g_ffn2[l]), w1_ffn2[l], w3_ffn2[l], w2_ffn2[l])
        return x, k, v, ki, conv_new, ssm_new

    bp = x_prompt.shape[0]
    xp = x_prompt
    kp, vp, kip, cp, sp = [], [], [], [], []
    for l in range(DEPTH):
        zero_conv = jnp.zeros((bp, CONV_WIDTH - 1, CONV_DIM), x_prompt.dtype)
        zero_ssm = jnp.zeros((bp, SSD_HEADS, SSD_HEAD_DIM, D_STATE), jnp.float32)
        attend = functools.partial(prompt_attend, rel_bias=rel_bias)
        xp, k, v, ki, cn, sn = layer(l, xp, attend, zero_conv, zero_ssm)
        kp.append(k); vp.append(v); kip.append(ki); cp.append(cn); sp.append(sn)
    y_prompt = rms_norm(xp, g_final)

    xs = x_sample
    ks_, vs_, kis_, cs_, ss_ = [], [], [], [], []
    for l in range(DEPTH):
        attend = functools.partial(sample_attend, rel_bias=rel_bias, pool_k=cache_k[l], pool_v=cache_v[l], pool_kidx=cache_kidx[l], page_table=page_table)
        xs, k, v, ki, cn, sn = layer(l, xs, attend, state_conv[l], state_ssm[l])
        ks_.append(k); vs_.append(v); kis_.append(ki); cs_.append(cn); ss_.append(sn)
    y_sample = rms_norm(xs, g_final)

    k_prompt = jnp.stack(kp)
    v_prompt = jnp.stack(vp)
    kidx_prompt = jnp.stack(kip)
    conv_prompt = jnp.stack(cp)
    ssm_prompt = jnp.stack(sp)
    k_sample = jnp.stack(ks_)
    v_sample = jnp.stack(vs_)
    kidx_sample = jnp.stack(kis_)
    conv_sample = jnp.stack(cs_)
    ssm_sample = jnp.stack(ss_)
    return (y_prompt, y_sample, k_prompt, v_prompt, kidx_prompt, conv_prompt, ssm_prompt, k_sample, v_sample, kidx_sample, conv_sample, ssm_sample)
```

```python
import functools
import math

import jax
import jax.numpy as jnp
from jax import lax
from jax.experimental import pallas as pl
from jax.experimental.pallas import tpu as pltpu

F32 = jnp.float32
BF16 = jnp.bfloat16
I32 = jnp.int32

D_MODEL = 2048
PAGE_SIZE = 128
N_HEADS = 8
HEAD_DIM = 128
N_KV_HEADS = 2
GQA_GROUP = N_HEADS // N_KV_HEADS
IDX_HEADS = 16
IDX_DIM = 64
TOPK = 256
N_BUCKETS = 32
MAX_DISTANCE = 128
SSD_HEADS = 16
SSD_HEAD_DIM = 64
SSD_GROUPS = 2
D_STATE = 128
CONV_WIDTH = 4
CHUNK = 128
D_ATTN = N_HEADS * HEAD_DIM
D_SSD = SSD_HEADS * SSD_HEAD_DIM
D_KV = N_KV_HEADS * HEAD_DIM
CONV_DIM = D_SSD + 2 * SSD_GROUPS * D_STATE
D_FF = 5632
EPS = 1e-6
IN_SPLITS = (D_ATTN, D_KV, D_KV, IDX_HEADS * IDX_DIM, IDX_DIM, IDX_HEADS, D_SSD, CONV_DIM, SSD_HEADS)

LANES = 128
COL_Q = 0
COL_QI = COL_Q + D_ATTN
COL_Z = COL_QI + IDX_HEADS * IDX_DIM
COL_XBC = COL_Z + D_SSD
COL_K = COL_XBC + CONV_DIM
COL_V = COL_K + D_KV
COL_SMALL = COL_V + D_KV
SM_KI = 0
SM_WI = SM_KI + IDX_DIM
SM_DT = SM_WI + IDX_HEADS
IN_TILE = 768
N_IN = ((COL_SMALL + LANES + IN_TILE - 1) // IN_TILE) * IN_TILE

INT_MIN = -(2 ** 31)
NEG_BIG = -1e30
VMEM_LIMIT = 56 * 1024 * 1024


def _params(sem):
    return pltpu.CompilerParams(dimension_semantics=sem, vmem_limit_bytes=VMEM_LIMIT)


def _rms(x, g):
    return x * lax.rsqrt(jnp.mean(x * x, axis=-1, keepdims=True) + EPS) * g


def _dot(a, b):
    return jnp.dot(a, b, preferred_element_type=F32)


def _dot_nt(a, b):
    return lax.dot_general(a, b, (((1,), (1,)), ((), ())), preferred_element_type=F32)


def _dot_exact(a, b):
    return jnp.dot(a, b, preferred_element_type=F32, precision=lax.Precision.HIGHEST)


def _sort_key(x):
    bits = lax.bitcast_convert_type(x + 0.0, I32)
    return bits ^ ((bits >> 31) & 0x7FFFFFFF)


def _ffn_kernel(x_ref, g_ref, w1_ref, w3_ref, w2_ref, gf_ref, o_ref, h_scr, *, final_norm):
    f = pl.program_id(1)

    @pl.when(f == 0)
    def _():
        h_scr[...] = _rms(x_ref[...], g_ref[...]).astype(BF16)
        o_ref[...] = jnp.zeros_like(o_ref)

    h = h_scr[...]
    a = _dot(h, w1_ref[...])
    b = _dot(h, w3_ref[...])
    u = (a * jax.nn.sigmoid(a) * b).astype(BF16)
    o_ref[...] += _dot(u, w2_ref[...])

    @pl.when(f == pl.num_programs(1) - 1)
    def _():
        y = x_ref[...] + 0.5 * o_ref[...]
        if final_norm:
            y = _rms(y, gf_ref[...])
        o_ref[...] = y


def _ffn(x, g, w1, w3, w2, gf, *, tm, tf, final_norm):
    t = x.shape[0]
    return pl.pallas_call(
        functools.partial(_ffn_kernel, final_norm=final_norm),
        grid=(t // tm, D_FF // tf),
        in_specs=[
            pl.BlockSpec((tm, D_MODEL), lambda i, f: (i, 0)),
            pl.BlockSpec((1, D_MODEL), lambda i, f: (0, 0)),
            pl.BlockSpec((D_MODEL, tf), lambda i, f: (0, f)),
            pl.BlockSpec((D_MODEL, tf), lambda i, f: (0, f)),
            pl.BlockSpec((tf, D_MODEL), lambda i, f: (f, 0)),
            pl.BlockSpec((1, D_MODEL), lambda i, f: (0, 0)),
        ],
        out_specs=pl.BlockSpec((tm, D_MODEL), lambda i, f: (i, 0)),
        out_shape=jax.ShapeDtypeStruct((t, D_MODEL), F32),
        scratch_shapes=[pltpu.VMEM((tm, D_MODEL), BF16)],
        compiler_params=_params(("parallel", "arbitrary")),
        name="ffn",
    )(x, g, w1, w3, w2, gf)


def _in_proj_kernel(x_ref, g_ref, w_ref, o_ref, h_scr):
    @pl.when(pl.program_id(1) == 0)
    def _():
        h_scr[...] = _rms(x_ref[...], g_ref[...]).astype(BF16)

    o_ref[...] = _dot(h_scr[...], w_ref[...])


def _in_proj(x, g, w, *, tm):
    t = x.shape[0]
    return pl.pallas_call(
        _in_proj_kernel,
        grid=(t // tm, N_IN // IN_TILE),
        in_specs=[
            pl.BlockSpec((tm, D_MODEL), lambda i, j: (i, 0)),
            pl.BlockSpec((1, D_MODEL), lambda i, j: (0, 0)),
            pl.BlockSpec((D_MODEL, IN_TILE), lambda i, j: (0, j)),
        ],
        out_specs=pl.BlockSpec((tm, IN_TILE), lambda i, j: (i, j)),
        out_shape=jax.ShapeDtypeStruct((t, N_IN), F32),
        scratch_shapes=[pltpu.VMEM((tm, D_MODEL), BF16)],
        compiler_params=_params(("parallel", "arbitrary")),
        name="in_proj",
    )(x, g, w)


def _out_proj_kernel(x_ref, a_ref, s_ref, wa_ref, ws_ref, o_ref):
    acc = _dot(a_ref[...].astype(BF16), wa_ref[...])
    acc += _dot(s_ref[...].astype(BF16), ws_ref[...])
    o_ref[...] = x_ref[...] + acc


def _out_proj(x, a, s, wa, ws, *, tm):
    t = x.shape[0]
    return pl.pallas_call(
        _out_proj_kernel,
        grid=(t // tm,),
        in_specs=[
            pl.BlockSpec((tm, D_MODEL), lambda i: (i, 0)),
            pl.BlockSpec((tm, D_ATTN), lambda i: (i, 0)),
            pl.BlockSpec((tm, D_SSD), lambda i: (i, 0)),
            pl.BlockSpec((D_ATTN, D_MODEL), lambda i: (0, 0)),
            pl.BlockSpec((D_SSD, D_MODEL), lambda i: (0, 0)),
        ],
        out_specs=pl.BlockSpec((tm, D_MODEL), lambda i: (i, 0)),
        out_shape=jax.ShapeDtypeStruct((t, D_MODEL), F32),
        compiler_params=_params(("parallel",)),
        name="out_proj",
    )(x, a, s, wa, ws)


NEAR = 2 * LANES


def _bias_kernel(rb_ref, o_ref):
    r = lax.broadcasted_iota(I32, (LANES, NEAR), 0)
    c = lax.broadcasted_iota(I32, (LANES, NEAR), 1)
    n = jnp.maximum(r + LANES - c, 0)
    max_exact = N_BUCKETS // 2
    nf = jnp.maximum(n, 1).astype(F32)
    large = max_exact + (jnp.log(nf / max_exact) / math.log(MAX_DISTANCE / max_exact)
                         * (N_BUCKETS - max_exact)).astype(I32)
    bucket = jnp.where(n < max_exact, n, jnp.minimum(large, N_BUCKETS - 1))
    for h in range(N_HEADS):
        acc = jnp.zeros((LANES, NEAR), F32)
        for b in range(N_BUCKETS):
            acc = jnp.where(bucket == b, rb_ref[b, h], acc)
        o_ref[h] = acc


def _bias_table(rel_bias):
    return pl.pallas_call(
        _bias_kernel,
        in_specs=[pl.BlockSpec(memory_space=pltpu.SMEM)],
        out_specs=pl.BlockSpec(memory_space=pltpu.VMEM),
        out_shape=jax.ShapeDtypeStruct((N_HEADS, LANES, NEAR), F32),
        name="bias_table",
    )(rel_bias)


def _count(mask):
    return jnp.sum(mask.astype(F32), axis=1, keepdims=True)


def _kth_largest(count_ge, rows):
    t = jnp.full((rows, 1), INT_MIN, I32)
    t = jnp.where(count_ge(jnp.zeros((rows, 1), I32)) >= TOPK, 0, t)

    def body(it, t):
        cand = t + lax.shift_left(jnp.int32(1), 30 - it)
        return jnp.where(count_ge(cand) >= TOPK, cand, t)

    return lax.fori_loop(0, 31, body, t)


def _tie_limit(count_eq_below, need, rows, idx_bits):
    def body(it, m):
        cand = m + lax.shift_left(jnp.int32(1), idx_bits - 1 - it)
        return jnp.where(count_eq_below(cand) < need, cand, m)

    return lax.fori_loop(0, idx_bits, body, jnp.zeros((rows, 1), I32))


def _dsa_prompt_kernel(q_ref, qi_ref, k_ref, v_ref, sm_ref, bt_ref, o_ref, sk_scr, *, seq):
    i = pl.program_id(1)
    q0 = pl.multiple_of(i * LANES, LANES)
    tq = LANES

    ki = sm_ref[:, SM_KI:SM_KI + IDX_DIM].astype(BF16)
    wi = sm_ref[pl.ds(q0, tq), SM_WI:SM_WI + IDX_HEADS] * (IDX_HEADS * IDX_DIM) ** -0.5
    score = jnp.zeros((tq, seq), F32)
    for h in range(IDX_HEADS):
        qh = qi_ref[:, h * IDX_DIM:(h + 1) * IDX_DIM].astype(BF16)
        score = score + wi[:, h:h + 1] * jnp.maximum(_dot_nt(qh, ki), 0.0)

    col = lax.broadcasted_iota(I32, (tq, seq), 1)
    row = lax.broadcasted_iota(I32, (tq, seq), 0) + q0
    adm = col <= row
    sk_scr[...] = jnp.where(adm, _sort_key(score), INT_MIN)

    thr = _kth_largest(lambda c: _count(sk_scr[...] >= c), tq)
    need = TOPK - _count(sk_scr[...] > thr)
    lim = _tie_limit(lambda m: _count((sk_scr[...] == thr) & (col < m)), need, tq,
                     int(math.log2(seq)))
    sk = sk_scr[...]
    sel = adm & ((sk > thr) | ((sk == thr) & (col <= lim)))

    tile = col // LANES
    scale = HEAD_DIM ** -0.5
    for g in range(N_KV_HEADS):
        kg = k_ref[:, g * HEAD_DIM:(g + 1) * HEAD_DIM].astype(BF16)
        vg = v_ref[:, g * HEAD_DIM:(g + 1) * HEAD_DIM].astype(BF16)
        for hh in range(GQA_GROUP):
            h = g * GQA_GROUP + hh
            qh = q_ref[:, h * HEAD_DIM:(h + 1) * HEAD_DIM].astype(BF16)
            far = bt_ref[h, 0:1, 0:1]
            prev = jnp.tile(bt_ref[h, :, 0:LANES], (1, seq // LANES))
            diag = jnp.tile(bt_ref[h, :, LANES:NEAR], (1, seq // LANES))
            bias = jnp.where(tile == i, diag, jnp.where(tile == i - 1, prev, far))
            lg = jnp.where(sel, _dot_nt(qh, kg) * scale + bias, -jnp.inf)
            m = jnp.max(lg, axis=1, keepdims=True)
            p = jnp.exp(lg - m)
            l = jnp.sum(p, axis=1, keepdims=True)
            o_ref[:, h * HEAD_DIM:(h + 1) * HEAD_DIM] = _dot(p.astype(BF16), vg) / l


def _dsa_prompt(proj, btab, *, batch, seq):
    nq = seq // LANES
    return pl.pallas_call(
        functools.partial(_dsa_prompt_kernel, seq=seq),
        grid=(batch, nq),
        in_specs=[
            pl.BlockSpec((LANES, D_ATTN), lambda b, i: (b * nq + i, COL_Q // D_ATTN)),
            pl.BlockSpec((LANES, IDX_HEADS * IDX_DIM), lambda b, i: (b * nq + i, COL_QI // (IDX_HEADS * IDX_DIM))),
            pl.BlockSpec((seq, D_KV), lambda b, i: (b, COL_K // D_KV)),
            pl.BlockSpec((seq, D_KV), lambda b, i: (b, COL_V // D_KV)),
            pl.BlockSpec((seq, LANES), lambda b, i: (b, COL_SMALL // LANES)),
            pl.BlockSpec((N_HEADS, LANES, NEAR), lambda b, i: (0, 0, 0)),
        ],
        out_specs=pl.BlockSpec((LANES, D_ATTN), lambda b, i: (b * nq + i, 0)),
        out_shape=jax.ShapeDtypeStruct((batch * seq, D_ATTN), F32),
        scratch_shapes=[pltpu.VMEM((LANES, seq), I32)],
        compiler_params=_params(("parallel", "parallel")),
        name="dsa_prompt",
    )(proj, proj, proj, proj, proj, btab)


SCORE_PAGES = 16
ATTN_PAGES = 8


def _sample_queries(qi_ref):
    return jnp.concatenate(
        [qi_ref[:, h * IDX_DIM:(h + 1) * IDX_DIM] for h in range(IDX_HEADS)], axis=0).astype(BF16)


def _sample_scores(rel, wi, nq):
    sc = jnp.zeros((nq, rel.shape[1]), F32)
    for h in range(IDX_HEADS):
        sc = sc + wi[:, h:h + 1] * jnp.maximum(rel[h * nq:(h + 1) * nq, :], 0.0)
    return sc


def _sample_score_kernel(pt_ref, qi_ref, sm_ref, *refs, nq):
    pages = refs[:SCORE_PAGES]
    past_ref, new_ref = refs[SCORE_PAGES:]
    s = pl.program_id(1)
    qx = _sample_queries(qi_ref)
    wi = sm_ref[:, SM_WI:SM_WI + IDX_HEADS] * (IDX_HEADS * IDX_DIM) ** -0.5
    for r in range(SCORE_PAGES):
        rel = _dot_nt(qx, pages[r][...].astype(BF16))
        past_ref[:, r * PAGE_SIZE:(r + 1) * PAGE_SIZE] = _sort_key(_sample_scores(rel, wi, nq))

    @pl.when(s == pl.num_programs(1) - 1)
    def _():
        ki_new = jnp.concatenate(
            [sm_ref[:, SM_KI:SM_KI + IDX_DIM], jnp.zeros((LANES - nq, IDX_DIM), F32)], axis=0).astype(BF16)
        sc = _sample_scores(_dot_nt(qx, ki_new), wi, nq)
        j = lax.broadcasted_iota(I32, (nq, LANES), 1)
        t = lax.broadcasted_iota(I32, (nq, LANES), 0)
        new_ref[...] = jnp.where(j <= t, _sort_key(sc), INT_MIN)


def _sample_score(page_table, proj_s, pool_kidx, *, batch, nq, n_pages):
    steps = n_pages // SCORE_PAGES
    page_specs = [
        pl.BlockSpec((None, PAGE_SIZE, IDX_DIM), functools.partial(
            lambda b, s, pt, r: (pt[b, s * SCORE_PAGES + r], 0, 0), r=r))
        for r in range(SCORE_PAGES)
    ]
    grid_spec = pltpu.PrefetchScalarGridSpec(
        num_scalar_prefetch=1,
        grid=(batch, steps),
        in_specs=[
            pl.BlockSpec((nq, IDX_HEADS * IDX_DIM), lambda b, s, pt: (b, COL_QI // (IDX_HEADS * IDX_DIM))),
            pl.BlockSpec((nq, LANES), lambda b, s, pt: (b, COL_SMALL // LANES)),
        ] + page_specs,
        out_specs=[
            pl.BlockSpec((None, nq, SCORE_PAGES * PAGE_SIZE), lambda b, s, pt: (b, 0, s)),
            pl.BlockSpec((None, nq, LANES), lambda b, s, pt: (b, 0, 0)),
        ],
    )
    return pl.pallas_call(
        functools.partial(_sample_score_kernel, nq=nq),
        grid_spec=grid_spec,
        out_shape=[
            jax.ShapeDtypeStruct((batch, nq, n_pages * PAGE_SIZE), I32),
            jax.ShapeDtypeStruct((batch, nq, LANES), I32),
        ],
        compiler_params=_params(("parallel", "arbitrary")),
        name="sample_score",
    )(page_table, proj_s, proj_s, *([pool_kidx] * SCORE_PAGES))


def _sample_select_kernel(past_ref, new_ref, thr_ref, lim_ref, *, past):
    rows = past_ref.shape[0]
    colp = lax.broadcasted_iota(I32, (rows, past), 1)
    coln = lax.broadcasted_iota(I32, (rows, LANES), 1) + past

    def count_ge(c):
        return _count(past_ref[...] >= c) + _count(new_ref[...] >= c)

    thr = _kth_largest(count_ge, rows)
    need = TOPK - (_count(past_ref[...] > thr) + _count(new_ref[...] > thr))

    def count_eq_below(m):
        return (_count((past_ref[...] == thr) & (colp < m))
                + _count((new_ref[...] == thr) & (coln < m)))

    lim = _tie_limit(count_eq_below, need, rows, int(math.log2(past)) + 1)
    thr_ref[...] = jnp.broadcast_to(thr, (rows, LANES))
    lim_ref[...] = jnp.broadcast_to(lim, (rows, LANES))


def _sample_select(sk_past, sk_new, *, rows_per_step):
    rows, past = sk_past.shape
    return pl.pallas_call(
        functools.partial(_sample_select_kernel, past=past),
        grid=(rows // rows_per_step,),
        in_specs=[
            pl.BlockSpec((rows_per_step, past), lambda i: (i, 0)),
            pl.BlockSpec((rows_per_step, LANES), lambda i: (i, 0)),
        ],
        out_specs=[
            pl.BlockSpec((rows_per_step, LANES), lambda i: (i, 0)),
            pl.BlockSpec((rows_per_step, LANES), lambda i: (i, 0)),
        ],
        out_shape=[jax.ShapeDtypeStruct((rows, LANES), I32)] * 2,
        compiler_params=_params(("parallel",)),
        name="sample_select",
    )(sk_past, sk_new)


def _sample_attn_kernel(pt_ref, q_ref, kn_ref, vn_ref, skp_ref, skn_ref, thr_ref, lim_ref, bt_ref, *refs,
                        nq, past):
    kpages = refs[:ATTN_PAGES]
    vpages = refs[ATTN_PAGES:2 * ATTN_PAGES]
    o_ref, kbuf, vbuf, m_scr, l_scr, acc_scr = refs[2 * ATTN_PAGES:]
    s = pl.program_id(1)
    last = pl.num_programs(1) - 1
    width = ATTN_PAGES * PAGE_SIZE
    grows = GQA_GROUP * nq
    scale = HEAD_DIM ** -0.5

    @pl.when(s == 0)
    def _():
        m_scr[...] = jnp.full_like(m_scr, NEG_BIG)
        l_scr[...] = jnp.zeros_like(l_scr)
        acc_scr[...] = jnp.zeros_like(acc_scr)

    thr = thr_ref[:, 0:1]
    lim = lim_ref[:, 0:1]

    def select(sk, idx):
        sel = (sk > thr) | ((sk == thr) & (idx <= lim))
        return jnp.concatenate([sel] * GQA_GROUP, axis=0)

    def far_bias(g):
        return jnp.concatenate(
            [jnp.broadcast_to(bt_ref[g * GQA_GROUP + hh, 0:1, 0:1], (nq, 1)) for hh in range(GQA_GROUP)], axis=0)

    def near_bias(g, lo, hi):
        return jnp.concatenate([bt_ref[g * GQA_GROUP + hh, 0:nq, lo:hi] for hh in range(GQA_GROUP)], axis=0)

    def group_queries(g):
        return jnp.concatenate(
            [q_ref[:, (g * GQA_GROUP + hh) * HEAD_DIM:(g * GQA_GROUP + hh + 1) * HEAD_DIM]
             for hh in range(GQA_GROUP)], axis=0).astype(BF16)

    def accumulate(g, lg, sel, vals):
        rs = slice(g * grows, (g + 1) * grows)
        m_old = m_scr[rs, :]
        m_new = jnp.maximum(m_old, jnp.max(jnp.where(sel, lg, NEG_BIG), axis=1, keepdims=True))
        p = jnp.where(sel, jnp.exp(lg - m_new), 0.0)
        alpha = jnp.exp(m_old - m_new)
        l_scr[rs, :] = alpha * l_scr[rs, :] + jnp.sum(p, axis=1, keepdims=True)
        acc_scr[rs, :] = alpha * acc_scr[rs, :] + _dot(p.astype(BF16), vals)
        m_scr[rs, :] = m_new

    for r in range(ATTN_PAGES):
        kbuf[r * PAGE_SIZE:(r + 1) * PAGE_SIZE, :] = kpages[r][...].astype(BF16)
        vbuf[r * PAGE_SIZE:(r + 1) * PAGE_SIZE, :] = vpages[r][...].astype(BF16)

    col = lax.broadcasted_iota(I32, (nq, width), 1)
    sel = select(skp_ref[...], col + s * width)
    colg = lax.broadcasted_iota(I32, (grows, width), 1)
    in_near = (s == last) & (colg >= width - PAGE_SIZE)
    for g in range(N_KV_HEADS):
        cs = slice(g * HEAD_DIM, (g + 1) * HEAD_DIM)
        near = jnp.tile(near_bias(g, 0, LANES), (1, ATTN_PAGES))
        bias = jnp.where(in_near, near, far_bias(g))
        lg = _dot_nt(group_queries(g), kbuf[:, cs]) * scale + bias
        accumulate(g, lg, sel, vbuf[:, cs])

    @pl.when(s == last)
    def _():
        pad = jnp.zeros((LANES - nq, D_KV), F32)
        kn = jnp.concatenate([kn_ref[...], pad], axis=0).astype(BF16)
        vn = jnp.concatenate([vn_ref[...], pad], axis=0).astype(BF16)
        coln = lax.broadcasted_iota(I32, (nq, LANES), 1) + past
        seln = select(skn_ref[...], coln)
        for g in range(N_KV_HEADS):
            cs = slice(g * HEAD_DIM, (g + 1) * HEAD_DIM)
            lg = _dot_nt(group_queries(g), kn[:, cs]) * scale + near_bias(g, LANES, NEAR)
            accumulate(g, lg, seln, vn[:, cs])
        out = acc_scr[...] / l_scr[...]
        for h in range(N_HEADS):
            o_ref[:, h * HEAD_DIM:(h + 1) * HEAD_DIM] = out[h * nq:(h + 1) * nq, :]


def _sample_attn(page_table, proj_s, sk_past, sk_new, thr, lim, btab, pool_k, pool_v, *, batch, nq, n_pages):
    steps = n_pages // ATTN_PAGES
    width = ATTN_PAGES * PAGE_SIZE
    past = n_pages * PAGE_SIZE

    def page_spec(r):
        return pl.BlockSpec((None, PAGE_SIZE, D_KV), functools.partial(
            lambda b, s, pt, r: (pt[b, s * ATTN_PAGES + r], 0, 0), r=r))

    grid_spec = pltpu.PrefetchScalarGridSpec(
        num_scalar_prefetch=1,
        grid=(batch, steps),
        in_specs=[
            pl.BlockSpec((nq, D_ATTN), lambda b, s, pt: (b, COL_Q // D_ATTN)),
            pl.BlockSpec((nq, D_KV), lambda b, s, pt: (b, COL_K // D_KV)),
            pl.BlockSpec((nq, D_KV), lambda b, s, pt: (b, COL_V // D_KV)),
            pl.BlockSpec((None, nq, width), lambda b, s, pt: (b, 0, s)),
            pl.BlockSpec((None, nq, LANES), lambda b, s, pt: (b, 0, 0)),
            pl.BlockSpec((nq, LANES), lambda b, s, pt: (b, 0)),
            pl.BlockSpec((nq, LANES), lambda b, s, pt: (b, 0)),
            pl.BlockSpec((N_HEADS, LANES, NEAR), lambda b, s, pt: (0, 0, 0)),
        ] + [page_spec(r) for r in range(ATTN_PAGES)] * 2,
        out_specs=pl.BlockSpec((nq, D_ATTN), lambda b, s, pt: (b, 0)),
        scratch_shapes=[
            pltpu.VMEM((width, D_KV), BF16),
            pltpu.VMEM((width, D_KV), BF16),
            pltpu.VMEM((N_HEADS * nq, 1), F32),
            pltpu.VMEM((N_HEADS * nq, 1), F32),
            pltpu.VMEM((N_HEADS * nq, HEAD_DIM), F32),
        ],
    )
    return pl.pallas_call(
        functools.partial(_sample_attn_kernel, nq=nq, past=past),
        grid_spec=grid_spec,
        out_shape=jax.ShapeDtypeStruct((batch * nq, D_ATTN), F32),
        compiler_params=_params(("parallel", "arbitrary")),
        name="sample_attn",
    )(page_table, proj_s, proj_s, proj_s, sk_past, sk_new, thr, lim, btab,
      *([pool_k] * ATTN_PAGES), *([pool_v] * ATTN_PAGES))


TAIL = 8


def _ssd_kernel(xbc_ref, z_ref, sm_ref, cprev_ref, hprev_ref, cw_ref, cb_ref, alog_ref, dtb_ref, dvec_ref,
                g_ref, y_ref, cnew_ref, hnew_ref, xpad_scr, sm_scr, h_scr, yd_scr, *, rows):
    c = pl.program_id(1)
    q = CHUNK
    keep = CONV_WIDTH - 1
    heads_per_group = SSD_HEADS // SSD_GROUPS

    @pl.when(c == 0)
    def _():
        xpad_scr[...] = jnp.zeros_like(xpad_scr)
        sm_scr[...] = jnp.zeros_like(sm_scr)
        xpad_scr[TAIL - keep:TAIL, :] = cprev_ref[0]
        h_scr[...] = hprev_ref[0]

    xpad_scr[TAIL:TAIL + rows, :] = xbc_ref[...]
    sm_scr[0:rows, :] = sm_ref[...]
    conv = sum(xpad_scr[TAIL - keep + k:TAIL - keep + k + q, :] * cw_ref[k:k + 1, :]
               for k in range(CONV_WIDTH)) + cb_ref[...]
    new_tail = xpad_scr[rows + TAIL - keep:rows + TAIL, :]
    cnew_ref[0] = new_tail
    xpad_scr[TAIL - keep:TAIL, :] = new_tail

    xc = conv * jax.nn.sigmoid(conv)
    xs = xc[:, :D_SSD]
    bm = xc[:, D_SSD:D_SSD + SSD_GROUPS * D_STATE].astype(BF16)
    cm = xc[:, D_SSD + SSD_GROUPS * D_STATE:].astype(BF16)

    ri = lax.broadcasted_iota(I32, (q, q), 0)
    ci = lax.broadcasted_iota(I32, (q, q), 1)
    causal = ri >= ci
    x = sm_scr[:, SM_DT:SM_DT + SSD_HEADS] + dtb_ref[...]
    dt = jnp.maximum(x, 0.0) + jnp.log1p(jnp.exp(-jnp.abs(x)))
    dt = jnp.where(lax.broadcasted_iota(I32, (q, SSD_HEADS), 0) < rows, dt, 0.0)
    a = -jnp.exp(alog_ref[...])
    acum = _dot_exact(causal.astype(F32), dt * a)
    acum_t = jnp.concatenate([acum, jnp.zeros((q, LANES - SSD_HEADS), F32)], axis=1).T

    expand = (lax.broadcasted_iota(I32, (SSD_HEADS, D_SSD), 1) // SSD_HEAD_DIM
              == lax.broadcasted_iota(I32, (SSD_HEADS, D_SSD), 0)).astype(F32)
    expand_t = (lax.broadcasted_iota(I32, (D_SSD, SSD_HEADS), 0) // SSD_HEAD_DIM
                == lax.broadcasted_iota(I32, (D_SSD, SSD_HEADS), 1)).astype(F32)
    acum_x = _dot_exact(acum, expand)
    xd = xs * _dot_exact(dt, expand)
    xdw_t = (xd * jnp.exp(acum_x[q - 1:q, :] - acum_x)).T.astype(BF16)
    xd = xd.astype(BF16)
    state_decay = jnp.exp(_dot_exact(expand_t, jnp.broadcast_to(acum_t[0:SSD_HEADS, q - 1:q], (SSD_HEADS, LANES))))

    for g in range(SSD_GROUPS):
        ns = slice(g * D_STATE, (g + 1) * D_STATE)
        hs = slice(g * heads_per_group * SSD_HEAD_DIM, (g + 1) * heads_per_group * SSD_HEAD_DIM)
        cb = _dot_nt(cm[:, ns], bm[:, ns])
        for e in range(heads_per_group):
            h = g * heads_per_group + e
            ps = slice(h * SSD_HEAD_DIM, (h + 1) * SSD_HEAD_DIM)
            seg = acum[:, h:h + 1] - acum_t[h:h + 1, :]
            lmat = jnp.exp(jnp.where(causal, seg, -jnp.inf))
            yd_scr[:, ps] = _dot((cb * lmat).astype(BF16), xd[:, ps])
        h_old = h_scr[hs, :]
        yd_scr[:, hs] += _dot_nt(cm[:, ns], h_old.astype(BF16)) * jnp.exp(acum_x[:, hs])
        h_scr[hs, :] = h_old * state_decay[hs, :] + _dot(xdw_t[hs, :], bm[:, ns])

    hnew_ref[0] = h_scr[...]
    y = yd_scr[0:rows, :] + dvec_ref[...] * xs[0:rows, :]
    zz = z_ref[...]
    y = y * (zz * jax.nn.sigmoid(zz))
    width = D_SSD // SSD_GROUPS
    for g in range(SSD_GROUPS):
        cs = slice(g * width, (g + 1) * width)
        y_ref[:, cs] = _rms(y[:, cs], g_ref[:, cs])


def _ssd(proj, conv_prev, ssm_prev, conv_w, conv_b, a_log, dt_bias, d_vec, g_ssd, *, batch, seq):
    rows = min(CHUNK, seq)
    nc = seq // rows
    full = lambda *shape: pl.BlockSpec(shape, lambda b, c: (0,) * len(shape))
    return pl.pallas_call(
        functools.partial(_ssd_kernel, rows=rows),
        grid=(batch, nc),
        in_specs=[
            pl.BlockSpec((rows, CONV_DIM), lambda b, c: (b * nc + c, COL_XBC // CONV_DIM)),
            pl.BlockSpec((rows, D_SSD), lambda b, c: (b * nc + c, COL_Z // D_SSD)),
            pl.BlockSpec((rows, LANES), lambda b, c: (b * nc + c, COL_SMALL // LANES)),
            pl.BlockSpec((1, CONV_WIDTH - 1, CONV_DIM), lambda b, c: (b, 0, 0)),
            pl.BlockSpec((1, D_SSD, D_STATE), lambda b, c: (b, 0, 0)),
            full(CONV_WIDTH, CONV_DIM), full(1, CONV_DIM), full(1, SSD_HEADS), full(1, SSD_HEADS),
            full(1, D_SSD), full(1, D_SSD),
        ],
        out_specs=[
            pl.BlockSpec((rows, D_SSD), lambda b, c: (b * nc + c, 0)),
            pl.BlockSpec((1, CONV_WIDTH - 1, CONV_DIM), lambda b, c: (b, 0, 0)),
            pl.BlockSpec((1, D_SSD, D_STATE), lambda b, c: (b, 0, 0)),
        ],
        out_shape=[
            jax.ShapeDtypeStruct((batch * seq, D_SSD), F32),
            jax.ShapeDtypeStruct((batch, CONV_WIDTH - 1, CONV_DIM), F32),
            jax.ShapeDtypeStruct((batch, D_SSD, D_STATE), F32),
        ],
        scratch_shapes=[
            pltpu.VMEM((CHUNK + TAIL, CONV_DIM), F32),
            pltpu.VMEM((CHUNK, LANES), F32),
            pltpu.VMEM((D_SSD, D_STATE), F32),
            pltpu.VMEM((CHUNK, D_SSD), F32),
        ],
        compiler_params=_params(("parallel", "arbitrary")),
        name="ssd",
    )(proj, proj, proj, conv_prev, ssm_prev, conv_w, conv_b, a_log, dt_bias, d_vec, g_ssd)


def _reorder_w_in(w_in):
    offs = [0]
    for s in IN_SPLITS:
        offs.append(offs[-1] + s)
    q, k, v, qi, ki, wi, z, xbc, dtr = [w_in[:, offs[n]:offs[n + 1]] for n in range(len(IN_SPLITS))]
    cols = jnp.concatenate([q, qi, z, xbc, k, v, ki, wi, dtr], axis=1)
    return jnp.pad(cols, ((0, 0), (0, N_IN - cols.shape[1]))).astype(BF16)


def kernel(x_prompt, x_sample, cache_k, cache_v, cache_kidx, state_conv, state_ssm, page_table, rel_bias, g_ffn1, w1_ffn1, w3_ffn1, w2_ffn1, g_mix, w_in, conv_w, conv_b, a_log, dt_bias, d_skip, g_ssd, w_out, g_ffn2, w1_ffn2, w3_ffn2, w2_ffn2, g_final):
    depth = w_in.shape[0]
    assert depth == 1
    l = 0
    bp, seq, _ = x_prompt.shape
    bs, nq, _ = x_sample.shape
    n_pages = page_table.shape[1]
    row = lambda t: t.reshape(1, -1)

    w1a, w3a, w2a = w1_ffn1[l].astype(BF16), w3_ffn1[l].astype(BF16), w2_ffn1[l].astype(BF16)
    w1b, w3b, w2b = w1_ffn2[l].astype(BF16), w3_ffn2[l].astype(BF16), w2_ffn2[l].astype(BF16)
    w_in_r = _reorder_w_in(w_in[l])
    w_out_a = w_out[l, :D_ATTN].astype(BF16)
    w_out_s = w_out[l, D_ATTN:].astype(BF16)
    d_vec = jnp.repeat(d_skip[l], SSD_HEAD_DIM).reshape(1, D_SSD)
    btab = _bias_table(rel_bias)
    gf = row(g_final)

    def trunk(x, tm, attend, conv_prev, ssm_prev, batch, length):
        x = _ffn(x, row(g_ffn1[l]), w1a, w3a, w2a, gf, tm=tm, tf=512, final_norm=False)
        proj = _in_proj(x, row(g_mix[l]), w_in_r, tm=tm)
        a_out = attend(proj)
        s_out, conv_new, ssm_new = _ssd(
            proj, conv_prev, ssm_prev.reshape(batch, D_SSD, D_STATE), conv_w[l], row(conv_b[l]), row(a_log[l]),
            row(dt_bias[l]), d_vec, row(g_ssd[l]), batch=batch, seq=length)
        x = _out_proj(x, a_out, s_out, w_out_a, w_out_s, tm=min(tm, 256))
        y = _ffn(x, row(g_ffn2[l]), w1b, w3b, w2b, gf, tm=tm, tf=512, final_norm=True)
        k = proj[:, COL_K:COL_K + D_KV].reshape(1, batch, length, N_KV_HEADS, HEAD_DIM)
        v = proj[:, COL_V:COL_V + D_KV].reshape(1, batch, length, N_KV_HEADS, HEAD_DIM)
        ki = proj[:, COL_SMALL + SM_KI:COL_SMALL + SM_KI + IDX_DIM].reshape(1, batch, length, IDX_DIM)
        ssm_new = ssm_new.reshape(1, batch, SSD_HEADS, SSD_HEAD_DIM, D_STATE)
        return y.reshape(batch, length, D_MODEL), k, v, ki, conv_new[None], ssm_new

    y_p, k_p, v_p, ki_p, conv_p, ssm_p = trunk(
        x_prompt.reshape(bp * seq, D_MODEL), 512,
        functools.partial(_dsa_prompt, btab=btab, batch=bp, seq=seq),
        jnp.zeros((bp, CONV_WIDTH - 1, CONV_DIM), F32), jnp.zeros((bp, SSD_HEADS, SSD_HEAD_DIM, D_STATE), F32),
        bp, seq)

    pool_k = cache_k[l].reshape(-1, PAGE_SIZE, D_KV)
    pool_v = cache_v[l].reshape(-1, PAGE_SIZE, D_KV)
    pool_kidx = cache_kidx[l]

    def sample_attend(proj):
        sk_past, sk_new = _sample_score(page_table, proj, pool_kidx, batch=bs, nq=nq, n_pages=n_pages)
        thr, lim = _sample_select(sk_past.reshape(bs * nq, -1), sk_new.reshape(bs * nq, LANES),
                                  rows_per_step=128)
        return _sample_attn(page_table, proj, sk_past, sk_new, thr, lim, btab, pool_k, pool_v,
                            batch=bs, nq=nq, n_pages=n_pages)

    y_s, k_s, v_s, ki_s, conv_s, ssm_s = trunk(
        x_sample.reshape(bs * nq, D_MODEL), bs * nq, sample_attend, state_conv[l], state_ssm[l], bs, nq)

    return (y_p, y_s, k_p, v_p, ki_p, conv_p, ssm_p, k_s, v_s, ki_s, conv_s, ssm_s)
```

```python
import functools
import math

import jax
import jax.numpy as jnp
from jax import lax
from jax.experimental import pallas as pl
from jax.experimental.pallas import tpu as pltpu

F32 = jnp.float32
BF16 = jnp.bfloat16
I32 = jnp.int32

D_MODEL = 2048
PAGE_SIZE = 128
N_HEADS = 8
HEAD_DIM = 128
N_KV_HEADS = 2
GQA_GROUP = N_HEADS // N_KV_HEADS
IDX_HEADS = 16
IDX_DIM = 64
TOPK = 256
N_BUCKETS = 32
MAX_DISTANCE = 128
SSD_HEADS = 16
SSD_HEAD_DIM = 64
SSD_GROUPS = 2
D_STATE = 128
CONV_WIDTH = 4
CHUNK = 128
D_ATTN = N_HEADS * HEAD_DIM
D_SSD = SSD_HEADS * SSD_HEAD_DIM
D_KV = N_KV_HEADS * HEAD_DIM
CONV_DIM = D_SSD + 2 * SSD_GROUPS * D_STATE
D_FF = 5632
EPS = 1e-6
IN_SPLITS = (D_ATTN, D_KV, D_KV, IDX_HEADS * IDX_DIM, IDX_DIM, IDX_HEADS, D_SSD, CONV_DIM, SSD_HEADS)

LANES = 128
COL_Q = 0
COL_QI = COL_Q + D_ATTN
COL_Z = COL_QI + IDX_HEADS * IDX_DIM
COL_XBC = COL_Z + D_SSD
COL_K = COL_XBC + CONV_DIM
COL_V = COL_K + D_KV
COL_SMALL = COL_V + D_KV
SM_KI = 0
SM_WI = SM_KI + IDX_DIM
SM_DT = SM_WI + IDX_HEADS
IN_TILE = 768
N_IN = ((COL_SMALL + LANES + IN_TILE - 1) // IN_TILE) * IN_TILE

INT_MIN = -(2 ** 31)
NEG_BIG = -1e30
VMEM_LIMIT = 56 * 1024 * 1024


def _params(sem):
    return pltpu.CompilerParams(dimension_semantics=sem, vmem_limit_bytes=VMEM_LIMIT)


def _rms(x, g):
    return x * lax.rsqrt(jnp.mean(x * x, axis=-1, keepdims=True) + EPS) * g


def _dot(a, b):
    return jnp.dot(a, b, preferred_element_type=F32)


def _dot_nt(a, b):
    return lax.dot_general(a, b, (((1,), (1,)), ((), ())), preferred_element_type=F32)


def _dot_exact(a, b):
    return jnp.dot(a, b, preferred_element_type=F32, precision=lax.Precision.HIGHEST)


def _sort_key(x):
    bits = lax.bitcast_convert_type(x + 0.0, I32)
    return bits ^ ((bits >> 31) & 0x7FFFFFFF)


def _ffn_kernel(x_ref, g_ref, w1_ref, w3_ref, w2_ref, gf_ref, o_ref, h_scr, *, final_norm):
    f = pl.program_id(1)

    @pl.when(f == 0)
    def _():
        h_scr[...] = _rms(x_ref[...], g_ref[...]).astype(BF16)
        o_ref[...] = jnp.zeros_like(o_ref)

    h = h_scr[...]
    a = _dot(h, w1_ref[...])
    b = _dot(h, w3_ref[...])
    u = (a * jax.nn.sigmoid(a) * b).astype(BF16)
    o_ref[...] += _dot(u, w2_ref[...])

    @pl.when(f == pl.num_programs(1) - 1)
    def _():
        y = x_ref[...] + 0.5 * o_ref[...]
        if final_norm:
            y = _rms(y, gf_ref[...])
        o_ref[...] = y


def _ffn(x, g, w1, w3, w2, gf, *, tm, tf, final_norm):
    t = x.shape[0]
    return pl.pallas_call(
        functools.partial(_ffn_kernel, final_norm=final_norm),
        grid=(t // tm, D_FF // tf),
        in_specs=[
            pl.BlockSpec((tm, D_MODEL), lambda i, f: (i, 0)),
            pl.BlockSpec((1, D_MODEL), lambda i, f: (0, 0)),
            pl.BlockSpec((D_MODEL, tf), lambda i, f: (0, f)),
            pl.BlockSpec((D_MODEL, tf), lambda i, f: (0, f)),
            pl.BlockSpec((tf, D_MODEL), lambda i, f: (f, 0)),
            pl.BlockSpec((1, D_MODEL), lambda i, f: (0, 0)),
        ],
        out_specs=pl.BlockSpec((tm, D_MODEL), lambda i, f: (i, 0)),
        out_shape=jax.ShapeDtypeStruct((t, D_MODEL), F32),
        scratch_shapes=[pltpu.VMEM((tm, D_MODEL), BF16)],
        compiler_params=_params(("parallel", "arbitrary")),
        name="ffn",
    )(x, g, w1, w3, w2, gf)


def _in_proj_kernel(x_ref, g_ref, w_ref, o_ref, h_scr):
    @pl.when(pl.program_id(1) == 0)
    def _():
        h_scr[...] = _rms(x_ref[...], g_ref[...]).astype(BF16)

    o_ref[...] = _dot(h_scr[...], w_ref[...])


def _in_proj(x, g, w, *, tm):
    t = x.shape[0]
    return pl.pallas_call(
        _in_proj_kernel,
        grid=(t // tm, N_IN // IN_TILE),
        in_specs=[
            pl.BlockSpec((tm, D_MODEL), lambda i, j: (i, 0)),
            pl.BlockSpec((1, D_MODEL), lambda i, j: (0, 0)),
            pl.BlockSpec((D_MODEL, IN_TILE), lambda i, j: (0, j)),
        ],
        out_specs=pl.BlockSpec((tm, IN_TILE), lambda i, j: (i, j)),
        out_shape=jax.ShapeDtypeStruct((t, N_IN), F32),
        scratch_shapes=[pltpu.VMEM((tm, D_MODEL), BF16)],
        compiler_params=_params(("parallel", "arbitrary")),
        name="in_proj",
    )(x, g, w)


def _out_proj_kernel(x_ref, a_ref, s_ref, wa_ref, ws_ref, o_ref):
    acc = _dot(a_ref[...].astype(BF16), wa_ref[...])
    acc += _dot(s_ref[...].astype(BF16), ws_ref[...])
    o_ref[...] = x_ref[...] + acc


def _out_proj(x, a, s, wa, ws, *, tm):
    t = x.shape[0]
    return pl.pallas_call(
        _out_proj_kernel,
        grid=(t // tm,),
        in_specs=[
            pl.BlockSpec((tm, D_MODEL), lambda i: (i, 0)),
            pl.BlockSpec((tm, D_ATTN), lambda i: (i, 0)),
            pl.BlockSpec((tm, D_SSD), lambda i: (i, 0)),
            pl.BlockSpec((D_ATTN, D_MODEL), lambda i: (0, 0)),
            pl.BlockSpec((D_SSD, D_MODEL), lambda i: (0, 0)),
        ],
        out_specs=pl.BlockSpec((tm, D_MODEL), lambda i: (i, 0)),
        out_shape=jax.ShapeDtypeStruct((t, D_MODEL), F32),
        compiler_params=_params(("parallel",)),
        name="out_proj",
    )(x, a, s, wa, ws)


NEAR = 2 * LANES


def _bias_kernel(rb_ref, o_ref):
    r = lax.broadcasted_iota(I32, (LANES, NEAR), 0)
    c = lax.broadcasted_iota(I32, (LANES, NEAR), 1)
    n = jnp.maximum(r + LANES - c, 0)
    max_exact = N_BUCKETS // 2
    nf = jnp.maximum(n, 1).astype(F32)
    large = max_exact + (jnp.log(nf / max_exact) / math.log(MAX_DISTANCE / max_exact)
                         * (N_BUCKETS - max_exact)).astype(I32)
    bucket = jnp.where(n < max_exact, n, jnp.minimum(large, N_BUCKETS - 1))
    for h in range(N_HEADS):
        acc = jnp.zeros((LANES, NEAR), F32)
        for b in range(N_BUCKETS):
            acc = jnp.where(bucket == b, rb_ref[b, h], acc)
        o_ref[h] = acc


def _bias_table(rel_bias):
    return pl.pallas_call(
        _bias_kernel,
        in_specs=[pl.BlockSpec(memory_space=pltpu.SMEM)],
        out_specs=pl.BlockSpec(memory_space=pltpu.VMEM),
        out_shape=jax.ShapeDtypeStruct((N_HEADS, LANES, NEAR), F32),
        name="bias_table",
    )(rel_bias)


def _count(mask, axis=1):
    x = mask.astype(F32)
    if axis == 0:
        x = jnp.sum(x.reshape(x.shape[0] // 64, 64, x.shape[1]), axis=0)
    return jnp.sum(x, axis=axis, keepdims=True)


def _kth_largest(count_ge, shape):
    t = jnp.full(shape, INT_MIN, I32)
    t = jnp.where(count_ge(jnp.zeros(shape, I32)) >= TOPK, 0, t)

    def body(it, t):
        cand = t + lax.shift_left(jnp.int32(1), 30 - it)
        return jnp.where(count_ge(cand) >= TOPK, cand, t)

    return lax.fori_loop(0, 31, body, t)


def _tie_limit(count_eq_below, need, shape, idx_bits):
    def body(it, m):
        cand = m + lax.shift_left(jnp.int32(1), idx_bits - 1 - it)
        return jnp.where(count_eq_below(cand) < need, cand, m)

    return lax.fori_loop(0, idx_bits, body, jnp.zeros(shape, I32))


CAUSAL_VARIANTS = 4
NEAR_TILES = 5


def _dsa_prompt_block(width, i, q_ref, qi_ref, k_ref, v_ref, sm_ref, bt_ref, o_ref, sk_scr, neg_scr):
    tq = LANES
    q0 = pl.multiple_of(i * LANES, LANES)
    shape = (1, tq)

    ki = sm_ref[0:width, SM_KI:SM_KI + IDX_DIM].astype(BF16)
    wi_t = sm_ref[pl.ds(q0, tq), :].T[SM_WI:SM_WI + IDX_HEADS, :] * (IDX_HEADS * IDX_DIM) ** -0.5
    score = jnp.zeros((width, tq), F32)
    for h in range(IDX_HEADS):
        qh = qi_ref[:, h * IDX_DIM:(h + 1) * IDX_DIM].astype(BF16)
        score = score + wi_t[h:h + 1, :] * jnp.maximum(_dot_nt(ki, qh), 0.0)

    key = lax.broadcasted_iota(I32, (width, tq), 0)
    pos = lax.broadcasted_iota(I32, (width, tq), 1) + q0
    sk_scr[0:width, :] = jnp.where(key <= pos, _sort_key(score), INT_MIN)

    def sk():
        return sk_scr[0:width, :]

    thr = _kth_largest(lambda c: _count(sk() >= c, 0), shape)
    excess = (_count(sk() >= thr, 0) > TOPK) & (thr > INT_MIN)
    any_excess = jnp.max(excess.astype(F32)) > 0.0

    @pl.when(jnp.logical_not(any_excess))
    def _():
        neg_scr[0:width, :] = jnp.where(sk() >= jnp.maximum(thr, INT_MIN + 1), 0.0, -jnp.inf)

    @pl.when(any_excess)
    def _():
        need = TOPK - _count(sk() > thr, 0)
        lim = _tie_limit(lambda m: _count((sk() == thr) & (key < m), 0), need, shape, (width - 1).bit_length())
        take = (sk() > thr) | ((sk() == thr) & (key <= lim) & (sk() > INT_MIN))
        neg_scr[0:width, :] = jnp.where(take, 0.0, -jnp.inf)

    neg = neg_scr[0:width, :].T
    neg = jnp.concatenate([neg] * GQA_GROUP, axis=0)

    far_w = max(width - NEAR_TILES * LANES, 0)
    near_w = width - far_w
    tile = lax.broadcasted_iota(I32, (tq, near_w), 1) // LANES + far_w // LANES
    scale = HEAD_DIM ** -0.5
    for g in range(N_KV_HEADS):
        heads = range(g * GQA_GROUP, (g + 1) * GQA_GROUP)
        kg = k_ref[0:width, g * HEAD_DIM:(g + 1) * HEAD_DIM].astype(BF16)
        vg = v_ref[0:width, g * HEAD_DIM:(g + 1) * HEAD_DIM].astype(BF16)
        qg = jnp.concatenate([q_ref[:, h * HEAD_DIM:(h + 1) * HEAD_DIM] for h in heads], axis=0).astype(BF16)
        far = jnp.concatenate([jnp.broadcast_to(bt_ref[h, 0:1, 0:1], (tq, 1)) for h in heads], axis=0)
        near = jnp.concatenate([
            jnp.where(tile == i, jnp.tile(bt_ref[h, :, LANES:NEAR], (1, near_w // LANES)),
                      jnp.where(tile == i - 1, jnp.tile(bt_ref[h, :, 0:LANES], (1, near_w // LANES)),
                                bt_ref[h, 0:1, 0:1]))
            for h in heads], axis=0)
        qk = _dot_nt(qg, kg) * scale
        lg = jnp.concatenate([qk[:, :far_w] + far, qk[:, far_w:] + near], axis=1) if far_w else qk + near
        lg = lg + neg
        m = jnp.max(lg, axis=1, keepdims=True)
        p = jnp.exp(lg - m)
        l = jnp.sum(p, axis=1, keepdims=True)
        out = _dot(p.astype(BF16), vg) / l
        for n, h in enumerate(heads):
            o_ref[:, h * HEAD_DIM:(h + 1) * HEAD_DIM] = out[n * tq:(n + 1) * tq, :]


def _dsa_prompt_kernel(*refs, seq):
    i = pl.program_id(1)
    per = seq // LANES // CAUSAL_VARIANTS
    for v in range(CAUSAL_VARIANTS):
        @pl.when(i // per == v)
        def _(v=v):
            _dsa_prompt_block((v + 1) * per * LANES, i, *refs)


def _dsa_prompt(proj, btab, *, batch, seq):
    nq = seq // LANES
    return pl.pallas_call(
        functools.partial(_dsa_prompt_kernel, seq=seq),
        grid=(batch, nq),
        in_specs=[
            pl.BlockSpec((LANES, D_ATTN), lambda b, i: (b * nq + i, COL_Q // D_ATTN)),
            pl.BlockSpec((LANES, IDX_HEADS * IDX_DIM), lambda b, i: (b * nq + i, COL_QI // (IDX_HEADS * IDX_DIM))),
            pl.BlockSpec((seq, D_KV), lambda b, i: (b, COL_K // D_KV)),
            pl.BlockSpec((seq, D_KV), lambda b, i: (b, COL_V // D_KV)),
            pl.BlockSpec((seq, LANES), lambda b, i: (b, COL_SMALL // LANES)),
            pl.BlockSpec((N_HEADS, LANES, NEAR), lambda b, i: (0, 0, 0)),
        ],
        out_specs=pl.BlockSpec((LANES, D_ATTN), lambda b, i: (b * nq + i, 0)),
        out_shape=jax.ShapeDtypeStruct((batch * seq, D_ATTN), F32),
        scratch_shapes=[pltpu.VMEM((seq, LANES), I32), pltpu.VMEM((seq, LANES), F32)],
        compiler_params=_params(("parallel", "parallel")),
        name="dsa_prompt",
    )(proj, proj, proj, proj, proj, btab)


SCORE_PAGES = 32
ATTN_PAGES = 8


def _sample_queries(qi_ref):
    return jnp.concatenate(
        [qi_ref[:, h * IDX_DIM:(h + 1) * IDX_DIM] for h in range(IDX_HEADS)], axis=0).astype(BF16)


def _sample_scores(rel, wi, nq):
    sc = jnp.zeros((nq, rel.shape[1]), F32)
    for h in range(IDX_HEADS):
        sc = sc + wi[:, h:h + 1] * jnp.maximum(rel[h * nq:(h + 1) * nq, :], 0.0)
    return sc


def _sample_score_kernel(pt_ref, qi_ref, sm_ref, *refs, nq):
    pages = refs[:SCORE_PAGES]
    past_ref, new_ref, kbuf = refs[SCORE_PAGES:]
    s = pl.program_id(1)
    qx = _sample_queries(qi_ref)
    wi = sm_ref[:, SM_WI:SM_WI + IDX_HEADS] * (IDX_HEADS * IDX_DIM) ** -0.5
    for r in range(SCORE_PAGES):
        kbuf[:, r * PAGE_SIZE:(r + 1) * PAGE_SIZE] = pages[r][...].astype(BF16)
    past_ref[...] = _sort_key(_sample_scores(_dot(qx, kbuf[...]), wi, nq))

    @pl.when(s == pl.num_programs(1) - 1)
    def _():
        ki_new = jnp.concatenate(
            [sm_ref[:, SM_KI:SM_KI + IDX_DIM], jnp.zeros((LANES - nq, IDX_DIM), F32)], axis=0).astype(BF16)
        sc = _sample_scores(_dot_nt(qx, ki_new), wi, nq)
        j = lax.broadcasted_iota(I32, (nq, LANES), 1)
        t = lax.broadcasted_iota(I32, (nq, LANES), 0)
        new_ref[...] = jnp.where(j <= t, _sort_key(sc), INT_MIN)


def _sample_score(page_table, proj_s, pool_kidx_t, *, batch, nq, n_pages):
    steps = n_pages // SCORE_PAGES
    page_specs = [
        pl.BlockSpec((None, IDX_DIM, PAGE_SIZE), functools.partial(
            lambda b, s, pt, r: (pt[b, s * SCORE_PAGES + r], 0, 0), r=r))
        for r in range(SCORE_PAGES)
    ]
    grid_spec = pltpu.PrefetchScalarGridSpec(
        num_scalar_prefetch=1,
        grid=(batch, steps),
        in_specs=[
            pl.BlockSpec((nq, IDX_HEADS * IDX_DIM), lambda b, s, pt: (b, COL_QI // (IDX_HEADS * IDX_DIM))),
            pl.BlockSpec((nq, LANES), lambda b, s, pt: (b, COL_SMALL // LANES)),
        ] + page_specs,
        out_specs=[
            pl.BlockSpec((None, nq, SCORE_PAGES * PAGE_SIZE), lambda b, s, pt: (b, 0, s)),
            pl.BlockSpec((None, nq, LANES), lambda b, s, pt: (b, 0, 0)),
        ],
        scratch_shapes=[pltpu.VMEM((IDX_DIM, SCORE_PAGES * PAGE_SIZE), BF16)],
    )
    return pl.pallas_call(
        functools.partial(_sample_score_kernel, nq=nq),
        grid_spec=grid_spec,
        out_shape=[
            jax.ShapeDtypeStruct((batch, nq, n_pages * PAGE_SIZE), I32),
            jax.ShapeDtypeStruct((batch, nq, LANES), I32),
        ],
        compiler_params=_params(("parallel", "arbitrary")),
        name="sample_score",
    )(page_table, proj_s, proj_s, *([pool_kidx_t] * SCORE_PAGES))


def _sample_select_kernel(past_ref, new_ref, thr_ref, lim_ref, *, past):
    rows = past_ref.shape[0]
    colp = lax.broadcasted_iota(I32, (rows, past), 1)
    coln = lax.broadcasted_iota(I32, (rows, LANES), 1) + past

    def count_ge(c):
        return _count(past_ref[...] >= c) + _count(new_ref[...] >= c)

    thr = _kth_largest(count_ge, (rows, 1))
    thr_ref[...] = jnp.broadcast_to(thr, (rows, LANES))
    lim_ref[...] = jnp.full((rows, LANES), 2 ** 31 - 1, I32)
    any_excess = jnp.max((count_ge(thr) > TOPK).astype(F32)) > 0.0

    @pl.when(any_excess)
    def _():
        need = TOPK - (_count(past_ref[...] > thr) + _count(new_ref[...] > thr))

        def count_eq_below(m):
            return (_count((past_ref[...] == thr) & (colp < m))
                    + _count((new_ref[...] == thr) & (coln < m)))

        lim = _tie_limit(count_eq_below, need, (rows, 1), (past + LANES - 1).bit_length())
        lim_ref[...] = jnp.broadcast_to(lim, (rows, LANES))


def _sample_select(sk_past, sk_new, *, rows_per_step):
    rows, past = sk_past.shape
    return pl.pallas_call(
        functools.partial(_sample_select_kernel, past=past),
        grid=(rows // rows_per_step,),
        in_specs=[
            pl.BlockSpec((rows_per_step, past), lambda i: (i, 0)),
            pl.BlockSpec((rows_per_step, LANES), lambda i: (i, 0)),
        ],
        out_specs=[
            pl.BlockSpec((rows_per_step, LANES), lambda i: (i, 0)),
            pl.BlockSpec((rows_per_step, LANES), lambda i: (i, 0)),
        ],
        out_shape=[jax.ShapeDtypeStruct((rows, LANES), I32)] * 2,
        compiler_params=_params(("parallel",)),
        name="sample_select",
    )(sk_past, sk_new)


def _sample_attn_kernel(pt_ref, q_ref, kn_ref, vn_ref, skp_ref, skn_ref, thr_ref, lim_ref, bt_ref, *refs,
                        nq, past):
    kpages = refs[:ATTN_PAGES]
    vpages = refs[ATTN_PAGES:2 * ATTN_PAGES]
    o_ref, kbuf, vbuf, m_scr, l_scr, acc_scr = refs[2 * ATTN_PAGES:]
    s = pl.program_id(1)
    last = pl.num_programs(1) - 1
    width = ATTN_PAGES * PAGE_SIZE
    grows = GQA_GROUP * nq
    scale = HEAD_DIM ** -0.5

    @pl.when(s == 0)
    def _():
        m_scr[...] = jnp.full_like(m_scr, NEG_BIG)
        l_scr[...] = jnp.zeros_like(l_scr)
        acc_scr[...] = jnp.zeros_like(acc_scr)

    thr = thr_ref[:, 0:1]
    lim = lim_ref[:, 0:1]

    def select(sk, idx):
        sel = (sk > thr) | ((sk == thr) & (idx <= lim))
        return jnp.concatenate([sel] * GQA_GROUP, axis=0)

    def far_bias(g):
        return jnp.concatenate(
            [jnp.broadcast_to(bt_ref[g * GQA_GROUP + hh, 0:1, 0:1], (nq, 1)) for hh in range(GQA_GROUP)], axis=0)

    def near_bias(g, lo, hi):
        return jnp.concatenate([bt_ref[g * GQA_GROUP + hh, 0:nq, lo:hi] for hh in range(GQA_GROUP)], axis=0)

    def group_queries(g):
        return jnp.concatenate(
            [q_ref[:, (g * GQA_GROUP + hh) * HEAD_DIM:(g * GQA_GROUP + hh + 1) * HEAD_DIM]
             for hh in range(GQA_GROUP)], axis=0).astype(BF16)

    def accumulate(g, lg, sel, vals):
        rs = slice(g * grows, (g + 1) * grows)
        m_old = m_scr[rs, :]
        m_new = jnp.maximum(m_old, jnp.max(jnp.where(sel, lg, NEG_BIG), axis=1, keepdims=True))
        p = jnp.where(sel, jnp.exp(lg - m_new), 0.0)
        alpha = jnp.exp(m_old - m_new)
        l_scr[rs, :] = alpha * l_scr[rs, :] + jnp.sum(p, axis=1, keepdims=True)
        acc_scr[rs, :] = alpha * acc_scr[rs, :] + _dot(p.astype(BF16), vals)
        m_scr[rs, :] = m_new

    for r in range(ATTN_PAGES):
        for g in range(N_KV_HEADS):
            rows = pl.ds(g, PAGE_SIZE, stride=N_KV_HEADS)
            kbuf[g, r * PAGE_SIZE:(r + 1) * PAGE_SIZE, :] = kpages[r][rows, :].astype(BF16)
            vbuf[g, r * PAGE_SIZE:(r + 1) * PAGE_SIZE, :] = vpages[r][rows, :].astype(BF16)

    col = lax.broadcasted_iota(I32, (nq, width), 1)
    sel = select(skp_ref[...], col + s * width)
    colg = lax.broadcasted_iota(I32, (grows, width), 1)
    in_near = (s == last) & (colg >= width - PAGE_SIZE)
    for g in range(N_KV_HEADS):
        near = jnp.tile(near_bias(g, 0, LANES), (1, ATTN_PAGES))
        bias = jnp.where(in_near, near, far_bias(g))
        lg = _dot_nt(group_queries(g), kbuf[g]) * scale + bias
        accumulate(g, lg, sel, vbuf[g])

    @pl.when(s == last)
    def _():
        pad = jnp.zeros((LANES - nq, D_KV), F32)
        kn = jnp.concatenate([kn_ref[...], pad], axis=0).astype(BF16)
        vn = jnp.concatenate([vn_ref[...], pad], axis=0).astype(BF16)
        coln = lax.broadcasted_iota(I32, (nq, LANES), 1) + past
        seln = select(skn_ref[...], coln)
        for g in range(N_KV_HEADS):
            cs = slice(g * HEAD_DIM, (g + 1) * HEAD_DIM)
            lg = _dot_nt(group_queries(g), kn[:, cs]) * scale + near_bias(g, LANES, NEAR)
            accumulate(g, lg, seln, vn[:, cs])
        out = acc_scr[...] / l_scr[...]
        for h in range(N_HEADS):
            o_ref[:, h * HEAD_DIM:(h + 1) * HEAD_DIM] = out[h * nq:(h + 1) * nq, :]


def _sample_attn(page_table, proj_s, sk_past, sk_new, thr, lim, btab, pool_k, pool_v, *, batch, nq, n_pages):
    steps = n_pages // ATTN_PAGES
    width = ATTN_PAGES * PAGE_SIZE
    past = n_pages * PAGE_SIZE

    def page_spec(r):
        return pl.BlockSpec((PAGE_SIZE * N_KV_HEADS, HEAD_DIM), functools.partial(
            lambda b, s, pt, r: (pt[b, s * ATTN_PAGES + r], 0), r=r))

    grid_spec = pltpu.PrefetchScalarGridSpec(
        num_scalar_prefetch=1,
        grid=(batch, steps),
        in_specs=[
            pl.BlockSpec((nq, D_ATTN), lambda b, s, pt: (b, COL_Q // D_ATTN)),
            pl.BlockSpec((nq, D_KV), lambda b, s, pt: (b, COL_K // D_KV)),
            pl.BlockSpec((nq, D_KV), lambda b, s, pt: (b, COL_V // D_KV)),
            pl.BlockSpec((None, nq, width), lambda b, s, pt: (b, 0, s)),
            pl.BlockSpec((None, nq, LANES), lambda b, s, pt: (b, 0, 0)),
            pl.BlockSpec((nq, LANES), lambda b, s, pt: (b, 0)),
            pl.BlockSpec((nq, LANES), lambda b, s, pt: (b, 0)),
            pl.BlockSpec((N_HEADS, LANES, NEAR), lambda b, s, pt: (0, 0, 0)),
        ] + [page_spec(r) for r in range(ATTN_PAGES)] * 2,
        out_specs=pl.BlockSpec((nq, D_ATTN), lambda b, s, pt: (b, 0)),
        scratch_shapes=[
            pltpu.VMEM((N_KV_HEADS, width, HEAD_DIM), BF16),
            pltpu.VMEM((N_KV_HEADS, width, HEAD_DIM), BF16),
            pltpu.VMEM((N_HEADS * nq, 1), F32),
            pltpu.VMEM((N_HEADS * nq, 1), F32),
            pltpu.VMEM((N_HEADS * nq, HEAD_DIM), F32),
        ],
    )
    return pl.pallas_call(
        functools.partial(_sample_attn_kernel, nq=nq, past=past),
        grid_spec=grid_spec,
        out_shape=jax.ShapeDtypeStruct((batch * nq, D_ATTN), F32),
        compiler_params=_params(("parallel", "arbitrary")),
        name="sample_attn",
    )(page_table, proj_s, proj_s, proj_s, sk_past, sk_new, thr, lim, btab,
      *([pool_k] * ATTN_PAGES), *([pool_v] * ATTN_PAGES))


TAIL = 8


def _ssd_kernel(xbc_ref, z_ref, sm_ref, cprev_ref, hprev_ref, cw_ref, cb_ref, alog_ref, dtb_ref, dvec_ref,
                g_ref, y_ref, cnew_ref, hnew_ref, xpad_scr, sm_scr, h_scr, yd_scr, *, rows):
    c = pl.program_id(1)
    q = CHUNK
    keep = CONV_WIDTH - 1
    heads_per_group = SSD_HEADS // SSD_GROUPS

    @pl.when(c == 0)
    def _():
        xpad_scr[...] = jnp.zeros_like(xpad_scr)
        sm_scr[...] = jnp.zeros_like(sm_scr)
        xpad_scr[TAIL - keep:TAIL, :] = cprev_ref[0]
        h_scr[...] = hprev_ref[0]

    xpad_scr[TAIL:TAIL + rows, :] = xbc_ref[...]
    sm_scr[0:rows, :] = sm_ref[...]
    conv = sum(xpad_scr[TAIL - keep + k:TAIL - keep + k + q, :] * cw_ref[k:k + 1, :]
               for k in range(CONV_WIDTH)) + cb_ref[...]
    new_tail = xpad_scr[rows + TAIL - keep:rows + TAIL, :]
    cnew_ref[0] = new_tail
    xpad_scr[TAIL - keep:TAIL, :] = new_tail

    xc = conv * jax.nn.sigmoid(conv)
    xs = xc[:, :D_SSD]
    bm = xc[:, D_SSD:D_SSD + SSD_GROUPS * D_STATE].astype(BF16)
    cm = xc[:, D_SSD + SSD_GROUPS * D_STATE:].astype(BF16)

    ri = lax.broadcasted_iota(I32, (q, q), 0)
    ci = lax.broadcasted_iota(I32, (q, q), 1)
    causal = ri >= ci
    x = sm_scr[:, SM_DT:SM_DT + SSD_HEADS] + dtb_ref[...]
    dt = jnp.maximum(x, 0.0) + jnp.log1p(jnp.exp(-jnp.abs(x)))
    dt = jnp.where(lax.broadcasted_iota(I32, (q, SSD_HEADS), 0) < rows, dt, 0.0)
    a = -jnp.exp(alog_ref[...])
    acum = _dot_exact(causal.astype(F32), dt * a)
    acum_t = jnp.concatenate([acum, jnp.zeros((q, LANES - SSD_HEADS), F32)], axis=1).T

    expand = (lax.broadcasted_iota(I32, (SSD_HEADS, D_SSD), 1) // SSD_HEAD_DIM
              == lax.broadcasted_iota(I32, (SSD_HEADS, D_SSD), 0)).astype(F32)
    expand_t = (lax.broadcasted_iota(I32, (D_SSD, SSD_HEADS), 0) // SSD_HEAD_DIM
                == lax.broadcasted_iota(I32, (D_SSD, SSD_HEADS), 1)).astype(F32)
    acum_x = _dot_exact(acum, expand)
    xd = xs * _dot_exact(dt, expand)
    xdw_t = (xd * jnp.exp(acum_x[q - 1:q, :] - acum_x)).T.astype(BF16)
    xd = xd.astype(BF16)
    state_decay = jnp.exp(_dot_exact(expand_t, jnp.broadcast_to(acum_t[0:SSD_HEADS, q - 1:q], (SSD_HEADS, LANES))))

    for g in range(SSD_GROUPS):
        ns = slice(g * D_STATE, (g + 1) * D_STATE)
        hs = slice(g * heads_per_group * SSD_HEAD_DIM, (g + 1) * heads_per_group * SSD_HEAD_DIM)
        cb = _dot_nt(cm[:, ns], bm[:, ns])
        for e in range(heads_per_group):
            h = g * heads_per_group + e
            ps = slice(h * SSD_HEAD_DIM, (h + 1) * SSD_HEAD_DIM)
            seg = acum[:, h:h + 1] - acum_t[h:h + 1, :]
            lmat = jnp.exp(jnp.where(causal, seg, -jnp.inf))
            yd_scr[:, ps] = _dot((cb * lmat).astype(BF16), xd[:, ps])
        h_old = h_scr[hs, :]
        yd_scr[:, hs] += _dot_nt(cm[:, ns], h_old.astype(BF16)) * jnp.exp(acum_x[:, hs])
        h_scr[hs, :] = h_old * state_decay[hs, :] + _dot(xdw_t[hs, :], bm[:, ns])

    hnew_ref[0] = h_scr[...]
    y = yd_scr[0:rows, :] + dvec_ref[...] * xs[0:rows, :]
    zz = z_ref[...]
    y = y * (zz * jax.nn.sigmoid(zz))
    width = D_SSD // SSD_GROUPS
    for g in range(SSD_GROUPS):
        cs = slice(g * width, (g + 1) * width)
        y_ref[:, cs] = _rms(y[:, cs], g_ref[:, cs])


def _ssd(proj, conv_prev, ssm_prev, conv_w, conv_b, a_log, dt_bias, d_vec, g_ssd, *, batch, seq):
    rows = min(CHUNK, seq)
    nc = seq // rows
    full = lambda *shape: pl.BlockSpec(shape, lambda b, c: (0,) * len(shape))
    return pl.pallas_call(
        functools.partial(_ssd_kernel, rows=rows),
        grid=(batch, nc),
        in_specs=[
            pl.BlockSpec((rows, CONV_DIM), lambda b, c: (b * nc + c, COL_XBC // CONV_DIM)),
            pl.BlockSpec((rows, D_SSD), lambda b, c: (b * nc + c, COL_Z // D_SSD)),
            pl.BlockSpec((rows, LANES), lambda b, c: (b * nc + c, COL_SMALL // LANES)),
            pl.BlockSpec((1, CONV_WIDTH - 1, CONV_DIM), lambda b, c: (b, 0, 0)),
            pl.BlockSpec((1, D_SSD, D_STATE), lambda b, c: (b, 0, 0)),
            full(CONV_WIDTH, CONV_DIM), full(1, CONV_DIM), full(1, SSD_HEADS), full(1, SSD_HEADS),
            full(1, D_SSD), full(1, D_SSD),
        ],
        out_specs=[
            pl.BlockSpec((rows, D_SSD), lambda b, c: (b * nc + c, 0)),
            pl.BlockSpec((1, CONV_WIDTH - 1, CONV_DIM), lambda b, c: (b, 0, 0)),
            pl.BlockSpec((1, D_SSD, D_STATE), lambda b, c: (b, 0, 0)),
        ],
        out_shape=[
            jax.ShapeDtypeStruct((batch * seq, D_SSD), F32),
            jax.ShapeDtypeStruct((batch, CONV_WIDTH - 1, CONV_DIM), F32),
            jax.ShapeDtypeStruct((batch, D_SSD, D_STATE), F32),
        ],
        scratch_shapes=[
            pltpu.VMEM((CHUNK + TAIL, CONV_DIM), F32),
            pltpu.VMEM((CHUNK, LANES), F32),
            pltpu.VMEM((D_SSD, D_STATE), F32),
            pltpu.VMEM((CHUNK, D_SSD), F32),
        ],
        compiler_params=_params(("parallel", "arbitrary")),
        name="ssd",
    )(proj, proj, proj, conv_prev, ssm_prev, conv_w, conv_b, a_log, dt_bias, d_vec, g_ssd)


def _reorder_w_in(w_in):
    offs = [0]
    for s in IN_SPLITS:
        offs.append(offs[-1] + s)
    q, k, v, qi, ki, wi, z, xbc, dtr = [w_in[:, offs[n]:offs[n + 1]] for n in range(len(IN_SPLITS))]
    cols = jnp.concatenate([q, qi, z, xbc, k, v, ki, wi, dtr], axis=1)
    return jnp.pad(cols, ((0, 0), (0, N_IN - cols.shape[1]))).astype(BF16)


def kernel(x_prompt, x_sample, cache_k, cache_v, cache_kidx, state_conv, state_ssm, page_table, rel_bias, g_ffn1, w1_ffn1, w3_ffn1, w2_ffn1, g_mix, w_in, conv_w, conv_b, a_log, dt_bias, d_skip, g_ssd, w_out, g_ffn2, w1_ffn2, w3_ffn2, w2_ffn2, g_final):
    depth = w_in.shape[0]
    assert depth == 1
    l = 0
    bp, seq, _ = x_prompt.shape
    bs, nq, _ = x_sample.shape
    n_pages = page_table.shape[1]
    row = lambda t: t.reshape(1, -1)

    w1a, w3a, w2a = w1_ffn1[l].astype(BF16), w3_ffn1[l].astype(BF16), w2_ffn1[l].astype(BF16)
    w1b, w3b, w2b = w1_ffn2[l].astype(BF16), w3_ffn2[l].astype(BF16), w2_ffn2[l].astype(BF16)
    w_in_r = _reorder_w_in(w_in[l])
    w_out_a = w_out[l, :D_ATTN].astype(BF16)
    w_out_s = w_out[l, D_ATTN:].astype(BF16)
    d_vec = jnp.repeat(d_skip[l], SSD_HEAD_DIM).reshape(1, D_SSD)
    btab = _bias_table(rel_bias)
    gf = row(g_final)

    def trunk(x, tm, attend, conv_prev, ssm_prev, batch, length):
        x = _ffn(x, row(g_ffn1[l]), w1a, w3a, w2a, gf, tm=tm, tf=512, final_norm=False)
        proj = _in_proj(x, row(g_mix[l]), w_in_r, tm=tm)
        a_out = attend(proj)
        s_out, conv_new, ssm_new = _ssd(
            proj, conv_prev, ssm_prev.reshape(batch, D_SSD, D_STATE), conv_w[l], row(conv_b[l]), row(a_log[l]),
            row(dt_bias[l]), d_vec, row(g_ssd[l]), batch=batch, seq=length)
        x = _out_proj(x, a_out, s_out, w_out_a, w_out_s, tm=min(tm, 256))
        y = _ffn(x, row(g_ffn2[l]), w1b, w3b, w2b, gf, tm=tm, tf=512, final_norm=True)
        k = proj[:, COL_K:COL_K + D_KV].reshape(1, batch, length, N_KV_HEADS, HEAD_DIM)
        v = proj[:, COL_V:COL_V + D_KV].reshape(1, batch, length, N_KV_HEADS, HEAD_DIM)
        ki = proj[:, COL_SMALL + SM_KI:COL_SMALL + SM_KI + IDX_DIM].reshape(1, batch, length, IDX_DIM)
        ssm_new = ssm_new.reshape(1, batch, SSD_HEADS, SSD_HEAD_DIM, D_STATE)
        return y.reshape(batch, length, D_MODEL), k, v, ki, conv_new[None], ssm_new

    y_p, k_p, v_p, ki_p, conv_p, ssm_p = trunk(
        x_prompt.reshape(bp * seq, D_MODEL), 512,
        functools.partial(_dsa_prompt, btab=btab, batch=bp, seq=seq),
        jnp.zeros((bp, CONV_WIDTH - 1, CONV_DIM), F32), jnp.zeros((bp, SSD_HEADS, SSD_HEAD_DIM, D_STATE), F32),
        bp, seq)

    pool_k = cache_k[l].reshape(-1, HEAD_DIM)
    pool_v = cache_v[l].reshape(-1, HEAD_DIM)
    pool_kidx = jnp.swapaxes(cache_kidx[l], 1, 2)

    def sample_attend(proj):
        sk_past, sk_new = _sample_score(page_table, proj, pool_kidx, batch=bs, nq=nq, n_pages=n_pages)
        thr, lim = _sample_select(sk_past.reshape(bs * nq, -1), sk_new.reshape(bs * nq, LANES),
                                  rows_per_step=128)
        return _sample_attn(page_table, proj, sk_past, sk_new, thr, lim, btab, pool_k, pool_v,
                            batch=bs, nq=nq, n_pages=n_pages)

    y_s, k_s, v_s, ki_s, conv_s, ssm_s = trunk(
        x_sample.reshape(bs * nq, D_MODEL), bs * nq, sample_attend, state_conv[l], state_ssm[l], bs, nq)

    return (y_p, y_s, k_p, v_p, ki_p, conv_p, ssm_p, k_s, v_s, ki_s, conv_s, ssm_s)
```

```python
import functools
import math

import jax
import jax.numpy as jnp
from jax import lax
from jax.experimental import pallas as pl
from jax.experimental.pallas import tpu as pltpu

F32 = jnp.float32
BF16 = jnp.bfloat16
I32 = jnp.int32

D_MODEL = 2048
PAGE_SIZE = 128
N_HEADS = 8
HEAD_DIM = 128
N_KV_HEADS = 2
GQA_GROUP = N_HEADS // N_KV_HEADS
IDX_HEADS = 16
IDX_DIM = 64
TOPK = 256
N_BUCKETS = 32
MAX_DISTANCE = 128
SSD_HEADS = 16
SSD_HEAD_DIM = 64
SSD_GROUPS = 2
D_STATE = 128
CONV_WIDTH = 4
CHUNK = 128
D_ATTN = N_HEADS * HEAD_DIM
D_SSD = SSD_HEADS * SSD_HEAD_DIM
D_KV = N_KV_HEADS * HEAD_DIM
CONV_DIM = D_SSD + 2 * SSD_GROUPS * D_STATE
D_FF = 5632
EPS = 1e-6
IN_SPLITS = (D_ATTN, D_KV, D_KV, IDX_HEADS * IDX_DIM, IDX_DIM, IDX_HEADS, D_SSD, CONV_DIM, SSD_HEADS)

LANES = 128
COL_Q = 0
COL_QI = COL_Q + D_ATTN
COL_Z = COL_QI + IDX_HEADS * IDX_DIM
COL_XBC = COL_Z + D_SSD
COL_K = COL_XBC + CONV_DIM
COL_V = COL_K + D_KV
COL_SMALL = COL_V + D_KV
SM_KI = 0
SM_WI = SM_KI + IDX_DIM
SM_DT = SM_WI + IDX_HEADS
IN_TILE = 768
N_IN = ((COL_SMALL + LANES + IN_TILE - 1) // IN_TILE) * IN_TILE

INT_MIN = -(2 ** 31)
NEG_BIG = -1e30
VMEM_LIMIT = 56 * 1024 * 1024


def _params(sem):
    return pltpu.CompilerParams(dimension_semantics=sem, vmem_limit_bytes=VMEM_LIMIT)


def _rms(x, g):
    return x * lax.rsqrt(jnp.mean(x * x, axis=-1, keepdims=True) + EPS) * g


def _dot(a, b):
    return jnp.dot(a, b, preferred_element_type=F32)


def _dot_nt(a, b):
    return lax.dot_general(a, b, (((1,), (1,)), ((), ())), preferred_element_type=F32)


def _split3(x):
    hi = x.astype(BF16)
    rest = x - hi.astype(F32)
    mid = rest.astype(BF16)
    return hi, mid, (rest - mid.astype(F32)).astype(BF16)


def _select_dot(a, b):
    if a.dtype == BF16:
        return sum(_dot(a, part) for part in _split3(b))
    return sum(_dot(part, b) for part in _split3(a))


def _sort_key(x):
    bits = lax.bitcast_convert_type(x + 0.0, I32)
    return bits ^ ((bits >> 31) & 0x7FFFFFFF)


def _ffn_kernel(x_ref, g_ref, w1_ref, w3_ref, w2_ref, gf_ref, o_ref, h_scr, *, final_norm):
    f = pl.program_id(1)

    @pl.when(f == 0)
    def _():
        h_scr[...] = _rms(x_ref[...], g_ref[...]).astype(BF16)
        o_ref[...] = jnp.zeros_like(o_ref)

    h = h_scr[...]
    a = _dot(h, w1_ref[...])
    b = _dot(h, w3_ref[...])
    u = (a * jax.nn.sigmoid(a) * b).astype(BF16)
    o_ref[...] += _dot(u, w2_ref[...])

    @pl.when(f == pl.num_programs(1) - 1)
    def _():
        y = x_ref[...] + 0.5 * o_ref[...]
        if final_norm:
            y = _rms(y, gf_ref[...])
        o_ref[...] = y


def _ffn(x, g, w1, w3, w2, gf, *, tm, tf, final_norm):
    t = x.shape[0]
    return pl.pallas_call(
        functools.partial(_ffn_kernel, final_norm=final_norm),
        grid=(t // tm, D_FF // tf),
        in_specs=[
            pl.BlockSpec((tm, D_MODEL), lambda i, f: (i, 0)),
            pl.BlockSpec((1, D_MODEL), lambda i, f: (0, 0)),
            pl.BlockSpec((D_MODEL, tf), lambda i, f: (0, f)),
            pl.BlockSpec((D_MODEL, tf), lambda i, f: (0, f)),
            pl.BlockSpec((tf, D_MODEL), lambda i, f: (f, 0)),
            pl.BlockSpec((1, D_MODEL), lambda i, f: (0, 0)),
        ],
        out_specs=pl.BlockSpec((tm, D_MODEL), lambda i, f: (i, 0)),
        out_shape=jax.ShapeDtypeStruct((t, D_MODEL), F32),
        scratch_shapes=[pltpu.VMEM((tm, D_MODEL), BF16)],
        compiler_params=_params(("parallel", "arbitrary")),
        name="ffn",
    )(x, g, w1, w3, w2, gf)


def _in_proj_kernel(x_ref, g_ref, w_ref, o_ref, h_scr):
    @pl.when(pl.program_id(1) == 0)
    def _():
        h_scr[...] = _rms(x_ref[...], g_ref[...]).astype(BF16)

    o_ref[...] = _dot(h_scr[...], w_ref[...])


def _in_proj(x, g, w, *, tm):
    t = x.shape[0]
    return pl.pallas_call(
        _in_proj_kernel,
        grid=(t // tm, N_IN // IN_TILE),
        in_specs=[
            pl.BlockSpec((tm, D_MODEL), lambda i, j: (i, 0)),
            pl.BlockSpec((1, D_MODEL), lambda i, j: (0, 0)),
            pl.BlockSpec((D_MODEL, IN_TILE), lambda i, j: (0, j)),
        ],
        out_specs=pl.BlockSpec((tm, IN_TILE), lambda i, j: (i, j)),
        out_shape=jax.ShapeDtypeStruct((t, N_IN), F32),
        scratch_shapes=[pltpu.VMEM((tm, D_MODEL), BF16)],
        compiler_params=_params(("parallel", "arbitrary")),
        name="in_proj",
    )(x, g, w)


def _out_proj_kernel(x_ref, a_ref, s_ref, wa_ref, ws_ref, o_ref):
    acc = _dot(a_ref[...].astype(BF16), wa_ref[...])
    acc += _dot(s_ref[...].astype(BF16), ws_ref[...])
    o_ref[...] = x_ref[...] + acc


def _out_proj(x, a, s, wa, ws, *, tm):
    t = x.shape[0]
    return pl.pallas_call(
        _out_proj_kernel,
        grid=(t // tm,),
        in_specs=[
            pl.BlockSpec((tm, D_MODEL), lambda i: (i, 0)),
            pl.BlockSpec((tm, D_ATTN), lambda i: (i, 0)),
            pl.BlockSpec((tm, D_SSD), lambda i: (i, 0)),
            pl.BlockSpec((D_ATTN, D_MODEL), lambda i: (0, 0)),
            pl.BlockSpec((D_SSD, D_MODEL), lambda i: (0, 0)),
        ],
        out_specs=pl.BlockSpec((tm, D_MODEL), lambda i: (i, 0)),
        out_shape=jax.ShapeDtypeStruct((t, D_MODEL), F32),
        compiler_params=_params(("parallel",)),
        name="out_proj",
    )(x, a, s, wa, ws)


NEAR = 2 * LANES


def _bias_kernel(rb_ref, o_ref):
    r = lax.broadcasted_iota(I32, (LANES, NEAR), 0)
    c = lax.broadcasted_iota(I32, (LANES, NEAR), 1)
    n = jnp.maximum(r + LANES - c, 0)
    max_exact = N_BUCKETS // 2
    nf = jnp.maximum(n, 1).astype(F32)
    large = max_exact + (jnp.log(nf / max_exact) / math.log(MAX_DISTANCE / max_exact)
                         * (N_BUCKETS - max_exact)).astype(I32)
    bucket = jnp.where(n < max_exact, n, jnp.minimum(large, N_BUCKETS - 1))
    for h in range(N_HEADS):
        acc = jnp.zeros((LANES, NEAR), F32)
        for b in range(N_BUCKETS):
            acc = jnp.where(bucket == b, rb_ref[b, h], acc)
        o_ref[h] = acc


def _bias_table(rel_bias):
    return pl.pallas_call(
        _bias_kernel,
        in_specs=[pl.BlockSpec(memory_space=pltpu.SMEM)],
        out_specs=pl.BlockSpec(memory_space=pltpu.VMEM),
        out_shape=jax.ShapeDtypeStruct((N_HEADS, LANES, NEAR), F32),
        name="bias_table",
    )(rel_bias)


def _count(mask, axis=1):
    x = mask.astype(F32)
    if axis == 0:
        x = jnp.sum(x.reshape(x.shape[0] // 64, 64, x.shape[1]), axis=0)
    return jnp.sum(x, axis=axis, keepdims=True)


def _kth_largest(count_ge, shape):
    t = jnp.full(shape, INT_MIN, I32)
    t = jnp.where(count_ge(jnp.zeros(shape, I32)) >= TOPK, 0, t)

    def body(it, t):
        cand = t + lax.shift_left(jnp.int32(1), 30 - it)
        return jnp.where(count_ge(cand) >= TOPK, cand, t)

    return lax.fori_loop(0, 31, body, t)


def _tie_limit(count_eq_below, need, shape, idx_bits):
    def body(it, m):
        cand = m + lax.shift_left(jnp.int32(1), idx_bits - 1 - it)
        return jnp.where(count_eq_below(cand) < need, cand, m)

    return lax.fori_loop(0, idx_bits, body, jnp.zeros(shape, I32))


CAUSAL_VARIANTS = 8


def _dsa_prompt_block(width, near_tiles, i, q_ref, qi_ref, k_ref, v_ref, sm_ref, bt_ref, o_ref, sk_scr, neg_scr):
    tq = LANES
    q0 = pl.multiple_of(i * LANES, LANES)
    shape = (1, tq)

    ki = sm_ref[0:width, SM_KI:SM_KI + IDX_DIM].astype(BF16)
    wi_t = sm_ref[pl.ds(q0, tq), :].T[SM_WI:SM_WI + IDX_HEADS, :] * (IDX_HEADS * IDX_DIM) ** -0.5
    score = jnp.zeros((width, tq), F32)
    for h in range(IDX_HEADS):
        qh = qi_ref[:, h * IDX_DIM:(h + 1) * IDX_DIM].astype(BF16)
        score = score + wi_t[h:h + 1, :] * jnp.maximum(_dot_nt(ki, qh), 0.0)

    key = lax.broadcasted_iota(I32, (width, tq), 0)
    pos = lax.broadcasted_iota(I32, (width, tq), 1) + q0
    sk_scr[0:width, :] = jnp.where(key <= pos, _sort_key(score), INT_MIN)

    def sk():
        return sk_scr[0:width, :]

    thr = _kth_largest(lambda c: _count(sk() >= c, 0), shape)
    excess = (_count(sk() >= thr, 0) > TOPK) & (thr > INT_MIN)
    any_excess = jnp.max(excess.astype(F32)) > 0.0

    @pl.when(jnp.logical_not(any_excess))
    def _():
        neg_scr[0:width, :] = jnp.where(sk() >= jnp.maximum(thr, INT_MIN + 1), 0.0, -jnp.inf)

    @pl.when(any_excess)
    def _():
        need = TOPK - _count(sk() > thr, 0)
        lim = _tie_limit(lambda m: _count((sk() == thr) & (key < m), 0), need, shape, (width - 1).bit_length())
        take = (sk() > thr) | ((sk() == thr) & (key <= lim) & (sk() > INT_MIN))
        neg_scr[0:width, :] = jnp.where(take, 0.0, -jnp.inf)

    neg = neg_scr[0:width, :].T
    neg = jnp.concatenate([neg] * GQA_GROUP, axis=0)

    far_w = max(width - near_tiles * LANES, 0)
    near_w = width - far_w
    tile = lax.broadcasted_iota(I32, (tq, near_w), 1) // LANES + far_w // LANES
    scale = HEAD_DIM ** -0.5
    for g in range(N_KV_HEADS):
        heads = range(g * GQA_GROUP, (g + 1) * GQA_GROUP)
        kg = k_ref[0:width, g * HEAD_DIM:(g + 1) * HEAD_DIM].astype(BF16)
        vg = v_ref[0:width, g * HEAD_DIM:(g + 1) * HEAD_DIM].astype(BF16)
        qg = jnp.concatenate([q_ref[:, h * HEAD_DIM:(h + 1) * HEAD_DIM] for h in heads], axis=0).astype(BF16)
        far = jnp.concatenate([jnp.broadcast_to(bt_ref[h, 0:1, 0:1], (tq, 1)) for h in heads], axis=0)
        near = jnp.concatenate([
            jnp.where(tile == i, jnp.tile(bt_ref[h, :, LANES:NEAR], (1, near_w // LANES)),
                      jnp.where(tile == i - 1, jnp.tile(bt_ref[h, :, 0:LANES], (1, near_w // LANES)),
                                bt_ref[h, 0:1, 0:1]))
            for h in heads], axis=0)
        qk = _dot_nt(qg, kg) * scale
        lg = jnp.concatenate([qk[:, :far_w] + far, qk[:, far_w:] + near], axis=1) if far_w else qk + near
        lg = lg + neg
        m = jnp.max(lg, axis=1, keepdims=True)
        p = jnp.exp(lg - m)
        l = jnp.sum(p, axis=1, keepdims=True)
        out = _dot(p.astype(BF16), vg) / l
        for n, h in enumerate(heads):
            o_ref[:, h * HEAD_DIM:(h + 1) * HEAD_DIM] = out[n * tq:(n + 1) * tq, :]


def _dsa_prompt_kernel(*refs, seq):
    i = pl.program_id(1)
    per = seq // LANES // CAUSAL_VARIANTS
    for v in range(CAUSAL_VARIANTS):
        @pl.when(i // per == v)
        def _(v=v):
            _dsa_prompt_block((v + 1) * per * LANES, per + 1, i, *refs)


def _dsa_prompt(proj, btab, *, batch, seq):
    nq = seq // LANES
    return pl.pallas_call(
        functools.partial(_dsa_prompt_kernel, seq=seq),
        grid=(batch, nq),
        in_specs=[
            pl.BlockSpec((LANES, D_ATTN), lambda b, i: (b * nq + i, COL_Q // D_ATTN)),
            pl.BlockSpec((LANES, IDX_HEADS * IDX_DIM), lambda b, i: (b * nq + i, COL_QI // (IDX_HEADS * IDX_DIM))),
            pl.BlockSpec((seq, D_KV), lambda b, i: (b, COL_K // D_KV)),
            pl.BlockSpec((seq, D_KV), lambda b, i: (b, COL_V // D_KV)),
            pl.BlockSpec((seq, LANES), lambda b, i: (b, COL_SMALL // LANES)),
            pl.BlockSpec((N_HEADS, LANES, NEAR), lambda b, i: (0, 0, 0)),
        ],
        out_specs=pl.BlockSpec((LANES, D_ATTN), lambda b, i: (b * nq + i, 0)),
        out_shape=jax.ShapeDtypeStruct((batch * seq, D_ATTN), F32),
        scratch_shapes=[pltpu.VMEM((seq, LANES), I32), pltpu.VMEM((seq, LANES), F32)],
        compiler_params=_params(("parallel", "parallel")),
        name="dsa_prompt",
    )(proj, proj, proj, proj, proj, btab)


SCORE_PAGES = 64
ATTN_PAGES = 32


def _sample_queries(qi_ref):
    return jnp.concatenate(
        [qi_ref[:, h * IDX_DIM:(h + 1) * IDX_DIM] for h in range(IDX_HEADS)], axis=0).astype(BF16)


def _sample_scores(rel, wi, nq):
    sc = jnp.zeros((nq, rel.shape[1]), F32)
    for h in range(IDX_HEADS):
        sc = sc + wi[:, h:h + 1] * jnp.maximum(rel[h * nq:(h + 1) * nq, :], 0.0)
    return sc


def _sample_score_kernel(pt_ref, qi_ref, sm_ref, *refs, nq):
    pages = refs[:SCORE_PAGES]
    past_ref, new_ref, kbuf = refs[SCORE_PAGES:]
    s = pl.program_id(1)
    qx = _sample_queries(qi_ref)
    wi = sm_ref[:, SM_WI:SM_WI + IDX_HEADS] * (IDX_HEADS * IDX_DIM) ** -0.5
    for r in range(SCORE_PAGES):
        kbuf[:, r * PAGE_SIZE:(r + 1) * PAGE_SIZE] = pages[r][...].astype(BF16)
    past_ref[...] = _sort_key(_sample_scores(_dot(qx, kbuf[...]), wi, nq))

    @pl.when(s == pl.num_programs(1) - 1)
    def _():
        ki_new = jnp.concatenate(
            [sm_ref[:, SM_KI:SM_KI + IDX_DIM], jnp.zeros((LANES - nq, IDX_DIM), F32)], axis=0).astype(BF16)
        sc = _sample_scores(_dot_nt(qx, ki_new), wi, nq)
        j = lax.broadcasted_iota(I32, (nq, LANES), 1)
        t = lax.broadcasted_iota(I32, (nq, LANES), 0)
        new_ref[...] = jnp.where(j <= t, _sort_key(sc), INT_MIN)


def _sample_score(page_table, proj_s, pool_kidx_t, *, batch, nq, n_pages):
    steps = n_pages // SCORE_PAGES
    page_specs = [
        pl.BlockSpec((None, IDX_DIM, PAGE_SIZE), functools.partial(
            lambda b, s, pt, r: (pt[b, s * SCORE_PAGES + r], 0, 0), r=r))
        for r in range(SCORE_PAGES)
    ]
    grid_spec = pltpu.PrefetchScalarGridSpec(
        num_scalar_prefetch=1,
        grid=(batch, steps),
        in_specs=[
            pl.BlockSpec((nq, IDX_HEADS * IDX_DIM), lambda b, s, pt: (b, COL_QI // (IDX_HEADS * IDX_DIM))),
            pl.BlockSpec((nq, LANES), lambda b, s, pt: (b, COL_SMALL // LANES)),
        ] + page_specs,
        out_specs=[
            pl.BlockSpec((None, nq, SCORE_PAGES * PAGE_SIZE), lambda b, s, pt: (b, 0, s)),
            pl.BlockSpec((None, nq, LANES), lambda b, s, pt: (b, 0, 0)),
        ],
        scratch_shapes=[pltpu.VMEM((IDX_DIM, SCORE_PAGES * PAGE_SIZE), BF16)],
    )
    return pl.pallas_call(
        functools.partial(_sample_score_kernel, nq=nq),
        grid_spec=grid_spec,
        out_shape=[
            jax.ShapeDtypeStruct((batch, nq, n_pages * PAGE_SIZE), I32),
            jax.ShapeDtypeStruct((batch, nq, LANES), I32),
        ],
        compiler_params=_params(("parallel", "arbitrary")),
        name="sample_score",
    )(page_table, proj_s, proj_s, *([pool_kidx_t] * SCORE_PAGES))


def _sample_select_kernel(past_ref, new_ref, thr_ref, lim_ref, *, past):
    rows = past_ref.shape[0]
    colp = lax.broadcasted_iota(I32, (rows, past), 1)
    coln = lax.broadcasted_iota(I32, (rows, LANES), 1) + past

    def count_ge(c):
        return _count(past_ref[...] >= c) + _count(new_ref[...] >= c)

    thr = _kth_largest(count_ge, (rows, 1))
    thr_ref[...] = jnp.broadcast_to(thr, (rows, LANES))
    lim_ref[...] = jnp.full((rows, LANES), 2 ** 31 - 1, I32)
    any_excess = jnp.max((count_ge(thr) > TOPK).astype(F32)) > 0.0

    @pl.when(any_excess)
    def _():
        need = TOPK - (_count(past_ref[...] > thr) + _count(new_ref[...] > thr))

        def count_eq_below(m):
            return (_count((past_ref[...] == thr) & (colp < m))
                    + _count((new_ref[...] == thr) & (coln < m)))

        lim = _tie_limit(count_eq_below, need, (rows, 1), (past + LANES - 1).bit_length())
        lim_ref[...] = jnp.broadcast_to(lim, (rows, LANES))


def _sample_select(sk_past, sk_new, *, rows_per_step):
    rows, past = sk_past.shape
    return pl.pallas_call(
        functools.partial(_sample_select_kernel, past=past),
        grid=(rows // rows_per_step,),
        in_specs=[
            pl.BlockSpec((rows_per_step, past), lambda i: (i, 0)),
            pl.BlockSpec((rows_per_step, LANES), lambda i: (i, 0)),
        ],
        out_specs=[
            pl.BlockSpec((rows_per_step, LANES), lambda i: (i, 0)),
            pl.BlockSpec((rows_per_step, LANES), lambda i: (i, 0)),
        ],
        out_shape=[jax.ShapeDtypeStruct((rows, LANES), I32)] * 2,
        compiler_params=_params(("parallel",)),
        name="sample_select",
    )(sk_past, sk_new)


def _sample_attn_kernel(pt_ref, q_ref, kn_ref, vn_ref, skp_ref, skn_ref, thr_ref, lim_ref, bt_ref, *refs,
                        nq, past):
    kpages = refs[:ATTN_PAGES]
    vpages = refs[ATTN_PAGES:2 * ATTN_PAGES]
    o_ref, kbuf, vbuf, m_scr, l_scr, acc_scr = refs[2 * ATTN_PAGES:]
    s = pl.program_id(1)
    last = pl.num_programs(1) - 1
    width = ATTN_PAGES * PAGE_SIZE
    grows = GQA_GROUP * nq
    scale = HEAD_DIM ** -0.5

    @pl.when(s == 0)
    def _():
        m_scr[...] = jnp.full_like(m_scr, NEG_BIG)
        l_scr[...] = jnp.zeros_like(l_scr)
        acc_scr[...] = jnp.zeros_like(acc_scr)

    thr = thr_ref[:, 0:1]
    lim = lim_ref[:, 0:1]

    def select(sk, idx):
        sel = (sk > thr) | ((sk == thr) & (idx <= lim))
        return jnp.concatenate([sel] * GQA_GROUP, axis=0)

    def far_bias(g):
        return jnp.concatenate(
            [jnp.broadcast_to(bt_ref[g * GQA_GROUP + hh, 0:1, 0:1], (nq, 1)) for hh in range(GQA_GROUP)], axis=0)

    def near_bias(g, lo, hi):
        return jnp.concatenate([bt_ref[g * GQA_GROUP + hh, 0:nq, lo:hi] for hh in range(GQA_GROUP)], axis=0)

    def group_queries(g):
        return jnp.concatenate(
            [q_ref[:, (g * GQA_GROUP + hh) * HEAD_DIM:(g * GQA_GROUP + hh + 1) * HEAD_DIM]
             for hh in range(GQA_GROUP)], axis=0).astype(BF16)

    def accumulate(g, lg, sel, vals):
        rs = slice(g * grows, (g + 1) * grows)
        m_old = m_scr[rs, :]
        m_new = jnp.maximum(m_old, jnp.max(jnp.where(sel, lg, NEG_BIG), axis=1, keepdims=True))
        p = jnp.where(sel, jnp.exp(lg - m_new), 0.0)
        alpha = jnp.exp(m_old - m_new)
        l_scr[rs, :] = alpha * l_scr[rs, :] + jnp.sum(p, axis=1, keepdims=True)
        acc_scr[rs, :] = alpha * acc_scr[rs, :] + _dot(p.astype(BF16), vals)
        m_scr[rs, :] = m_new

    for r in range(ATTN_PAGES):
        for g in range(N_KV_HEADS):
            rows = pl.ds(g, PAGE_SIZE, stride=N_KV_HEADS)
            kbuf[g, r * PAGE_SIZE:(r + 1) * PAGE_SIZE, :] = kpages[r][rows, :].astype(BF16)
            vbuf[g, r * PAGE_SIZE:(r + 1) * PAGE_SIZE, :] = vpages[r][rows, :].astype(BF16)

    col = lax.broadcasted_iota(I32, (nq, width), 1)
    sel = select(skp_ref[...], col + s * width)
    colg = lax.broadcasted_iota(I32, (grows, width), 1)
    in_near = (s == last) & (colg >= width - PAGE_SIZE)
    for g in range(N_KV_HEADS):
        near = jnp.tile(near_bias(g, 0, LANES), (1, ATTN_PAGES))
        bias = jnp.where(in_near, near, far_bias(g))
        lg = _dot_nt(group_queries(g), kbuf[g]) * scale + bias
        accumulate(g, lg, sel, vbuf[g])

    @pl.when(s == last)
    def _():
        pad = jnp.zeros((LANES - nq, D_KV), F32)
        kn = jnp.concatenate([kn_ref[...], pad], axis=0).astype(BF16)
        vn = jnp.concatenate([vn_ref[...], pad], axis=0).astype(BF16)
        coln = lax.broadcasted_iota(I32, (nq, LANES), 1) + past
        seln = select(skn_ref[...], coln)
        for g in range(N_KV_HEADS):
            cs = slice(g * HEAD_DIM, (g + 1) * HEAD_DIM)
            lg = _dot_nt(group_queries(g), kn[:, cs]) * scale + near_bias(g, LANES, NEAR)
            accumulate(g, lg, seln, vn[:, cs])
        out = acc_scr[...] / l_scr[...]
        for h in range(N_HEADS):
            o_ref[:, h * HEAD_DIM:(h + 1) * HEAD_DIM] = out[h * nq:(h + 1) * nq, :]


def _sample_attn(page_table, proj_s, sk_past, sk_new, thr, lim, btab, pool_k, pool_v, *, batch, nq, n_pages):
    steps = n_pages // ATTN_PAGES
    width = ATTN_PAGES * PAGE_SIZE
    past = n_pages * PAGE_SIZE

    def page_spec(r):
        return pl.BlockSpec((PAGE_SIZE * N_KV_HEADS, HEAD_DIM), functools.partial(
            lambda b, s, pt, r: (pt[b, s * ATTN_PAGES + r], 0), r=r))

    grid_spec = pltpu.PrefetchScalarGridSpec(
        num_scalar_prefetch=1,
        grid=(batch, steps),
        in_specs=[
            pl.BlockSpec((nq, D_ATTN), lambda b, s, pt: (b, COL_Q // D_ATTN)),
            pl.BlockSpec((nq, D_KV), lambda b, s, pt: (b, COL_K // D_KV)),
            pl.BlockSpec((nq, D_KV), lambda b, s, pt: (b, COL_V // D_KV)),
            pl.BlockSpec((None, nq, width), lambda b, s, pt: (b, 0, s)),
            pl.BlockSpec((None, nq, LANES), lambda b, s, pt: (b, 0, 0)),
            pl.BlockSpec((nq, LANES), lambda b, s, pt: (b, 0)),
            pl.BlockSpec((nq, LANES), lambda b, s, pt: (b, 0)),
            pl.BlockSpec((N_HEADS, LANES, NEAR), lambda b, s, pt: (0, 0, 0)),
        ] + [page_spec(r) for r in range(ATTN_PAGES)] * 2,
        out_specs=pl.BlockSpec((nq, D_ATTN), lambda b, s, pt: (b, 0)),
        scratch_shapes=[
            pltpu.VMEM((N_KV_HEADS, width, HEAD_DIM), BF16),
            pltpu.VMEM((N_KV_HEADS, width, HEAD_DIM), BF16),
            pltpu.VMEM((N_HEADS * nq, 1), F32),
            pltpu.VMEM((N_HEADS * nq, 1), F32),
            pltpu.VMEM((N_HEADS * nq, HEAD_DIM), F32),
        ],
    )
    return pl.pallas_call(
        functools.partial(_sample_attn_kernel, nq=nq, past=past),
        grid_spec=grid_spec,
        out_shape=jax.ShapeDtypeStruct((batch * nq, D_ATTN), F32),
        compiler_params=_params(("parallel", "arbitrary")),
        name="sample_attn",
    )(page_table, proj_s, proj_s, proj_s, sk_past, sk_new, thr, lim, btab,
      *([pool_k] * ATTN_PAGES), *([pool_v] * ATTN_PAGES))


TAIL = 8


def _ssd_kernel(xbc_ref, z_ref, sm_ref, cprev_ref, hprev_ref, cw_ref, cb_ref, alog_ref, dtb_ref, dvec_ref,
                g_ref, y_ref, cnew_ref, hnew_ref, xpad_scr, sm_scr, h_scr, yd_scr, *, rows):
    c = pl.program_id(1)
    q = CHUNK
    keep = CONV_WIDTH - 1
    heads_per_group = SSD_HEADS // SSD_GROUPS

    @pl.when(c == 0)
    def _():
        xpad_scr[...] = jnp.zeros_like(xpad_scr)
        sm_scr[...] = jnp.zeros_like(sm_scr)
        xpad_scr[TAIL - keep:TAIL, :] = cprev_ref[0]
        h_scr[...] = hprev_ref[0]

    xpad_scr[TAIL:TAIL + rows, :] = xbc_ref[...]
    sm_scr[0:rows, :] = sm_ref[...]
    conv = sum(xpad_scr[TAIL - keep + k:TAIL - keep + k + q, :] * cw_ref[k:k + 1, :]
               for k in range(CONV_WIDTH)) + cb_ref[...]
    new_tail = xpad_scr[rows + TAIL - keep:rows + TAIL, :]
    cnew_ref[0] = new_tail
    xpad_scr[TAIL - keep:TAIL, :] = new_tail

    xc = conv * jax.nn.sigmoid(conv)
    xs = xc[:, :D_SSD]
    bm = xc[:, D_SSD:D_SSD + SSD_GROUPS * D_STATE].astype(BF16)
    cm = xc[:, D_SSD + SSD_GROUPS * D_STATE:].astype(BF16)

    ri = lax.broadcasted_iota(I32, (q, q), 0)
    ci = lax.broadcasted_iota(I32, (q, q), 1)
    causal = ri >= ci
    x = sm_scr[:, SM_DT:SM_DT + SSD_HEADS] + dtb_ref[...]
    dt = jnp.maximum(x, 0.0) + jnp.log1p(jnp.exp(-jnp.abs(x)))
    dt = jnp.where(lax.broadcasted_iota(I32, (q, SSD_HEADS), 0) < rows, dt, 0.0)
    a = -jnp.exp(alog_ref[...])
    acum = _select_dot(causal.astype(BF16), dt * a)
    acum_t = jnp.concatenate([acum, jnp.zeros((q, LANES - SSD_HEADS), F32)], axis=1).T

    expand = (lax.broadcasted_iota(I32, (SSD_HEADS, D_SSD), 1) // SSD_HEAD_DIM
              == lax.broadcasted_iota(I32, (SSD_HEADS, D_SSD), 0)).astype(BF16)
    expand_t = (lax.broadcasted_iota(I32, (D_SSD, SSD_HEADS), 0) // SSD_HEAD_DIM
                == lax.broadcasted_iota(I32, (D_SSD, SSD_HEADS), 1)).astype(BF16)
    acum_x = _select_dot(acum, expand)
    xd = xs * _select_dot(dt, expand)
    xdw_t = (xd * jnp.exp(acum_x[q - 1:q, :] - acum_x)).T.astype(BF16)
    xd = xd.astype(BF16)
    state_decay = jnp.exp(_select_dot(expand_t, jnp.broadcast_to(acum_t[0:SSD_HEADS, q - 1:q], (SSD_HEADS, LANES))))

    for g in range(SSD_GROUPS):
        ns = slice(g * D_STATE, (g + 1) * D_STATE)
        hs = slice(g * heads_per_group * SSD_HEAD_DIM, (g + 1) * heads_per_group * SSD_HEAD_DIM)
        cb = _dot_nt(cm[:, ns], bm[:, ns])
        for e in range(heads_per_group):
            h = g * heads_per_group + e
            ps = slice(h * SSD_HEAD_DIM, (h + 1) * SSD_HEAD_DIM)
            seg = acum[:, h:h + 1] - acum_t[h:h + 1, :]
            lmat = jnp.exp(jnp.where(causal, seg, -jnp.inf))
            yd_scr[:, ps] = _dot((cb * lmat).astype(BF16), xd[:, ps])
        h_old = h_scr[hs, :]
        yd_scr[:, hs] += _dot_nt(cm[:, ns], h_old.astype(BF16)) * jnp.exp(acum_x[:, hs])
        h_scr[hs, :] = h_old * state_decay[hs, :] + _dot(xdw_t[hs, :], bm[:, ns])

    hnew_ref[0] = h_scr[...]
    y = yd_scr[0:rows, :] + dvec_ref[...] * xs[0:rows, :]
    zz = z_ref[...]
    y = y * (zz * jax.nn.sigmoid(zz))
    width = D_SSD // SSD_GROUPS
    for g in range(SSD_GROUPS):
        cs = slice(g * width, (g + 1) * width)
        y_ref[:, cs] = _rms(y[:, cs], g_ref[:, cs])


def _ssd(proj, conv_prev, ssm_prev, conv_w, conv_b, a_log, dt_bias, d_vec, g_ssd, *, batch, seq):
    rows = min(CHUNK, seq)
    nc = seq // rows
    full = lambda *shape: pl.BlockSpec(shape, lambda b, c: (0,) * len(shape))
    return pl.pallas_call(
        functools.partial(_ssd_kernel, rows=rows),
        grid=(batch, nc),
        in_specs=[
            pl.BlockSpec((rows, CONV_DIM), lambda b, c: (b * nc + c, COL_XBC // CONV_DIM)),
            pl.BlockSpec((rows, D_SSD), lambda b, c: (b * nc + c, COL_Z // D_SSD)),
            pl.BlockSpec((rows, LANES), lambda b, c: (b * nc + c, COL_SMALL // LANES)),
            pl.BlockSpec((1, CONV_WIDTH - 1, CONV_DIM), lambda b, c: (b, 0, 0)),
            pl.BlockSpec((1, D_SSD, D_STATE), lambda b, c: (b, 0, 0)),
            full(CONV_WIDTH, CONV_DIM), full(1, CONV_DIM), full(1, SSD_HEADS), full(1, SSD_HEADS),
            full(1, D_SSD), full(1, D_SSD),
        ],
        out_specs=[
            pl.BlockSpec((rows, D_SSD), lambda b, c: (b * nc + c, 0)),
            pl.BlockSpec((1, CONV_WIDTH - 1, CONV_DIM), lambda b, c: (b, 0, 0)),
            pl.BlockSpec((1, D_SSD, D_STATE), lambda b, c: (b, 0, 0)),
        ],
        out_shape=[
            jax.ShapeDtypeStruct((batch * seq, D_SSD), F32),
            jax.ShapeDtypeStruct((batch, CONV_WIDTH - 1, CONV_DIM), F32),
            jax.ShapeDtypeStruct((batch, D_SSD, D_STATE), F32),
        ],
        scratch_shapes=[
            pltpu.VMEM((CHUNK + TAIL, CONV_DIM), F32),
            pltpu.VMEM((CHUNK, LANES), F32),
            pltpu.VMEM((D_SSD, D_STATE), F32),
            pltpu.VMEM((CHUNK, D_SSD), F32),
        ],
        compiler_params=_params(("parallel", "arbitrary")),
        name="ssd",
    )(proj, proj, proj, conv_prev, ssm_prev, conv_w, conv_b, a_log, dt_bias, d_vec, g_ssd)


def _reorder_w_in(w_in):
    offs = [0]
    for s in IN_SPLITS:
        offs.append(offs[-1] + s)
    q, k, v, qi, ki, wi, z, xbc, dtr = [w_in[:, offs[n]:offs[n + 1]] for n in range(len(IN_SPLITS))]
    cols = jnp.concatenate([q, qi, z, xbc, k, v, ki, wi, dtr], axis=1)
    return jnp.pad(cols, ((0, 0), (0, N_IN - cols.shape[1]))).astype(BF16)


def kernel(x_prompt, x_sample, cache_k, cache_v, cache_kidx, state_conv, state_ssm, page_table, rel_bias, g_ffn1, w1_ffn1, w3_ffn1, w2_ffn1, g_mix, w_in, conv_w, conv_b, a_log, dt_bias, d_skip, g_ssd, w_out, g_ffn2, w1_ffn2, w3_ffn2, w2_ffn2, g_final):
    depth = w_in.shape[0]
    assert depth == 1
    l = 0
    bp, seq, _ = x_prompt.shape
    bs, nq, _ = x_sample.shape
    n_pages = page_table.shape[1]
    row = lambda t: t.reshape(1, -1)

    w1a, w3a, w2a = w1_ffn1[l].astype(BF16), w3_ffn1[l].astype(BF16), w2_ffn1[l].astype(BF16)
    w1b, w3b, w2b = w1_ffn2[l].astype(BF16), w3_ffn2[l].astype(BF16), w2_ffn2[l].astype(BF16)
    w_in_r = _reorder_w_in(w_in[l])
    w_out_a = w_out[l, :D_ATTN].astype(BF16)
    w_out_s = w_out[l, D_ATTN:].astype(BF16)
    d_vec = jnp.repeat(d_skip[l], SSD_HEAD_DIM).reshape(1, D_SSD)
    btab = _bias_table(rel_bias)
    gf = row(g_final)

    def trunk(x, attend, conv_prev, ssm_prev, batch, length):
        tm = min(x.shape[0], 1024)
        tf = 256 if tm > 512 else 512
        x = _ffn(x, row(g_ffn1[l]), w1a, w3a, w2a, gf, tm=tm, tf=tf, final_norm=False)
        proj = _in_proj(x, row(g_mix[l]), w_in_r, tm=tm)
        a_out = attend(proj)
        s_out, conv_new, ssm_new = _ssd(
            proj, conv_prev, ssm_prev.reshape(batch, D_SSD, D_STATE), conv_w[l], row(conv_b[l]), row(a_log[l]),
            row(dt_bias[l]), d_vec, row(g_ssd[l]), batch=batch, seq=length)
        x = _out_proj(x, a_out, s_out, w_out_a, w_out_s, tm=min(tm, 256))
        y = _ffn(x, row(g_ffn2[l]), w1b, w3b, w2b, gf, tm=tm, tf=tf, final_norm=True)
        k = proj[:, COL_K:COL_K + D_KV].reshape(1, batch, length, N_KV_HEADS, HEAD_DIM)
        v = proj[:, COL_V:COL_V + D_KV].reshape(1, batch, length, N_KV_HEADS, HEAD_DIM)
        ki = proj[:, COL_SMALL + SM_KI:COL_SMALL + SM_KI + IDX_DIM].reshape(1, batch, length, IDX_DIM)
        ssm_new = ssm_new.reshape(1, batch, SSD_HEADS, SSD_HEAD_DIM, D_STATE)
        return y.reshape(batch, length, D_MODEL), k, v, ki, conv_new[None], ssm_new

    y_p, k_p, v_p, ki_p, conv_p, ssm_p = trunk(
        x_prompt.reshape(bp * seq, D_MODEL),
        functools.partial(_dsa_prompt, btab=btab, batch=bp, seq=seq),
        jnp.zeros((bp, CONV_WIDTH - 1, CONV_DIM), F32), jnp.zeros((bp, SSD_HEADS, SSD_HEAD_DIM, D_STATE), F32),
        bp, seq)

    pool_k = cache_k[l].reshape(-1, HEAD_DIM)
    pool_v = cache_v[l].reshape(-1, HEAD_DIM)
    pool_kidx = jnp.swapaxes(cache_kidx[l], 1, 2)

    def sample_attend(proj):
        sk_past, sk_new = _sample_score(page_table, proj, pool_kidx, batch=bs, nq=nq, n_pages=n_pages)
        thr, lim = _sample_select(sk_past.reshape(bs * nq, -1), sk_new.reshape(bs * nq, LANES),
                                  rows_per_step=128)
        return _sample_attn(page_table, proj, sk_past, sk_new, thr, lim, btab, pool_k, pool_v,
                            batch=bs, nq=nq, n_pages=n_pages)

    y_s, k_s, v_s, ki_s, conv_s, ssm_s = trunk(
        x_sample.reshape(bs * nq, D_MODEL), sample_attend, state_conv[l], state_ssm[l], bs, nq)

    return (y_p, y_s, k_p, v_p, ki_p, conv_p, ssm_p, k_s, v_s, ki_s, conv_s, ssm_s)
```

```python
import functools
import math

import jax
import jax.numpy as jnp
from jax import lax
from jax.experimental import pallas as pl
from jax.experimental.pallas import tpu as pltpu

F32 = jnp.float32
BF16 = jnp.bfloat16
I32 = jnp.int32

D_MODEL = 2048
PAGE_SIZE = 128
N_HEADS = 8
HEAD_DIM = 128
N_KV_HEADS = 2
GQA_GROUP = N_HEADS // N_KV_HEADS
IDX_HEADS = 16
IDX_DIM = 64
TOPK = 256
N_BUCKETS = 32
MAX_DISTANCE = 128
SSD_HEADS = 16
SSD_HEAD_DIM = 64
SSD_GROUPS = 2
D_STATE = 128
CONV_WIDTH = 4
CHUNK = 128
D_ATTN = N_HEADS * HEAD_DIM
D_SSD = SSD_HEADS * SSD_HEAD_DIM
D_KV = N_KV_HEADS * HEAD_DIM
CONV_DIM = D_SSD + 2 * SSD_GROUPS * D_STATE
D_FF = 5632
EPS = 1e-6
IN_SPLITS = (D_ATTN, D_KV, D_KV, IDX_HEADS * IDX_DIM, IDX_DIM, IDX_HEADS, D_SSD, CONV_DIM, SSD_HEADS)

LANES = 128
COL_Q = 0
COL_QI = COL_Q + D_ATTN
COL_Z = COL_QI + IDX_HEADS * IDX_DIM
COL_XBC = COL_Z + D_SSD
COL_K = COL_XBC + CONV_DIM
COL_V = COL_K + D_KV
COL_SMALL = COL_V + D_KV
SM_KI = 0
SM_WI = SM_KI + IDX_DIM
SM_DT = SM_WI + IDX_HEADS
IN_TILE = 768
N_IN = ((COL_SMALL + LANES + IN_TILE - 1) // IN_TILE) * IN_TILE

INT_MIN = -(2 ** 31)
NEG_BIG = -1e30
VMEM_LIMIT = 56 * 1024 * 1024


def _params(sem):
    return pltpu.CompilerParams(dimension_semantics=sem, vmem_limit_bytes=VMEM_LIMIT)


def _rms(x, g):
    return x * lax.rsqrt(jnp.mean(x * x, axis=-1, keepdims=True) + EPS) * g


def _dot(a, b):
    return jnp.dot(a, b, preferred_element_type=F32)


def _dot_nt(a, b):
    return lax.dot_general(a, b, (((1,), (1,)), ((), ())), preferred_element_type=F32)


def _split3(x):
    hi = x.astype(BF16)
    rest = x - hi.astype(F32)
    mid = rest.astype(BF16)
    return hi, mid, (rest - mid.astype(F32)).astype(BF16)


def _select_dot(a, b):
    if a.dtype == BF16:
        return sum(_dot(a, part) for part in _split3(b))
    return sum(_dot(part, b) for part in _split3(a))


def _sort_key(x):
    bits = lax.bitcast_convert_type(x + 0.0, I32)
    return bits ^ ((bits >> 31) & 0x7FFFFFFF)


def _ffn_kernel(x_ref, g_ref, w1_ref, w3_ref, w2_ref, gf_ref, o_ref, h_scr, *, final_norm):
    f = pl.program_id(1)

    @pl.when(f == 0)
    def _():
        h_scr[...] = _rms(x_ref[...], g_ref[...]).astype(BF16)
        o_ref[...] = jnp.zeros_like(o_ref)

    h = h_scr[...]
    a = _dot(h, w1_ref[...])
    b = _dot(h, w3_ref[...])
    u = (a * jax.nn.sigmoid(a) * b).astype(BF16)
    o_ref[...] += _dot(u, w2_ref[...])

    @pl.when(f == pl.num_programs(1) - 1)
    def _():
        y = x_ref[...] + 0.5 * o_ref[...]
        if final_norm:
            y = _rms(y, gf_ref[...])
        o_ref[...] = y


def _ffn(x, g, w1, w3, w2, gf, *, tm, tf, final_norm):
    t = x.shape[0]
    return pl.pallas_call(
        functools.partial(_ffn_kernel, final_norm=final_norm),
        grid=(t // tm, D_FF // tf),
        in_specs=[
            pl.BlockSpec((tm, D_MODEL), lambda i, f: (i, 0)),
            pl.BlockSpec((1, D_MODEL), lambda i, f: (0, 0)),
            pl.BlockSpec((D_MODEL, tf), lambda i, f: (0, f)),
            pl.BlockSpec((D_MODEL, tf), lambda i, f: (0, f)),
            pl.BlockSpec((tf, D_MODEL), lambda i, f: (f, 0)),
            pl.BlockSpec((1, D_MODEL), lambda i, f: (0, 0)),
        ],
        out_specs=pl.BlockSpec((tm, D_MODEL), lambda i, f: (i, 0)),
        out_shape=jax.ShapeDtypeStruct((t, D_MODEL), F32),
        scratch_shapes=[pltpu.VMEM((tm, D_MODEL), BF16)],
        compiler_params=_params(("parallel", "arbitrary")),
        name="ffn",
    )(x, g, w1, w3, w2, gf)


def _in_proj_kernel(x_ref, g_ref, w_ref, o_ref, h_scr):
    @pl.when(pl.program_id(1) == 0)
    def _():
        h_scr[...] = _rms(x_ref[...], g_ref[...]).astype(BF16)

    o_ref[...] = _dot(h_scr[...], w_ref[...])


def _in_proj(x, g, w, *, tm):
    t = x.shape[0]
    return pl.pallas_call(
        _in_proj_kernel,
        grid=(t // tm, N_IN // IN_TILE),
        in_specs=[
            pl.BlockSpec((tm, D_MODEL), lambda i, j: (i, 0)),
            pl.BlockSpec((1, D_MODEL), lambda i, j: (0, 0)),
            pl.BlockSpec((D_MODEL, IN_TILE), lambda i, j: (0, j)),
        ],
        out_specs=pl.BlockSpec((tm, IN_TILE), lambda i, j: (i, j)),
        out_shape=jax.ShapeDtypeStruct((t, N_IN), F32),
        scratch_shapes=[pltpu.VMEM((tm, D_MODEL), BF16)],
        compiler_params=_params(("parallel", "arbitrary")),
        name="in_proj",
    )(x, g, w)


def _out_proj_kernel(x_ref, a_ref, s_ref, wa_ref, ws_ref, o_ref):
    acc = _dot(a_ref[...].astype(BF16), wa_ref[...])
    acc += _dot(s_ref[...].astype(BF16), ws_ref[...])
    o_ref[...] = x_ref[...] + acc


def _out_proj(x, a, s, wa, ws, *, tm):
    t = x.shape[0]
    return pl.pallas_call(
        _out_proj_kernel,
        grid=(t // tm,),
        in_specs=[
            pl.BlockSpec((tm, D_MODEL), lambda i: (i, 0)),
            pl.BlockSpec((tm, D_ATTN), lambda i: (i, 0)),
            pl.BlockSpec((tm, D_SSD), lambda i: (i, 0)),
            pl.BlockSpec((D_ATTN, D_MODEL), lambda i: (0, 0)),
            pl.BlockSpec((D_SSD, D_MODEL), lambda i: (0, 0)),
        ],
        out_specs=pl.BlockSpec((tm, D_MODEL), lambda i: (i, 0)),
        out_shape=jax.ShapeDtypeStruct((t, D_MODEL), F32),
        compiler_params=_params(("parallel",)),
        name="out_proj",
    )(x, a, s, wa, ws)


NEAR = 2 * LANES


def _bias_kernel(rb_ref, o_ref):
    r = lax.broadcasted_iota(I32, (LANES, NEAR), 0)
    c = lax.broadcasted_iota(I32, (LANES, NEAR), 1)
    n = jnp.maximum(r + LANES - c, 0)
    max_exact = N_BUCKETS // 2
    nf = jnp.maximum(n, 1).astype(F32)
    large = max_exact + (jnp.log(nf / max_exact) / math.log(MAX_DISTANCE / max_exact)
                         * (N_BUCKETS - max_exact)).astype(I32)
    bucket = jnp.where(n < max_exact, n, jnp.minimum(large, N_BUCKETS - 1))
    for h in range(N_HEADS):
        acc = jnp.zeros((LANES, NEAR), F32)
        for b in range(N_BUCKETS):
            acc = jnp.where(bucket == b, rb_ref[b, h], acc)
        o_ref[h] = acc


def _bias_table(rel_bias):
    return pl.pallas_call(
        _bias_kernel,
        in_specs=[pl.BlockSpec(memory_space=pltpu.SMEM)],
        out_specs=pl.BlockSpec(memory_space=pltpu.VMEM),
        out_shape=jax.ShapeDtypeStruct((N_HEADS, LANES, NEAR), F32),
        name="bias_table",
    )(rel_bias)


def _count(mask, axis=1):
    x = mask.astype(F32)
    if axis == 0:
        x = jnp.sum(x.reshape(x.shape[0] // 64, 64, x.shape[1]), axis=0)
    return jnp.sum(x, axis=axis, keepdims=True)


def _kth_largest(count_ge, shape):
    t = jnp.full(shape, INT_MIN, I32)
    t = jnp.where(count_ge(jnp.zeros(shape, I32)) >= TOPK, 0, t)

    def body(it, t):
        cand = t + lax.shift_left(jnp.int32(1), 30 - it)
        return jnp.where(count_ge(cand) >= TOPK, cand, t)

    return lax.fori_loop(0, 31, body, t)


def _tie_limit(count_eq_below, need, shape, idx_bits):
    def body(it, m):
        cand = m + lax.shift_left(jnp.int32(1), idx_bits - 1 - it)
        return jnp.where(count_eq_below(cand) < need, cand, m)

    return lax.fori_loop(0, idx_bits, body, jnp.zeros(shape, I32))


CAUSAL_VARIANTS = 4


def _dsa_prompt_block(width, near_tiles, i, q_ref, qi_ref, k_ref, v_ref, sm_ref, bt_ref, o_ref, sk_scr, neg_scr):
    tq = LANES
    q0 = pl.multiple_of(i * LANES, LANES)
    shape = (1, tq)

    ki = sm_ref[0:width, SM_KI:SM_KI + IDX_DIM].astype(BF16)
    wi_t = sm_ref[pl.ds(q0, tq), :].T[SM_WI:SM_WI + IDX_HEADS, :] * (IDX_HEADS * IDX_DIM) ** -0.5
    score = jnp.zeros((width, tq), F32)
    for h in range(IDX_HEADS):
        qh = qi_ref[:, h * IDX_DIM:(h + 1) * IDX_DIM].astype(BF16)
        score = score + wi_t[h:h + 1, :] * jnp.maximum(_dot_nt(ki, qh), 0.0)

    key = lax.broadcasted_iota(I32, (width, tq), 0)
    pos = lax.broadcasted_iota(I32, (width, tq), 1) + q0
    sk_scr[0:width, :] = jnp.where(key <= pos, _sort_key(score), INT_MIN)

    def sk():
        return sk_scr[0:width, :]

    thr = _kth_largest(lambda c: _count(sk() >= c, 0), shape)
    excess = (_count(sk() >= thr, 0) > TOPK) & (thr > INT_MIN)
    any_excess = jnp.max(excess.astype(F32)) > 0.0

    @pl.when(jnp.logical_not(any_excess))
    def _():
        neg_scr[0:width, :] = jnp.where(sk() >= jnp.maximum(thr, INT_MIN + 1), 0.0, -jnp.inf)

    @pl.when(any_excess)
    def _():
        need = TOPK - _count(sk() > thr, 0)
        lim = _tie_limit(lambda m: _count((sk() == thr) & (key < m), 0), need, shape, (width - 1).bit_length())
        take = (sk() > thr) | ((sk() == thr) & (key <= lim) & (sk() > INT_MIN))
        neg_scr[0:width, :] = jnp.where(take, 0.0, -jnp.inf)

    neg = neg_scr[0:width, :].T
    neg = jnp.concatenate([neg] * GQA_GROUP, axis=0)

    far_w = max(width - near_tiles * LANES, 0)
    near_w = width - far_w
    tile = lax.broadcasted_iota(I32, (tq, near_w), 1) // LANES + far_w // LANES
    scale = HEAD_DIM ** -0.5
    for g in range(N_KV_HEADS):
        heads = range(g * GQA_GROUP, (g + 1) * GQA_GROUP)
        kg = k_ref[0:width, g * HEAD_DIM:(g + 1) * HEAD_DIM].astype(BF16)
        vg = v_ref[0:width, g * HEAD_DIM:(g + 1) * HEAD_DIM].astype(BF16)
        qg = jnp.concatenate([q_ref[:, h * HEAD_DIM:(h + 1) * HEAD_DIM] for h in heads], axis=0).astype(BF16)
        far = jnp.concatenate([jnp.broadcast_to(bt_ref[h, 0:1, 0:1], (tq, 1)) for h in heads], axis=0)
        near = jnp.concatenate([
            jnp.where(tile == i, jnp.tile(bt_ref[h, :, LANES:NEAR], (1, near_w // LANES)),
                      jnp.where(tile == i - 1, jnp.tile(bt_ref[h, :, 0:LANES], (1, near_w // LANES)),
                                bt_ref[h, 0:1, 0:1]))
            for h in heads], axis=0)
        qk = _dot_nt(qg, kg) * scale
        lg = jnp.concatenate([qk[:, :far_w] + far, qk[:, far_w:] + near], axis=1) if far_w else qk + near
        lg = lg + neg
        m = jnp.max(lg, axis=1, keepdims=True)
        p = jnp.exp(lg - m)
        l = jnp.sum(p, axis=1, keepdims=True)
        out = _dot(p.astype(BF16), vg) / l
        for n, h in enumerate(heads):
            o_ref[:, h * HEAD_DIM:(h + 1) * HEAD_DIM] = out[n * tq:(n + 1) * tq, :]


def _dsa_prompt_kernel(*refs, seq):
    i = pl.program_id(1)
    per = seq // LANES // CAUSAL_VARIANTS
    for v in range(CAUSAL_VARIANTS):
        @pl.when(i // per == v)
        def _(v=v):
            _dsa_prompt_block((v + 1) * per * LANES, per + 1, i, *refs)


def _dsa_prompt(proj, btab, *, batch, seq):
    nq = seq // LANES
    return pl.pallas_call(
        functools.partial(_dsa_prompt_kernel, seq=seq),
        grid=(batch, nq),
        in_specs=[
            pl.BlockSpec((LANES, D_ATTN), lambda b, i: (b * nq + i, COL_Q // D_ATTN)),
            pl.BlockSpec((LANES, IDX_HEADS * IDX_DIM), lambda b, i: (b * nq + i, COL_QI // (IDX_HEADS * IDX_DIM))),
            pl.BlockSpec((seq, D_KV), lambda b, i: (b, COL_K // D_KV)),
            pl.BlockSpec((seq, D_KV), lambda b, i: (b, COL_V // D_KV)),
            pl.BlockSpec((seq, LANES), lambda b, i: (b, COL_SMALL // LANES)),
            pl.BlockSpec((N_HEADS, LANES, NEAR), lambda b, i: (0, 0, 0)),
        ],
        out_specs=pl.BlockSpec((LANES, D_ATTN), lambda b, i: (b * nq + i, 0)),
        out_shape=jax.ShapeDtypeStruct((batch * seq, D_ATTN), F32),
        scratch_shapes=[pltpu.VMEM((seq, LANES), I32), pltpu.VMEM((seq, LANES), F32)],
        compiler_params=_params(("parallel", "parallel")),
        name="dsa_prompt",
    )(proj, proj, proj, proj, proj, btab)


SCORE_PAGES = 64
ATTN_PAGES = 32


def _sample_queries(qi_ref):
    return jnp.concatenate(
        [qi_ref[:, h * IDX_DIM:(h + 1) * IDX_DIM] for h in range(IDX_HEADS)], axis=0).astype(BF16)


def _sample_scores(rel, wi, nq):
    sc = jnp.zeros((nq, rel.shape[1]), F32)
    for h in range(IDX_HEADS):
        sc = sc + wi[:, h:h + 1] * jnp.maximum(rel[h * nq:(h + 1) * nq, :], 0.0)
    return sc


def _sample_score_kernel(pt_ref, qi_ref, sm_ref, *refs, nq):
    pages = refs[:SCORE_PAGES]
    past_ref, new_ref, kbuf = refs[SCORE_PAGES:]
    s = pl.program_id(1)
    qx = _sample_queries(qi_ref)
    wi = sm_ref[:, SM_WI:SM_WI + IDX_HEADS] * (IDX_HEADS * IDX_DIM) ** -0.5
    for r in range(SCORE_PAGES):
        kbuf[:, r * PAGE_SIZE:(r + 1) * PAGE_SIZE] = pages[r][...].astype(BF16)
    past_ref[...] = _sort_key(_sample_scores(_dot(qx, kbuf[...]), wi, nq))

    @pl.when(s == pl.num_programs(1) - 1)
    def _():
        ki_new = jnp.concatenate(
            [sm_ref[:, SM_KI:SM_KI + IDX_DIM], jnp.zeros((LANES - nq, IDX_DIM), F32)], axis=0).astype(BF16)
        sc = _sample_scores(_dot_nt(qx, ki_new), wi, nq)
        j = lax.broadcasted_iota(I32, (nq, LANES), 1)
        t = lax.broadcasted_iota(I32, (nq, LANES), 0)
        new_ref[...] = jnp.where(j <= t, _sort_key(sc), INT_MIN)


def _sample_score(page_table, proj_s, pool_kidx_t, *, batch, nq, n_pages):
    steps = n_pages // SCORE_PAGES
    page_specs = [
        pl.BlockSpec((None, IDX_DIM, PAGE_SIZE), functools.partial(
            lambda b, s, pt, r: (pt[b, s * SCORE_PAGES + r], 0, 0), r=r))
        for r in range(SCORE_PAGES)
    ]
    grid_spec = pltpu.PrefetchScalarGridSpec(
        num_scalar_prefetch=1,
        grid=(batch, steps),
        in_specs=[
            pl.BlockSpec((nq, IDX_HEADS * IDX_DIM), lambda b, s, pt: (b, COL_QI // (IDX_HEADS * IDX_DIM))),
            pl.BlockSpec((nq, LANES), lambda b, s, pt: (b, COL_SMALL // LANES)),
        ] + page_specs,
        out_specs=[
            pl.BlockSpec((None, nq, SCORE_PAGES * PAGE_SIZE), lambda b, s, pt: (b, 0, s)),
            pl.BlockSpec((None, nq, LANES), lambda b, s, pt: (b, 0, 0)),
        ],
        scratch_shapes=[pltpu.VMEM((IDX_DIM, SCORE_PAGES * PAGE_SIZE), BF16)],
    )
    return pl.pallas_call(
        functools.partial(_sample_score_kernel, nq=nq),
        grid_spec=grid_spec,
        out_shape=[
            jax.ShapeDtypeStruct((batch, nq, n_pages * PAGE_SIZE), I32),
            jax.ShapeDtypeStruct((batch, nq, LANES), I32),
        ],
        compiler_params=_params(("parallel", "arbitrary")),
        name="sample_score",
    )(page_table, proj_s, proj_s, *([pool_kidx_t] * SCORE_PAGES))


def _sample_select_kernel(past_ref, new_ref, thr_ref, lim_ref, *, past):
    rows = past_ref.shape[0]
    colp = lax.broadcasted_iota(I32, (rows, past), 1)
    coln = lax.broadcasted_iota(I32, (rows, LANES), 1) + past

    def count_ge(c):
        return _count(past_ref[...] >= c) + _count(new_ref[...] >= c)

    thr = _kth_largest(count_ge, (rows, 1))
    thr_ref[...] = jnp.broadcast_to(thr, (rows, LANES))
    lim_ref[...] = jnp.full((rows, LANES), 2 ** 31 - 1, I32)
    any_excess = jnp.max((count_ge(thr) > TOPK).astype(F32)) > 0.0

    @pl.when(any_excess)
    def _():
        need = TOPK - (_count(past_ref[...] > thr) + _count(new_ref[...] > thr))

        def count_eq_below(m):
            return (_count((past_ref[...] == thr) & (colp < m))
                    + _count((new_ref[...] == thr) & (coln < m)))

        lim = _tie_limit(count_eq_below, need, (rows, 1), (past + LANES - 1).bit_length())
        lim_ref[...] = jnp.broadcast_to(lim, (rows, LANES))


def _sample_select(sk_past, sk_new, *, rows_per_step):
    rows, past = sk_past.shape
    return pl.pallas_call(
        functools.partial(_sample_select_kernel, past=past),
        grid=(rows // rows_per_step,),
        in_specs=[
            pl.BlockSpec((rows_per_step, past), lambda i: (i, 0)),
            pl.BlockSpec((rows_per_step, LANES), lambda i: (i, 0)),
        ],
        out_specs=[
            pl.BlockSpec((rows_per_step, LANES), lambda i: (i, 0)),
            pl.BlockSpec((rows_per_step, LANES), lambda i: (i, 0)),
        ],
        out_shape=[jax.ShapeDtypeStruct((rows, LANES), I32)] * 2,
        compiler_params=_params(("parallel",)),
        name="sample_select",
    )(sk_past, sk_new)


def _sample_attn_kernel(pt_ref, q_ref, kn_ref, vn_ref, skp_ref, skn_ref, thr_ref, lim_ref, bt_ref, *refs,
                        nq, past):
    kpages = refs[:ATTN_PAGES]
    vpages = refs[ATTN_PAGES:2 * ATTN_PAGES]
    o_ref, kbuf, vbuf, m_scr, l_scr, acc_scr = refs[2 * ATTN_PAGES:]
    s = pl.program_id(1)
    last = pl.num_programs(1) - 1
    width = ATTN_PAGES * PAGE_SIZE
    grows = GQA_GROUP * nq
    scale = HEAD_DIM ** -0.5

    @pl.when(s == 0)
    def _():
        m_scr[...] = jnp.full_like(m_scr, NEG_BIG)
        l_scr[...] = jnp.zeros_like(l_scr)
        acc_scr[...] = jnp.zeros_like(acc_scr)

    thr = thr_ref[:, 0:1]
    lim = lim_ref[:, 0:1]

    def select(sk, idx):
        sel = (sk > thr) | ((sk == thr) & (idx <= lim))
        return jnp.concatenate([sel] * GQA_GROUP, axis=0)

    def far_bias(g):
        return jnp.concatenate(
            [jnp.broadcast_to(bt_ref[g * GQA_GROUP + hh, 0:1, 0:1], (nq, 1)) for hh in range(GQA_GROUP)], axis=0)

    def near_bias(g, lo, hi):
        return jnp.concatenate([bt_ref[g * GQA_GROUP + hh, 0:nq, lo:hi] for hh in range(GQA_GROUP)], axis=0)

    def group_queries(g):
        return jnp.concatenate(
            [q_ref[:, (g * GQA_GROUP + hh) * HEAD_DIM:(g * GQA_GROUP + hh + 1) * HEAD_DIM]
             for hh in range(GQA_GROUP)], axis=0).astype(BF16)

    def accumulate(g, lg, sel, vals):
        rs = slice(g * grows, (g + 1) * grows)
        m_old = m_scr[rs, :]
        m_new = jnp.maximum(m_old, jnp.max(jnp.where(sel, lg, NEG_BIG), axis=1, keepdims=True))
        p = jnp.where(sel, jnp.exp(lg - m_new), 0.0)
        alpha = jnp.exp(m_old - m_new)
        l_scr[rs, :] = alpha * l_scr[rs, :] + jnp.sum(p, axis=1, keepdims=True)
        acc_scr[rs, :] = alpha * acc_scr[rs, :] + _dot(p.astype(BF16), vals)
        m_scr[rs, :] = m_new

    for r in range(ATTN_PAGES):
        for g in range(N_KV_HEADS):
            rows = pl.ds(g, PAGE_SIZE, stride=N_KV_HEADS)
            kbuf[g, r * PAGE_SIZE:(r + 1) * PAGE_SIZE, :] = kpages[r][rows, :].astype(BF16)
            vbuf[g, r * PAGE_SIZE:(r + 1) * PAGE_SIZE, :] = vpages[r][rows, :].astype(BF16)

    col = lax.broadcasted_iota(I32, (nq, width), 1)
    sel = select(skp_ref[...], col + s * width)
    colg = lax.broadcasted_iota(I32, (grows, width), 1)
    in_near = (s == last) & (colg >= width - PAGE_SIZE)
    for g in range(N_KV_HEADS):
        near = jnp.tile(near_bias(g, 0, LANES), (1, ATTN_PAGES))
        bias = jnp.where(in_near, near, far_bias(g))
        lg = _dot_nt(group_queries(g), kbuf[g]) * scale + bias
        accumulate(g, lg, sel, vbuf[g])

    @pl.when(s == last)
    def _():
        pad = jnp.zeros((LANES - nq, D_KV), F32)
        kn = jnp.concatenate([kn_ref[...], pad], axis=0).astype(BF16)
        vn = jnp.concatenate([vn_ref[...], pad], axis=0).astype(BF16)
        coln = lax.broadcasted_iota(I32, (nq, LANES), 1) + past
        seln = select(skn_ref[...], coln)
        for g in range(N_KV_HEADS):
            cs = slice(g * HEAD_DIM, (g + 1) * HEAD_DIM)
            lg = _dot_nt(group_queries(g), kn[:, cs]) * scale + near_bias(g, LANES, NEAR)
            accumulate(g, lg, seln, vn[:, cs])
        out = acc_scr[...] / l_scr[...]
        for h in range(N_HEADS):
            o_ref[:, h * HEAD_DIM:(h + 1) * HEAD_DIM] = out[h * nq:(h + 1) * nq, :]


def _sample_attn(page_table, proj_s, sk_past, sk_new, thr, lim, btab, pool_k, pool_v, *, batch, nq, n_pages):
    steps = n_pages // ATTN_PAGES
    width = ATTN_PAGES * PAGE_SIZE
    past = n_pages * PAGE_SIZE

    def page_spec(r):
        return pl.BlockSpec((PAGE_SIZE * N_KV_HEADS, HEAD_DIM), functools.partial(
            lambda b, s, pt, r: (pt[b, s * ATTN_PAGES + r], 0), r=r))

    grid_spec = pltpu.PrefetchScalarGridSpec(
        num_scalar_prefetch=1,
        grid=(batch, steps),
        in_specs=[
            pl.BlockSpec((nq, D_ATTN), lambda b, s, pt: (b, COL_Q // D_ATTN)),
            pl.BlockSpec((nq, D_KV), lambda b, s, pt: (b, COL_K // D_KV)),
            pl.BlockSpec((nq, D_KV), lambda b, s, pt: (b, COL_V // D_KV)),
            pl.BlockSpec((None, nq, width), lambda b, s, pt: (b, 0, s)),
            pl.BlockSpec((None, nq, LANES), lambda b, s, pt: (b, 0, 0)),
            pl.BlockSpec((nq, LANES), lambda b, s, pt: (b, 0)),
            pl.BlockSpec((nq, LANES), lambda b, s, pt: (b, 0)),
            pl.BlockSpec((N_HEADS, LANES, NEAR), lambda b, s, pt: (0, 0, 0)),
        ] + [page_spec(r) for r in range(ATTN_PAGES)] * 2,
        out_specs=pl.BlockSpec((nq, D_ATTN), lambda b, s, pt: (b, 0)),
        scratch_shapes=[
            pltpu.VMEM((N_KV_HEADS, width, HEAD_DIM), BF16),
            pltpu.VMEM((N_KV_HEADS, width, HEAD_DIM), BF16),
            pltpu.VMEM((N_HEADS * nq, 1), F32),
            pltpu.VMEM((N_HEADS * nq, 1), F32),
            pltpu.VMEM((N_HEADS * nq, HEAD_DIM), F32),
        ],
    )
    return pl.pallas_call(
        functools.partial(_sample_attn_kernel, nq=nq, past=past),
        grid_spec=grid_spec,
        out_shape=jax.ShapeDtypeStruct((batch * nq, D_ATTN), F32),
        compiler_params=_params(("parallel", "arbitrary")),
        name="sample_attn",
    )(page_table, proj_s, proj_s, proj_s, sk_past, sk_new, thr, lim, btab,
      *([pool_k] * ATTN_PAGES), *([pool_v] * ATTN_PAGES))


TAIL = 8


def _ssd_kernel(xbc_ref, z_ref, sm_ref, cprev_ref, hprev_ref, cw_ref, cb_ref, alog_ref, dtb_ref, dvec_ref,
                g_ref, y_ref, cnew_ref, hnew_ref, xpad_scr, sm_scr, h_scr, yd_scr, *, rows):
    c = pl.program_id(1)
    q = CHUNK
    keep = CONV_WIDTH - 1
    heads_per_group = SSD_HEADS // SSD_GROUPS

    @pl.when(c == 0)
    def _():
        xpad_scr[...] = jnp.zeros_like(xpad_scr)
        sm_scr[...] = jnp.zeros_like(sm_scr)
        xpad_scr[TAIL - keep:TAIL, :] = cprev_ref[0]
        h_scr[...] = hprev_ref[0]

    xpad_scr[TAIL:TAIL + rows, :] = xbc_ref[...]
    sm_scr[0:rows, :] = sm_ref[...]
    conv = sum(xpad_scr[TAIL - keep + k:TAIL - keep + k + q, :] * cw_ref[k:k + 1, :]
               for k in range(CONV_WIDTH)) + cb_ref[...]
    new_tail = xpad_scr[rows + TAIL - keep:rows + TAIL, :]
    cnew_ref[0] = new_tail
    xpad_scr[TAIL - keep:TAIL, :] = new_tail

    xc = conv * jax.nn.sigmoid(conv)
    xs = xc[:, :D_SSD]
    bm = xc[:, D_SSD:D_SSD + SSD_GROUPS * D_STATE].astype(BF16)
    cm = xc[:, D_SSD + SSD_GROUPS * D_STATE:].astype(BF16)

    ri = lax.broadcasted_iota(I32, (q, q), 0)
    ci = lax.broadcasted_iota(I32, (q, q), 1)
    causal = ri >= ci
    x = sm_scr[:, SM_DT:SM_DT + SSD_HEADS] + dtb_ref[...]
    dt = jnp.maximum(x, 0.0) + jnp.log1p(jnp.exp(-jnp.abs(x)))
    dt = jnp.where(lax.broadcasted_iota(I32, (q, SSD_HEADS), 0) < rows, dt, 0.0)
    a = -jnp.exp(alog_ref[...])
    acum = _select_dot(causal.astype(BF16), dt * a)
    acum_t = jnp.concatenate([acum, jnp.zeros((q, LANES - SSD_HEADS), F32)], axis=1).T

    expand = (lax.broadcasted_iota(I32, (SSD_HEADS, D_SSD), 1) // SSD_HEAD_DIM
              == lax.broadcasted_iota(I32, (SSD_HEADS, D_SSD), 0)).astype(BF16)
    expand_t = (lax.broadcasted_iota(I32, (D_SSD, SSD_HEADS), 0) // SSD_HEAD_DIM
                == lax.broadcasted_iota(I32, (D_SSD, SSD_HEADS), 1)).astype(BF16)
    acum_x = _select_dot(acum, expand)
    xd = xs * _select_dot(dt, expand)
    xdw_t = (xd * jnp.exp(acum_x[q - 1:q, :] - acum_x)).T.astype(BF16)
    xd = xd.astype(BF16)
    state_decay = jnp.exp(_select_dot(expand_t, jnp.broadcast_to(acum_t[0:SSD_HEADS, q - 1:q], (SSD_HEADS, LANES))))

    for g in range(SSD_GROUPS):
        ns = slice(g * D_STATE, (g + 1) * D_STATE)
        hs = slice(g * heads_per_group * SSD_HEAD_DIM, (g + 1) * heads_per_group * SSD_HEAD_DIM)
        cb = _dot_nt(cm[:, ns], bm[:, ns])
        for e in range(heads_per_group):
            h = g * heads_per_group + e
            ps = slice(h * SSD_HEAD_DIM, (h + 1) * SSD_HEAD_DIM)
            seg = acum[:, h:h + 1] - acum_t[h:h + 1, :]
            lmat = jnp.exp(jnp.where(causal, seg, -jnp.inf))
            yd_scr[:, ps] = _dot((cb * lmat).astype(BF16), xd[:, ps])
        h_old = h_scr[hs, :]
        yd_scr[:, hs] += _dot_nt(cm[:, ns], h_old.astype(BF16)) * jnp.exp(acum_x[:, hs])
        h_scr[hs, :] = h_old * state_decay[hs, :] + _dot(xdw_t[hs, :], bm[:, ns])

    hnew_ref[0] = h_scr[...]
    y = yd_scr[0:rows, :] + dvec_ref[...] * xs[0:rows, :]
    zz = z_ref[...]
    y = y * (zz * jax.nn.sigmoid(zz))
    width = D_SSD // SSD_GROUPS
    for g in range(SSD_GROUPS):
        cs = slice(g * width, (g + 1) * width)
        y_ref[:, cs] = _rms(y[:, cs], g_ref[:, cs])


def _ssd(proj, conv_prev, ssm_prev, conv_w, conv_b, a_log, dt_bias, d_vec, g_ssd, *, batch, seq):
    rows = min(CHUNK, seq)
    nc = seq // rows
    full = lambda *shape: pl.BlockSpec(shape, lambda b, c: (0,) * len(shape))
    return pl.pallas_call(
        functools.partial(_ssd_kernel, rows=rows),
        grid=(batch, nc),
        in_specs=[
            pl.BlockSpec((rows, CONV_DIM), lambda b, c: (b * nc + c, COL_XBC // CONV_DIM)),
            pl.BlockSpec((rows, D_SSD), lambda b, c: (b * nc + c, COL_Z // D_SSD)),
            pl.BlockSpec((rows, LANES), lambda b, c: (b * nc + c, COL_SMALL // LANES)),
            pl.BlockSpec((1, CONV_WIDTH - 1, CONV_DIM), lambda b, c: (b, 0, 0)),
            pl.BlockSpec((1, D_SSD, D_STATE), lambda b, c: (b, 0, 0)),
            full(CONV_WIDTH, CONV_DIM), full(1, CONV_DIM), full(1, SSD_HEADS), full(1, SSD_HEADS),
            full(1, D_SSD), full(1, D_SSD),
        ],
        out_specs=[
            pl.BlockSpec((rows, D_SSD), lambda b, c: (b * nc + c, 0)),
            pl.BlockSpec((1, CONV_WIDTH - 1, CONV_DIM), lambda b, c: (b, 0, 0)),
            pl.BlockSpec((1, D_SSD, D_STATE), lambda b, c: (b, 0, 0)),
        ],
        out_shape=[
            jax.ShapeDtypeStruct((batch * seq, D_SSD), F32),
            jax.ShapeDtypeStruct((batch, CONV_WIDTH - 1, CONV_DIM), F32),
            jax.ShapeDtypeStruct((batch, D_SSD, D_STATE), F32),
        ],
        scratch_shapes=[
            pltpu.VMEM((CHUNK + TAIL, CONV_DIM), F32),
            pltpu.VMEM((CHUNK, LANES), F32),
            pltpu.VMEM((D_SSD, D_STATE), F32),
            pltpu.VMEM((CHUNK, D_SSD), F32),
        ],
        compiler_params=_params(("parallel", "arbitrary")),
        name="ssd",
    )(proj, proj, proj, conv_prev, ssm_prev, conv_w, conv_b, a_log, dt_bias, d_vec, g_ssd)


def _reorder_w_in(w_in):
    offs = [0]
    for s in IN_SPLITS:
        offs.append(offs[-1] + s)
    q, k, v, qi, ki, wi, z, xbc, dtr = [w_in[:, offs[n]:offs[n + 1]] for n in range(len(IN_SPLITS))]
    parts = [t.astype(BF16) for t in (q, qi, z, xbc, k, v, ki, wi, dtr)]
    parts.append(jnp.zeros((w_in.shape[0], N_IN - sum(IN_SPLITS)), BF16))
    return jnp.concatenate(parts, axis=1)


def kernel(x_prompt, x_sample, cache_k, cache_v, cache_kidx, state_conv, state_ssm, page_table, rel_bias, g_ffn1, w1_ffn1, w3_ffn1, w2_ffn1, g_mix, w_in, conv_w, conv_b, a_log, dt_bias, d_skip, g_ssd, w_out, g_ffn2, w1_ffn2, w3_ffn2, w2_ffn2, g_final):
    depth = w_in.shape[0]
    assert depth == 1
    l = 0
    bp, seq, _ = x_prompt.shape
    bs, nq, _ = x_sample.shape
    n_pages = page_table.shape[1]
    row = lambda t: t.reshape(1, -1)

    w1a, w3a, w2a = w1_ffn1[l].astype(BF16), w3_ffn1[l].astype(BF16), w2_ffn1[l].astype(BF16)
    w1b, w3b, w2b = w1_ffn2[l].astype(BF16), w3_ffn2[l].astype(BF16), w2_ffn2[l].astype(BF16)
    w_in_r = _reorder_w_in(w_in[l])
    w_out_a = w_out[l, :D_ATTN].astype(BF16)
    w_out_s = w_out[l, D_ATTN:].astype(BF16)
    d_vec = jnp.repeat(d_skip[l], SSD_HEAD_DIM).reshape(1, D_SSD)
    btab = _bias_table(rel_bias)
    gf = row(g_final)

    def trunk(x, attend, conv_prev, ssm_prev, batch, length):
        tm = min(x.shape[0], 1024)
        tf = 256 if tm > 512 else 512
        x = _ffn(x, row(g_ffn1[l]), w1a, w3a, w2a, gf, tm=tm, tf=tf, final_norm=False)
        proj = _in_proj(x, row(g_mix[l]), w_in_r, tm=tm)
        a_out = attend(proj)
        s_out, conv_new, ssm_new = _ssd(
            proj, conv_prev, ssm_prev.reshape(batch, D_SSD, D_STATE), conv_w[l], row(conv_b[l]), row(a_log[l]),
            row(dt_bias[l]), d_vec, row(g_ssd[l]), batch=batch, seq=length)
        x = _out_proj(x, a_out, s_out, w_out_a, w_out_s, tm=min(tm, 512))
        y = _ffn(x, row(g_ffn2[l]), w1b, w3b, w2b, gf, tm=tm, tf=tf, final_norm=True)
        k = proj[:, COL_K:COL_K + D_KV].reshape(1, batch, length, N_KV_HEADS, HEAD_DIM)
        v = proj[:, COL_V:COL_V + D_KV].reshape(1, batch, length, N_KV_HEADS, HEAD_DIM)
        ki = proj[:, COL_SMALL + SM_KI:COL_SMALL + SM_KI + IDX_DIM].reshape(1, batch, length, IDX_DIM)
        ssm_new = ssm_new.reshape(1, batch, SSD_HEADS, SSD_HEAD_DIM, D_STATE)
        return y.reshape(batch, length, D_MODEL), k, v, ki, conv_new[None], ssm_new

    y_p, k_p, v_p, ki_p, conv_p, ssm_p = trunk(
        x_prompt.reshape(bp * seq, D_MODEL),
        functools.partial(_dsa_prompt, btab=btab, batch=bp, seq=seq),
        jnp.zeros((bp, CONV_WIDTH - 1, CONV_DIM), F32), jnp.zeros((bp, SSD_HEADS, SSD_HEAD_DIM, D_STATE), F32),
        bp, seq)

    pool_k = cache_k[l].reshape(-1, HEAD_DIM)
    pool_v = cache_v[l].reshape(-1, HEAD_DIM)
    pool_kidx = jnp.swapaxes(cache_kidx[l], 1, 2)

    def sample_attend(proj):
        sk_past, sk_new = _sample_score(page_table, proj, pool_kidx, batch=bs, nq=nq, n_pages=n_pages)
        thr, lim = _sample_select(sk_past.reshape(bs * nq, -1), sk_new.reshape(bs * nq, LANES),
                                  rows_per_step=128)
        return _sample_attn(page_table, proj, sk_past, sk_new, thr, lim, btab, pool_k, pool_v,
                            batch=bs, nq=nq, n_pages=n_pages)

    y_s, k_s, v_s, ki_s, conv_s, ssm_s = trunk(
        x_sample.reshape(bs * nq, D_MODEL), sample_attend, state_conv[l], state_ssm[l], bs, nq)

    return (y_p, y_s, k_p, v_p, ki_p, conv_p, ssm_p, k_s, v_s, ki_s, conv_s, ssm_s)
```

```python
import functools
import math

import jax
import jax.numpy as jnp
from jax import lax
from jax.experimental import pallas as pl
from jax.experimental.pallas import tpu as pltpu

F32 = jnp.float32
BF16 = jnp.bfloat16
I32 = jnp.int32

D_MODEL = 2048
PAGE_SIZE = 128
N_HEADS = 8
HEAD_DIM = 128
N_KV_HEADS = 2
GQA_GROUP = N_HEADS // N_KV_HEADS
IDX_HEADS = 16
IDX_DIM = 64
TOPK = 256
N_BUCKETS = 32
MAX_DISTANCE = 128
SSD_HEADS = 16
SSD_HEAD_DIM = 64
SSD_GROUPS = 2
D_STATE = 128
CONV_WIDTH = 4
CHUNK = 128
D_ATTN = N_HEADS * HEAD_DIM
D_SSD = SSD_HEADS * SSD_HEAD_DIM
D_KV = N_KV_HEADS * HEAD_DIM
CONV_DIM = D_SSD + 2 * SSD_GROUPS * D_STATE
D_FF = 5632
EPS = 1e-6
IN_SPLITS = (D_ATTN, D_KV, D_KV, IDX_HEADS * IDX_DIM, IDX_DIM, IDX_HEADS, D_SSD, CONV_DIM, SSD_HEADS)

LANES = 128
COL_Q = 0
COL_QI = COL_Q + D_ATTN
COL_Z = COL_QI + IDX_HEADS * IDX_DIM
COL_XBC = COL_Z + D_SSD
COL_K = COL_XBC + CONV_DIM
COL_V = COL_K + D_KV
COL_SMALL = COL_V + D_KV
SM_KI = 0
SM_WI = SM_KI + IDX_DIM
SM_DT = SM_WI + IDX_HEADS
IN_TILE = 768
N_IN = ((COL_SMALL + LANES + IN_TILE - 1) // IN_TILE) * IN_TILE

INT_MIN = -(2 ** 31)
NEG_BIG = -1e30
VMEM_LIMIT = 56 * 1024 * 1024


def _params(sem):
    return pltpu.CompilerParams(dimension_semantics=sem, vmem_limit_bytes=VMEM_LIMIT)


def _rms(x, g):
    return x * lax.rsqrt(jnp.mean(x * x, axis=-1, keepdims=True) + EPS) * g


def _dot(a, b):
    return jnp.dot(a, b, preferred_element_type=F32)


def _dot_nt(a, b):
    return lax.dot_general(a, b, (((1,), (1,)), ((), ())), preferred_element_type=F32)


def _split3(x):
    hi = x.astype(BF16)
    rest = x - hi.astype(F32)
    mid = rest.astype(BF16)
    return hi, mid, (rest - mid.astype(F32)).astype(BF16)


def _select_dot(a, b):
    if a.dtype == BF16:
        return sum(_dot(a, part) for part in _split3(b))
    return sum(_dot(part, b) for part in _split3(a))


def _sort_key(x):
    bits = lax.bitcast_convert_type(x + 0.0, I32)
    return bits ^ ((bits >> 31) & 0x7FFFFFFF)


def _ffn_kernel(x_ref, g_ref, w1_ref, w3_ref, w2_ref, gf_ref, o_ref, h_scr, *, final_norm):
    f = pl.program_id(1)

    @pl.when(f == 0)
    def _():
        h_scr[...] = _rms(x_ref[...], g_ref[...]).astype(BF16)
        o_ref[...] = jnp.zeros_like(o_ref)

    h = h_scr[...]
    a = _dot(h, w1_ref[...].astype(BF16))
    b = _dot(h, w3_ref[...].astype(BF16))
    u = (a * jax.nn.sigmoid(a) * b).astype(BF16)
    o_ref[...] += _dot(u, w2_ref[...].astype(BF16))

    @pl.when(f == pl.num_programs(1) - 1)
    def _():
        y = x_ref[...] + 0.5 * o_ref[...]
        if final_norm:
            y = _rms(y, gf_ref[...])
        o_ref[...] = y


def _ffn(x, g, w1, w3, w2, gf, *, tm, tf, final_norm):
    t = x.shape[0]
    return pl.pallas_call(
        functools.partial(_ffn_kernel, final_norm=final_norm),
        grid=(t // tm, D_FF // tf),
        in_specs=[
            pl.BlockSpec((tm, D_MODEL), lambda i, f: (i, 0), pipeline_mode=pl.Buffered(1)),
            pl.BlockSpec((1, D_MODEL), lambda i, f: (0, 0)),
            pl.BlockSpec((D_MODEL, tf), lambda i, f: (0, f)),
            pl.BlockSpec((D_MODEL, tf), lambda i, f: (0, f)),
            pl.BlockSpec((tf, D_MODEL), lambda i, f: (f, 0)),
            pl.BlockSpec((1, D_MODEL), lambda i, f: (0, 0)),
        ],
        out_specs=pl.BlockSpec((tm, D_MODEL), lambda i, f: (i, 0)),
        out_shape=jax.ShapeDtypeStruct((t, D_MODEL), F32),
        scratch_shapes=[pltpu.VMEM((tm, D_MODEL), BF16)],
        compiler_params=_params(("parallel", "arbitrary")),
        name="ffn",
    )(x, g, w1, w3, w2, gf)


def _in_proj_kernel(x_ref, g_ref, w_ref, o_ref, h_scr):
    @pl.when(pl.program_id(1) == 0)
    def _():
        h_scr[...] = _rms(x_ref[...], g_ref[...]).astype(BF16)

    o_ref[...] = _dot(h_scr[...], w_ref[...])


def _in_proj(x, g, w, *, tm):
    t = x.shape[0]
    return pl.pallas_call(
        _in_proj_kernel,
        grid=(t // tm, N_IN // IN_TILE),
        in_specs=[
            pl.BlockSpec((tm, D_MODEL), lambda i, j: (i, 0)),
            pl.BlockSpec((1, D_MODEL), lambda i, j: (0, 0)),
            pl.BlockSpec((D_MODEL, IN_TILE), lambda i, j: (0, j)),
        ],
        out_specs=pl.BlockSpec((tm, IN_TILE), lambda i, j: (i, j)),
        out_shape=jax.ShapeDtypeStruct((t, N_IN), F32),
        scratch_shapes=[pltpu.VMEM((tm, D_MODEL), BF16)],
        compiler_params=_params(("parallel", "arbitrary")),
        name="in_proj",
    )(x, g, w)


def _out_proj_kernel(x_ref, a_ref, s_ref, wa_ref, ws_ref, o_ref):
    acc = _dot(a_ref[...].astype(BF16), wa_ref[...])
    acc += _dot(s_ref[...].astype(BF16), ws_ref[...])
    o_ref[...] = x_ref[...] + acc


def _out_proj(x, a, s, wa, ws, *, tm):
    t = x.shape[0]
    return pl.pallas_call(
        _out_proj_kernel,
        grid=(t // tm,),
        in_specs=[
            pl.BlockSpec((tm, D_MODEL), lambda i: (i, 0)),
            pl.BlockSpec((tm, D_ATTN), lambda i: (i, 0)),
            pl.BlockSpec((tm, D_SSD), lambda i: (i, 0)),
            pl.BlockSpec((D_ATTN, D_MODEL), lambda i: (0, 0)),
            pl.BlockSpec((D_SSD, D_MODEL), lambda i: (0, 0)),
        ],
        out_specs=pl.BlockSpec((tm, D_MODEL), lambda i: (i, 0)),
        out_shape=jax.ShapeDtypeStruct((t, D_MODEL), F32),
        compiler_params=_params(("parallel",)),
        name="out_proj",
    )(x, a, s, wa, ws)


NEAR = 2 * LANES


def _bias_kernel(rb_ref, o_ref):
    r = lax.broadcasted_iota(I32, (LANES, NEAR), 0)
    c = lax.broadcasted_iota(I32, (LANES, NEAR), 1)
    n = jnp.maximum(r + LANES - c, 0)
    max_exact = N_BUCKETS // 2
    nf = jnp.maximum(n, 1).astype(F32)
    large = max_exact + (jnp.log(nf / max_exact) / math.log(MAX_DISTANCE / max_exact)
                         * (N_BUCKETS - max_exact)).astype(I32)
    bucket = jnp.where(n < max_exact, n, jnp.minimum(large, N_BUCKETS - 1))
    for h in range(N_HEADS):
        acc = jnp.zeros((LANES, NEAR), F32)
        for b in range(N_BUCKETS):
            acc = jnp.where(bucket == b, rb_ref[b, h], acc)
        o_ref[h] = acc


def _bias_table(rel_bias):
    return pl.pallas_call(
        _bias_kernel,
        in_specs=[pl.BlockSpec(memory_space=pltpu.SMEM)],
        out_specs=pl.BlockSpec(memory_space=pltpu.VMEM),
        out_shape=jax.ShapeDtypeStruct((N_HEADS, LANES, NEAR), F32),
        name="bias_table",
    )(rel_bias)


def _count(mask, axis=1):
    x = mask.astype(F32)
    if axis == 0:
        x = jnp.sum(x.reshape(x.shape[0] // 64, 64, x.shape[1]), axis=0)
    return jnp.sum(x, axis=axis, keepdims=True)


def _kth_largest(count_ge, shape):
    t = jnp.full(shape, INT_MIN, I32)
    t = jnp.where(count_ge(jnp.zeros(shape, I32)) >= TOPK, 0, t)

    def body(it, t):
        cand = t + lax.shift_left(jnp.int32(1), 30 - it)
        return jnp.where(count_ge(cand) >= TOPK, cand, t)

    return lax.fori_loop(0, 31, body, t)


def _tie_limit(count_eq_below, need, shape, idx_bits):
    def body(it, m):
        cand = m + lax.shift_left(jnp.int32(1), idx_bits - 1 - it)
        return jnp.where(count_eq_below(cand) < need, cand, m)

    return lax.fori_loop(0, idx_bits, body, jnp.zeros(shape, I32))


CAUSAL_VARIANTS = 4


def _dsa_prompt_block(width, near_tiles, i, q_ref, qi_ref, k_ref, v_ref, sm_ref, bt_ref, o_ref, sk_scr, neg_scr):
    tq = LANES
    q0 = pl.multiple_of(i * LANES, LANES)
    shape = (1, tq)

    ki = sm_ref[0:width, SM_KI:SM_KI + IDX_DIM].astype(BF16)
    wi_t = sm_ref[pl.ds(q0, tq), :].T[SM_WI:SM_WI + IDX_HEADS, :] * (IDX_HEADS * IDX_DIM) ** -0.5
    score = jnp.zeros((width, tq), F32)
    for h in range(IDX_HEADS):
        qh = qi_ref[:, h * IDX_DIM:(h + 1) * IDX_DIM].astype(BF16)
        score = score + wi_t[h:h + 1, :] * jnp.maximum(_dot_nt(ki, qh), 0.0)

    key = lax.broadcasted_iota(I32, (width, tq), 0)
    pos = lax.broadcasted_iota(I32, (width, tq), 1) + q0
    sk_scr[0:width, :] = jnp.where(key <= pos, _sort_key(score), INT_MIN)

    def sk():
        return sk_scr[0:width, :]

    thr = _kth_largest(lambda c: _count(sk() >= c, 0), shape)
    excess = (_count(sk() >= thr, 0) > TOPK) & (thr > INT_MIN)
    any_excess = jnp.max(excess.astype(F32)) > 0.0

    @pl.when(jnp.logical_not(any_excess))
    def _():
        neg_scr[0:width, :] = jnp.where(sk() >= jnp.maximum(thr, INT_MIN + 1), 0.0, -jnp.inf)

    @pl.when(any_excess)
    def _():
        need = TOPK - _count(sk() > thr, 0)
        lim = _tie_limit(lambda m: _count((sk() == thr) & (key < m), 0), need, shape, (width - 1).bit_length())
        take = (sk() > thr) | ((sk() == thr) & (key <= lim) & (sk() > INT_MIN))
        neg_scr[0:width, :] = jnp.where(take, 0.0, -jnp.inf)

    neg = neg_scr[0:width, :].T
    neg = jnp.concatenate([neg] * GQA_GROUP, axis=0)

    far_w = max(width - near_tiles * LANES, 0)
    near_w = width - far_w
    tile = lax.broadcasted_iota(I32, (tq, near_w), 1) // LANES + far_w // LANES
    log2e = math.log2(math.e)
    scale = HEAD_DIM ** -0.5 * log2e
    for g in range(N_KV_HEADS):
        heads = range(g * GQA_GROUP, (g + 1) * GQA_GROUP)
        kg = k_ref[0:width, g * HEAD_DIM:(g + 1) * HEAD_DIM].astype(BF16)
        vg = v_ref[0:width, g * HEAD_DIM:(g + 1) * HEAD_DIM].astype(BF16)
        qg = jnp.concatenate([q_ref[:, h * HEAD_DIM:(h + 1) * HEAD_DIM] for h in heads], axis=0).astype(BF16)
        near = jnp.concatenate([
            (jnp.where(tile == i, jnp.tile(bt_ref[h, :, LANES:NEAR], (1, near_w // LANES)),
                       jnp.where(tile == i - 1, jnp.tile(bt_ref[h, :, 0:LANES], (1, near_w // LANES)),
                                 bt_ref[h, 0:1, 0:1])) - bt_ref[h, 0:1, 0:1]) * log2e
            for h in heads], axis=0)
        qk = _dot_nt(qg, kg) * scale
        lg = jnp.concatenate([qk[:, :far_w], qk[:, far_w:] + near], axis=1) if far_w else qk + near
        lg = lg + neg
        m = jnp.max(lg, axis=1, keepdims=True)
        p = jnp.exp2(lg - m)
        l = jnp.sum(p, axis=1, keepdims=True)
        out = _dot(p.astype(BF16), vg) / l
        for n, h in enumerate(heads):
            o_ref[:, h * HEAD_DIM:(h + 1) * HEAD_DIM] = out[n * tq:(n + 1) * tq, :]


def _dsa_prompt_kernel(*refs, seq):
    i = pl.program_id(1)
    per = seq // LANES // CAUSAL_VARIANTS
    for v in range(CAUSAL_VARIANTS):
        @pl.when(i // per == v)
        def _(v=v):
            _dsa_prompt_block((v + 1) * per * LANES, per + 1, i, *refs)


def _dsa_prompt(proj, btab, *, batch, seq):
    nq = seq // LANES
    return pl.pallas_call(
        functools.partial(_dsa_prompt_kernel, seq=seq),
        grid=(batch, nq),
        in_specs=[
            pl.BlockSpec((LANES, D_ATTN), lambda b, i: (b * nq + i, COL_Q // D_ATTN)),
            pl.BlockSpec((LANES, IDX_HEADS * IDX_DIM), lambda b, i: (b * nq + i, COL_QI // (IDX_HEADS * IDX_DIM))),
            pl.BlockSpec((seq, D_KV), lambda b, i: (b, COL_K // D_KV)),
            pl.BlockSpec((seq, D_KV), lambda b, i: (b, COL_V // D_KV)),
            pl.BlockSpec((seq, LANES), lambda b, i: (b, COL_SMALL // LANES)),
            pl.BlockSpec((N_HEADS, LANES, NEAR), lambda b, i: (0, 0, 0)),
        ],
        out_specs=pl.BlockSpec((LANES, D_ATTN), lambda b, i: (b * nq + i, 0)),
        out_shape=jax.ShapeDtypeStruct((batch * seq, D_ATTN), F32),
        scratch_shapes=[pltpu.VMEM((seq, LANES), I32), pltpu.VMEM((seq, LANES), F32)],
        compiler_params=_params(("parallel", "parallel")),
        name="dsa_prompt",
    )(proj, proj, proj, proj, proj, btab)


SCORE_PAGES = 64
ATTN_PAGES = 32


def _sample_queries(qi_ref):
    return jnp.concatenate(
        [qi_ref[:, h * IDX_DIM:(h + 1) * IDX_DIM] for h in range(IDX_HEADS)], axis=0).astype(BF16)


def _sample_scores(rel, wi, nq):
    sc = jnp.zeros((nq, rel.shape[1]), F32)
    for h in range(IDX_HEADS):
        sc = sc + wi[:, h:h + 1] * jnp.maximum(rel[h * nq:(h + 1) * nq, :], 0.0)
    return sc


def _page_copies(pt_ref, pools, bufs, sems, b, s, slot, pages, page_rows):
    copies = []
    for r in range(pages):
        row0 = pl.multiple_of(pt_ref[b, s * pages + r] * page_rows, page_rows)
        for pool, buf, sem in zip(pools, bufs, sems):
            copies.append(pltpu.make_async_copy(pool.at[pl.ds(row0, page_rows), :], buf.at[slot, r], sem.at[slot]))
    return copies


def _page_pipeline(pt_ref, pools, bufs, sems, pages, page_rows):
    b, s = pl.program_id(0), pl.program_id(1)
    steps = pl.num_programs(1)
    t = b * steps + s
    slot = lax.rem(t, 2)
    copies = functools.partial(_page_copies, pt_ref, pools, bufs, sems, pages=pages, page_rows=page_rows)

    @pl.when(t == 0)
    def _():
        for c in copies(b, s, slot):
            c.start()

    @pl.when(t + 1 < pl.num_programs(0) * steps)
    def _():
        wrap = s + 1 == steps
        for c in copies(jnp.where(wrap, b + 1, b), jnp.where(wrap, 0, s + 1), 1 - slot):
            c.start()

    for c in copies(b, s, slot):
        c.wait()
    return slot


def _sample_score_kernel(pt_ref, qi_ref, sm_ref, pool_ref, past_ref, new_ref, pages, sem, kbuf, *, nq):
    s = pl.program_id(1)
    slot = _page_pipeline(pt_ref, [pool_ref], [pages], [sem], SCORE_PAGES, IDX_DIM)
    qx = _sample_queries(qi_ref)
    wi = sm_ref[:, SM_WI:SM_WI + IDX_HEADS] * (IDX_HEADS * IDX_DIM) ** -0.5
    for r in range(SCORE_PAGES):
        kbuf[:, r * PAGE_SIZE:(r + 1) * PAGE_SIZE] = pages[slot, r].astype(BF16)
    past_ref[...] = _sort_key(_sample_scores(_dot(qx, kbuf[...]), wi, nq))

    @pl.when(s == pl.num_programs(1) - 1)
    def _():
        ki_new = jnp.concatenate(
            [sm_ref[:, SM_KI:SM_KI + IDX_DIM], jnp.zeros((LANES - nq, IDX_DIM), F32)], axis=0).astype(BF16)
        sc = _sample_scores(_dot_nt(qx, ki_new), wi, nq)
        j = lax.broadcasted_iota(I32, (nq, LANES), 1)
        t = lax.broadcasted_iota(I32, (nq, LANES), 0)
        new_ref[...] = jnp.where(j <= t, _sort_key(sc), INT_MIN)


def _sample_score(page_table, proj_s, pool_kidx_t, *, batch, nq, n_pages):
    steps = n_pages // SCORE_PAGES
    grid_spec = pltpu.PrefetchScalarGridSpec(
        num_scalar_prefetch=1,
        grid=(batch, steps),
        in_specs=[
            pl.BlockSpec((nq, IDX_HEADS * IDX_DIM), lambda b, s, pt: (b, COL_QI // (IDX_HEADS * IDX_DIM))),
            pl.BlockSpec((nq, LANES), lambda b, s, pt: (b, COL_SMALL // LANES)),
            pl.BlockSpec(memory_space=pl.ANY),
        ],
        out_specs=[
            pl.BlockSpec((None, nq, SCORE_PAGES * PAGE_SIZE), lambda b, s, pt: (b, 0, s)),
            pl.BlockSpec((None, nq, LANES), lambda b, s, pt: (b, 0, 0)),
        ],
        scratch_shapes=[
            pltpu.VMEM((2, SCORE_PAGES, IDX_DIM, PAGE_SIZE), F32),
            pltpu.SemaphoreType.DMA((2,)),
            pltpu.VMEM((IDX_DIM, SCORE_PAGES * PAGE_SIZE), BF16),
        ],
    )
    return pl.pallas_call(
        functools.partial(_sample_score_kernel, nq=nq),
        grid_spec=grid_spec,
        out_shape=[
            jax.ShapeDtypeStruct((batch, nq, n_pages * PAGE_SIZE), I32),
            jax.ShapeDtypeStruct((batch, nq, LANES), I32),
        ],
        compiler_params=_params(("arbitrary", "arbitrary")),
        name="sample_score",
    )(page_table, proj_s, proj_s, pool_kidx_t)


def _sample_select_kernel(past_ref, new_ref, thr_ref, lim_ref, *, past):
    rows = past_ref.shape[0]
    colp = lax.broadcasted_iota(I32, (rows, past), 1)
    coln = lax.broadcasted_iota(I32, (rows, LANES), 1) + past

    def count_ge(c):
        return _count(past_ref[...] >= c) + _count(new_ref[...] >= c)

    thr = _kth_largest(count_ge, (rows, 1))
    thr_ref[...] = jnp.broadcast_to(thr, (rows, LANES))
    lim_ref[...] = jnp.full((rows, LANES), 2 ** 31 - 1, I32)
    any_excess = jnp.max((count_ge(thr) > TOPK).astype(F32)) > 0.0

    @pl.when(any_excess)
    def _():
        need = TOPK - (_count(past_ref[...] > thr) + _count(new_ref[...] > thr))

        def count_eq_below(m):
            return (_count((past_ref[...] == thr) & (colp < m))
                    + _count((new_ref[...] == thr) & (coln < m)))

        lim = _tie_limit(count_eq_below, need, (rows, 1), (past + LANES - 1).bit_length())
        lim_ref[...] = jnp.broadcast_to(lim, (rows, LANES))


def _sample_select(sk_past, sk_new, *, rows_per_step):
    rows, past = sk_past.shape
    return pl.pallas_call(
        functools.partial(_sample_select_kernel, past=past),
        grid=(rows // rows_per_step,),
        in_specs=[
            pl.BlockSpec((rows_per_step, past), lambda i: (i, 0)),
            pl.BlockSpec((rows_per_step, LANES), lambda i: (i, 0)),
        ],
        out_specs=[
            pl.BlockSpec((rows_per_step, LANES), lambda i: (i, 0)),
            pl.BlockSpec((rows_per_step, LANES), lambda i: (i, 0)),
        ],
        out_shape=[jax.ShapeDtypeStruct((rows, LANES), I32)] * 2,
        compiler_params=_params(("parallel",)),
        name="sample_select",
    )(sk_past, sk_new)


def _sample_attn_kernel(pt_ref, q_ref, kn_ref, vn_ref, skp_ref, skn_ref, thr_ref, lim_ref, bt_ref, poolk_ref,
                        poolv_ref, o_ref, kpages, vpages, ksem, vsem, kbuf, vbuf, m_scr, l_scr, acc_scr, *,
                        nq, past):
    slot = _page_pipeline(pt_ref, [poolk_ref, poolv_ref], [kpages, vpages], [ksem, vsem], ATTN_PAGES,
                          PAGE_SIZE * N_KV_HEADS)
    s = pl.program_id(1)
    last = pl.num_programs(1) - 1
    width = ATTN_PAGES * PAGE_SIZE
    grows = GQA_GROUP * nq
    scale = HEAD_DIM ** -0.5

    @pl.when(s == 0)
    def _():
        m_scr[...] = jnp.full_like(m_scr, NEG_BIG)
        l_scr[...] = jnp.zeros_like(l_scr)
        acc_scr[...] = jnp.zeros_like(acc_scr)

    thr = thr_ref[:, 0:1]
    lim = lim_ref[:, 0:1]

    def select(sk, idx):
        sel = (sk > thr) | ((sk == thr) & (idx <= lim))
        return jnp.concatenate([sel] * GQA_GROUP, axis=0)

    def far_bias(g):
        return jnp.concatenate(
            [jnp.broadcast_to(bt_ref[g * GQA_GROUP + hh, 0:1, 0:1], (nq, 1)) for hh in range(GQA_GROUP)], axis=0)

    def near_bias(g, lo, hi):
        return jnp.concatenate([bt_ref[g * GQA_GROUP + hh, 0:nq, lo:hi] for hh in range(GQA_GROUP)], axis=0)

    def group_queries(g):
        return jnp.concatenate(
            [q_ref[:, (g * GQA_GROUP + hh) * HEAD_DIM:(g * GQA_GROUP + hh + 1) * HEAD_DIM]
             for hh in range(GQA_GROUP)], axis=0).astype(BF16)

    def accumulate(g, lg, sel, vals):
        rs = slice(g * grows, (g + 1) * grows)
        m_old = m_scr[rs, :]
        m_new = jnp.maximum(m_old, jnp.max(jnp.where(sel, lg, NEG_BIG), axis=1, keepdims=True))
        p = jnp.where(sel, jnp.exp(lg - m_new), 0.0)
        alpha = jnp.exp(m_old - m_new)
        l_scr[rs, :] = alpha * l_scr[rs, :] + jnp.sum(p, axis=1, keepdims=True)
        acc_scr[rs, :] = alpha * acc_scr[rs, :] + _dot(p.astype(BF16), vals)
        m_scr[rs, :] = m_new

    for r in range(ATTN_PAGES):
        for g in range(N_KV_HEADS):
            rows = pl.ds(g, PAGE_SIZE, stride=N_KV_HEADS)
            kbuf[g, r * PAGE_SIZE:(r + 1) * PAGE_SIZE, :] = kpages[slot, r, rows, :].astype(BF16)
            vbuf[g, r * PAGE_SIZE:(r + 1) * PAGE_SIZE, :] = vpages[slot, r, rows, :].astype(BF16)

    col = lax.broadcasted_iota(I32, (nq, width), 1)
    sel = select(skp_ref[...], col + s * width)
    colg = lax.broadcasted_iota(I32, (grows, width), 1)
    in_near = (s == last) & (colg >= width - PAGE_SIZE)
    for g in range(N_KV_HEADS):
        near = jnp.tile(near_bias(g, 0, LANES), (1, ATTN_PAGES))
        bias = jnp.where(in_near, near, far_bias(g))
        lg = _dot_nt(group_queries(g), kbuf[g]) * scale + bias
        accumulate(g, lg, sel, vbuf[g])

    @pl.when(s == last)
    def _():
        pad = jnp.zeros((LANES - nq, D_KV), F32)
        kn = jnp.concatenate([kn_ref[...], pad], axis=0).astype(BF16)
        vn = jnp.concatenate([vn_ref[...], pad], axis=0).astype(BF16)
        coln = lax.broadcasted_iota(I32, (nq, LANES), 1) + past
        seln = select(skn_ref[...], coln)
        for g in range(N_KV_HEADS):
            cs = slice(g * HEAD_DIM, (g + 1) * HEAD_DIM)
            lg = _dot_nt(group_queries(g), kn[:, cs]) * scale + near_bias(g, LANES, NEAR)
            accumulate(g, lg, seln, vn[:, cs])
        out = acc_scr[...] / l_scr[...]
        for h in range(N_HEADS):
            o_ref[:, h * HEAD_DIM:(h + 1) * HEAD_DIM] = out[h * nq:(h + 1) * nq, :]


def _sample_attn(page_table, proj_s, sk_past, sk_new, thr, lim, btab, pool_k, pool_v, *, batch, nq, n_pages):
    steps = n_pages // ATTN_PAGES
    width = ATTN_PAGES * PAGE_SIZE
    past = n_pages * PAGE_SIZE
    page_buf = pltpu.VMEM((2, ATTN_PAGES, PAGE_SIZE * N_KV_HEADS, HEAD_DIM), F32)
    grid_spec = pltpu.PrefetchScalarGridSpec(
        num_scalar_prefetch=1,
        grid=(batch, steps),
        in_specs=[
            pl.BlockSpec((nq, D_ATTN), lambda b, s, pt: (b, COL_Q // D_ATTN)),
            pl.BlockSpec((nq, D_KV), lambda b, s, pt: (b, COL_K // D_KV)),
            pl.BlockSpec((nq, D_KV), lambda b, s, pt: (b, COL_V // D_KV)),
            pl.BlockSpec((None, nq, width), lambda b, s, pt: (b, 0, s)),
            pl.BlockSpec((None, nq, LANES), lambda b, s, pt: (b, 0, 0)),
            pl.BlockSpec((nq, LANES), lambda b, s, pt: (b, 0)),
            pl.BlockSpec((nq, LANES), lambda b, s, pt: (b, 0)),
            pl.BlockSpec((N_HEADS, LANES, NEAR), lambda b, s, pt: (0, 0, 0)),
            pl.BlockSpec(memory_space=pl.ANY),
            pl.BlockSpec(memory_space=pl.ANY),
        ],
        out_specs=pl.BlockSpec((nq, D_ATTN), lambda b, s, pt: (b, 0)),
        scratch_shapes=[
            page_buf, page_buf, pltpu.SemaphoreType.DMA((2,)), pltpu.SemaphoreType.DMA((2,)),
            pltpu.VMEM((N_KV_HEADS, width, HEAD_DIM), BF16),
            pltpu.VMEM((N_KV_HEADS, width, HEAD_DIM), BF16),
            pltpu.VMEM((N_HEADS * nq, 1), F32),
            pltpu.VMEM((N_HEADS * nq, 1), F32),
            pltpu.VMEM((N_HEADS * nq, HEAD_DIM), F32),
        ],
    )
    return pl.pallas_call(
        functools.partial(_sample_attn_kernel, nq=nq, past=past),
        grid_spec=grid_spec,
        out_shape=jax.ShapeDtypeStruct((batch * nq, D_ATTN), F32),
        compiler_params=_params(("arbitrary", "arbitrary")),
        name="sample_attn",
    )(page_table, proj_s, proj_s, proj_s, sk_past, sk_new, thr, lim, btab, pool_k, pool_v)


TAIL = 8


def _ssd_kernel(xbc_ref, z_ref, sm_ref, cprev_ref, hprev_ref, cw_ref, cb_ref, alog_ref, dtb_ref, dvec_ref,
                g_ref, y_ref, cnew_ref, hnew_ref, xpad_scr, sm_scr, h_scr, yd_scr, *, rows):
    c = pl.program_id(1)
    q = CHUNK
    keep = CONV_WIDTH - 1
    heads_per_group = SSD_HEADS // SSD_GROUPS

    @pl.when(c == 0)
    def _():
        xpad_scr[...] = jnp.zeros_like(xpad_scr)
        sm_scr[...] = jnp.zeros_like(sm_scr)
        xpad_scr[TAIL - keep:TAIL, :] = cprev_ref[0]
        h_scr[...] = hprev_ref[0]

    xpad_scr[TAIL:TAIL + rows, :] = xbc_ref[...]
    sm_scr[0:rows, :] = sm_ref[...]
    conv = sum(xpad_scr[TAIL - keep + k:TAIL - keep + k + q, :] * cw_ref[k:k + 1, :]
               for k in range(CONV_WIDTH)) + cb_ref[...]
    new_tail = xpad_scr[rows + TAIL - keep:rows + TAIL, :]
    cnew_ref[0] = new_tail
    xpad_scr[TAIL - keep:TAIL, :] = new_tail

    xc = conv * jax.nn.sigmoid(conv)
    xs = xc[:, :D_SSD]
    bm = xc[:, D_SSD:D_SSD + SSD_GROUPS * D_STATE].astype(BF16)
    cm = xc[:, D_SSD + SSD_GROUPS * D_STATE:].astype(BF16)

    ri = lax.broadcasted_iota(I32, (q, q), 0)
    ci = lax.broadcasted_iota(I32, (q, q), 1)
    causal = ri >= ci
    x = sm_scr[:, SM_DT:SM_DT + SSD_HEADS] + dtb_ref[...]
    dt = jnp.maximum(x, 0.0) + jnp.log1p(jnp.exp(-jnp.abs(x)))
    dt = jnp.where(lax.broadcasted_iota(I32, (q, SSD_HEADS), 0) < rows, dt, 0.0)
    a = -jnp.exp(alog_ref[...])
    acum = _select_dot(causal.astype(BF16), dt * a)
    acum_t = jnp.concatenate([acum, jnp.zeros((q, LANES - SSD_HEADS), F32)], axis=1).T

    expand = (lax.broadcasted_iota(I32, (SSD_HEADS, D_SSD), 1) // SSD_HEAD_DIM
              == lax.broadcasted_iota(I32, (SSD_HEADS, D_SSD), 0)).astype(BF16)
    expand_t = (lax.broadcasted_iota(I32, (D_SSD, SSD_HEADS), 0) // SSD_HEAD_DIM
                == lax.broadcasted_iota(I32, (D_SSD, SSD_HEADS), 1)).astype(BF16)
    acum_x = _select_dot(acum, expand)
    xd = xs * _select_dot(dt, expand)
    xdw_t = (xd * jnp.exp(acum_x[q - 1:q, :] - acum_x)).T.astype(BF16)
    xd = xd.astype(BF16)
    state_decay = jnp.exp(_select_dot(expand_t, jnp.broadcast_to(acum_t[0:SSD_HEADS, q - 1:q], (SSD_HEADS, LANES))))

    for g in range(SSD_GROUPS):
        ns = slice(g * D_STATE, (g + 1) * D_STATE)
        hs = slice(g * heads_per_group * SSD_HEAD_DIM, (g + 1) * heads_per_group * SSD_HEAD_DIM)
        cb = _dot_nt(cm[:, ns], bm[:, ns])
        for e in range(heads_per_group):
            h = g * heads_per_group + e
            ps = slice(h * SSD_HEAD_DIM, (h + 1) * SSD_HEAD_DIM)
            seg = acum[:, h:h + 1] - acum_t[h:h + 1, :]
            lmat = jnp.exp(jnp.where(causal, seg, -jnp.inf))
            yd_scr[:, ps] = _dot((cb * lmat).astype(BF16), xd[:, ps])
        h_old = h_scr[hs, :]
        yd_scr[:, hs] += _dot_nt(cm[:, ns], h_old.astype(BF16)) * jnp.exp(acum_x[:, hs])
        h_scr[hs, :] = h_old * state_decay[hs, :] + _dot(xdw_t[hs, :], bm[:, ns])

    hnew_ref[0] = h_scr[...]
    y = yd_scr[0:rows, :] + dvec_ref[...] * xs[0:rows, :]
    zz = z_ref[...]
    y = y * (zz * jax.nn.sigmoid(zz))
    width = D_SSD // SSD_GROUPS
    for g in range(SSD_GROUPS):
        cs = slice(g * width, (g + 1) * width)
        y_ref[:, cs] = _rms(y[:, cs], g_ref[:, cs])


def _ssd(proj, conv_prev, ssm_prev, conv_w, conv_b, a_log, dt_bias, d_vec, g_ssd, *, batch, seq):
    rows = min(CHUNK, seq)
    nc = seq // rows
    full = lambda *shape: pl.BlockSpec(shape, lambda b, c: (0,) * len(shape))
    return pl.pallas_call(
        functools.partial(_ssd_kernel, rows=rows),
        grid=(batch, nc),
        in_specs=[
            pl.BlockSpec((rows, CONV_DIM), lambda b, c: (b * nc + c, COL_XBC // CONV_DIM)),
            pl.BlockSpec((rows, D_SSD), lambda b, c: (b * nc + c, COL_Z // D_SSD)),
            pl.BlockSpec((rows, LANES), lambda b, c: (b * nc + c, COL_SMALL // LANES)),
            pl.BlockSpec((1, CONV_WIDTH - 1, CONV_DIM), lambda b, c: (b, 0, 0)),
            pl.BlockSpec((1, D_SSD, D_STATE), lambda b, c: (b, 0, 0)),
            full(CONV_WIDTH, CONV_DIM), full(1, CONV_DIM), full(1, SSD_HEADS), full(1, SSD_HEADS),
            full(1, D_SSD), full(1, D_SSD),
        ],
        out_specs=[
            pl.BlockSpec((rows, D_SSD), lambda b, c: (b * nc + c, 0)),
            pl.BlockSpec((1, CONV_WIDTH - 1, CONV_DIM), lambda b, c: (b, 0, 0)),
            pl.BlockSpec((1, D_SSD, D_STATE), lambda b, c: (b, 0, 0)),
        ],
        out_shape=[
            jax.ShapeDtypeStruct((batch * seq, D_SSD), F32),
            jax.ShapeDtypeStruct((batch, CONV_WIDTH - 1, CONV_DIM), F32),
            jax.ShapeDtypeStruct((batch, D_SSD, D_STATE), F32),
        ],
        scratch_shapes=[
            pltpu.VMEM((CHUNK + TAIL, CONV_DIM), F32),
            pltpu.VMEM((CHUNK, LANES), F32),
            pltpu.VMEM((D_SSD, D_STATE), F32),
            pltpu.VMEM((CHUNK, D_SSD), F32),
        ],
        compiler_params=_params(("parallel", "arbitrary")),
        name="ssd",
    )(proj, proj, proj, conv_prev, ssm_prev, conv_w, conv_b, a_log, dt_bias, d_vec, g_ssd)


def _reorder_w_in(w_in):
    offs = [0]
    for s in IN_SPLITS:
        offs.append(offs[-1] + s)
    q, k, v, qi, ki, wi, z, xbc, dtr = [w_in[:, offs[n]:offs[n + 1]] for n in range(len(IN_SPLITS))]
    parts = [t.astype(BF16) for t in (q, qi, z, xbc, k, v, ki, wi, dtr)]
    parts.append(jnp.zeros((w_in.shape[0], N_IN - sum(IN_SPLITS)), BF16))
    return jnp.concatenate(parts, axis=1)


def kernel(x_prompt, x_sample, cache_k, cache_v, cache_kidx, state_conv, state_ssm, page_table, rel_bias, g_ffn1, w1_ffn1, w3_ffn1, w2_ffn1, g_mix, w_in, conv_w, conv_b, a_log, dt_bias, d_skip, g_ssd, w_out, g_ffn2, w1_ffn2, w3_ffn2, w2_ffn2, g_final):
    depth = w_in.shape[0]
    assert depth == 1
    l = 0
    bp, seq, _ = x_prompt.shape
    bs, nq, _ = x_sample.shape
    n_pages = page_table.shape[1]
    row = lambda t: t.reshape(1, -1)

    w1a, w3a, w2a = w1_ffn1[l], w3_ffn1[l], w2_ffn1[l]
    w1b, w3b, w2b = w1_ffn2[l], w3_ffn2[l], w2_ffn2[l]
    w_in_r = _reorder_w_in(w_in[l])
    w_out_a = w_out[l, :D_ATTN].astype(BF16)
    w_out_s = w_out[l, D_ATTN:].astype(BF16)
    d_vec = jnp.repeat(d_skip[l], SSD_HEAD_DIM).reshape(1, D_SSD)
    btab = _bias_table(rel_bias)
    gf = row(g_final)

    def trunk(x, attend, conv_prev, ssm_prev, batch, length):
        tm = min(x.shape[0], 1024)
        tf = 256 if tm > 512 else 512
        x = _ffn(x, row(g_ffn1[l]), w1a, w3a, w2a, gf, tm=tm, tf=tf, final_norm=False)
        proj = _in_proj(x, row(g_mix[l]), w_in_r, tm=tm)
        a_out = attend(proj)
        s_out, conv_new, ssm_new = _ssd(
            proj, conv_prev, ssm_prev.reshape(batch, D_SSD, D_STATE), conv_w[l], row(conv_b[l]), row(a_log[l]),
            row(dt_bias[l]), d_vec, row(g_ssd[l]), batch=batch, seq=length)
        x = _out_proj(x, a_out, s_out, w_out_a, w_out_s, tm=min(tm, 512))
        y = _ffn(x, row(g_ffn2[l]), w1b, w3b, w2b, gf, tm=tm, tf=tf, final_norm=True)
        k = proj[:, COL_K:COL_K + D_KV].reshape(1, batch, length, N_KV_HEADS, HEAD_DIM)
        v = proj[:, COL_V:COL_V + D_KV].reshape(1, batch, length, N_KV_HEADS, HEAD_DIM)
        ki = proj[:, COL_SMALL + SM_KI:COL_SMALL + SM_KI + IDX_DIM].reshape(1, batch, length, IDX_DIM)
        ssm_new = ssm_new.reshape(1, batch, SSD_HEADS, SSD_HEAD_DIM, D_STATE)
        return y.reshape(batch, length, D_MODEL), k, v, ki, conv_new[None], ssm_new

    y_p, k_p, v_p, ki_p, conv_p, ssm_p = trunk(
        x_prompt.reshape(bp * seq, D_MODEL),
        functools.partial(_dsa_prompt, btab=btab, batch=bp, seq=seq),
        jnp.zeros((bp, CONV_WIDTH - 1, CONV_DIM), F32), jnp.zeros((bp, SSD_HEADS, SSD_HEAD_DIM, D_STATE), F32),
        bp, seq)

    pool_k = cache_k[l].reshape(-1, HEAD_DIM)
    pool_v = cache_v[l].reshape(-1, HEAD_DIM)
    pool_kidx = jnp.swapaxes(cache_kidx[l], 1, 2).reshape(-1, PAGE_SIZE)

    def sample_attend(proj):
        sk_past, sk_new = _sample_score(page_table, proj, pool_kidx, batch=bs, nq=nq, n_pages=n_pages)
        thr, lim = _sample_select(sk_past.reshape(bs * nq, -1), sk_new.reshape(bs * nq, LANES),
                                  rows_per_step=128)
        return _sample_attn(page_table, proj, sk_past, sk_new, thr, lim, btab, pool_k, pool_v,
                            batch=bs, nq=nq, n_pages=n_pages)

    y_s, k_s, v_s, ki_s, conv_s, ssm_s = trunk(
        x_sample.reshape(bs * nq, D_MODEL), sample_attend, state_conv[l], state_ssm[l], bs, nq)

    return (y_p, y_s, k_p, v_p, ki_p, conv_p, ssm_p, k_s, v_s, ki_s, conv_s, ssm_s)
```

```python
import functools
import math

import jax
import jax.numpy as jnp
from jax import lax
from jax.experimental import pallas as pl
from jax.experimental.pallas import tpu as pltpu

F32 = jnp.float32
BF16 = jnp.bfloat16
I32 = jnp.int32

D_MODEL = 2048
PAGE_SIZE = 128
N_HEADS = 8
HEAD_DIM = 128
N_KV_HEADS = 2
GQA_GROUP = N_HEADS // N_KV_HEADS
IDX_HEADS = 16
IDX_DIM = 64
TOPK = 256
N_BUCKETS = 32
MAX_DISTANCE = 128
SSD_HEADS = 16
SSD_HEAD_DIM = 64
SSD_GROUPS = 2
D_STATE = 128
CONV_WIDTH = 4
CHUNK = 128
D_ATTN = N_HEADS * HEAD_DIM
D_SSD = SSD_HEADS * SSD_HEAD_DIM
D_KV = N_KV_HEADS * HEAD_DIM
CONV_DIM = D_SSD + 2 * SSD_GROUPS * D_STATE
D_FF = 5632
EPS = 1e-6
IN_SPLITS = (D_ATTN, D_KV, D_KV, IDX_HEADS * IDX_DIM, IDX_DIM, IDX_HEADS, D_SSD, CONV_DIM, SSD_HEADS)

LANES = 128
COL_Q = 0
COL_QI = COL_Q + D_ATTN
COL_Z = COL_QI + IDX_HEADS * IDX_DIM
COL_XBC = COL_Z + D_SSD
COL_K = COL_XBC + CONV_DIM
COL_V = COL_K + D_KV
COL_SMALL = COL_V + D_KV
SM_KI = 0
SM_WI = SM_KI + IDX_DIM
SM_DT = SM_WI + IDX_HEADS
IN_TILE = 768
N_IN = ((COL_SMALL + LANES + IN_TILE - 1) // IN_TILE) * IN_TILE

INT_MIN = -(2 ** 31)
NEG_BIG = -1e30
VMEM_LIMIT = 60000 * 1024


def _params(sem):
    return pltpu.CompilerParams(dimension_semantics=sem, vmem_limit_bytes=VMEM_LIMIT)


def _rms(x, g):
    return x * lax.rsqrt(jnp.mean(x * x, axis=-1, keepdims=True) + EPS) * g


def _dot(a, b):
    return jnp.dot(a, b, preferred_element_type=F32)


def _dot_nt(a, b):
    return lax.dot_general(a, b, (((1,), (1,)), ((), ())), preferred_element_type=F32)


def _split3(x):
    hi = x.astype(BF16)
    rest = x - hi.astype(F32)
    mid = rest.astype(BF16)
    return hi, mid, (rest - mid.astype(F32)).astype(BF16)


def _select_dot(a, b):
    if a.dtype == BF16:
        return sum(_dot(a, part) for part in _split3(b))
    return sum(_dot(part, b) for part in _split3(a))


def _sort_key(x):
    bits = lax.bitcast_convert_type(x + 0.0, I32)
    return bits ^ ((bits >> 31) & 0x7FFFFFFF)


def _ffn_kernel(x_ref, g_ref, gf_ref, w1_hbm, w3_hbm, w2_hbm, o_ref, h_scr, w1_buf, w3_buf, w2_buf, sem, *,
                tf, final_norm):
    i = pl.program_id(0)
    nf = D_FF // tf
    total = pl.num_programs(0) * nf

    def copies(f, slot):
        c0 = pl.multiple_of(f * tf, tf)
        return (pltpu.make_async_copy(w1_hbm.at[:, pl.ds(c0, tf)], w1_buf.at[slot], sem.at[0, slot]),
                pltpu.make_async_copy(w3_hbm.at[:, pl.ds(c0, tf)], w3_buf.at[slot], sem.at[1, slot]),
                pltpu.make_async_copy(w2_hbm.at[pl.ds(c0, tf), :], w2_buf.at[slot], sem.at[2, slot]))

    @pl.when(i == 0)
    def _():
        for c in copies(0, 0):
            c.start()

    h_scr[...] = _rms(x_ref[...], g_ref[...]).astype(BF16)
    o_ref[...] = jnp.zeros_like(o_ref)

    def body(f, carry):
        t = i * nf + f
        slot = lax.rem(t, 2)

        @pl.when(t + 1 < total)
        def _():
            for c in copies(jnp.where(f + 1 == nf, 0, f + 1), 1 - slot):
                c.start()

        for c in copies(f, slot):
            c.wait()
        h = h_scr[...]
        a = _dot(h, w1_buf[slot].astype(BF16))
        b = _dot(h, w3_buf[slot].astype(BF16))
        u = (a * jax.nn.sigmoid(a) * b).astype(BF16)
        o_ref[...] += _dot(u, w2_buf[slot].astype(BF16))
        return carry

    lax.fori_loop(0, nf, body, 0)
    y = x_ref[...] + 0.5 * o_ref[...]
    if final_norm:
        y = _rms(y, gf_ref[...])
    o_ref[...] = y


def _ffn(x, g, w1, w3, w2, gf, *, tm, tf, final_norm):
    t = x.shape[0]
    return pl.pallas_call(
        functools.partial(_ffn_kernel, tf=tf, final_norm=final_norm),
        grid=(t // tm,),
        in_specs=[
            pl.BlockSpec((tm, D_MODEL), lambda i: (i, 0)),
            pl.BlockSpec((1, D_MODEL), lambda i: (0, 0)),
            pl.BlockSpec((1, D_MODEL), lambda i: (0, 0)),
            pl.BlockSpec(memory_space=pl.ANY),
            pl.BlockSpec(memory_space=pl.ANY),
            pl.BlockSpec(memory_space=pl.ANY),
        ],
        out_specs=pl.BlockSpec((tm, D_MODEL), lambda i: (i, 0)),
        out_shape=jax.ShapeDtypeStruct((t, D_MODEL), F32),
        scratch_shapes=[
            pltpu.VMEM((tm, D_MODEL), BF16),
            pltpu.VMEM((2, D_MODEL, tf), F32),
            pltpu.VMEM((2, D_MODEL, tf), F32),
            pltpu.VMEM((2, tf, D_MODEL), F32),
            pltpu.SemaphoreType.DMA((3, 2)),
        ],
        compiler_params=_params(("arbitrary",)),
        name="ffn",
    )(x, g, gf, w1, w3, w2)


def _in_proj_kernel(x_ref, g_ref, w_ref, o_ref, h_scr):
    @pl.when(pl.program_id(1) == 0)
    def _():
        h_scr[...] = _rms(x_ref[...], g_ref[...]).astype(BF16)

    o_ref[...] = _dot(h_scr[...], w_ref[...])


def _in_proj(x, g, w, *, tm):
    t = x.shape[0]
    return pl.pallas_call(
        _in_proj_kernel,
        grid=(t // tm, N_IN // IN_TILE),
        in_specs=[
            pl.BlockSpec((tm, D_MODEL), lambda i, j: (i, 0)),
            pl.BlockSpec((1, D_MODEL), lambda i, j: (0, 0)),
            pl.BlockSpec((D_MODEL, IN_TILE), lambda i, j: (0, j)),
        ],
        out_specs=pl.BlockSpec((tm, IN_TILE), lambda i, j: (i, j)),
        out_shape=jax.ShapeDtypeStruct((t, N_IN), F32),
        scratch_shapes=[pltpu.VMEM((tm, D_MODEL), BF16)],
        compiler_params=_params(("parallel", "arbitrary")),
        name="in_proj",
    )(x, g, w)


def _out_proj_kernel(x_ref, a_ref, s_ref, wa_ref, ws_ref, o_ref):
    acc = _dot(a_ref[...].astype(BF16), wa_ref[...])
    acc += _dot(s_ref[...].astype(BF16), ws_ref[...])
    o_ref[...] = x_ref[...] + acc


def _out_proj(x, a, s, wa, ws, *, tm):
    t = x.shape[0]
    return pl.pallas_call(
        _out_proj_kernel,
        grid=(t // tm,),
        in_specs=[
            pl.BlockSpec((tm, D_MODEL), lambda i: (i, 0)),
            pl.BlockSpec((tm, D_ATTN), lambda i: (i, 0)),
            pl.BlockSpec((tm, D_SSD), lambda i: (i, 0)),
            pl.BlockSpec((D_ATTN, D_MODEL), lambda i: (0, 0)),
            pl.BlockSpec((D_SSD, D_MODEL), lambda i: (0, 0)),
        ],
        out_specs=pl.BlockSpec((tm, D_MODEL), lambda i: (i, 0)),
        out_shape=jax.ShapeDtypeStruct((t, D_MODEL), F32),
        compiler_params=_params(("parallel",)),
        name="out_proj",
    )(x, a, s, wa, ws)


NEAR = 2 * LANES


def _bias_kernel(rb_ref, o_ref):
    r = lax.broadcasted_iota(I32, (LANES, NEAR), 0)
    c = lax.broadcasted_iota(I32, (LANES, NEAR), 1)
    n = jnp.maximum(r + LANES - c, 0)
    max_exact = N_BUCKETS // 2
    nf = jnp.maximum(n, 1).astype(F32)
    large = max_exact + (jnp.log(nf / max_exact) / math.log(MAX_DISTANCE / max_exact)
                         * (N_BUCKETS - max_exact)).astype(I32)
    bucket = jnp.where(n < max_exact, n, jnp.minimum(large, N_BUCKETS - 1))
    for h in range(N_HEADS):
        acc = jnp.zeros((LANES, NEAR), F32)
        for b in range(N_BUCKETS):
            acc = jnp.where(bucket == b, rb_ref[b, h], acc)
        o_ref[h] = acc


def _bias_table(rel_bias):
    return pl.pallas_call(
        _bias_kernel,
        in_specs=[pl.BlockSpec(memory_space=pltpu.SMEM)],
        out_specs=pl.BlockSpec(memory_space=pltpu.VMEM),
        out_shape=jax.ShapeDtypeStruct((N_HEADS, LANES, NEAR), F32),
        name="bias_table",
    )(rel_bias)


def _count(mask, axis=1):
    x = mask.astype(F32)
    if axis == 0:
        x = jnp.sum(x.reshape(x.shape[0] // 64, 64, x.shape[1]), axis=0)
    return jnp.sum(x, axis=axis, keepdims=True)


def _kth_largest(count_ge, shape):
    t = jnp.full(shape, INT_MIN, I32)
    t = jnp.where(count_ge(jnp.zeros(shape, I32)) >= TOPK, 0, t)

    def body(it, t):
        cand = t + lax.shift_left(jnp.int32(1), 30 - it)
        return jnp.where(count_ge(cand) >= TOPK, cand, t)

    return lax.fori_loop(0, 31, body, t)


def _tie_limit(count_eq_below, need, shape, idx_bits):
    def body(it, m):
        cand = m + lax.shift_left(jnp.int32(1), idx_bits - 1 - it)
        return jnp.where(count_eq_below(cand) < need, cand, m)

    return lax.fori_loop(0, idx_bits, body, jnp.zeros(shape, I32))


CAUSAL_VARIANTS = 4


def _dsa_prompt_block(width, near_tiles, i, q_ref, qi_ref, k_ref, v_ref, sm_ref, bt_ref, o_ref, sk_scr, neg_scr):
    tq = LANES
    q0 = pl.multiple_of(i * LANES, LANES)
    shape = (1, tq)

    wi_t = sm_ref[pl.ds(q0, tq), :].T[SM_WI:SM_WI + IDX_HEADS, :] * (IDX_HEADS * IDX_DIM) ** -0.5
    head_group = 4
    qi_groups = [
        jnp.concatenate([qi_ref[:, h * IDX_DIM:(h + 1) * IDX_DIM] for h in range(hg, hg + head_group)],
                        axis=0).astype(BF16)
        for hg in range(0, IDX_HEADS, head_group)]
    chunk = 2 * LANES
    pos = lax.broadcasted_iota(I32, (chunk, tq), 1) + q0
    for c in range(0, width, chunk):
        ki = sm_ref[c:c + chunk, SM_KI:SM_KI + IDX_DIM].astype(BF16)
        score = jnp.zeros((chunk, tq), F32)
        for n, hg in enumerate(range(0, IDX_HEADS, head_group)):
            rel = _dot_nt(ki, qi_groups[n])
            for e in range(head_group):
                score = score + wi_t[hg + e:hg + e + 1, :] * jnp.maximum(rel[:, e * tq:(e + 1) * tq], 0.0)
        key_c = lax.broadcasted_iota(I32, (chunk, tq), 0) + c
        sk_scr[c:c + chunk, :] = jnp.where(key_c <= pos, _sort_key(score), INT_MIN)

    key = lax.broadcasted_iota(I32, (width, tq), 0)

    def sk():
        return sk_scr[0:width, :]

    thr = _kth_largest(lambda c: _count(sk() >= c, 0), shape)
    excess = (_count(sk() >= thr, 0) > TOPK) & (thr > INT_MIN)
    any_excess = jnp.max(excess.astype(F32)) > 0.0

    @pl.when(jnp.logical_not(any_excess))
    def _():
        neg_scr[0:width, :] = jnp.where(sk() >= jnp.maximum(thr, INT_MIN + 1), 0.0, -jnp.inf)

    @pl.when(any_excess)
    def _():
        need = TOPK - _count(sk() > thr, 0)
        lim = _tie_limit(lambda m: _count((sk() == thr) & (key < m), 0), need, shape, (width - 1).bit_length())
        take = (sk() > thr) | ((sk() == thr) & (key <= lim) & (sk() > INT_MIN))
        neg_scr[0:width, :] = jnp.where(take, 0.0, -jnp.inf)

    neg = neg_scr[0:width, :].T

    log2e = math.log2(math.e)
    scale = HEAD_DIM ** -0.5 * log2e
    tiles = width // LANES
    rows = GQA_GROUP * tq
    for g in range(N_KV_HEADS):
        heads = range(g * GQA_GROUP, (g + 1) * GQA_GROUP)
        kg = k_ref[0:width, g * HEAD_DIM:(g + 1) * HEAD_DIM].astype(BF16)
        vg = v_ref[0:width, g * HEAD_DIM:(g + 1) * HEAD_DIM].astype(BF16)
        qg = jnp.concatenate([q_ref[:, h * HEAD_DIM:(h + 1) * HEAD_DIM] for h in heads], axis=0).astype(BF16)
        far = jnp.stack([jnp.broadcast_to(bt_ref[h, 0:1, 0:1], (tq, LANES)) for h in heads])
        diag = (jnp.stack([bt_ref[h, :, LANES:NEAR] for h in heads]) - far) * log2e
        prev = (jnp.stack([bt_ref[h, :, 0:LANES] for h in heads]) - far) * log2e
        qk = _dot_nt(qg, kg)
        logits = []
        for t in range(tiles):
            lt = qk[:, t * LANES:(t + 1) * LANES].reshape(GQA_GROUP, tq, LANES) * scale
            lt = lt + neg[None, :, t * LANES:(t + 1) * LANES]
            if t >= tiles - near_tiles:
                lt = lt + jnp.where(i == t, diag, jnp.where(i - 1 == t, prev, 0.0))
            logits.append(lt)
        m = functools.reduce(jnp.maximum, logits)
        m = jnp.max(m, axis=2, keepdims=True)
        probs = [jnp.exp2(lt - m) for lt in logits]
        l = jnp.sum(functools.reduce(jnp.add, probs), axis=2, keepdims=True)
        p = jnp.concatenate([pt.reshape(rows, LANES).astype(BF16) for pt in probs], axis=1)
        out = _dot(p, vg) / l.reshape(rows, 1)
        for n, h in enumerate(heads):
            o_ref[:, h * HEAD_DIM:(h + 1) * HEAD_DIM] = out[n * tq:(n + 1) * tq, :]


def _dsa_prompt_kernel(*refs, seq):
    i = pl.program_id(1)
    per = seq // LANES // CAUSAL_VARIANTS
    for v in range(CAUSAL_VARIANTS):
        @pl.when(i // per == v)
        def _(v=v):
            _dsa_prompt_block((v + 1) * per * LANES, per + 1, i, *refs)


def _dsa_prompt(proj, btab, *, batch, seq):
    nq = seq // LANES
    return pl.pallas_call(
        functools.partial(_dsa_prompt_kernel, seq=seq),
        grid=(batch, nq),
        in_specs=[
            pl.BlockSpec((LANES, D_ATTN), lambda b, i: (b * nq + i, COL_Q // D_ATTN)),
            pl.BlockSpec((LANES, IDX_HEADS * IDX_DIM), lambda b, i: (b * nq + i, COL_QI // (IDX_HEADS * IDX_DIM))),
            pl.BlockSpec((seq, D_KV), lambda b, i: (b, COL_K // D_KV)),
            pl.BlockSpec((seq, D_KV), lambda b, i: (b, COL_V // D_KV)),
            pl.BlockSpec((seq, LANES), lambda b, i: (b, COL_SMALL // LANES)),
            pl.BlockSpec((N_HEADS, LANES, NEAR), lambda b, i: (0, 0, 0)),
        ],
        out_specs=pl.BlockSpec((LANES, D_ATTN), lambda b, i: (b * nq + i, 0)),
        out_shape=jax.ShapeDtypeStruct((batch * seq, D_ATTN), F32),
        scratch_shapes=[pltpu.VMEM((seq, LANES), I32), pltpu.VMEM((seq, LANES), F32)],
        compiler_params=_params(("parallel", "parallel")),
        name="dsa_prompt",
    )(proj, proj, proj, proj, proj, btab)


SCORE_PAGES = 64
ATTN_PAGES = 32


def _sample_queries(qi_ref):
    return jnp.concatenate(
        [qi_ref[:, h * IDX_DIM:(h + 1) * IDX_DIM] for h in range(IDX_HEADS)], axis=0).astype(BF16)


def _sample_scores(rel, wi, nq):
    sc = jnp.zeros((nq, rel.shape[1]), F32)
    for h in range(IDX_HEADS):
        sc = sc + wi[:, h:h + 1] * jnp.maximum(rel[h * nq:(h + 1) * nq, :], 0.0)
    return sc


def _page_copies(pt_ref, pools, bufs, sems, b, s, slot, pages, page_rows):
    copies = []
    for r in range(pages):
        row0 = pl.multiple_of(pt_ref[b, s * pages + r] * page_rows, page_rows)
        for pool, buf, sem in zip(pools, bufs, sems):
            copies.append(pltpu.make_async_copy(pool.at[pl.ds(row0, page_rows), :], buf.at[slot, r], sem.at[slot]))
    return copies


def _page_pipeline(pt_ref, pools, bufs, sems, pages, page_rows):
    b, s = pl.program_id(0), pl.program_id(1)
    steps = pl.num_programs(1)
    t = b * steps + s
    slot = lax.rem(t, 2)
    copies = functools.partial(_page_copies, pt_ref, pools, bufs, sems, pages=pages, page_rows=page_rows)

    @pl.when(t == 0)
    def _():
        for c in copies(b, s, slot):
            c.start()

    @pl.when(t + 1 < pl.num_programs(0) * steps)
    def _():
        wrap = s + 1 == steps
        for c in copies(jnp.where(wrap, b + 1, b), jnp.where(wrap, 0, s + 1), 1 - slot):
            c.start()

    for c in copies(b, s, slot):
        c.wait()
    return slot


def _sample_score_kernel(pt_ref, qi_ref, sm_ref, pool_ref, past_ref, new_ref, pages, sem, kbuf, *, nq):
    s = pl.program_id(1)
    slot = _page_pipeline(pt_ref, [pool_ref], [pages], [sem], SCORE_PAGES, IDX_DIM)
    qx = _sample_queries(qi_ref)
    wi = sm_ref[:, SM_WI:SM_WI + IDX_HEADS] * (IDX_HEADS * IDX_DIM) ** -0.5
    for r in range(SCORE_PAGES):
        kbuf[:, r * PAGE_SIZE:(r + 1) * PAGE_SIZE] = pages[slot, r].astype(BF16)
    past_ref[...] = _sort_key(_sample_scores(_dot(qx, kbuf[...]), wi, nq))

    @pl.when(s == pl.num_programs(1) - 1)
    def _():
        ki_new = jnp.concatenate(
            [sm_ref[:, SM_KI:SM_KI + IDX_DIM], jnp.zeros((LANES - nq, IDX_DIM), F32)], axis=0).astype(BF16)
        sc = _sample_scores(_dot_nt(qx, ki_new), wi, nq)
        j = lax.broadcasted_iota(I32, (nq, LANES), 1)
        t = lax.broadcasted_iota(I32, (nq, LANES), 0)
        new_ref[...] = jnp.where(j <= t, _sort_key(sc), INT_MIN)


def _sample_score(page_table, proj_s, pool_kidx_t, *, batch, nq, n_pages):
    steps = n_pages // SCORE_PAGES
    grid_spec = pltpu.PrefetchScalarGridSpec(
        num_scalar_prefetch=1,
        grid=(batch, steps),
        in_specs=[
            pl.BlockSpec((nq, IDX_HEADS * IDX_DIM), lambda b, s, pt: (b, COL_QI // (IDX_HEADS * IDX_DIM))),
            pl.BlockSpec((nq, LANES), lambda b, s, pt: (b, COL_SMALL // LANES)),
            pl.BlockSpec(memory_space=pl.ANY),
        ],
        out_specs=[
            pl.BlockSpec((None, nq, SCORE_PAGES * PAGE_SIZE), lambda b, s, pt: (b, 0, s)),
            pl.BlockSpec((None, nq, LANES), lambda b, s, pt: (b, 0, 0)),
        ],
        scratch_shapes=[
            pltpu.VMEM((2, SCORE_PAGES, IDX_DIM, PAGE_SIZE), F32),
            pltpu.SemaphoreType.DMA((2,)),
            pltpu.VMEM((IDX_DIM, SCORE_PAGES * PAGE_SIZE), BF16),
        ],
    )
    return pl.pallas_call(
        functools.partial(_sample_score_kernel, nq=nq),
        grid_spec=grid_spec,
        out_shape=[
            jax.ShapeDtypeStruct((batch, nq, n_pages * PAGE_SIZE), I32),
            jax.ShapeDtypeStruct((batch, nq, LANES), I32),
        ],
        compiler_params=_params(("arbitrary", "arbitrary")),
        name="sample_score",
    )(page_table, proj_s, proj_s, pool_kidx_t)


def _sample_select_kernel(past_ref, new_ref, thr_ref, lim_ref, *, past):
    rows = past_ref.shape[0]
    colp = lax.broadcasted_iota(I32, (rows, past), 1)
    coln = lax.broadcasted_iota(I32, (rows, LANES), 1) + past

    def count_ge(c):
        return _count(past_ref[...] >= c) + _count(new_ref[...] >= c)

    thr = _kth_largest(count_ge, (rows, 1))
    thr_ref[...] = jnp.broadcast_to(thr, (rows, LANES))
    lim_ref[...] = jnp.full((rows, LANES), 2 ** 31 - 1, I32)
    any_excess = jnp.max((count_ge(thr) > TOPK).astype(F32)) > 0.0

    @pl.when(any_excess)
    def _():
        need = TOPK - (_count(past_ref[...] > thr) + _count(new_ref[...] > thr))

        def count_eq_below(m):
            return (_count((past_ref[...] == thr) & (colp < m))
                    + _count((new_ref[...] == thr) & (coln < m)))

        lim = _tie_limit(count_eq_below, need, (rows, 1), (past + LANES - 1).bit_length())
        lim_ref[...] = jnp.broadcast_to(lim, (rows, LANES))


def _sample_select(sk_past, sk_new, *, rows_per_step):
    rows, past = sk_past.shape
    return pl.pallas_call(
        functools.partial(_sample_select_kernel, past=past),
        grid=(rows // rows_per_step,),
        in_specs=[
            pl.BlockSpec((rows_per_step, past), lambda i: (i, 0)),
            pl.BlockSpec((rows_per_step, LANES), lambda i: (i, 0)),
        ],
        out_specs=[
            pl.BlockSpec((rows_per_step, LANES), lambda i: (i, 0)),
            pl.BlockSpec((rows_per_step, LANES), lambda i: (i, 0)),
        ],
        out_shape=[jax.ShapeDtypeStruct((rows, LANES), I32)] * 2,
        compiler_params=_params(("parallel",)),
        name="sample_select",
    )(sk_past, sk_new)


def _sample_attn_kernel(pt_ref, q_ref, kn_ref, vn_ref, skp_ref, skn_ref, thr_ref, lim_ref, bt_ref, poolk_ref,
                        poolv_ref, o_ref, kpages, vpages, ksem, vsem, kbuf, vbuf, m_scr, l_scr, acc_scr, *,
                        nq, past):
    slot = _page_pipeline(pt_ref, [poolk_ref, poolv_ref], [kpages, vpages], [ksem, vsem], ATTN_PAGES,
                          PAGE_SIZE * N_KV_HEADS)
    s = pl.program_id(1)
    last = pl.num_programs(1) - 1
    width = ATTN_PAGES * PAGE_SIZE
    grows = GQA_GROUP * nq
    scale = HEAD_DIM ** -0.5

    @pl.when(s == 0)
    def _():
        m_scr[...] = jnp.full_like(m_scr, NEG_BIG)
        l_scr[...] = jnp.zeros_like(l_scr)
        acc_scr[...] = jnp.zeros_like(acc_scr)

    thr = thr_ref[:, 0:1]
    lim = lim_ref[:, 0:1]

    def select(sk, idx):
        sel = (sk > thr) | ((sk == thr) & (idx <= lim))
        return jnp.concatenate([sel] * GQA_GROUP, axis=0)

    def far_bias(g):
        return jnp.concatenate(
            [jnp.broadcast_to(bt_ref[g * GQA_GROUP + hh, 0:1, 0:1], (nq, 1)) for hh in range(GQA_GROUP)], axis=0)

    def near_bias(g, lo, hi):
        return jnp.concatenate([bt_ref[g * GQA_GROUP + hh, 0:nq, lo:hi] for hh in range(GQA_GROUP)], axis=0)

    def group_queries(g):
        return jnp.concatenate(
            [q_ref[:, (g * GQA_GROUP + hh) * HEAD_DIM:(g * GQA_GROUP + hh + 1) * HEAD_DIM]
             for hh in range(GQA_GROUP)], axis=0).astype(BF16)

    def accumulate(g, lg, sel, vals):
        rs = slice(g * grows, (g + 1) * grows)
        m_old = m_scr[rs, :]
        m_new = jnp.maximum(m_old, jnp.max(jnp.where(sel, lg, NEG_BIG), axis=1, keepdims=True))
        p = jnp.where(sel, jnp.exp(lg - m_new), 0.0)
        alpha = jnp.exp(m_old - m_new)
        l_scr[rs, :] = alpha * l_scr[rs, :] + jnp.sum(p, axis=1, keepdims=True)
        acc_scr[rs, :] = alpha * acc_scr[rs, :] + _dot(p.astype(BF16), vals)
        m_scr[rs, :] = m_new

    for r in range(ATTN_PAGES):
        for g in range(N_KV_HEADS):
            rows = pl.ds(g, PAGE_SIZE, stride=N_KV_HEADS)
            kbuf[g, r * PAGE_SIZE:(r + 1) * PAGE_SIZE, :] = kpages[slot, r, rows, :].astype(BF16)
            vbuf[g, r * PAGE_SIZE:(r + 1) * PAGE_SIZE, :] = vpages[slot, r, rows, :].astype(BF16)

    col = lax.broadcasted_iota(I32, (nq, width), 1)
    sel = select(skp_ref[...], col + s * width)
    colg = lax.broadcasted_iota(I32, (grows, width), 1)
    in_near = (s == last) & (colg >= width - PAGE_SIZE)
    for g in range(N_KV_HEADS):
        near = jnp.tile(near_bias(g, 0, LANES), (1, ATTN_PAGES))
        bias = jnp.where(in_near, near, far_bias(g))
        lg = _dot_nt(group_queries(g), kbuf[g]) * scale + bias
        accumulate(g, lg, sel, vbuf[g])

    @pl.when(s == last)
    def _():
        pad = jnp.zeros((LANES - nq, D_KV), F32)
        kn = jnp.concatenate([kn_ref[...], pad], axis=0).astype(BF16)
        vn = jnp.concatenate([vn_ref[...], pad], axis=0).astype(BF16)
        coln = lax.broadcasted_iota(I32, (nq, LANES), 1) + past
        seln = select(skn_ref[...], coln)
        for g in range(N_KV_HEADS):
            cs = slice(g * HEAD_DIM, (g + 1) * HEAD_DIM)
            lg = _dot_nt(group_queries(g), kn[:, cs]) * scale + near_bias(g, LANES, NEAR)
            accumulate(g, lg, seln, vn[:, cs])
        out = acc_scr[...] / l_scr[...]
        for h in range(N_HEADS):
            o_ref[:, h * HEAD_DIM:(h + 1) * HEAD_DIM] = out[h * nq:(h + 1) * nq, :]


def _sample_attn(page_table, proj_s, sk_past, sk_new, thr, lim, btab, pool_k, pool_v, *, batch, nq, n_pages):
    steps = n_pages // ATTN_PAGES
    width = ATTN_PAGES * PAGE_SIZE
    past = n_pages * PAGE_SIZE
    page_buf = pltpu.VMEM((2, ATTN_PAGES, PAGE_SIZE * N_KV_HEADS, HEAD_DIM), F32)
    grid_spec = pltpu.PrefetchScalarGridSpec(
        num_scalar_prefetch=1,
        grid=(batch, steps),
        in_specs=[
            pl.BlockSpec((nq, D_ATTN), lambda b, s, pt: (b, COL_Q // D_ATTN)),
            pl.BlockSpec((nq, D_KV), lambda b, s, pt: (b, COL_K // D_KV)),
            pl.BlockSpec((nq, D_KV), lambda b, s, pt: (b, COL_V // D_KV)),
            pl.BlockSpec((None, nq, width), lambda b, s, pt: (b, 0, s)),
            pl.BlockSpec((None, nq, LANES), lambda b, s, pt: (b, 0, 0)),
            pl.BlockSpec((nq, LANES), lambda b, s, pt: (b, 0)),
            pl.BlockSpec((nq, LANES), lambda b, s, pt: (b, 0)),
            pl.BlockSpec((N_HEADS, LANES, NEAR), lambda b, s, pt: (0, 0, 0)),
            pl.BlockSpec(memory_space=pl.ANY),
            pl.BlockSpec(memory_space=pl.ANY),
        ],
        out_specs=pl.BlockSpec((nq, D_ATTN), lambda b, s, pt: (b, 0)),
        scratch_shapes=[
            page_buf, page_buf, pltpu.SemaphoreType.DMA((2,)), pltpu.SemaphoreType.DMA((2,)),
            pltpu.VMEM((N_KV_HEADS, width, HEAD_DIM), BF16),
            pltpu.VMEM((N_KV_HEADS, width, HEAD_DIM), BF16),
            pltpu.VMEM((N_HEADS * nq, 1), F32),
            pltpu.VMEM((N_HEADS * nq, 1), F32),
            pltpu.VMEM((N_HEADS * nq, HEAD_DIM), F32),
        ],
    )
    return pl.pallas_call(
        functools.partial(_sample_attn_kernel, nq=nq, past=past),
        grid_spec=grid_spec,
        out_shape=jax.ShapeDtypeStruct((batch * nq, D_ATTN), F32),
        compiler_params=_params(("arbitrary", "arbitrary")),
        name="sample_attn",
    )(page_table, proj_s, proj_s, proj_s, sk_past, sk_new, thr, lim, btab, pool_k, pool_v)


TAIL = 8


def _ssd_kernel(xbc_ref, z_ref, sm_ref, cprev_ref, hprev_ref, cw_ref, cb_ref, alog_ref, dtb_ref, dvec_ref,
                g_ref, y_ref, cnew_ref, hnew_ref, xpad_scr, sm_scr, h_scr, yd_scr, *, rows):
    c = pl.program_id(1)
    q = CHUNK
    keep = CONV_WIDTH - 1
    heads_per_group = SSD_HEADS // SSD_GROUPS

    @pl.when(c == 0)
    def _():
        xpad_scr[...] = jnp.zeros_like(xpad_scr)
        sm_scr[...] = jnp.zeros_like(sm_scr)
        xpad_scr[TAIL - keep:TAIL, :] = cprev_ref[0]
        h_scr[...] = hprev_ref[0]

    xpad_scr[TAIL:TAIL + rows, :] = xbc_ref[...]
    sm_scr[0:rows, :] = sm_ref[...]
    conv = sum(xpad_scr[TAIL - keep + k:TAIL - keep + k + q, :] * cw_ref[k:k + 1, :]
               for k in range(CONV_WIDTH)) + cb_ref[...]
    new_tail = xpad_scr[rows + TAIL - keep:rows + TAIL, :]
    cnew_ref[0] = new_tail
    xpad_scr[TAIL - keep:TAIL, :] = new_tail

    xc = conv * jax.nn.sigmoid(conv)
    xs = xc[:, :D_SSD]
    bm = xc[:, D_SSD:D_SSD + SSD_GROUPS * D_STATE].astype(BF16)
    cm = xc[:, D_SSD + SSD_GROUPS * D_STATE:].astype(BF16)

    ri = lax.broadcasted_iota(I32, (q, q), 0)
    ci = lax.broadcasted_iota(I32, (q, q), 1)
    causal = ri >= ci
    x = sm_scr[:, SM_DT:SM_DT + SSD_HEADS] + dtb_ref[...]
    dt = jnp.maximum(x, 0.0) + jnp.log1p(jnp.exp(-jnp.abs(x)))
    dt = jnp.where(lax.broadcasted_iota(I32, (q, SSD_HEADS), 0) < rows, dt, 0.0)
    a = -jnp.exp(alog_ref[...])
    acum = _select_dot(causal.astype(BF16), dt * a)
    acum_t = jnp.concatenate([acum, jnp.zeros((q, LANES - SSD_HEADS), F32)], axis=1).T

    expand = (lax.broadcasted_iota(I32, (SSD_HEADS, D_SSD), 1) // SSD_HEAD_DIM
              == lax.broadcasted_iota(I32, (SSD_HEADS, D_SSD), 0)).astype(BF16)
    expand_t = (lax.broadcasted_iota(I32, (D_SSD, SSD_HEADS), 0) // SSD_HEAD_DIM
                == lax.broadcasted_iota(I32, (D_SSD, SSD_HEADS), 1)).astype(BF16)
    acum_x = _select_dot(acum, expand)
    xd = xs * _select_dot(dt, expand)
    xdw_t = (xd * jnp.exp(acum_x[q - 1:q, :] - acum_x)).T.astype(BF16)
    xd = xd.astype(BF16)
    state_decay = jnp.exp(_select_dot(expand_t, jnp.broadcast_to(acum_t[0:SSD_HEADS, q - 1:q], (SSD_HEADS, LANES))))

    for g in range(SSD_GROUPS):
        ns = slice(g * D_STATE, (g + 1) * D_STATE)
        hs = slice(g * heads_per_group * SSD_HEAD_DIM, (g + 1) * heads_per_group * SSD_HEAD_DIM)
        cb = _dot_nt(cm[:, ns], bm[:, ns])
        for e in range(heads_per_group):
            h = g * heads_per_group + e
            ps = slice(h * SSD_HEAD_DIM, (h + 1) * SSD_HEAD_DIM)
            seg = acum[:, h:h + 1] - acum_t[h:h + 1, :]
            lmat = jnp.exp(jnp.where(causal, seg, -jnp.inf))
            yd_scr[:, ps] = _dot((cb * lmat).astype(BF16), xd[:, ps])
        h_old = h_scr[hs, :]
        yd_scr[:, hs] += _dot_nt(cm[:, ns], h_old.astype(BF16)) * jnp.exp(acum_x[:, hs])
        h_scr[hs, :] = h_old * state_decay[hs, :] + _dot(xdw_t[hs, :], bm[:, ns])

    hnew_ref[0] = h_scr[...]
    y = yd_scr[0:rows, :] + dvec_ref[...] * xs[0:rows, :]
    zz = z_ref[...]
    y = y * (zz * jax.nn.sigmoid(zz))
    width = D_SSD // SSD_GROUPS
    for g in range(SSD_GROUPS):
        cs = slice(g * width, (g + 1) * width)
        y_ref[:, cs] = _rms(y[:, cs], g_ref[:, cs])


def _ssd(proj, conv_prev, ssm_prev, conv_w, conv_b, a_log, dt_bias, d_vec, g_ssd, *, batch, seq):
    rows = min(CHUNK, seq)
    nc = seq // rows
    full = lambda *shape: pl.BlockSpec(shape, lambda b, c: (0,) * len(shape))
    return pl.pallas_call(
        functools.partial(_ssd_kernel, rows=rows),
        grid=(batch, nc),
        in_specs=[
            pl.BlockSpec((rows, CONV_DIM), lambda b, c: (b * nc + c, COL_XBC // CONV_DIM)),
            pl.BlockSpec((rows, D_SSD), lambda b, c: (b * nc + c, COL_Z // D_SSD)),
            pl.BlockSpec((rows, LANES), lambda b, c: (b * nc + c, COL_SMALL // LANES)),
            pl.BlockSpec((1, CONV_WIDTH - 1, CONV_DIM), lambda b, c: (b, 0, 0)),
            pl.BlockSpec((1, D_SSD, D_STATE), lambda b, c: (b, 0, 0)),
            full(CONV_WIDTH, CONV_DIM), full(1, CONV_DIM), full(1, SSD_HEADS), full(1, SSD_HEADS),
            full(1, D_SSD), full(1, D_SSD),
        ],
        out_specs=[
            pl.BlockSpec((rows, D_SSD), lambda b, c: (b * nc + c, 0)),
            pl.BlockSpec((1, CONV_WIDTH - 1, CONV_DIM), lambda b, c: (b, 0, 0)),
            pl.BlockSpec((1, D_SSD, D_STATE), lambda b, c: (b, 0, 0)),
        ],
        out_shape=[
            jax.ShapeDtypeStruct((batch * seq, D_SSD), F32),
            jax.ShapeDtypeStruct((batch, CONV_WIDTH - 1, CONV_DIM), F32),
            jax.ShapeDtypeStruct((batch, D_SSD, D_STATE), F32),
        ],
        scratch_shapes=[
            pltpu.VMEM((CHUNK + TAIL, CONV_DIM), F32),
            pltpu.VMEM((CHUNK, LANES), F32),
            pltpu.VMEM((D_SSD, D_STATE), F32),
            pltpu.VMEM((CHUNK, D_SSD), F32),
        ],
        compiler_params=_params(("parallel", "arbitrary")),
        name="ssd",
    )(proj, proj, proj, conv_prev, ssm_prev, conv_w, conv_b, a_log, dt_bias, d_vec, g_ssd)


def _reorder_w_in(w_in):
    offs = [0]
    for s in IN_SPLITS:
        offs.append(offs[-1] + s)
    q, k, v, qi, ki, wi, z, xbc, dtr = [w_in[:, offs[n]:offs[n + 1]] for n in range(len(IN_SPLITS))]
    parts = [t.astype(BF16) for t in (q, qi, z, xbc, k, v, ki, wi, dtr)]
    parts.append(jnp.zeros((w_in.shape[0], N_IN - sum(IN_SPLITS)), BF16))
    return jnp.concatenate(parts, axis=1)


def kernel(x_prompt, x_sample, cache_k, cache_v, cache_kidx, state_conv, state_ssm, page_table, rel_bias, g_ffn1, w1_ffn1, w3_ffn1, w2_ffn1, g_mix, w_in, conv_w, conv_b, a_log, dt_bias, d_skip, g_ssd, w_out, g_ffn2, w1_ffn2, w3_ffn2, w2_ffn2, g_final):
    depth = w_in.shape[0]
    assert depth == 1
    l = 0
    bp, seq, _ = x_prompt.shape
    bs, nq, _ = x_sample.shape
    n_pages = page_table.shape[1]
    row = lambda t: t.reshape(1, -1)

    w1a, w3a, w2a = w1_ffn1[l], w3_ffn1[l], w2_ffn1[l]
    w1b, w3b, w2b = w1_ffn2[l], w3_ffn2[l], w2_ffn2[l]
    w_in_r = _reorder_w_in(w_in[l])
    w_out_a = w_out[l, :D_ATTN].astype(BF16)
    w_out_s = w_out[l, D_ATTN:].astype(BF16)
    d_vec = jnp.repeat(d_skip[l], SSD_HEAD_DIM).reshape(1, D_SSD)
    btab = _bias_table(rel_bias)
    gf = row(g_final)

    def trunk(x, attend, conv_prev, ssm_prev, batch, length):
        tm = min(x.shape[0], 1024)
        tf = 256 if tm > 512 else 512
        x = _ffn(x, row(g_ffn1[l]), w1a, w3a, w2a, gf, tm=tm, tf=tf, final_norm=False)
        proj = _in_proj(x, row(g_mix[l]), w_in_r, tm=tm)
        a_out = attend(proj)
        s_out, conv_new, ssm_new = _ssd(
            proj, conv_prev, ssm_prev.reshape(batch, D_SSD, D_STATE), conv_w[l], row(conv_b[l]), row(a_log[l]),
            row(dt_bias[l]), d_vec, row(g_ssd[l]), batch=batch, seq=length)
        x = _out_proj(x, a_out, s_out, w_out_a, w_out_s, tm=min(tm, 512))
        y = _ffn(x, row(g_ffn2[l]), w1b, w3b, w2b, gf, tm=tm, tf=tf, final_norm=True)
        k = proj[:, COL_K:COL_K + D_KV].reshape(1, batch, length, N_KV_HEADS, HEAD_DIM)
        v = proj[:, COL_V:COL_V + D_KV].reshape(1, batch, length, N_KV_HEADS, HEAD_DIM)
        ki = proj[:, COL_SMALL + SM_KI:COL_SMALL + SM_KI + IDX_DIM].reshape(1, batch, length, IDX_DIM)
        ssm_new = ssm_new.reshape(1, batch, SSD_HEADS, SSD_HEAD_DIM, D_STATE)
        return y.reshape(batch, length, D_MODEL), k, v, ki, conv_new[None], ssm_new

    y_p, k_p, v_p, ki_p, conv_p, ssm_p = trunk(
        x_prompt.reshape(bp * seq, D_MODEL),
        functools.partial(_dsa_prompt, btab=btab, batch=bp, seq=seq),
        jnp.zeros((bp, CONV_WIDTH - 1, CONV_DIM), F32), jnp.zeros((bp, SSD_HEADS, SSD_HEAD_DIM, D_STATE), F32),
        bp, seq)

    pool_k = cache_k[l].reshape(-1, HEAD_DIM)
    pool_v = cache_v[l].reshape(-1, HEAD_DIM)
    pool_kidx = jnp.swapaxes(cache_kidx[l], 1, 2).reshape(-1, PAGE_SIZE)

    def sample_attend(proj):
        sk_past, sk_new = _sample_score(page_table, proj, pool_kidx, batch=bs, nq=nq, n_pages=n_pages)
        thr, lim = _sample_select(sk_past.reshape(bs * nq, -1), sk_new.reshape(bs * nq, LANES),
                                  rows_per_step=128)
        return _sample_attn(page_table, proj, sk_past, sk_new, thr, lim, btab, pool_k, pool_v,
                            batch=bs, nq=nq, n_pages=n_pages)

    y_s, k_s, v_s, ki_s, conv_s, ssm_s = trunk(
        x_sample.reshape(bs * nq, D_MODEL), sample_attend, state_conv[l], state_ssm[l], bs, nq)

    return (y_p, y_s, k_p, v_p, ki_p, conv_p, ssm_p, k_s, v_s, ki_s, conv_s, ssm_s)
```

```python
import functools
import math

import jax
import jax.numpy as jnp
from jax import lax
from jax.experimental import pallas as pl
from jax.experimental.pallas import tpu as pltpu

F32 = jnp.float32
BF16 = jnp.bfloat16
I32 = jnp.int32

D_MODEL = 2048
PAGE_SIZE = 128
N_HEADS = 8
HEAD_DIM = 128
N_KV_HEADS = 2
GQA_GROUP = N_HEADS // N_KV_HEADS
IDX_HEADS = 16
IDX_DIM = 64
TOPK = 256
N_BUCKETS = 32
MAX_DISTANCE = 128
SSD_HEADS = 16
SSD_HEAD_DIM = 64
SSD_GROUPS = 2
D_STATE = 128
CONV_WIDTH = 4
CHUNK = 128
D_ATTN = N_HEADS * HEAD_DIM
D_SSD = SSD_HEADS * SSD_HEAD_DIM
D_KV = N_KV_HEADS * HEAD_DIM
CONV_DIM = D_SSD + 2 * SSD_GROUPS * D_STATE
D_FF = 5632
EPS = 1e-6
IN_SPLITS = (D_ATTN, D_KV, D_KV, IDX_HEADS * IDX_DIM, IDX_DIM, IDX_HEADS, D_SSD, CONV_DIM, SSD_HEADS)

LANES = 128
COL_Q = 0
COL_QI = COL_Q + D_ATTN
COL_Z = COL_QI + IDX_HEADS * IDX_DIM
COL_XBC = COL_Z + D_SSD
COL_K = COL_XBC + CONV_DIM
COL_V = COL_K + D_KV
COL_SMALL = COL_V + D_KV
SM_KI = 0
SM_WI = SM_KI + IDX_DIM
SM_DT = SM_WI + IDX_HEADS
IN_TILE = 768
N_IN = ((COL_SMALL + LANES + IN_TILE - 1) // IN_TILE) * IN_TILE

INT_MIN = -(2 ** 31)
NEG_BIG = -1e30
VMEM_LIMIT = 60000 * 1024


def _params(sem):
    return pltpu.CompilerParams(dimension_semantics=sem, vmem_limit_bytes=VMEM_LIMIT)


def _rms(x, g):
    return x * lax.rsqrt(jnp.mean(x * x, axis=-1, keepdims=True) + EPS) * g


def _dot(a, b):
    return jnp.dot(a, b, preferred_element_type=F32)


def _dot_nt(a, b):
    return lax.dot_general(a, b, (((1,), (1,)), ((), ())), preferred_element_type=F32)


def _split3(x):
    hi = x.astype(BF16)
    rest = x - hi.astype(F32)
    mid = rest.astype(BF16)
    return hi, mid, (rest - mid.astype(F32)).astype(BF16)


def _select_dot(a, b):
    if a.dtype == BF16:
        return sum(_dot(a, part) for part in _split3(b))
    return sum(_dot(part, b) for part in _split3(a))


def _sort_key(x):
    bits = lax.bitcast_convert_type(x + 0.0, I32)
    return bits ^ ((bits >> 31) & 0x7FFFFFFF)


def _ffn_kernel(x_ref, g_ref, gf_ref, w1_hbm, w3_hbm, w2_hbm, o_ref, h_scr, w1_buf, w3_buf, w2_buf, sem, *,
                tf, final_norm):
    i = pl.program_id(0)
    nf = D_FF // tf
    total = pl.num_programs(0) * nf

    def copies(f, slot):
        c0 = pl.multiple_of(f * tf, tf)
        return (pltpu.make_async_copy(w1_hbm.at[:, pl.ds(c0, tf)], w1_buf.at[slot], sem.at[0, slot]),
                pltpu.make_async_copy(w3_hbm.at[:, pl.ds(c0, tf)], w3_buf.at[slot], sem.at[1, slot]),
                pltpu.make_async_copy(w2_hbm.at[pl.ds(c0, tf), :], w2_buf.at[slot], sem.at[2, slot]))

    @pl.when(i == 0)
    def _():
        for c in copies(0, 0):
            c.start()

    h_scr[...] = _rms(x_ref[...], g_ref[...]).astype(BF16)
    o_ref[...] = jnp.zeros_like(o_ref)

    def body(f, carry):
        t = i * nf + f
        slot = lax.rem(t, 2)

        @pl.when(t + 1 < total)
        def _():
            for c in copies(jnp.where(f + 1 == nf, 0, f + 1), 1 - slot):
                c.start()

        for c in copies(f, slot):
            c.wait()
        h = h_scr[...]
        a = _dot(h, w1_buf[slot].astype(BF16))
        b = _dot(h, w3_buf[slot].astype(BF16))
        u = (a * jax.nn.sigmoid(a) * b).astype(BF16)
        o_ref[...] += _dot(u, w2_buf[slot].astype(BF16))
        return carry

    lax.fori_loop(0, nf, body, 0)
    y = x_ref[...] + 0.5 * o_ref[...]
    if final_norm:
        y = _rms(y, gf_ref[...])
    o_ref[...] = y


def _ffn(x, g, w1, w3, w2, gf, *, tm, tf, final_norm):
    t = x.shape[0]
    return pl.pallas_call(
        functools.partial(_ffn_kernel, tf=tf, final_norm=final_norm),
        grid=(t // tm,),
        in_specs=[
            pl.BlockSpec((tm, D_MODEL), lambda i: (i, 0)),
            pl.BlockSpec((1, D_MODEL), lambda i: (0, 0)),
            pl.BlockSpec((1, D_MODEL), lambda i: (0, 0)),
            pl.BlockSpec(memory_space=pl.ANY),
            pl.BlockSpec(memory_space=pl.ANY),
            pl.BlockSpec(memory_space=pl.ANY),
        ],
        out_specs=pl.BlockSpec((tm, D_MODEL), lambda i: (i, 0)),
        out_shape=jax.ShapeDtypeStruct((t, D_MODEL), F32),
        scratch_shapes=[
            pltpu.VMEM((tm, D_MODEL), BF16),
            pltpu.VMEM((2, D_MODEL, tf), F32),
            pltpu.VMEM((2, D_MODEL, tf), F32),
            pltpu.VMEM((2, tf, D_MODEL), F32),
            pltpu.SemaphoreType.DMA((3, 2)),
        ],
        compiler_params=_params(("arbitrary",)),
        name="ffn",
    )(x, g, gf, w1, w3, w2)


def _in_proj_kernel(x_ref, g_ref, w_ref, o_ref, h_scr):
    @pl.when(pl.program_id(1) == 0)
    def _():
        h_scr[...] = _rms(x_ref[...], g_ref[...]).astype(BF16)

    o_ref[...] = _dot(h_scr[...], w_ref[...])


def _in_proj(x, g, w, *, tm):
    t = x.shape[0]
    return pl.pallas_call(
        _in_proj_kernel,
        grid=(t // tm, N_IN // IN_TILE),
        in_specs=[
            pl.BlockSpec((tm, D_MODEL), lambda i, j: (i, 0)),
            pl.BlockSpec((1, D_MODEL), lambda i, j: (0, 0)),
            pl.BlockSpec((D_MODEL, IN_TILE), lambda i, j: (0, j)),
        ],
        out_specs=pl.BlockSpec((tm, IN_TILE), lambda i, j: (i, j)),
        out_shape=jax.ShapeDtypeStruct((t, N_IN), F32),
        scratch_shapes=[pltpu.VMEM((tm, D_MODEL), BF16)],
        compiler_params=_params(("parallel", "arbitrary")),
        name="in_proj",
    )(x, g, w)


def _out_proj_kernel(x_ref, a_ref, s_ref, wa_ref, ws_ref, o_ref):
    acc = _dot(a_ref[...].astype(BF16), wa_ref[...])
    acc += _dot(s_ref[...].astype(BF16), ws_ref[...])
    o_ref[...] = x_ref[...] + acc


def _out_proj(x, a, s, wa, ws, *, tm):
    t = x.shape[0]
    return pl.pallas_call(
        _out_proj_kernel,
        grid=(t // tm,),
        in_specs=[
            pl.BlockSpec((tm, D_MODEL), lambda i: (i, 0)),
            pl.BlockSpec((tm, D_ATTN), lambda i: (i, 0)),
            pl.BlockSpec((tm, D_SSD), lambda i: (i, 0)),
            pl.BlockSpec((D_ATTN, D_MODEL), lambda i: (0, 0)),
            pl.BlockSpec((D_SSD, D_MODEL), lambda i: (0, 0)),
        ],
        out_specs=pl.BlockSpec((tm, D_MODEL), lambda i: (i, 0)),
        out_shape=jax.ShapeDtypeStruct((t, D_MODEL), F32),
        compiler_params=_params(("parallel",)),
        name="out_proj",
    )(x, a, s, wa, ws)


NEAR = 2 * LANES


def _bias_kernel(rb_ref, o_ref):
    r = lax.broadcasted_iota(I32, (LANES, NEAR), 0)
    c = lax.broadcasted_iota(I32, (LANES, NEAR), 1)
    n = jnp.maximum(r + LANES - c, 0)
    max_exact = N_BUCKETS // 2
    nf = jnp.maximum(n, 1).astype(F32)
    large = max_exact + (jnp.log(nf / max_exact) / math.log(MAX_DISTANCE / max_exact)
                         * (N_BUCKETS - max_exact)).astype(I32)
    bucket = jnp.where(n < max_exact, n, jnp.minimum(large, N_BUCKETS - 1))
    for h in range(N_HEADS):
        acc = jnp.zeros((LANES, NEAR), F32)
        for b in range(N_BUCKETS):
            acc = jnp.where(bucket == b, rb_ref[b, h], acc)
        o_ref[h] = acc


def _bias_table(rel_bias):
    return pl.pallas_call(
        _bias_kernel,
        in_specs=[pl.BlockSpec(memory_space=pltpu.SMEM)],
        out_specs=pl.BlockSpec(memory_space=pltpu.VMEM),
        out_shape=jax.ShapeDtypeStruct((N_HEADS, LANES, NEAR), F32),
        name="bias_table",
    )(rel_bias)


def _count(mask, axis=1):
    x = mask.astype(F32)
    if axis == 0:
        x = jnp.sum(x.reshape(x.shape[0] // 64, 64, x.shape[1]), axis=0)
    return jnp.sum(x, axis=axis, keepdims=True)


I16 = jnp.int16
I16_MIN = -(2 ** 15)


def _kth_largest(sk_refs, hi_refs, lo_refs, extents, axis):
    shape = (1, sk_refs[0].shape[1]) if axis == 0 else (sk_refs[0].shape[0], 1)

    def window(ref, n):
        return ref[0:n, :] if axis == 0 else ref[:, 0:n]

    def tile(x, c):
        return x[c * LANES:(c + 1) * LANES, :] if axis == 0 else x[:, c * LANES:(c + 1) * LANES]

    def count(refs, compare, t):
        total = jnp.zeros(shape, I32)
        for ref, n in zip(refs, extents):
            ones = jnp.where(compare(window(ref, n), t.astype(I16)), jnp.int16(1), jnp.int16(0))
            part = functools.reduce(jnp.add, [tile(ones, c) for c in range(n // LANES)])
            total = total + jnp.sum(part.astype(I32), axis=axis, keepdims=True)
        return total

    def search(refs, k):
        t = jnp.full(shape, I16_MIN, I32)
        t = jnp.where(count(refs, jnp.greater_equal, jnp.zeros(shape, I32)) >= k, 0, t)

        def body(it, t):
            cand = t + lax.shift_left(jnp.int32(1), 14 - it)
            return jnp.where(count(refs, jnp.greater_equal, cand) >= k, cand, t)

        return lax.fori_loop(0, 15, body, t)

    for sk_ref, hi_ref, n in zip(sk_refs, hi_refs, extents):
        sk = window(sk_ref, n)
        if axis == 0:
            hi_ref[0:n, :] = (sk >> 16).astype(I16)
        else:
            hi_ref[:, 0:n] = (sk >> 16).astype(I16)
    t_hi = search(hi_refs, TOPK)
    k_lo = TOPK - count(hi_refs, jnp.greater, t_hi)
    for sk_ref, hi_ref, lo_ref, n in zip(sk_refs, hi_refs, lo_refs, extents):
        low = ((window(sk_ref, n) & 0xFFFF) + I16_MIN).astype(I16)
        low = jnp.where(window(hi_ref, n) == t_hi.astype(I16), low, jnp.int16(I16_MIN))
        if axis == 0:
            lo_ref[0:n, :] = low
        else:
            lo_ref[:, 0:n] = low
    t_lo = search(lo_refs, k_lo)
    return t_hi * 65536 + (t_lo - I16_MIN)


def _tie_limit(count_eq_below, need, shape, idx_bits):
    def body(it, m):
        cand = m + lax.shift_left(jnp.int32(1), idx_bits - 1 - it)
        return jnp.where(count_eq_below(cand) < need, cand, m)

    return lax.fori_loop(0, idx_bits, body, jnp.zeros(shape, I32))


CAUSAL_VARIANTS = 4


def _dsa_prompt_block(width, near_tiles, i, q_ref, qi_ref, k_ref, v_ref, sm_ref, bt_ref, o_ref, sk_scr, neg_scr,
                      hi_scr, lo_scr):
    tq = LANES
    q0 = pl.multiple_of(i * LANES, LANES)
    shape = (1, tq)

    wi_t = sm_ref[pl.ds(q0, tq), :].T[SM_WI:SM_WI + IDX_HEADS, :] * (IDX_HEADS * IDX_DIM) ** -0.5
    head_group = 4
    qi_groups = [
        jnp.concatenate([qi_ref[:, h * IDX_DIM:(h + 1) * IDX_DIM] for h in range(hg, hg + head_group)],
                        axis=0).astype(BF16)
        for hg in range(0, IDX_HEADS, head_group)]
    chunk = 2 * LANES
    pos = lax.broadcasted_iota(I32, (chunk, tq), 1) + q0
    for c in range(0, width, chunk):
        ki = sm_ref[c:c + chunk, SM_KI:SM_KI + IDX_DIM].astype(BF16)
        score = jnp.zeros((chunk, tq), F32)
        for n, hg in enumerate(range(0, IDX_HEADS, head_group)):
            rel = _dot_nt(ki, qi_groups[n])
            for e in range(head_group):
                score = score + wi_t[hg + e:hg + e + 1, :] * jnp.maximum(rel[:, e * tq:(e + 1) * tq], 0.0)
        key_c = lax.broadcasted_iota(I32, (chunk, tq), 0) + c
        sk_scr[c:c + chunk, :] = jnp.where(key_c <= pos, _sort_key(score), INT_MIN)

    key = lax.broadcasted_iota(I32, (width, tq), 0)

    def sk():
        return sk_scr[0:width, :]

    thr = _kth_largest([sk_scr], [hi_scr], [lo_scr], [width], axis=0)
    excess = (_count(sk() >= thr, 0) > TOPK) & (thr > INT_MIN)
    any_excess = jnp.max(excess.astype(F32)) > 0.0

    @pl.when(jnp.logical_not(any_excess))
    def _():
        neg_scr[0:width, :] = jnp.where(sk() >= jnp.maximum(thr, INT_MIN + 1), 0.0, -jnp.inf)

    @pl.when(any_excess)
    def _():
        need = TOPK - _count(sk() > thr, 0)
        lim = _tie_limit(lambda m: _count((sk() == thr) & (key < m), 0), need, shape, (width - 1).bit_length())
        take = (sk() > thr) | ((sk() == thr) & (key <= lim) & (sk() > INT_MIN))
        neg_scr[0:width, :] = jnp.where(take, 0.0, -jnp.inf)

    neg = neg_scr[0:width, :].T

    log2e = math.log2(math.e)
    scale = HEAD_DIM ** -0.5 * log2e
    tiles = width // LANES
    rows = GQA_GROUP * tq
    for g in range(N_KV_HEADS):
        heads = range(g * GQA_GROUP, (g + 1) * GQA_GROUP)
        kg = k_ref[0:width, g * HEAD_DIM:(g + 1) * HEAD_DIM].astype(BF16)
        vg = v_ref[0:width, g * HEAD_DIM:(g + 1) * HEAD_DIM].astype(BF16)
        qg = jnp.concatenate([q_ref[:, h * HEAD_DIM:(h + 1) * HEAD_DIM] for h in heads], axis=0).astype(BF16)
        far = jnp.stack([jnp.broadcast_to(bt_ref[h, 0:1, 0:1], (tq, LANES)) for h in heads])
        diag = (jnp.stack([bt_ref[h, :, LANES:NEAR] for h in heads]) - far) * log2e
        prev = (jnp.stack([bt_ref[h, :, 0:LANES] for h in heads]) - far) * log2e
        qk = _dot_nt(qg, kg)
        logits = []
        for t in range(tiles):
            lt = qk[:, t * LANES:(t + 1) * LANES].reshape(GQA_GROUP, tq, LANES) * scale
            lt = lt + neg[None, :, t * LANES:(t + 1) * LANES]
            if t >= tiles - near_tiles:
                lt = lt + jnp.where(i == t, diag, jnp.where(i - 1 == t, prev, 0.0))
            logits.append(lt)
        m = functools.reduce(jnp.maximum, logits)
        m = jnp.max(m, axis=2, keepdims=True)
        probs = [jnp.exp2(lt - m) for lt in logits]
        l = jnp.sum(functools.reduce(jnp.add, probs), axis=2, keepdims=True)
        p = jnp.concatenate([pt.reshape(rows, LANES).astype(BF16) for pt in probs], axis=1)
        out = _dot(p, vg) / l.reshape(rows, 1)
        for n, h in enumerate(heads):
            o_ref[:, h * HEAD_DIM:(h + 1) * HEAD_DIM] = out[n * tq:(n + 1) * tq, :]


def _dsa_prompt_kernel(*refs, seq):
    i = pl.program_id(1)
    per = seq // LANES // CAUSAL_VARIANTS
    for v in range(CAUSAL_VARIANTS):
        @pl.when(i // per == v)
        def _(v=v):
            _dsa_prompt_block((v + 1) * per * LANES, per + 1, i, *refs)


def _dsa_prompt(proj, btab, *, batch, seq):
    nq = seq // LANES
    return pl.pallas_call(
        functools.partial(_dsa_prompt_kernel, seq=seq),
        grid=(batch, nq),
        in_specs=[
            pl.BlockSpec((LANES, D_ATTN), lambda b, i: (b * nq + i, COL_Q // D_ATTN)),
            pl.BlockSpec((LANES, IDX_HEADS * IDX_DIM), lambda b, i: (b * nq + i, COL_QI // (IDX_HEADS * IDX_DIM))),
            pl.BlockSpec((seq, D_KV), lambda b, i: (b, COL_K // D_KV)),
            pl.BlockSpec((seq, D_KV), lambda b, i: (b, COL_V // D_KV)),
            pl.BlockSpec((seq, LANES), lambda b, i: (b, COL_SMALL // LANES)),
            pl.BlockSpec((N_HEADS, LANES, NEAR), lambda b, i: (0, 0, 0)),
        ],
        out_specs=pl.BlockSpec((LANES, D_ATTN), lambda b, i: (b * nq + i, 0)),
        out_shape=jax.ShapeDtypeStruct((batch * seq, D_ATTN), F32),
        scratch_shapes=[pltpu.VMEM((seq, LANES), I32), pltpu.VMEM((seq, LANES), F32),
                        pltpu.VMEM((seq, LANES), I16), pltpu.VMEM((seq, LANES), I16)],
        compiler_params=_params(("parallel", "parallel")),
        name="dsa_prompt",
    )(proj, proj, proj, proj, proj, btab)


SCORE_PAGES = 64
ATTN_PAGES = 32


def _sample_queries(qi_ref):
    return jnp.concatenate(
        [qi_ref[:, h * IDX_DIM:(h + 1) * IDX_DIM] for h in range(IDX_HEADS)], axis=0).astype(BF16)


def _sample_scores(rel, wi, nq):
    sc = jnp.zeros((nq, rel.shape[1]), F32)
    for h in range(IDX_HEADS):
        sc = sc + wi[:, h:h + 1] * jnp.maximum(rel[h * nq:(h + 1) * nq, :], 0.0)
    return sc


def _page_copies(pt_ref, pools, bufs, sems, b, s, slot, pages, page_rows):
    copies = []
    for r in range(pages):
        row0 = pl.multiple_of(pt_ref[b, s * pages + r] * page_rows, page_rows)
        for pool, buf, sem in zip(pools, bufs, sems):
            copies.append(pltpu.make_async_copy(pool.at[pl.ds(row0, page_rows), :], buf.at[slot, r], sem.at[slot]))
    return copies


def _page_pipeline(pt_ref, pools, bufs, sems, pages, page_rows):
    b, s = pl.program_id(0), pl.program_id(1)
    steps = pl.num_programs(1)
    t = b * steps + s
    slot = lax.rem(t, 2)
    copies = functools.partial(_page_copies, pt_ref, pools, bufs, sems, pages=pages, page_rows=page_rows)

    @pl.when(t == 0)
    def _():
        for c in copies(b, s, slot):
            c.start()

    @pl.when(t + 1 < pl.num_programs(0) * steps)
    def _():
        wrap = s + 1 == steps
        for c in copies(jnp.where(wrap, b + 1, b), jnp.where(wrap, 0, s + 1), 1 - slot):
            c.start()

    for c in copies(b, s, slot):
        c.wait()
    return slot


def _sample_score_kernel(pt_ref, qi_ref, sm_ref, pool_ref, past_ref, new_ref, pages, sem, kbuf, *, nq):
    s = pl.program_id(1)
    slot = _page_pipeline(pt_ref, [pool_ref], [pages], [sem], SCORE_PAGES, IDX_DIM)
    qx = _sample_queries(qi_ref)
    wi = sm_ref[:, SM_WI:SM_WI + IDX_HEADS] * (IDX_HEADS * IDX_DIM) ** -0.5
    for r in range(SCORE_PAGES):
        kbuf[:, r * PAGE_SIZE:(r + 1) * PAGE_SIZE] = pages[slot, r].astype(BF16)
    past_ref[...] = _sort_key(_sample_scores(_dot(qx, kbuf[...]), wi, nq))

    @pl.when(s == pl.num_programs(1) - 1)
    def _():
        ki_new = jnp.concatenate(
            [sm_ref[:, SM_KI:SM_KI + IDX_DIM], jnp.zeros((LANES - nq, IDX_DIM), F32)], axis=0).astype(BF16)
        sc = _sample_scores(_dot_nt(qx, ki_new), wi, nq)
        j = lax.broadcasted_iota(I32, (nq, LANES), 1)
        t = lax.broadcasted_iota(I32, (nq, LANES), 0)
        new_ref[...] = jnp.where(j <= t, _sort_key(sc), INT_MIN)


def _sample_score(page_table, proj_s, pool_kidx_t, *, batch, nq, n_pages):
    steps = n_pages // SCORE_PAGES
    grid_spec = pltpu.PrefetchScalarGridSpec(
        num_scalar_prefetch=1,
        grid=(batch, steps),
        in_specs=[
            pl.BlockSpec((nq, IDX_HEADS * IDX_DIM), lambda b, s, pt: (b, COL_QI // (IDX_HEADS * IDX_DIM))),
            pl.BlockSpec((nq, LANES), lambda b, s, pt: (b, COL_SMALL // LANES)),
            pl.BlockSpec(memory_space=pl.ANY),
        ],
        out_specs=[
            pl.BlockSpec((None, nq, SCORE_PAGES * PAGE_SIZE), lambda b, s, pt: (b, 0, s)),
            pl.BlockSpec((None, nq, LANES), lambda b, s, pt: (b, 0, 0)),
        ],
        scratch_shapes=[
            pltpu.VMEM((2, SCORE_PAGES, IDX_DIM, PAGE_SIZE), F32),
            pltpu.SemaphoreType.DMA((2,)),
            pltpu.VMEM((IDX_DIM, SCORE_PAGES * PAGE_SIZE), BF16),
        ],
    )
    return pl.pallas_call(
        functools.partial(_sample_score_kernel, nq=nq),
        grid_spec=grid_spec,
        out_shape=[
            jax.ShapeDtypeStruct((batch, nq, n_pages * PAGE_SIZE), I32),
            jax.ShapeDtypeStruct((batch, nq, LANES), I32),
        ],
        compiler_params=_params(("arbitrary", "arbitrary")),
        name="sample_score",
    )(page_table, proj_s, proj_s, pool_kidx_t)


def _sample_select_kernel(past_ref, new_ref, thr_ref, lim_ref, hi_past, lo_past, hi_new, lo_new, *, past):
    rows = past_ref.shape[0]
    colp = lax.broadcasted_iota(I32, (rows, past), 1)
    coln = lax.broadcasted_iota(I32, (rows, LANES), 1) + past

    def count_ge(c):
        return _count(past_ref[...] >= c) + _count(new_ref[...] >= c)

    thr = _kth_largest([past_ref, new_ref], [hi_past, hi_new], [lo_past, lo_new], [past, LANES], axis=1)
    thr_ref[...] = jnp.broadcast_to(thr, (rows, LANES))
    lim_ref[...] = jnp.full((rows, LANES), 2 ** 31 - 1, I32)
    any_excess = jnp.max((count_ge(thr) > TOPK).astype(F32)) > 0.0

    @pl.when(any_excess)
    def _():
        need = TOPK - (_count(past_ref[...] > thr) + _count(new_ref[...] > thr))

        def count_eq_below(m):
            return (_count((past_ref[...] == thr) & (colp < m))
                    + _count((new_ref[...] == thr) & (coln < m)))

        lim = _tie_limit(count_eq_below, need, (rows, 1), (past + LANES - 1).bit_length())
        lim_ref[...] = jnp.broadcast_to(lim, (rows, LANES))


def _sample_select(sk_past, sk_new, *, rows_per_step):
    rows, past = sk_past.shape
    return pl.pallas_call(
        functools.partial(_sample_select_kernel, past=past),
        grid=(rows // rows_per_step,),
        in_specs=[
            pl.BlockSpec((rows_per_step, past), lambda i: (i, 0)),
            pl.BlockSpec((rows_per_step, LANES), lambda i: (i, 0)),
        ],
        out_specs=[
            pl.BlockSpec((rows_per_step, LANES), lambda i: (i, 0)),
            pl.BlockSpec((rows_per_step, LANES), lambda i: (i, 0)),
        ],
        out_shape=[jax.ShapeDtypeStruct((rows, LANES), I32)] * 2,
        scratch_shapes=[pltpu.VMEM((rows_per_step, past), I16), pltpu.VMEM((rows_per_step, past), I16),
                        pltpu.VMEM((rows_per_step, LANES), I16), pltpu.VMEM((rows_per_step, LANES), I16)],
        compiler_params=_params(("parallel",)),
        name="sample_select",
    )(sk_past, sk_new)


def _sample_attn_kernel(pt_ref, q_ref, kn_ref, vn_ref, skp_ref, skn_ref, thr_ref, lim_ref, bt_ref, poolk_ref,
                        poolv_ref, o_ref, kpages, vpages, ksem, vsem, kbuf, vbuf, m_scr, l_scr, acc_scr, *,
                        nq, past):
    slot = _page_pipeline(pt_ref, [poolk_ref, poolv_ref], [kpages, vpages], [ksem, vsem], ATTN_PAGES,
                          PAGE_SIZE * N_KV_HEADS)
    s = pl.program_id(1)
    last = pl.num_programs(1) - 1
    width = ATTN_PAGES * PAGE_SIZE
    grows = GQA_GROUP * nq
    scale = HEAD_DIM ** -0.5

    @pl.when(s == 0)
    def _():
        m_scr[...] = jnp.full_like(m_scr, NEG_BIG)
        l_scr[...] = jnp.zeros_like(l_scr)
        acc_scr[...] = jnp.zeros_like(acc_scr)

    thr = thr_ref[:, 0:1]
    lim = lim_ref[:, 0:1]

    def select(sk, idx):
        sel = (sk > thr) | ((sk == thr) & (idx <= lim))
        return jnp.concatenate([sel] * GQA_GROUP, axis=0)

    def far_bias(g):
        return jnp.concatenate(
            [jnp.broadcast_to(bt_ref[g * GQA_GROUP + hh, 0:1, 0:1], (nq, 1)) for hh in range(GQA_GROUP)], axis=0)

    def near_bias(g, lo, hi):
        return jnp.concatenate([bt_ref[g * GQA_GROUP + hh, 0:nq, lo:hi] for hh in range(GQA_GROUP)], axis=0)

    def group_queries(g):
        return jnp.concatenate(
            [q_ref[:, (g * GQA_GROUP + hh) * HEAD_DIM:(g * GQA_GROUP + hh + 1) * HEAD_DIM]
             for hh in range(GQA_GROUP)], axis=0).astype(BF16)

    def accumulate(g, lg, sel, vals):
        rs = slice(g * grows, (g + 1) * grows)
        m_old = m_scr[rs, :]
        m_new = jnp.maximum(m_old, jnp.max(jnp.where(sel, lg, NEG_BIG), axis=1, keepdims=True))
        p = jnp.where(sel, jnp.exp(lg - m_new), 0.0)
        alpha = jnp.exp(m_old - m_new)
        l_scr[rs, :] = alpha * l_scr[rs, :] + jnp.sum(p, axis=1, keepdims=True)
        acc_scr[rs, :] = alpha * acc_scr[rs, :] + _dot(p.astype(BF16), vals)
        m_scr[rs, :] = m_new

    for r in range(ATTN_PAGES):
        for g in range(N_KV_HEADS):
            rows = pl.ds(g, PAGE_SIZE, stride=N_KV_HEADS)
            kbuf[g, r * PAGE_SIZE:(r + 1) * PAGE_SIZE, :] = kpages[slot, r, rows, :].astype(BF16)
            vbuf[g, r * PAGE_SIZE:(r + 1) * PAGE_SIZE, :] = vpages[slot, r, rows, :].astype(BF16)

    col = lax.broadcasted_iota(I32, (nq, width), 1)
    sel = select(skp_ref[...], col + s * width)
    colg = lax.broadcasted_iota(I32, (grows, width), 1)
    in_near = (s == last) & (colg >= width - PAGE_SIZE)
    for g in range(N_KV_HEADS):
        near = jnp.tile(near_bias(g, 0, LANES), (1, ATTN_PAGES))
        bias = jnp.where(in_near, near, far_bias(g))
        lg = _dot_nt(group_queries(g), kbuf[g]) * scale + bias
        accumulate(g, lg, sel, vbuf[g])

    @pl.when(s == last)
    def _():
        pad = jnp.zeros((LANES - nq, D_KV), F32)
        kn = jnp.concatenate([kn_ref[...], pad], axis=0).astype(BF16)
        vn = jnp.concatenate([vn_ref[...], pad], axis=0).astype(BF16)
        coln = lax.broadcasted_iota(I32, (nq, LANES), 1) + past
        seln = select(skn_ref[...], coln)
        for g in range(N_KV_HEADS):
            cs = slice(g * HEAD_DIM, (g + 1) * HEAD_DIM)
            lg = _dot_nt(group_queries(g), kn[:, cs]) * scale + near_bias(g, LANES, NEAR)
            accumulate(g, lg, seln, vn[:, cs])
        out = acc_scr[...] / l_scr[...]
        for h in range(N_HEADS):
            o_ref[:, h * HEAD_DIM:(h + 1) * HEAD_DIM] = out[h * nq:(h + 1) * nq, :]


def _sample_attn(page_table, proj_s, sk_past, sk_new, thr, lim, btab, pool_k, pool_v, *, batch, nq, n_pages):
    steps = n_pages // ATTN_PAGES
    width = ATTN_PAGES * PAGE_SIZE
    past = n_pages * PAGE_SIZE
    page_buf = pltpu.VMEM((2, ATTN_PAGES, PAGE_SIZE * N_KV_HEADS, HEAD_DIM), F32)
    grid_spec = pltpu.PrefetchScalarGridSpec(
        num_scalar_prefetch=1,
        grid=(batch, steps),
        in_specs=[
            pl.BlockSpec((nq, D_ATTN), lambda b, s, pt: (b, COL_Q // D_ATTN)),
            pl.BlockSpec((nq, D_KV), lambda b, s, pt: (b, COL_K // D_KV)),
            pl.BlockSpec((nq, D_KV), lambda b, s, pt: (b, COL_V // D_KV)),
            pl.BlockSpec((None, nq, width), lambda b, s, pt: (b, 0, s)),
            pl.BlockSpec((None, nq, LANES), lambda b, s, pt: (b, 0, 0)),
            pl.BlockSpec((nq, LANES), lambda b, s, pt: (b, 0)),
            pl.BlockSpec((nq, LANES), lambda b, s, pt: (b, 0)),
            pl.BlockSpec((N_HEADS, LANES, NEAR), lambda b, s, pt: (0, 0, 0)),
            pl.BlockSpec(memory_space=pl.ANY),
            pl.BlockSpec(memory_space=pl.ANY),
        ],
        out_specs=pl.BlockSpec((nq, D_ATTN), lambda b, s, pt: (b, 0)),
        scratch_shapes=[
            page_buf, page_buf, pltpu.SemaphoreType.DMA((2,)), pltpu.SemaphoreType.DMA((2,)),
            pltpu.VMEM((N_KV_HEADS, width, HEAD_DIM), BF16),
            pltpu.VMEM((N_KV_HEADS, width, HEAD_DIM), BF16),
            pltpu.VMEM((N_HEADS * nq, 1), F32),
            pltpu.VMEM((N_HEADS * nq, 1), F32),
            pltpu.VMEM((N_HEADS * nq, HEAD_DIM), F32),
        ],
    )
    return pl.pallas_call(
        functools.partial(_sample_attn_kernel, nq=nq, past=past),
        grid_spec=grid_spec,
        out_shape=jax.ShapeDtypeStruct((batch * nq, D_ATTN), F32),
        compiler_params=_params(("arbitrary", "arbitrary")),
        name="sample_attn",
    )(page_table, proj_s, proj_s, proj_s, sk_past, sk_new, thr, lim, btab, pool_k, pool_v)


TAIL = 8


def _ssd_kernel(xbc_ref, z_ref, sm_ref, cprev_ref, hprev_ref, cw_ref, cb_ref, alog_ref, dtb_ref, dvec_ref,
                g_ref, y_ref, cnew_ref, hnew_ref, xpad_scr, sm_scr, h_scr, yd_scr, *, rows):
    c = pl.program_id(1)
    q = CHUNK
    keep = CONV_WIDTH - 1
    heads_per_group = SSD_HEADS // SSD_GROUPS

    @pl.when(c == 0)
    def _():
        xpad_scr[...] = jnp.zeros_like(xpad_scr)
        sm_scr[...] = jnp.zeros_like(sm_scr)
        xpad_scr[TAIL - keep:TAIL, :] = cprev_ref[0]
        h_scr[...] = hprev_ref[0]

    xpad_scr[TAIL:TAIL + rows, :] = xbc_ref[...]
    sm_scr[0:rows, :] = sm_ref[...]
    conv = sum(xpad_scr[TAIL - keep + k:TAIL - keep + k + q, :] * cw_ref[k:k + 1, :]
               for k in range(CONV_WIDTH)) + cb_ref[...]
    new_tail = xpad_scr[rows + TAIL - keep:rows + TAIL, :]
    cnew_ref[0] = new_tail
    xpad_scr[TAIL - keep:TAIL, :] = new_tail

    xc = conv * jax.nn.sigmoid(conv)
    xs = xc[:, :D_SSD]
    bm = xc[:, D_SSD:D_SSD + SSD_GROUPS * D_STATE].astype(BF16)
    cm = xc[:, D_SSD + SSD_GROUPS * D_STATE:].astype(BF16)

    ri = lax.broadcasted_iota(I32, (q, q), 0)
    ci = lax.broadcasted_iota(I32, (q, q), 1)
    causal = ri >= ci
    x = sm_scr[:, SM_DT:SM_DT + SSD_HEADS] + dtb_ref[...]
    dt = jnp.maximum(x, 0.0) + jnp.log1p(jnp.exp(-jnp.abs(x)))
    dt = jnp.where(lax.broadcasted_iota(I32, (q, SSD_HEADS), 0) < rows, dt, 0.0)
    a = -jnp.exp(alog_ref[...])
    acum = _select_dot(causal.astype(BF16), dt * a)
    acum_t = jnp.concatenate([acum, jnp.zeros((q, LANES - SSD_HEADS), F32)], axis=1).T

    expand = (lax.broadcasted_iota(I32, (SSD_HEADS, D_SSD), 1) // SSD_HEAD_DIM
              == lax.broadcasted_iota(I32, (SSD_HEADS, D_SSD), 0)).astype(BF16)
    expand_t = (lax.broadcasted_iota(I32, (D_SSD, SSD_HEADS), 0) // SSD_HEAD_DIM
                == lax.broadcasted_iota(I32, (D_SSD, SSD_HEADS), 1)).astype(BF16)
    acum_x = _select_dot(acum, expand)
    xd = xs * _select_dot(dt, expand)
    xdw_t = (xd * jnp.exp(acum_x[q - 1:q, :] - acum_x)).T.astype(BF16)
    xd = xd.astype(BF16)
    state_decay = jnp.exp(_select_dot(expand_t, jnp.broadcast_to(acum_t[0:SSD_HEADS, q - 1:q], (SSD_HEADS, LANES))))

    for g in range(SSD_GROUPS):
        ns = slice(g * D_STATE, (g + 1) * D_STATE)
        hs = slice(g * heads_per_group * SSD_HEAD_DIM, (g + 1) * heads_per_group * SSD_HEAD_DIM)
        cb = _dot_nt(cm[:, ns], bm[:, ns])
        for e in range(heads_per_group):
            h = g * heads_per_group + e
            ps = slice(h * SSD_HEAD_DIM, (h + 1) * SSD_HEAD_DIM)
            seg = acum[:, h:h + 1] - acum_t[h:h + 1, :]
            lmat = jnp.exp(jnp.where(causal, seg, -jnp.inf))
            yd_scr[:, ps] = _dot((cb * lmat).astype(BF16), xd[:, ps])
        h_old = h_scr[hs, :]
        yd_scr[:, hs] += _dot_nt(cm[:, ns], h_old.astype(BF16)) * jnp.exp(acum_x[:, hs])
        h_scr[hs, :] = h_old * state_decay[hs, :] + _dot(xdw_t[hs, :], bm[:, ns])

    hnew_ref[0] = h_scr[...]
    y = yd_scr[0:rows, :] + dvec_ref[...] * xs[0:rows, :]
    zz = z_ref[...]
    y = y * (zz * jax.nn.sigmoid(zz))
    width = D_SSD // SSD_GROUPS
    for g in range(SSD_GROUPS):
        cs = slice(g * width, (g + 1) * width)
        y_ref[:, cs] = _rms(y[:, cs], g_ref[:, cs])


def _ssd(proj, conv_prev, ssm_prev, conv_w, conv_b, a_log, dt_bias, d_vec, g_ssd, *, batch, seq):
    rows = min(CHUNK, seq)
    nc = seq // rows
    full = lambda *shape: pl.BlockSpec(shape, lambda b, c: (0,) * len(shape))
    return pl.pallas_call(
        functools.partial(_ssd_kernel, rows=rows),
        grid=(batch, nc),
        in_specs=[
            pl.BlockSpec((rows, CONV_DIM), lambda b, c: (b * nc + c, COL_XBC // CONV_DIM)),
            pl.BlockSpec((rows, D_SSD), lambda b, c: (b * nc + c, COL_Z // D_SSD)),
            pl.BlockSpec((rows, LANES), lambda b, c: (b * nc + c, COL_SMALL // LANES)),
            pl.BlockSpec((1, CONV_WIDTH - 1, CONV_DIM), lambda b, c: (b, 0, 0)),
            pl.BlockSpec((1, D_SSD, D_STATE), lambda b, c: (b, 0, 0)),
            full(CONV_WIDTH, CONV_DIM), full(1, CONV_DIM), full(1, SSD_HEADS), full(1, SSD_HEADS),
            full(1, D_SSD), full(1, D_SSD),
        ],
        out_specs=[
            pl.BlockSpec((rows, D_SSD), lambda b, c: (b * nc + c, 0)),
            pl.BlockSpec((1, CONV_WIDTH - 1, CONV_DIM), lambda b, c: (b, 0, 0)),
            pl.BlockSpec((1, D_SSD, D_STATE), lambda b, c: (b, 0, 0)),
        ],
        out_shape=[
            jax.ShapeDtypeStruct((batch * seq, D_SSD), F32),
            jax.ShapeDtypeStruct((batch, CONV_WIDTH - 1, CONV_DIM), F32),
            jax.ShapeDtypeStruct((batch, D_SSD, D_STATE), F32),
        ],
        scratch_shapes=[
            pltpu.VMEM((CHUNK + TAIL, CONV_DIM), F32),
            pltpu.VMEM((CHUNK, LANES), F32),
            pltpu.VMEM((D_SSD, D_STATE), F32),
            pltpu.VMEM((CHUNK, D_SSD), F32),
        ],
        compiler_params=_params(("parallel", "arbitrary")),
        name="ssd",
    )(proj, proj, proj, conv_prev, ssm_prev, conv_w, conv_b, a_log, dt_bias, d_vec, g_ssd)


def _reorder_w_in(w_in):
    offs = [0]
    for s in IN_SPLITS:
        offs.append(offs[-1] + s)
    q, k, v, qi, ki, wi, z, xbc, dtr = [w_in[:, offs[n]:offs[n + 1]] for n in range(len(IN_SPLITS))]
    parts = [t.astype(BF16) for t in (q, qi, z, xbc, k, v, ki, wi, dtr)]
    parts.append(jnp.zeros((w_in.shape[0], N_IN - sum(IN_SPLITS)), BF16))
    return jnp.concatenate(parts, axis=1)


def kernel(x_prompt, x_sample, cache_k, cache_v, cache_kidx, state_conv, state_ssm, page_table, rel_bias, g_ffn1, w1_ffn1, w3_ffn1, w2_ffn1, g_mix, w_in, conv_w, conv_b, a_log, dt_bias, d_skip, g_ssd, w_out, g_ffn2, w1_ffn2, w3_ffn2, w2_ffn2, g_final):
    depth = w_in.shape[0]
    assert depth == 1
    l = 0
    bp, seq, _ = x_prompt.shape
    bs, nq, _ = x_sample.shape
    n_pages = page_table.shape[1]
    row = lambda t: t.reshape(1, -1)

    w1a, w3a, w2a = w1_ffn1[l], w3_ffn1[l], w2_ffn1[l]
    w1b, w3b, w2b = w1_ffn2[l], w3_ffn2[l], w2_ffn2[l]
    w_in_r = _reorder_w_in(w_in[l])
    w_out_a = w_out[l, :D_ATTN].astype(BF16)
    w_out_s = w_out[l, D_ATTN:].astype(BF16)
    d_vec = jnp.repeat(d_skip[l], SSD_HEAD_DIM).reshape(1, D_SSD)
    btab = _bias_table(rel_bias)
    gf = row(g_final)

    def trunk(x, attend, conv_prev, ssm_prev, batch, length):
        tm = min(x.shape[0], 1024)
        tf = 256 if tm > 512 else 512
        x = _ffn(x, row(g_ffn1[l]), w1a, w3a, w2a, gf, tm=tm, tf=tf, final_norm=False)
        proj = _in_proj(x, row(g_mix[l]), w_in_r, tm=tm)
        a_out = attend(proj)
        s_out, conv_new, ssm_new = _ssd(
            proj, conv_prev, ssm_prev.reshape(batch, D_SSD, D_STATE), conv_w[l], row(conv_b[l]), row(a_log[l]),
            row(dt_bias[l]), d_vec, row(g_ssd[l]), batch=batch, seq=length)
        x = _out_proj(x, a_out, s_out, w_out_a, w_out_s, tm=min(tm, 512))
        y = _ffn(x, row(g_ffn2[l]), w1b, w3b, w2b, gf, tm=tm, tf=tf, final_norm=True)
        k = proj[:, COL_K:COL_K + D_KV].reshape(1, batch, length, N_KV_HEADS, HEAD_DIM)
        v = proj[:, COL_V:COL_V + D_KV].reshape(1, batch, length, N_KV_HEADS, HEAD_DIM)
        ki = proj[:, COL_SMALL + SM_KI:COL_SMALL + SM_KI + IDX_DIM].reshape(1, batch, length, IDX_DIM)
        ssm_new = ssm_new.reshape(1, batch, SSD_HEADS, SSD_HEAD_DIM, D_STATE)
        return y.reshape(batch, length, D_MODEL), k, v, ki, conv_new[None], ssm_new

    y_p, k_p, v_p, ki_p, conv_p, ssm_p = trunk(
        x_prompt.reshape(bp * seq, D_MODEL),
        functools.partial(_dsa_prompt, btab=btab, batch=bp, seq=seq),
        jnp.zeros((bp, CONV_WIDTH - 1, CONV_DIM), F32), jnp.zeros((bp, SSD_HEADS, SSD_HEAD_DIM, D_STATE), F32),
        bp, seq)

    pool_k = cache_k[l].reshape(-1, HEAD_DIM)
    pool_v = cache_v[l].reshape(-1, HEAD_DIM)
    pool_kidx = jnp.swapaxes(cache_kidx[l], 1, 2).reshape(-1, PAGE_SIZE)

    def sample_attend(proj):
        sk_past, sk_new = _sample_score(page_table, proj, pool_kidx, batch=bs, nq=nq, n_pages=n_pages)
        thr, lim = _sample_select(sk_past.reshape(bs * nq, -1), sk_new.reshape(bs * nq, LANES),
                                  rows_per_step=128)
        return _sample_attn(page_table, proj, sk_past, sk_new, thr, lim, btab, pool_k, pool_v,
                            batch=bs, nq=nq, n_pages=n_pages)

    y_s, k_s, v_s, ki_s, conv_s, ssm_s = trunk(
        x_sample.reshape(bs * nq, D_MODEL), sample_attend, state_conv[l], state_ssm[l], bs, nq)

    return (y_p, y_s, k_p, v_p, ki_p, conv_p, ssm_p, k_s, v_s, ki_s, conv_s, ssm_s)
```

```python
import functools
import math

import jax
import jax.numpy as jnp
from jax import lax
from jax.experimental import pallas as pl
from jax.experimental.pallas import tpu as pltpu

F32 = jnp.float32
BF16 = jnp.bfloat16
I32 = jnp.int32

D_MODEL = 2048
PAGE_SIZE = 128
N_HEADS = 8
HEAD_DIM = 128
N_KV_HEADS = 2
GQA_GROUP = N_HEADS // N_KV_HEADS
IDX_HEADS = 16
IDX_DIM = 64
TOPK = 256
N_BUCKETS = 32
MAX_DISTANCE = 128
SSD_HEADS = 16
SSD_HEAD_DIM = 64
SSD_GROUPS = 2
D_STATE = 128
CONV_WIDTH = 4
CHUNK = 128
D_ATTN = N_HEADS * HEAD_DIM
D_SSD = SSD_HEADS * SSD_HEAD_DIM
D_KV = N_KV_HEADS * HEAD_DIM
CONV_DIM = D_SSD + 2 * SSD_GROUPS * D_STATE
D_FF = 5632
EPS = 1e-6
IN_SPLITS = (D_ATTN, D_KV, D_KV, IDX_HEADS * IDX_DIM, IDX_DIM, IDX_HEADS, D_SSD, CONV_DIM, SSD_HEADS)

LANES = 128
COL_Q = 0
COL_QI = COL_Q + D_ATTN
COL_Z = COL_QI + IDX_HEADS * IDX_DIM
COL_XBC = COL_Z + D_SSD
COL_K = COL_XBC + CONV_DIM
COL_V = COL_K + D_KV
COL_SMALL = COL_V + D_KV
SM_KI = 0
SM_WI = SM_KI + IDX_DIM
SM_DT = SM_WI + IDX_HEADS
IN_TILE = 768
N_IN = ((COL_SMALL + LANES + IN_TILE - 1) // IN_TILE) * IN_TILE

INT_MIN = -(2 ** 31)
NEG_BIG = -1e30
VMEM_LIMIT = 60000 * 1024


def _params(sem):
    return pltpu.CompilerParams(dimension_semantics=sem, vmem_limit_bytes=VMEM_LIMIT)


def _rms(x, g):
    return x * lax.rsqrt(jnp.mean(x * x, axis=-1, keepdims=True) + EPS) * g


def _dot(a, b):
    return jnp.dot(a, b, preferred_element_type=F32)


def _dot_nt(a, b):
    return lax.dot_general(a, b, (((1,), (1,)), ((), ())), preferred_element_type=F32)


def _split3(x):
    hi = x.astype(BF16)
    rest = x - hi.astype(F32)
    mid = rest.astype(BF16)
    return hi, mid, (rest - mid.astype(F32)).astype(BF16)


def _select_dot(a, b):
    if a.dtype == BF16:
        return sum(_dot(a, part) for part in _split3(b))
    return sum(_dot(part, b) for part in _split3(a))


def _sort_key(x):
    bits = lax.bitcast_convert_type(x + 0.0, I32)
    return bits ^ ((bits >> 31) & 0x7FFFFFFF)


def _ffn_kernel(x_ref, g_ref, gf_ref, w1_hbm, w3_hbm, w2_hbm, o_ref, h_scr, w1_buf, w3_buf, w2_buf, sem, *,
                tf, final_norm):
    i = pl.program_id(0)
    nf = D_FF // tf
    total = pl.num_programs(0) * nf

    def copies(f, slot):
        c0 = pl.multiple_of(f * tf, tf)
        return (pltpu.make_async_copy(w1_hbm.at[:, pl.ds(c0, tf)], w1_buf.at[slot], sem.at[0, slot]),
                pltpu.make_async_copy(w3_hbm.at[:, pl.ds(c0, tf)], w3_buf.at[slot], sem.at[1, slot]),
                pltpu.make_async_copy(w2_hbm.at[pl.ds(c0, tf), :], w2_buf.at[slot], sem.at[2, slot]))

    @pl.when(i == 0)
    def _():
        for c in copies(0, 0):
            c.start()

    h_scr[...] = _rms(x_ref[...], g_ref[...]).astype(BF16)
    o_ref[...] = jnp.zeros_like(o_ref)

    def body(f, carry):
        t = i * nf + f
        slot = lax.rem(t, 2)

        @pl.when(t + 1 < total)
        def _():
            for c in copies(jnp.where(f + 1 == nf, 0, f + 1), 1 - slot):
                c.start()

        for c in copies(f, slot):
            c.wait()
        h = h_scr[...]
        a = _dot(h, w1_buf[slot].astype(BF16))
        b = _dot(h, w3_buf[slot].astype(BF16))
        u = (a * jax.nn.sigmoid(a) * b).astype(BF16)
        o_ref[...] += _dot(u, w2_buf[slot].astype(BF16))
        return carry

    lax.fori_loop(0, nf, body, 0)
    y = x_ref[...] + 0.5 * o_ref[...]
    if final_norm:
        y = _rms(y, gf_ref[...])
    o_ref[...] = y


def _ffn(x, g, w1, w3, w2, gf, *, tm, tf, final_norm):
    t = x.shape[0]
    return pl.pallas_call(
        functools.partial(_ffn_kernel, tf=tf, final_norm=final_norm),
        grid=(t // tm,),
        in_specs=[
            pl.BlockSpec((tm, D_MODEL), lambda i: (i, 0)),
            pl.BlockSpec((1, D_MODEL), lambda i: (0, 0)),
            pl.BlockSpec((1, D_MODEL), lambda i: (0, 0)),
            pl.BlockSpec(memory_space=pl.ANY),
            pl.BlockSpec(memory_space=pl.ANY),
            pl.BlockSpec(memory_space=pl.ANY),
        ],
        out_specs=pl.BlockSpec((tm, D_MODEL), lambda i: (i, 0)),
        out_shape=jax.ShapeDtypeStruct((t, D_MODEL), F32),
        scratch_shapes=[
            pltpu.VMEM((tm, D_MODEL), BF16),
            pltpu.VMEM((2, D_MODEL, tf), F32),
            pltpu.VMEM((2, D_MODEL, tf), F32),
            pltpu.VMEM((2, tf, D_MODEL), F32),
            pltpu.SemaphoreType.DMA((3, 2)),
        ],
        compiler_params=_params(("arbitrary",)),
        name="ffn",
    )(x, g, gf, w1, w3, w2)


def _in_proj_kernel(x_ref, g_ref, w_ref, o_ref, h_scr):
    @pl.when(pl.program_id(1) == 0)
    def _():
        h_scr[...] = _rms(x_ref[...], g_ref[...]).astype(BF16)

    o_ref[...] = _dot(h_scr[...], w_ref[...])


def _in_proj(x, g, w, *, tm):
    t = x.shape[0]
    return pl.pallas_call(
        _in_proj_kernel,
        grid=(t // tm, N_IN // IN_TILE),
        in_specs=[
            pl.BlockSpec((tm, D_MODEL), lambda i, j: (i, 0)),
            pl.BlockSpec((1, D_MODEL), lambda i, j: (0, 0)),
            pl.BlockSpec((D_MODEL, IN_TILE), lambda i, j: (0, j)),
        ],
        out_specs=pl.BlockSpec((tm, IN_TILE), lambda i, j: (i, j)),
        out_shape=jax.ShapeDtypeStruct((t, N_IN), F32),
        scratch_shapes=[pltpu.VMEM((tm, D_MODEL), BF16)],
        compiler_params=_params(("parallel", "arbitrary")),
        name="in_proj",
    )(x, g, w)


def _out_proj_kernel(x_ref, a_ref, s_ref, wa_ref, ws_ref, o_ref):
    acc = _dot(a_ref[...].astype(BF16), wa_ref[...])
    acc += _dot(s_ref[...].astype(BF16), ws_ref[...])
    o_ref[...] = x_ref[...] + acc


def _out_proj(x, a, s, wa, ws, *, tm):
    t = x.shape[0]
    return pl.pallas_call(
        _out_proj_kernel,
        grid=(t // tm,),
        in_specs=[
            pl.BlockSpec((tm, D_MODEL), lambda i: (i, 0)),
            pl.BlockSpec((tm, D_ATTN), lambda i: (i, 0)),
            pl.BlockSpec((tm, D_SSD), lambda i: (i, 0)),
            pl.BlockSpec((D_ATTN, D_MODEL), lambda i: (0, 0)),
            pl.BlockSpec((D_SSD, D_MODEL), lambda i: (0, 0)),
        ],
        out_specs=pl.BlockSpec((tm, D_MODEL), lambda i: (i, 0)),
        out_shape=jax.ShapeDtypeStruct((t, D_MODEL), F32),
        compiler_params=_params(("parallel",)),
        name="out_proj",
    )(x, a, s, wa, ws)


NEAR = 2 * LANES


def _bias_kernel(rb_ref, o_ref):
    r = lax.broadcasted_iota(I32, (LANES, NEAR), 0)
    c = lax.broadcasted_iota(I32, (LANES, NEAR), 1)
    n = jnp.maximum(r + LANES - c, 0)
    max_exact = N_BUCKETS // 2
    nf = jnp.maximum(n, 1).astype(F32)
    large = max_exact + (jnp.log(nf / max_exact) / math.log(MAX_DISTANCE / max_exact)
                         * (N_BUCKETS - max_exact)).astype(I32)
    bucket = jnp.where(n < max_exact, n, jnp.minimum(large, N_BUCKETS - 1))
    for h in range(N_HEADS):
        acc = jnp.zeros((LANES, NEAR), F32)
        for b in range(N_BUCKETS):
            acc = jnp.where(bucket == b, rb_ref[b, h], acc)
        o_ref[h] = acc


def _bias_table(rel_bias):
    return pl.pallas_call(
        _bias_kernel,
        in_specs=[pl.BlockSpec(memory_space=pltpu.SMEM)],
        out_specs=pl.BlockSpec(memory_space=pltpu.VMEM),
        out_shape=jax.ShapeDtypeStruct((N_HEADS, LANES, NEAR), F32),
        name="bias_table",
    )(rel_bias)


def _count(mask, axis=1):
    x = mask.astype(F32)
    if axis == 0:
        x = jnp.sum(x.reshape(x.shape[0] // 64, 64, x.shape[1]), axis=0)
    return jnp.sum(x, axis=axis, keepdims=True)


def _kth_largest(count_ge, shape):
    t = jnp.full(shape, INT_MIN, I32)
    t = jnp.where(count_ge(jnp.zeros(shape, I32)) >= TOPK, 0, t)

    def body(it, t):
        cand = t + lax.shift_left(jnp.int32(1), 30 - it)
        return jnp.where(count_ge(cand) >= TOPK, cand, t)

    return lax.fori_loop(0, 31, body, t)


def _tie_limit(count_eq_below, need, shape, idx_bits):
    def body(it, m):
        cand = m + lax.shift_left(jnp.int32(1), idx_bits - 1 - it)
        return jnp.where(count_eq_below(cand) < need, cand, m)

    return lax.fori_loop(0, idx_bits, body, jnp.zeros(shape, I32))


CAUSAL_VARIANTS = 4


def _dsa_prompt_block(width, near_tiles, i, q_ref, qi_ref, k_ref, v_ref, sm_ref, bt_ref, o_ref, sk_scr, neg_scr):
    tq = LANES
    q0 = pl.multiple_of(i * LANES, LANES)
    shape = (1, tq)

    wi_t = sm_ref[pl.ds(q0, tq), :].T[SM_WI:SM_WI + IDX_HEADS, :] * (IDX_HEADS * IDX_DIM) ** -0.5
    head_group = 4
    qi_groups = [
        jnp.concatenate([qi_ref[:, h * IDX_DIM:(h + 1) * IDX_DIM] for h in range(hg, hg + head_group)],
                        axis=0).astype(BF16)
        for hg in range(0, IDX_HEADS, head_group)]
    chunk = 2 * LANES
    pos = lax.broadcasted_iota(I32, (chunk, tq), 1) + q0
    for c in range(0, width, chunk):
        ki = sm_ref[c:c + chunk, SM_KI:SM_KI + IDX_DIM].astype(BF16)
        score = jnp.zeros((chunk, tq), F32)
        for n, hg in enumerate(range(0, IDX_HEADS, head_group)):
            rel = _dot_nt(ki, qi_groups[n])
            for e in range(head_group):
                score = score + wi_t[hg + e:hg + e + 1, :] * jnp.maximum(rel[:, e * tq:(e + 1) * tq], 0.0)
        key_c = lax.broadcasted_iota(I32, (chunk, tq), 0) + c
        sk_scr[c:c + chunk, :] = jnp.where(key_c <= pos, _sort_key(score), INT_MIN)

    key = lax.broadcasted_iota(I32, (width, tq), 0)

    def sk():
        return sk_scr[0:width, :]

    thr = _kth_largest(lambda c: _count(sk() >= c, 0), shape)
    excess = (_count(sk() >= thr, 0) > TOPK) & (thr > INT_MIN)
    any_excess = jnp.max(excess.astype(F32)) > 0.0

    @pl.when(jnp.logical_not(any_excess))
    def _():
        neg_scr[0:width, :] = jnp.where(sk() >= jnp.maximum(thr, INT_MIN + 1), 0.0, -jnp.inf)

    @pl.when(any_excess)
    def _():
        need = TOPK - _count(sk() > thr, 0)
        lim = _tie_limit(lambda m: _count((sk() == thr) & (key < m), 0), need, shape, (width - 1).bit_length())
        take = (sk() > thr) | ((sk() == thr) & (key <= lim) & (sk() > INT_MIN))
        neg_scr[0:width, :] = jnp.where(take, 0.0, -jnp.inf)

    log2e = math.log2(math.e)
    scale = HEAD_DIM ** -0.5 * log2e
    tiles = width // LANES
    for g in range(N_KV_HEADS):
        heads = range(g * GQA_GROUP, (g + 1) * GQA_GROUP)
        kg = k_ref[0:width, g * HEAD_DIM:(g + 1) * HEAD_DIM].astype(BF16)
        vg_t = v_ref[0:width, g * HEAD_DIM:(g + 1) * HEAD_DIM].T.astype(BF16)
        qg = jnp.concatenate([q_ref[:, h * HEAD_DIM:(h + 1) * HEAD_DIM] for h in heads], axis=0).astype(BF16)
        diag_t = [((bt_ref[h, :, LANES:NEAR] - bt_ref[h, 0:1, 0:1]) * log2e).T for h in heads]
        prev_t = [((bt_ref[h, :, 0:LANES] - bt_ref[h, 0:1, 0:1]) * log2e).T for h in heads]
        qk = _dot_nt(kg, qg)
        logits = []
        for t in range(tiles):
            rows = slice(t * LANES, (t + 1) * LANES)
            per_head = []
            for e in range(GQA_GROUP):
                lt = qk[rows, e * tq:(e + 1) * tq] * scale + neg_scr[rows, :]
                if t >= tiles - near_tiles:
                    lt = lt + jnp.where(i == t, diag_t[e], jnp.where(i - 1 == t, prev_t[e], 0.0))
                per_head.append(lt)
            logits.append(per_head)
        p_t = []
        inv_l = []
        for e in range(GQA_GROUP):
            m = jnp.max(functools.reduce(jnp.maximum, [lt[e] for lt in logits]), axis=0, keepdims=True)
            probs = [jnp.exp2(lt[e] - m) for lt in logits]
            inv_l.append(1.0 / jnp.sum(functools.reduce(jnp.add, probs), axis=0, keepdims=True))
            p_t.append(jnp.concatenate([p.astype(BF16) for p in probs], axis=0))
        out_t = _dot(vg_t, jnp.concatenate(p_t, axis=1))
        for e, h in enumerate(heads):
            o_ref[:, h * HEAD_DIM:(h + 1) * HEAD_DIM] = (out_t[:, e * tq:(e + 1) * tq] * inv_l[e]).T


def _dsa_prompt_kernel(*refs, seq):
    i = pl.program_id(1)
    per = seq // LANES // CAUSAL_VARIANTS
    for v in range(CAUSAL_VARIANTS):
        @pl.when(i // per == v)
        def _(v=v):
            _dsa_prompt_block((v + 1) * per * LANES, per + 1, i, *refs)


def _dsa_prompt(proj, btab, *, batch, seq):
    nq = seq // LANES
    return pl.pallas_call(
        functools.partial(_dsa_prompt_kernel, seq=seq),
        grid=(batch, nq),
        in_specs=[
            pl.BlockSpec((LANES, D_ATTN), lambda b, i: (b * nq + i, COL_Q // D_ATTN)),
            pl.BlockSpec((LANES, IDX_HEADS * IDX_DIM), lambda b, i: (b * nq + i, COL_QI // (IDX_HEADS * IDX_DIM))),
            pl.BlockSpec((seq, D_KV), lambda b, i: (b, COL_K // D_KV)),
            pl.BlockSpec((seq, D_KV), lambda b, i: (b, COL_V // D_KV)),
            pl.BlockSpec((seq, LANES), lambda b, i: (b, COL_SMALL // LANES)),
            pl.BlockSpec((N_HEADS, LANES, NEAR), lambda b, i: (0, 0, 0)),
        ],
        out_specs=pl.BlockSpec((LANES, D_ATTN), lambda b, i: (b * nq + i, 0)),
        out_shape=jax.ShapeDtypeStruct((batch * seq, D_ATTN), F32),
        scratch_shapes=[pltpu.VMEM((seq, LANES), I32), pltpu.VMEM((seq, LANES), F32)],
        compiler_params=_params(("parallel", "parallel")),
        name="dsa_prompt",
    )(proj, proj, proj, proj, proj, btab)


SCORE_PAGES = 64
ATTN_PAGES = 32


def _sample_queries(qi_ref):
    return jnp.concatenate(
        [qi_ref[:, h * IDX_DIM:(h + 1) * IDX_DIM] for h in range(IDX_HEADS)], axis=0).astype(BF16)


def _sample_scores(rel, wi, nq):
    sc = jnp.zeros((nq, rel.shape[1]), F32)
    for h in range(IDX_HEADS):
        sc = sc + wi[:, h:h + 1] * jnp.maximum(rel[h * nq:(h + 1) * nq, :], 0.0)
    return sc


def _page_copies(pt_ref, pools, bufs, sems, b, s, slot, pages, page_rows):
    copies = []
    for r in range(pages):
        row0 = pl.multiple_of(pt_ref[b, s * pages + r] * page_rows, page_rows)
        for pool, buf, sem in zip(pools, bufs, sems):
            copies.append(pltpu.make_async_copy(pool.at[pl.ds(row0, page_rows), :], buf.at[slot, r], sem.at[slot]))
    return copies


def _page_pipeline(pt_ref, pools, bufs, sems, pages, page_rows):
    b, s = pl.program_id(0), pl.program_id(1)
    steps = pl.num_programs(1)
    t = b * steps + s
    slot = lax.rem(t, 2)
    copies = functools.partial(_page_copies, pt_ref, pools, bufs, sems, pages=pages, page_rows=page_rows)

    @pl.when(t == 0)
    def _():
        for c in copies(b, s, slot):
            c.start()

    @pl.when(t + 1 < pl.num_programs(0) * steps)
    def _():
        wrap = s + 1 == steps
        for c in copies(jnp.where(wrap, b + 1, b), jnp.where(wrap, 0, s + 1), 1 - slot):
            c.start()

    for c in copies(b, s, slot):
        c.wait()
    return slot


def _sample_score_kernel(pt_ref, qi_ref, sm_ref, pool_ref, past_ref, new_ref, pages, sem, kbuf, *, nq):
    s = pl.program_id(1)
    slot = _page_pipeline(pt_ref, [pool_ref], [pages], [sem], SCORE_PAGES, IDX_DIM)
    qx = _sample_queries(qi_ref)
    wi = sm_ref[:, SM_WI:SM_WI + IDX_HEADS] * (IDX_HEADS * IDX_DIM) ** -0.5
    for r in range(SCORE_PAGES):
        kbuf[:, r * PAGE_SIZE:(r + 1) * PAGE_SIZE] = pages[slot, r].astype(BF16)
    past_ref[...] = _sort_key(_sample_scores(_dot(qx, kbuf[...]), wi, nq))

    @pl.when(s == pl.num_programs(1) - 1)
    def _():
        ki_new = jnp.concatenate(
            [sm_ref[:, SM_KI:SM_KI + IDX_DIM], jnp.zeros((LANES - nq, IDX_DIM), F32)], axis=0).astype(BF16)
        sc = _sample_scores(_dot_nt(qx, ki_new), wi, nq)
        j = lax.broadcasted_iota(I32, (nq, LANES), 1)
        t = lax.broadcasted_iota(I32, (nq, LANES), 0)
        new_ref[...] = jnp.where(j <= t, _sort_key(sc), INT_MIN)


def _sample_score(page_table, proj_s, pool_kidx_t, *, batch, nq, n_pages):
    steps = n_pages // SCORE_PAGES
    grid_spec = pltpu.PrefetchScalarGridSpec(
        num_scalar_prefetch=1,
        grid=(batch, steps),
        in_specs=[
            pl.BlockSpec((nq, IDX_HEADS * IDX_DIM), lambda b, s, pt: (b, COL_QI // (IDX_HEADS * IDX_DIM))),
            pl.BlockSpec((nq, LANES), lambda b, s, pt: (b, COL_SMALL // LANES)),
            pl.BlockSpec(memory_space=pl.ANY),
        ],
        out_specs=[
            pl.BlockSpec((None, nq, SCORE_PAGES * PAGE_SIZE), lambda b, s, pt: (b, 0, s)),
            pl.BlockSpec((None, nq, LANES), lambda b, s, pt: (b, 0, 0)),
        ],
        scratch_shapes=[
            pltpu.VMEM((2, SCORE_PAGES, IDX_DIM, PAGE_SIZE), F32),
            pltpu.SemaphoreType.DMA((2,)),
            pltpu.VMEM((IDX_DIM, SCORE_PAGES * PAGE_SIZE), BF16),
        ],
    )
    return pl.pallas_call(
        functools.partial(_sample_score_kernel, nq=nq),
        grid_spec=grid_spec,
        out_shape=[
            jax.ShapeDtypeStruct((batch, nq, n_pages * PAGE_SIZE), I32),
            jax.ShapeDtypeStruct((batch, nq, LANES), I32),
        ],
        compiler_params=_params(("arbitrary", "arbitrary")),
        name="sample_score",
    )(page_table, proj_s, proj_s, pool_kidx_t)


def _sample_select_kernel(past_ref, new_ref, thr_ref, lim_ref, *, past):
    rows = past_ref.shape[0]
    colp = lax.broadcasted_iota(I32, (rows, past), 1)
    coln = lax.broadcasted_iota(I32, (rows, LANES), 1) + past

    def count_ge(c):
        return _count(past_ref[...] >= c) + _count(new_ref[...] >= c)

    thr = _kth_largest(count_ge, (rows, 1))
    thr_ref[...] = jnp.broadcast_to(thr, (rows, LANES))
    lim_ref[...] = jnp.full((rows, LANES), 2 ** 31 - 1, I32)
    any_excess = jnp.max((count_ge(thr) > TOPK).astype(F32)) > 0.0

    @pl.when(any_excess)
    def _():
        need = TOPK - (_count(past_ref[...] > thr) + _count(new_ref[...] > thr))

        def count_eq_below(m):
            return (_count((past_ref[...] == thr) & (colp < m))
                    + _count((new_ref[...] == thr) & (coln < m)))

        lim = _tie_limit(count_eq_below, need, (rows, 1), (past + LANES - 1).bit_length())
        lim_ref[...] = jnp.broadcast_to(lim, (rows, LANES))


def _sample_select(sk_past, sk_new, *, rows_per_step):
    rows, past = sk_past.shape
    return pl.pallas_call(
        functools.partial(_sample_select_kernel, past=past),
        grid=(rows // rows_per_step,),
        in_specs=[
            pl.BlockSpec((rows_per_step, past), lambda i: (i, 0)),
            pl.BlockSpec((rows_per_step, LANES), lambda i: (i, 0)),
        ],
        out_specs=[
            pl.BlockSpec((rows_per_step, LANES), lambda i: (i, 0)),
            pl.BlockSpec((rows_per_step, LANES), lambda i: (i, 0)),
        ],
        out_shape=[jax.ShapeDtypeStruct((rows, LANES), I32)] * 2,
        compiler_params=_params(("parallel",)),
        name="sample_select",
    )(sk_past, sk_new)


def _sample_attn_kernel(pt_ref, q_ref, kn_ref, vn_ref, skp_ref, skn_ref, thr_ref, lim_ref, bt_ref, poolk_ref,
                        poolv_ref, o_ref, kpages, vpages, ksem, vsem, kbuf, vbuf, m_scr, l_scr, acc_scr, *,
                        nq, past):
    slot = _page_pipeline(pt_ref, [poolk_ref, poolv_ref], [kpages, vpages], [ksem, vsem], ATTN_PAGES,
                          PAGE_SIZE * N_KV_HEADS)
    s = pl.program_id(1)
    last = pl.num_programs(1) - 1
    width = ATTN_PAGES * PAGE_SIZE
    grows = GQA_GROUP * nq
    scale = HEAD_DIM ** -0.5

    @pl.when(s == 0)
    def _():
        m_scr[...] = jnp.full_like(m_scr, NEG_BIG)
        l_scr[...] = jnp.zeros_like(l_scr)
        acc_scr[...] = jnp.zeros_like(acc_scr)

    thr = thr_ref[:, 0:1]
    lim = lim_ref[:, 0:1]

    def select(sk, idx):
        sel = (sk > thr) | ((sk == thr) & (idx <= lim))
        return jnp.concatenate([sel] * GQA_GROUP, axis=0)

    def far_bias(g):
        return jnp.concatenate(
            [jnp.broadcast_to(bt_ref[g * GQA_GROUP + hh, 0:1, 0:1], (nq, 1)) for hh in range(GQA_GROUP)], axis=0)

    def near_bias(g, lo, hi):
        return jnp.concatenate([bt_ref[g * GQA_GROUP + hh, 0:nq, lo:hi] for hh in range(GQA_GROUP)], axis=0)

    def group_queries(g):
        return jnp.concatenate(
            [q_ref[:, (g * GQA_GROUP + hh) * HEAD_DIM:(g * GQA_GROUP + hh + 1) * HEAD_DIM]
             for hh in range(GQA_GROUP)], axis=0).astype(BF16)

    def accumulate(g, lg, sel, vals):
        rs = slice(g * grows, (g + 1) * grows)
        m_old = m_scr[rs, :]
        m_new = jnp.maximum(m_old, jnp.max(jnp.where(sel, lg, NEG_BIG), axis=1, keepdims=True))
        p = jnp.where(sel, jnp.exp(lg - m_new), 0.0)
        alpha = jnp.exp(m_old - m_new)
        l_scr[rs, :] = alpha * l_scr[rs, :] + jnp.sum(p, axis=1, keepdims=True)
        acc_scr[rs, :] = alpha * acc_scr[rs, :] + _dot(p.astype(BF16), vals)
        m_scr[rs, :] = m_new

    for r in range(ATTN_PAGES):
        for g in range(N_KV_HEADS):
            rows = pl.ds(g, PAGE_SIZE, stride=N_KV_HEADS)
            kbuf[g, r * PAGE_SIZE:(r + 1) * PAGE_SIZE, :] = kpages[slot, r, rows, :].astype(BF16)
            vbuf[g, r * PAGE_SIZE:(r + 1) * PAGE_SIZE, :] = vpages[slot, r, rows, :].astype(BF16)

    col = lax.broadcasted_iota(I32, (nq, width), 1)
    sel = select(skp_ref[...], col + s * width)
    colg = lax.broadcasted_iota(I32, (grows, width), 1)
    in_near = (s == last) & (colg >= width - PAGE_SIZE)
    for g in range(N_KV_HEADS):
        near = jnp.tile(near_bias(g, 0, LANES), (1, ATTN_PAGES))
        bias = jnp.where(in_near, near, far_bias(g))
        lg = _dot_nt(group_queries(g), kbuf[g]) * scale + bias
        accumulate(g, lg, sel, vbuf[g])

    @pl.when(s == last)
    def _():
        pad = jnp.zeros((LANES - nq, D_KV), F32)
        kn = jnp.concatenate([kn_ref[...], pad], axis=0).astype(BF16)
        vn = jnp.concatenate([vn_ref[...], pad], axis=0).astype(BF16)
        coln = lax.broadcasted_iota(I32, (nq, LANES), 1) + past
        seln = select(skn_ref[...], coln)
        for g in range(N_KV_HEADS):
            cs = slice(g * HEAD_DIM, (g + 1) * HEAD_DIM)
            lg = _dot_nt(group_queries(g), kn[:, cs]) * scale + near_bias(g, LANES, NEAR)
            accumulate(g, lg, seln, vn[:, cs])
        out = acc_scr[...] / l_scr[...]
        for h in range(N_HEADS):
            o_ref[:, h * HEAD_DIM:(h + 1) * HEAD_DIM] = out[h * nq:(h + 1) * nq, :]


def _sample_attn(page_table, proj_s, sk_past, sk_new, thr, lim, btab, pool_k, pool_v, *, batch, nq, n_pages):
    steps = n_pages // ATTN_PAGES
    width = ATTN_PAGES * PAGE_SIZE
    past = n_pages * PAGE_SIZE
    page_buf = pltpu.VMEM((2, ATTN_PAGES, PAGE_SIZE * N_KV_HEADS, HEAD_DIM), F32)
    grid_spec = pltpu.PrefetchScalarGridSpec(
        num_scalar_prefetch=1,
        grid=(batch, steps),
        in_specs=[
            pl.BlockSpec((nq, D_ATTN), lambda b, s, pt: (b, COL_Q // D_ATTN)),
            pl.BlockSpec((nq, D_KV), lambda b, s, pt: (b, COL_K // D_KV)),
            pl.BlockSpec((nq, D_KV), lambda b, s, pt: (b, COL_V // D_KV)),
            pl.BlockSpec((None, nq, width), lambda b, s, pt: (b, 0, s)),
            pl.BlockSpec((None, nq, LANES), lambda b, s, pt: (b, 0, 0)),
            pl.BlockSpec((nq, LANES), lambda b, s, pt: (b, 0)),
            pl.BlockSpec((nq, LANES), lambda b, s, pt: (b, 0)),
            pl.BlockSpec((N_HEADS, LANES, NEAR), lambda b, s, pt: (0, 0, 0)),
            pl.BlockSpec(memory_space=pl.ANY),
            pl.BlockSpec(memory_space=pl.ANY),
        ],
        out_specs=pl.BlockSpec((nq, D_ATTN), lambda b, s, pt: (b, 0)),
        scratch_shapes=[
            page_buf, page_buf, pltpu.SemaphoreType.DMA((2,)), pltpu.SemaphoreType.DMA((2,)),
            pltpu.VMEM((N_KV_HEADS, width, HEAD_DIM), BF16),
            pltpu.VMEM((N_KV_HEADS, width, HEAD_DIM), BF16),
            pltpu.VMEM((N_HEADS * nq, 1), F32),
            pltpu.VMEM((N_HEADS * nq, 1), F32),
            pltpu.VMEM((N_HEADS * nq, HEAD_DIM), F32),
        ],
    )
    return pl.pallas_call(
        functools.partial(_sample_attn_kernel, nq=nq, past=past),
        grid_spec=grid_spec,
        out_shape=jax.ShapeDtypeStruct((batch * nq, D_ATTN), F32),
        compiler_params=_params(("arbitrary", "arbitrary")),
        name="sample_attn",
    )(page_table, proj_s, proj_s, proj_s, sk_past, sk_new, thr, lim, btab, pool_k, pool_v)


TAIL = 8


def _ssd_kernel(xbc_ref, z_ref, sm_ref, cprev_ref, hprev_ref, cw_ref, cb_ref, alog_ref, dtb_ref, dvec_ref,
                g_ref, y_ref, cnew_ref, hnew_ref, xpad_scr, sm_scr, h_scr, yd_scr, *, rows):
    c = pl.program_id(1)
    q = CHUNK
    keep = CONV_WIDTH - 1
    heads_per_group = SSD_HEADS // SSD_GROUPS

    @pl.when(c == 0)
    def _():
        xpad_scr[...] = jnp.zeros_like(xpad_scr)
        sm_scr[...] = jnp.zeros_like(sm_scr)
        xpad_scr[TAIL - keep:TAIL, :] = cprev_ref[0]
        h_scr[...] = hprev_ref[0]

    xpad_scr[TAIL:TAIL + rows, :] = xbc_ref[...]
    sm_scr[0:rows, :] = sm_ref[...]
    conv = sum(xpad_scr[TAIL - keep + k:TAIL - keep + k + q, :] * cw_ref[k:k + 1, :]
               for k in range(CONV_WIDTH)) + cb_ref[...]
    new_tail = xpad_scr[rows + TAIL - keep:rows + TAIL, :]
    cnew_ref[0] = new_tail
    xpad_scr[TAIL - keep:TAIL, :] = new_tail

    xc = conv * jax.nn.sigmoid(conv)
    xs = xc[:, :D_SSD]
    bm = xc[:, D_SSD:D_SSD + SSD_GROUPS * D_STATE].astype(BF16)
    cm = xc[:, D_SSD + SSD_GROUPS * D_STATE:].astype(BF16)

    ri = lax.broadcasted_iota(I32, (q, q), 0)
    ci = lax.broadcasted_iota(I32, (q, q), 1)
    causal = ri >= ci
    x = sm_scr[:, SM_DT:SM_DT + SSD_HEADS] + dtb_ref[...]
    dt = jnp.maximum(x, 0.0) + jnp.log1p(jnp.exp(-jnp.abs(x)))
    dt = jnp.where(lax.broadcasted_iota(I32, (q, SSD_HEADS), 0) < rows, dt, 0.0)
    a = -jnp.exp(alog_ref[...])
    acum = _select_dot(causal.astype(BF16), dt * a)
    acum_t = jnp.concatenate([acum, jnp.zeros((q, LANES - SSD_HEADS), F32)], axis=1).T

    expand = (lax.broadcasted_iota(I32, (SSD_HEADS, D_SSD), 1) // SSD_HEAD_DIM
              == lax.broadcasted_iota(I32, (SSD_HEADS, D_SSD), 0)).astype(BF16)
    expand_t = (lax.broadcasted_iota(I32, (D_SSD, SSD_HEADS), 0) // SSD_HEAD_DIM
                == lax.broadcasted_iota(I32, (D_SSD, SSD_HEADS), 1)).astype(BF16)
    acum_x = _select_dot(acum, expand)
    xd = xs * _select_dot(dt, expand)
    xdw_t = (xd * jnp.exp(acum_x[q - 1:q, :] - acum_x)).T.astype(BF16)
    xd = xd.astype(BF16)
    state_decay = jnp.exp(_select_dot(expand_t, jnp.broadcast_to(acum_t[0:SSD_HEADS, q - 1:q], (SSD_HEADS, LANES))))

    for g in range(SSD_GROUPS):
        ns = slice(g * D_STATE, (g + 1) * D_STATE)
        hs = slice(g * heads_per_group * SSD_HEAD_DIM, (g + 1) * heads_per_group * SSD_HEAD_DIM)
        cb = _dot_nt(cm[:, ns], bm[:, ns])
        for e in range(heads_per_group):
            h = g * heads_per_group + e
            ps = slice(h * SSD_HEAD_DIM, (h + 1) * SSD_HEAD_DIM)
            seg = acum[:, h:h + 1] - acum_t[h:h + 1, :]
            lmat = jnp.exp(jnp.where(causal, seg, -jnp.inf))
            yd_scr[:, ps] = _dot((cb * lmat).astype(BF16), xd[:, ps])
        h_old = h_scr[hs, :]
        yd_scr[:, hs] += _dot_nt(cm[:, ns], h_old.astype(BF16)) * jnp.exp(acum_x[:, hs])
        h_scr[hs, :] = h_old * state_decay[hs, :] + _dot(xdw_t[hs, :], bm[:, ns])

    hnew_ref[0] = h_scr[...]
    y = yd_scr[0:rows, :] + dvec_ref[...] * xs[0:rows, :]
    zz = z_ref[...]
    y = y * (zz * jax.nn.sigmoid(zz))
    width = D_SSD // SSD_GROUPS
    for g in range(SSD_GROUPS):
        cs = slice(g * width, (g + 1) * width)
        y_ref[:, cs] = _rms(y[:, cs], g_ref[:, cs])


def _ssd(proj, conv_prev, ssm_prev, conv_w, conv_b, a_log, dt_bias, d_vec, g_ssd, *, batch, seq):
    rows = min(CHUNK, seq)
    nc = seq // rows
    full = lambda *shape: pl.BlockSpec(shape, lambda b, c: (0,) * len(shape))
    return pl.pallas_call(
        functools.partial(_ssd_kernel, rows=rows),
        grid=(batch, nc),
        in_specs=[
            pl.BlockSpec((rows, CONV_DIM), lambda b, c: (b * nc + c, COL_XBC // CONV_DIM)),
            pl.BlockSpec((rows, D_SSD), lambda b, c: (b * nc + c, COL_Z // D_SSD)),
            pl.BlockSpec((rows, LANES), lambda b, c: (b * nc + c, COL_SMALL // LANES)),
            pl.BlockSpec((1, CONV_WIDTH - 1, CONV_DIM), lambda b, c: (b, 0, 0)),
            pl.BlockSpec((1, D_SSD, D_STATE), lambda b, c: (b, 0, 0)),
            full(CONV_WIDTH, CONV_DIM), full(1, CONV_DIM), full(1, SSD_HEADS), full(1, SSD_HEADS),
            full(1, D_SSD), full(1, D_SSD),
        ],
        out_specs=[
            pl.BlockSpec((rows, D_SSD), lambda b, c: (b * nc + c, 0)),
            pl.BlockSpec((1, CONV_WIDTH - 1, CONV_DIM), lambda b, c: (b, 0, 0)),
            pl.BlockSpec((1, D_SSD, D_STATE), lambda b, c: (b, 0, 0)),
        ],
        out_shape=[
            jax.ShapeDtypeStruct((batch * seq, D_SSD), F32),
            jax.ShapeDtypeStruct((batch, CONV_WIDTH - 1, CONV_DIM), F32),
            jax.ShapeDtypeStruct((batch, D_SSD, D_STATE), F32),
        ],
        scratch_shapes=[
            pltpu.VMEM((CHUNK + TAIL, CONV_DIM), F32),
            pltpu.VMEM((CHUNK, LANES), F32),
            pltpu.VMEM((D_SSD, D_STATE), F32),
            pltpu.VMEM((CHUNK, D_SSD), F32),
        ],
        compiler_params=_params(("parallel", "arbitrary")),
        name="ssd",
    )(proj, proj, proj, conv_prev, ssm_prev, conv_w, conv_b, a_log, dt_bias, d_vec, g_ssd)


def _reorder_w_in(w_in):
    offs = [0]
    for s in IN_SPLITS:
        offs.append(offs[-1] + s)
    q, k, v, qi, ki, wi, z, xbc, dtr = [w_in[:, offs[n]:offs[n + 1]] for n in range(len(IN_SPLITS))]
    parts = [t.astype(BF16) for t in (q, qi, z, xbc, k, v, ki, wi, dtr)]
    parts.append(jnp.zeros((w_in.shape[0], N_IN - sum(IN_SPLITS)), BF16))
    return jnp.concatenate(parts, axis=1)


def kernel(x_prompt, x_sample, cache_k, cache_v, cache_kidx, state_conv, state_ssm, page_table, rel_bias, g_ffn1, w1_ffn1, w3_ffn1, w2_ffn1, g_mix, w_in, conv_w, conv_b, a_log, dt_bias, d_skip, g_ssd, w_out, g_ffn2, w1_ffn2, w3_ffn2, w2_ffn2, g_final):
    depth = w_in.shape[0]
    assert depth == 1
    l = 0
    bp, seq, _ = x_prompt.shape
    bs, nq, _ = x_sample.shape
    n_pages = page_table.shape[1]
    row = lambda t: t.reshape(1, -1)

    w1a, w3a, w2a = w1_ffn1[l], w3_ffn1[l], w2_ffn1[l]
    w1b, w3b, w2b = w1_ffn2[l], w3_ffn2[l], w2_ffn2[l]
    w_in_r = _reorder_w_in(w_in[l])
    w_out_a = w_out[l, :D_ATTN].astype(BF16)
    w_out_s = w_out[l, D_ATTN:].astype(BF16)
    d_vec = jnp.repeat(d_skip[l], SSD_HEAD_DIM).reshape(1, D_SSD)
    btab = _bias_table(rel_bias)
    gf = row(g_final)

    def trunk(x, attend, conv_prev, ssm_prev, batch, length):
        tm = min(x.shape[0], 1024)
        tf = 256 if tm > 512 else 512
        x = _ffn(x, row(g_ffn1[l]), w1a, w3a, w2a, gf, tm=tm, tf=tf, final_norm=False)
        proj = _in_proj(x, row(g_mix[l]), w_in_r, tm=tm)
        a_out = attend(proj)
        s_out, conv_new, ssm_new = _ssd(
            proj, conv_prev, ssm_prev.reshape(batch, D_SSD, D_STATE), conv_w[l], row(conv_b[l]), row(a_log[l]),
            row(dt_bias[l]), d_vec, row(g_ssd[l]), batch=batch, seq=length)
        x = _out_proj(x, a_out, s_out, w_out_a, w_out_s, tm=min(tm, 512))
        y = _ffn(x, row(g_ffn2[l]), w1b, w3b, w2b, gf, tm=tm, tf=tf, final_norm=True)
        k = proj[:, COL_K:COL_K + D_KV].reshape(1, batch, length, N_KV_HEADS, HEAD_DIM)
        v = proj[:, COL_V:COL_V + D_KV].reshape(1, batch, length, N_KV_HEADS, HEAD_DIM)
        ki = proj[:, COL_SMALL + SM_KI:COL_SMALL + SM_KI + IDX_DIM].reshape(1, batch, length, IDX_DIM)
        ssm_new = ssm_new.reshape(1, batch, SSD_HEADS, SSD_HEAD_DIM, D_STATE)
        return y.reshape(batch, length, D_MODEL), k, v, ki, conv_new[None], ssm_new

    y_p, k_p, v_p, ki_p, conv_p, ssm_p = trunk(
        x_prompt.reshape(bp * seq, D_MODEL),
        functools.partial(_dsa_prompt, btab=btab, batch=bp, seq=seq),
        jnp.zeros((bp, CONV_WIDTH - 1, CONV_DIM), F32), jnp.zeros((bp, SSD_HEADS, SSD_HEAD_DIM, D_STATE), F32),
        bp, seq)

    pool_k = cache_k[l].reshape(-1, HEAD_DIM)
    pool_v = cache_v[l].reshape(-1, HEAD_DIM)
    pool_kidx = jnp.swapaxes(cache_kidx[l], 1, 2).reshape(-1, PAGE_SIZE)

    def sample_attend(proj):
        sk_past, sk_new = _sample_score(page_table, proj, pool_kidx, batch=bs, nq=nq, n_pages=n_pages)
        thr, lim = _sample_select(sk_past.reshape(bs * nq, -1), sk_new.reshape(bs * nq, LANES),
                                  rows_per_step=128)
        return _sample_attn(page_table, proj, sk_past, sk_new, thr, lim, btab, pool_k, pool_v,
                            batch=bs, nq=nq, n_pages=n_pages)

    y_s, k_s, v_s, ki_s, conv_s, ssm_s = trunk(
        x_sample.reshape(bs * nq, D_MODEL), sample_attend, state_conv[l], state_ssm[l], bs, nq)

    return (y_p, y_s, k_p, v_p, ki_p, conv_p, ssm_p, k_s, v_s, ki_s, conv_s, ssm_s)
```

```python
import functools
import math

import jax
import jax.numpy as jnp
from jax import lax
from jax.experimental import pallas as pl
from jax.experimental.pallas import tpu as pltpu

F32 = jnp.float32
BF16 = jnp.bfloat16
I32 = jnp.int32

D_MODEL = 2048
PAGE_SIZE = 128
N_HEADS = 8
HEAD_DIM = 128
N_KV_HEADS = 2
GQA_GROUP = N_HEADS // N_KV_HEADS
IDX_HEADS = 16
IDX_DIM = 64
TOPK = 256
N_BUCKETS = 32
MAX_DISTANCE = 128
SSD_HEADS = 16
SSD_HEAD_DIM = 64
SSD_GROUPS = 2
D_STATE = 128
CONV_WIDTH = 4
CHUNK = 128
D_ATTN = N_HEADS * HEAD_DIM
D_SSD = SSD_HEADS * SSD_HEAD_DIM
D_KV = N_KV_HEADS * HEAD_DIM
CONV_DIM = D_SSD + 2 * SSD_GROUPS * D_STATE
D_FF = 5632
EPS = 1e-6
IN_SPLITS = (D_ATTN, D_KV, D_KV, IDX_HEADS * IDX_DIM, IDX_DIM, IDX_HEADS, D_SSD, CONV_DIM, SSD_HEADS)

LANES = 128
COL_Q = 0
COL_QI = COL_Q + D_ATTN
COL_Z = COL_QI + IDX_HEADS * IDX_DIM
COL_XBC = COL_Z + D_SSD
COL_K = COL_XBC + CONV_DIM
COL_V = COL_K + D_KV
COL_SMALL = COL_V + D_KV
SM_KI = 0
SM_WI = SM_KI + IDX_DIM
SM_DT = SM_WI + IDX_HEADS
IN_TILE = 768
N_IN = ((COL_SMALL + LANES + IN_TILE - 1) // IN_TILE) * IN_TILE

INT_MIN = -(2 ** 31)
NEG_BIG = -1e30
VMEM_LIMIT = 60000 * 1024


def _params(sem):
    return pltpu.CompilerParams(dimension_semantics=sem, vmem_limit_bytes=VMEM_LIMIT)


def _rms(x, g):
    return x * lax.rsqrt(jnp.mean(x * x, axis=-1, keepdims=True) + EPS) * g


def _dot(a, b):
    return jnp.dot(a, b, preferred_element_type=F32)


def _dot_nt(a, b):
    return lax.dot_general(a, b, (((1,), (1,)), ((), ())), preferred_element_type=F32)


def _split3(x):
    hi = x.astype(BF16)
    rest = x - hi.astype(F32)
    mid = rest.astype(BF16)
    return hi, mid, (rest - mid.astype(F32)).astype(BF16)


def _select_dot(a, b):
    if a.dtype == BF16:
        return sum(_dot(a, part) for part in _split3(b))
    return sum(_dot(part, b) for part in _split3(a))


def _sort_key(x):
    bits = lax.bitcast_convert_type(x + 0.0, I32)
    return bits ^ ((bits >> 31) & 0x7FFFFFFF)


def _ffn_kernel(x_ref, xs_ref, g_ref, gf_ref, w1_hbm, w3_hbm, w2_hbm, o_ref, os_ref, h_scr, w1_buf, w3_buf,
                w2_buf, sem, *, tf, final_norm):
    i = pl.program_id(0)
    nf = D_FF // tf
    total = pl.num_programs(0) * nf
    tm = x_ref.shape[0]

    def copies(f, slot):
        c0 = pl.multiple_of(f * tf, tf)
        return (pltpu.make_async_copy(w1_hbm.at[:, pl.ds(c0, tf)], w1_buf.at[slot], sem.at[0, slot]),
                pltpu.make_async_copy(w3_hbm.at[:, pl.ds(c0, tf)], w3_buf.at[slot], sem.at[1, slot]),
                pltpu.make_async_copy(w2_hbm.at[pl.ds(c0, tf), :], w2_buf.at[slot], sem.at[2, slot]))

    @pl.when(i == 0)
    def _():
        for c in copies(0, 0):
            c.start()

    h_scr[0:tm, :] = _rms(x_ref[...], g_ref[...]).astype(BF16)
    h_scr[tm:, :] = _rms(xs_ref[...], g_ref[...]).astype(BF16)
    o_ref[...] = jnp.zeros_like(o_ref)
    os_ref[...] = jnp.zeros_like(os_ref)

    def body(f, carry):
        t = i * nf + f
        slot = lax.rem(t, 2)

        @pl.when(t + 1 < total)
        def _():
            for c in copies(jnp.where(f + 1 == nf, 0, f + 1), 1 - slot):
                c.start()

        for c in copies(f, slot):
            c.wait()
        h = h_scr[...]
        a = _dot(h, w1_buf[slot].astype(BF16))
        b = _dot(h, w3_buf[slot].astype(BF16))
        u = (a * jax.nn.sigmoid(a) * b).astype(BF16)
        y = _dot(u, w2_buf[slot].astype(BF16))
        o_ref[...] += y[0:tm, :]
        os_ref[...] += y[tm:, :]
        return carry

    lax.fori_loop(0, nf, body, 0)
    for src, dst in ((x_ref, o_ref), (xs_ref, os_ref)):
        y = src[...] + 0.5 * dst[...]
        if final_norm:
            y = _rms(y, gf_ref[...])
        dst[...] = y


def _ffn(x, xs, g, w1, w3, w2, gf, *, tm, tf, final_norm):
    t = x.shape[0]
    ts = xs.shape[0] // (t // tm)
    assert ts * (t // tm) == xs.shape[0] and ts % 16 == 0
    return pl.pallas_call(
        functools.partial(_ffn_kernel, tf=tf, final_norm=final_norm),
        grid=(t // tm,),
        in_specs=[
            pl.BlockSpec((tm, D_MODEL), lambda i: (i, 0)),
            pl.BlockSpec((ts, D_MODEL), lambda i: (i, 0)),
            pl.BlockSpec((1, D_MODEL), lambda i: (0, 0)),
            pl.BlockSpec((1, D_MODEL), lambda i: (0, 0)),
            pl.BlockSpec(memory_space=pl.ANY),
            pl.BlockSpec(memory_space=pl.ANY),
            pl.BlockSpec(memory_space=pl.ANY),
        ],
        out_specs=[pl.BlockSpec((tm, D_MODEL), lambda i: (i, 0)), pl.BlockSpec((ts, D_MODEL), lambda i: (i, 0))],
        out_shape=[jax.ShapeDtypeStruct((t, D_MODEL), F32), jax.ShapeDtypeStruct(xs.shape, F32)],
        scratch_shapes=[
            pltpu.VMEM((tm + ts, D_MODEL), BF16),
            pltpu.VMEM((2, D_MODEL, tf), F32),
            pltpu.VMEM((2, D_MODEL, tf), F32),
            pltpu.VMEM((2, tf, D_MODEL), F32),
            pltpu.SemaphoreType.DMA((3, 2)),
        ],
        compiler_params=_params(("arbitrary",)),
        name="ffn",
    )(x, xs, g, gf, w1, w3, w2)


def _in_proj_kernel(x_ref, g_ref, w_ref, o_ref, h_scr):
    @pl.when(pl.program_id(1) == 0)
    def _():
        h_scr[...] = _rms(x_ref[...], g_ref[...]).astype(BF16)

    o_ref[...] = _dot(h_scr[...], w_ref[...])


def _in_proj(x, g, w, *, tm):
    t = x.shape[0]
    return pl.pallas_call(
        _in_proj_kernel,
        grid=(t // tm, N_IN // IN_TILE),
        in_specs=[
            pl.BlockSpec((tm, D_MODEL), lambda i, j: (i, 0)),
            pl.BlockSpec((1, D_MODEL), lambda i, j: (0, 0)),
            pl.BlockSpec((D_MODEL, IN_TILE), lambda i, j: (0, j)),
        ],
        out_specs=pl.BlockSpec((tm, IN_TILE), lambda i, j: (i, j)),
        out_shape=jax.ShapeDtypeStruct((t, N_IN), F32),
        scratch_shapes=[pltpu.VMEM((tm, D_MODEL), BF16)],
        compiler_params=_params(("parallel", "arbitrary")),
        name="in_proj",
    )(x, g, w)


def _out_proj_kernel(x_ref, a_ref, s_ref, wa_ref, ws_ref, o_ref):
    acc = _dot(a_ref[...].astype(BF16), wa_ref[...])
    acc += _dot(s_ref[...].astype(BF16), ws_ref[...])
    o_ref[...] = x_ref[...] + acc


def _out_proj(x, a, s, wa, ws, *, tm):
    t = x.shape[0]
    return pl.pallas_call(
        _out_proj_kernel,
        grid=(t // tm,),
        in_specs=[
            pl.BlockSpec((tm, D_MODEL), lambda i: (i, 0)),
            pl.BlockSpec((tm, D_ATTN), lambda i: (i, 0)),
            pl.BlockSpec((tm, D_SSD), lambda i: (i, 0)),
            pl.BlockSpec((D_ATTN, D_MODEL), lambda i: (0, 0)),
            pl.BlockSpec((D_SSD, D_MODEL), lambda i: (0, 0)),
        ],
        out_specs=pl.BlockSpec((tm, D_MODEL), lambda i: (i, 0)),
        out_shape=jax.ShapeDtypeStruct((t, D_MODEL), F32),
        compiler_params=_params(("parallel",)),
        name="out_proj",
    )(x, a, s, wa, ws)


NEAR = 2 * LANES


def _bias_kernel(rb_ref, o_ref):
    r = lax.broadcasted_iota(I32, (LANES, NEAR), 0)
    c = lax.broadcasted_iota(I32, (LANES, NEAR), 1)
    n = jnp.maximum(r + LANES - c, 0)
    max_exact = N_BUCKETS // 2
    nf = jnp.maximum(n, 1).astype(F32)
    large = max_exact + (jnp.log(nf / max_exact) / math.log(MAX_DISTANCE / max_exact)
                         * (N_BUCKETS - max_exact)).astype(I32)
    bucket = jnp.where(n < max_exact, n, jnp.minimum(large, N_BUCKETS - 1))
    for h in range(N_HEADS):
        acc = jnp.zeros((LANES, NEAR), F32)
        for b in range(N_BUCKETS):
            acc = jnp.where(bucket == b, rb_ref[b, h], acc)
        o_ref[h] = acc


def _bias_table(rel_bias):
    return pl.pallas_call(
        _bias_kernel,
        in_specs=[pl.BlockSpec(memory_space=pltpu.SMEM)],
        out_specs=pl.BlockSpec(memory_space=pltpu.VMEM),
        out_shape=jax.ShapeDtypeStruct((N_HEADS, LANES, NEAR), F32),
        name="bias_table",
    )(rel_bias)


def _count(mask, axis=1):
    x = mask.astype(F32)
    if axis == 0:
        x = jnp.sum(x.reshape(x.shape[0] // 64, 64, x.shape[1]), axis=0)
    return jnp.sum(x, axis=axis, keepdims=True)


def _kth_largest(count_ge, shape):
    t = jnp.full(shape, INT_MIN, I32)
    t = jnp.where(count_ge(jnp.zeros(shape, I32)) >= TOPK, 0, t)

    def body(it, t):
        cand = t + lax.shift_left(jnp.int32(1), 30 - it)
        return jnp.where(count_ge(cand) >= TOPK, cand, t)

    return lax.fori_loop(0, 31, body, t)


def _tie_limit(count_eq_below, need, shape, idx_bits):
    def body(it, m):
        cand = m + lax.shift_left(jnp.int32(1), idx_bits - 1 - it)
        return jnp.where(count_eq_below(cand) < need, cand, m)

    return lax.fori_loop(0, idx_bits, body, jnp.zeros(shape, I32))


CAUSAL_VARIANTS = 4


def _dsa_prompt_block(width, near_tiles, i, q_ref, qi_ref, k_ref, v_ref, sm_ref, bt_ref, o_ref, sk_scr, neg_scr):
    tq = LANES
    q0 = pl.multiple_of(i * LANES, LANES)
    shape = (1, tq)

    wi_t = sm_ref[pl.ds(q0, tq), :].T[SM_WI:SM_WI + IDX_HEADS, :] * (IDX_HEADS * IDX_DIM) ** -0.5
    head_group = 4
    qi_groups = [
        jnp.concatenate([qi_ref[:, h * IDX_DIM:(h + 1) * IDX_DIM] for h in range(hg, hg + head_group)],
                        axis=0).astype(BF16)
        for hg in range(0, IDX_HEADS, head_group)]
    chunk = 2 * LANES
    pos = lax.broadcasted_iota(I32, (chunk, tq), 1) + q0
    for c in range(0, width, chunk):
        ki = sm_ref[c:c + chunk, SM_KI:SM_KI + IDX_DIM].astype(BF16)
        score = jnp.zeros((chunk, tq), F32)
        for n, hg in enumerate(range(0, IDX_HEADS, head_group)):
            rel = _dot_nt(ki, qi_groups[n])
            for e in range(head_group):
                score = score + wi_t[hg + e:hg + e + 1, :] * jnp.maximum(rel[:, e * tq:(e + 1) * tq], 0.0)
        key_c = lax.broadcasted_iota(I32, (chunk, tq), 0) + c
        sk_scr[c:c + chunk, :] = jnp.where(key_c <= pos, _sort_key(score), INT_MIN)

    key = lax.broadcasted_iota(I32, (width, tq), 0)

    def sk():
        return sk_scr[0:width, :]

    thr = _kth_largest(lambda c: _count(sk() >= c, 0), shape)
    excess = (_count(sk() >= thr, 0) > TOPK) & (thr > INT_MIN)
    any_excess = jnp.max(excess.astype(F32)) > 0.0

    @pl.when(jnp.logical_not(any_excess))
    def _():
        neg_scr[0:width, :] = jnp.where(sk() >= jnp.maximum(thr, INT_MIN + 1), 0.0, -jnp.inf)

    @pl.when(any_excess)
    def _():
        need = TOPK - _count(sk() > thr, 0)
        lim = _tie_limit(lambda m: _count((sk() == thr) & (key < m), 0), need, shape, (width - 1).bit_length())
        take = (sk() > thr) | ((sk() == thr) & (key <= lim) & (sk() > INT_MIN))
        neg_scr[0:width, :] = jnp.where(take, 0.0, -jnp.inf)

    log2e = math.log2(math.e)
    scale = HEAD_DIM ** -0.5 * log2e
    tiles = width // LANES
    for g in range(N_KV_HEADS):
        heads = range(g * GQA_GROUP, (g + 1) * GQA_GROUP)
        kg = k_ref[0:width, g * HEAD_DIM:(g + 1) * HEAD_DIM].astype(BF16)
        vg_t = v_ref[0:width, g * HEAD_DIM:(g + 1) * HEAD_DIM].T.astype(BF16)
        qg = jnp.concatenate([q_ref[:, h * HEAD_DIM:(h + 1) * HEAD_DIM] for h in heads], axis=0).astype(BF16)
        diag_t = [((bt_ref[h, :, LANES:NEAR] - bt_ref[h, 0:1, 0:1]) * log2e).T for h in heads]
        prev_t = [((bt_ref[h, :, 0:LANES] - bt_ref[h, 0:1, 0:1]) * log2e).T for h in heads]
        qk = _dot_nt(kg, qg)
        logits = []
        for t in range(tiles):
            rows = slice(t * LANES, (t + 1) * LANES)
            per_head = []
            for e in range(GQA_GROUP):
                lt = qk[rows, e * tq:(e + 1) * tq] * scale + neg_scr[rows, :]
                if t >= tiles - near_tiles:
                    lt = lt + jnp.where(i == t, diag_t[e], jnp.where(i - 1 == t, prev_t[e], 0.0))
                per_head.append(lt)
            logits.append(per_head)
        p_t = []
        inv_l = []
        for e in range(GQA_GROUP):
            m = jnp.max(functools.reduce(jnp.maximum, [lt[e] for lt in logits]), axis=0, keepdims=True)
            probs = [jnp.exp2(lt[e] - m) for lt in logits]
            inv_l.append(1.0 / jnp.sum(functools.reduce(jnp.add, probs), axis=0, keepdims=True))
            p_t.append(jnp.concatenate([p.astype(BF16) for p in probs], axis=0))
        out_t = _dot(vg_t, jnp.concatenate(p_t, axis=1))
        for e, h in enumerate(heads):
            o_ref[:, h * HEAD_DIM:(h + 1) * HEAD_DIM] = (out_t[:, e * tq:(e + 1) * tq] * inv_l[e]).T


def _dsa_prompt_kernel(*refs, seq):
    i = pl.program_id(1)
    per = seq // LANES // CAUSAL_VARIANTS
    for v in range(CAUSAL_VARIANTS):
        @pl.when(i // per == v)
        def _(v=v):
            _dsa_prompt_block((v + 1) * per * LANES, per + 1, i, *refs)


def _dsa_prompt(proj, btab, *, batch, seq):
    nq = seq // LANES
    return pl.pallas_call(
        functools.partial(_dsa_prompt_kernel, seq=seq),
        grid=(batch, nq),
        in_specs=[
            pl.BlockSpec((LANES, D_ATTN), lambda b, i: (b * nq + i, COL_Q // D_ATTN)),
            pl.BlockSpec((LANES, IDX_HEADS * IDX_DIM), lambda b, i: (b * nq + i, COL_QI // (IDX_HEADS * IDX_DIM))),
            pl.BlockSpec((seq, D_KV), lambda b, i: (b, COL_K // D_KV)),
            pl.BlockSpec((seq, D_KV), lambda b, i: (b, COL_V // D_KV)),
            pl.BlockSpec((seq, LANES), lambda b, i: (b, COL_SMALL // LANES)),
            pl.BlockSpec((N_HEADS, LANES, NEAR), lambda b, i: (0, 0, 0)),
        ],
        out_specs=pl.BlockSpec((LANES, D_ATTN), lambda b, i: (b * nq + i, 0)),
        out_shape=jax.ShapeDtypeStruct((batch * seq, D_ATTN), F32),
        scratch_shapes=[pltpu.VMEM((seq, LANES), I32), pltpu.VMEM((seq, LANES), F32)],
        compiler_params=_params(("parallel", "parallel")),
        name="dsa_prompt",
    )(proj, proj, proj, proj, proj, btab)


SCORE_PAGES = 64
ATTN_PAGES = 32


def _sample_queries(qi_ref):
    return jnp.concatenate(
        [qi_ref[:, h * IDX_DIM:(h + 1) * IDX_DIM] for h in range(IDX_HEADS)], axis=0).astype(BF16)


def _sample_scores(rel, wi, nq):
    sc = jnp.zeros((nq, rel.shape[1]), F32)
    for h in range(IDX_HEADS):
        sc = sc + wi[:, h:h + 1] * jnp.maximum(rel[h * nq:(h + 1) * nq, :], 0.0)
    return sc


def _page_copies(pt_ref, pools, bufs, sems, b, s, slot, pages, page_rows):
    copies = []
    for r in range(pages):
        row0 = pl.multiple_of(pt_ref[b, s * pages + r] * page_rows, page_rows)
        for pool, buf, sem in zip(pools, bufs, sems):
            copies.append(pltpu.make_async_copy(pool.at[pl.ds(row0, page_rows), :], buf.at[slot, r], sem.at[slot]))
    return copies


def _page_pipeline(pt_ref, pools, bufs, sems, pages, page_rows):
    b, s = pl.program_id(0), pl.program_id(1)
    steps = pl.num_programs(1)
    t = b * steps + s
    slot = lax.rem(t, 2)
    copies = functools.partial(_page_copies, pt_ref, pools, bufs, sems, pages=pages, page_rows=page_rows)

    @pl.when(t == 0)
    def _():
        for c in copies(b, s, slot):
            c.start()

    @pl.when(t + 1 < pl.num_programs(0) * steps)
    def _():
        wrap = s + 1 == steps
        for c in copies(jnp.where(wrap, b + 1, b), jnp.where(wrap, 0, s + 1), 1 - slot):
            c.start()

    for c in copies(b, s, slot):
        c.wait()
    return slot


def _sample_score_kernel(pt_ref, qi_ref, sm_ref, pool_ref, past_ref, new_ref, pages, sem, kbuf, *, nq):
    s = pl.program_id(1)
    slot = _page_pipeline(pt_ref, [pool_ref], [pages], [sem], SCORE_PAGES, IDX_DIM)
    qx = _sample_queries(qi_ref)
    wi = sm_ref[:, SM_WI:SM_WI + IDX_HEADS] * (IDX_HEADS * IDX_DIM) ** -0.5
    for r in range(SCORE_PAGES):
        kbuf[:, r * PAGE_SIZE:(r + 1) * PAGE_SIZE] = pages[slot, r].astype(BF16)
    past_ref[...] = _sort_key(_sample_scores(_dot(qx, kbuf[...]), wi, nq))

    @pl.when(s == pl.num_programs(1) - 1)
    def _():
        ki_new = jnp.concatenate(
            [sm_ref[:, SM_KI:SM_KI + IDX_DIM], jnp.zeros((LANES - nq, IDX_DIM), F32)], axis=0).astype(BF16)
        sc = _sample_scores(_dot_nt(qx, ki_new), wi, nq)
        j = lax.broadcasted_iota(I32, (nq, LANES), 1)
        t = lax.broadcasted_iota(I32, (nq, LANES), 0)
        new_ref[...] = jnp.where(j <= t, _sort_key(sc), INT_MIN)


def _sample_score(page_table, proj_s, pool_kidx_t, *, batch, nq, n_pages):
    steps = n_pages // SCORE_PAGES
    grid_spec = pltpu.PrefetchScalarGridSpec(
        num_scalar_prefetch=1,
        grid=(batch, steps),
        in_specs=[
            pl.BlockSpec((nq, IDX_HEADS * IDX_DIM), lambda b, s, pt: (b, COL_QI // (IDX_HEADS * IDX_DIM))),
            pl.BlockSpec((nq, LANES), lambda b, s, pt: (b, COL_SMALL // LANES)),
            pl.BlockSpec(memory_space=pl.ANY),
        ],
        out_specs=[
            pl.BlockSpec((None, nq, SCORE_PAGES * PAGE_SIZE), lambda b, s, pt: (b, 0, s)),
            pl.BlockSpec((None, nq, LANES), lambda b, s, pt: (b, 0, 0)),
        ],
        scratch_shapes=[
            pltpu.VMEM((2, SCORE_PAGES, IDX_DIM, PAGE_SIZE), F32),
            pltpu.SemaphoreType.DMA((2,)),
            pltpu.VMEM((IDX_DIM, SCORE_PAGES * PAGE_SIZE), BF16),
        ],
    )
    return pl.pallas_call(
        functools.partial(_sample_score_kernel, nq=nq),
        grid_spec=grid_spec,
        out_shape=[
            jax.ShapeDtypeStruct((batch, nq, n_pages * PAGE_SIZE), I32),
            jax.ShapeDtypeStruct((batch, nq, LANES), I32),
        ],
        compiler_params=_params(("arbitrary", "arbitrary")),
        name="sample_score",
    )(page_table, proj_s, proj_s, pool_kidx_t)


def _sample_select_kernel(past_ref, new_ref, thr_ref, lim_ref, *, past):
    rows = past_ref.shape[0]
    colp = lax.broadcasted_iota(I32, (rows, past), 1)
    coln = lax.broadcasted_iota(I32, (rows, LANES), 1) + past

    def count_ge(c):
        return _count(past_ref[...] >= c) + _count(new_ref[...] >= c)

    thr = _kth_largest(count_ge, (rows, 1))
    thr_ref[...] = jnp.broadcast_to(thr, (rows, LANES))
    lim_ref[...] = jnp.full((rows, LANES), 2 ** 31 - 1, I32)
    any_excess = jnp.max((count_ge(thr) > TOPK).astype(F32)) > 0.0

    @pl.when(any_excess)
    def _():
        need = TOPK - (_count(past_ref[...] > thr) + _count(new_ref[...] > thr))

        def count_eq_below(m):
            return (_count((past_ref[...] == thr) & (colp < m))
                    + _count((new_ref[...] == thr) & (coln < m)))

        lim = _tie_limit(count_eq_below, need, (rows, 1), (past + LANES - 1).bit_length())
        lim_ref[...] = jnp.broadcast_to(lim, (rows, LANES))


def _sample_select(sk_past, sk_new, *, rows_per_step):
    rows, past = sk_past.shape
    return pl.pallas_call(
        functools.partial(_sample_select_kernel, past=past),
        grid=(rows // rows_per_step,),
        in_specs=[
            pl.BlockSpec((rows_per_step, past), lambda i: (i, 0)),
            pl.BlockSpec((rows_per_step, LANES), lambda i: (i, 0)),
        ],
        out_specs=[
            pl.BlockSpec((rows_per_step, LANES), lambda i: (i, 0)),
            pl.BlockSpec((rows_per_step, LANES), lambda i: (i, 0)),
        ],
        out_shape=[jax.ShapeDtypeStruct((rows, LANES), I32)] * 2,
        compiler_params=_params(("parallel",)),
        name="sample_select",
    )(sk_past, sk_new)


def _sample_attn_kernel(pt_ref, q_ref, kn_ref, vn_ref, skp_ref, skn_ref, thr_ref, lim_ref, bt_ref, poolk_ref,
                        poolv_ref, o_ref, kpages, vpages, ksem, vsem, kbuf, vbuf, m_scr, l_scr, acc_scr, *,
                        nq, past):
    slot = _page_pipeline(pt_ref, [poolk_ref, poolv_ref], [kpages, vpages], [ksem, vsem], ATTN_PAGES,
                          PAGE_SIZE * N_KV_HEADS)
    s = pl.program_id(1)
    last = pl.num_programs(1) - 1
    width = ATTN_PAGES * PAGE_SIZE
    grows = GQA_GROUP * nq
    scale = HEAD_DIM ** -0.5

    @pl.when(s == 0)
    def _():
        m_scr[...] = jnp.full_like(m_scr, NEG_BIG)
        l_scr[...] = jnp.zeros_like(l_scr)
        acc_scr[...] = jnp.zeros_like(acc_scr)

    thr = thr_ref[:, 0:1]
    lim = lim_ref[:, 0:1]

    def select(sk, idx):
        sel = (sk > thr) | ((sk == thr) & (idx <= lim))
        return jnp.concatenate([sel] * GQA_GROUP, axis=0)

    def far_bias(g):
        return jnp.concatenate(
            [jnp.broadcast_to(bt_ref[g * GQA_GROUP + hh, 0:1, 0:1], (nq, 1)) for hh in range(GQA_GROUP)], axis=0)

    def near_bias(g, lo, hi):
        return jnp.concatenate([bt_ref[g * GQA_GROUP + hh, 0:nq, lo:hi] for hh in range(GQA_GROUP)], axis=0)

    def group_queries(g):
        return jnp.concatenate(
            [q_ref[:, (g * GQA_GROUP + hh) * HEAD_DIM:(g * GQA_GROUP + hh + 1) * HEAD_DIM]
             for hh in range(GQA_GROUP)], axis=0).astype(BF16)

    def accumulate(g, lg, sel, vals):
        rs = slice(g * grows, (g + 1) * grows)
        m_old = m_scr[rs, :]
        m_new = jnp.maximum(m_old, jnp.max(jnp.where(sel, lg, NEG_BIG), axis=1, keepdims=True))
        p = jnp.where(sel, jnp.exp(lg - m_new), 0.0)
        alpha = jnp.exp(m_old - m_new)
        l_scr[rs, :] = alpha * l_scr[rs, :] + jnp.sum(p, axis=1, keepdims=True)
        acc_scr[rs, :] = alpha * acc_scr[rs, :] + _dot(p.astype(BF16), vals)
        m_scr[rs, :] = m_new

    for r in range(ATTN_PAGES):
        for g in range(N_KV_HEADS):
            rows = pl.ds(g, PAGE_SIZE, stride=N_KV_HEADS)
            kbuf[g, r * PAGE_SIZE:(r + 1) * PAGE_SIZE, :] = kpages[slot, r, rows, :].astype(BF16)
            vbuf[g, r * PAGE_SIZE:(r + 1) * PAGE_SIZE, :] = vpages[slot, r, rows, :].astype(BF16)

    col = lax.broadcasted_iota(I32, (nq, width), 1)
    sel = select(skp_ref[...], col + s * width)
    colg = lax.broadcasted_iota(I32, (grows, width), 1)
    in_near = (s == last) & (colg >= width - PAGE_SIZE)
    for g in range(N_KV_HEADS):
        near = jnp.tile(near_bias(g, 0, LANES), (1, ATTN_PAGES))
        bias = jnp.where(in_near, near, far_bias(g))
        lg = _dot_nt(group_queries(g), kbuf[g]) * scale + bias
        accumulate(g, lg, sel, vbuf[g])

    @pl.when(s == last)
    def _():
        pad = jnp.zeros((LANES - nq, D_KV), F32)
        kn = jnp.concatenate([kn_ref[...], pad], axis=0).astype(BF16)
        vn = jnp.concatenate([vn_ref[...], pad], axis=0).astype(BF16)
        coln = lax.broadcasted_iota(I32, (nq, LANES), 1) + past
        seln = select(skn_ref[...], coln)
        for g in range(N_KV_HEADS):
            cs = slice(g * HEAD_DIM, (g + 1) * HEAD_DIM)
            lg = _dot_nt(group_queries(g), kn[:, cs]) * scale + near_bias(g, LANES, NEAR)
            accumulate(g, lg, seln, vn[:, cs])
        out = acc_scr[...] / l_scr[...]
        for h in range(N_HEADS):
            o_ref[:, h * HEAD_DIM:(h + 1) * HEAD_DIM] = out[h * nq:(h + 1) * nq, :]


def _sample_attn(page_table, proj_s, sk_past, sk_new, thr, lim, btab, pool_k, pool_v, *, batch, nq, n_pages):
    steps = n_pages // ATTN_PAGES
    width = ATTN_PAGES * PAGE_SIZE
    past = n_pages * PAGE_SIZE
    page_buf = pltpu.VMEM((2, ATTN_PAGES, PAGE_SIZE * N_KV_HEADS, HEAD_DIM), F32)
    grid_spec = pltpu.PrefetchScalarGridSpec(
        num_scalar_prefetch=1,
        grid=(batch, steps),
        in_specs=[
            pl.BlockSpec((nq, D_ATTN), lambda b, s, pt: (b, COL_Q // D_ATTN)),
            pl.BlockSpec((nq, D_KV), lambda b, s, pt: (b, COL_K // D_KV)),
            pl.BlockSpec((nq, D_KV), lambda b, s, pt: (b, COL_V // D_KV)),
            pl.BlockSpec((None, nq, width), lambda b, s, pt: (b, 0, s)),
            pl.BlockSpec((None, nq, LANES), lambda b, s, pt: (b, 0, 0)),
            pl.BlockSpec((nq, LANES), lambda b, s, pt: (b, 0)),
            pl.BlockSpec((nq, LANES), lambda b, s, pt: (b, 0)),
            pl.BlockSpec((N_HEADS, LANES, NEAR), lambda b, s, pt: (0, 0, 0)),
            pl.BlockSpec(memory_space=pl.ANY),
            pl.BlockSpec(memory_space=pl.ANY),
        ],
        out_specs=pl.BlockSpec((nq, D_ATTN), lambda b, s, pt: (b, 0)),
        scratch_shapes=[
            page_buf, page_buf, pltpu.SemaphoreType.DMA((2,)), pltpu.SemaphoreType.DMA((2,)),
            pltpu.VMEM((N_KV_HEADS, width, HEAD_DIM), BF16),
            pltpu.VMEM((N_KV_HEADS, width, HEAD_DIM), BF16),
            pltpu.VMEM((N_HEADS * nq, 1), F32),
            pltpu.VMEM((N_HEADS * nq, 1), F32),
            pltpu.VMEM((N_HEADS * nq, HEAD_DIM), F32),
        ],
    )
    return pl.pallas_call(
        functools.partial(_sample_attn_kernel, nq=nq, past=past),
        grid_spec=grid_spec,
        out_shape=jax.ShapeDtypeStruct((batch * nq, D_ATTN), F32),
        compiler_params=_params(("arbitrary", "arbitrary")),
        name="sample_attn",
    )(page_table, proj_s, proj_s, proj_s, sk_past, sk_new, thr, lim, btab, pool_k, pool_v)


TAIL = 8


def _ssd_kernel(xbc_ref, z_ref, sm_ref, cprev_ref, hprev_ref, cw_ref, cb_ref, alog_ref, dtb_ref, dvec_ref,
                g_ref, y_ref, cnew_ref, hnew_ref, xpad_scr, sm_scr, h_scr, yd_scr, *, rows):
    c = pl.program_id(1)
    q = CHUNK
    keep = CONV_WIDTH - 1
    heads_per_group = SSD_HEADS // SSD_GROUPS

    @pl.when(c == 0)
    def _():
        xpad_scr[...] = jnp.zeros_like(xpad_scr)
        sm_scr[...] = jnp.zeros_like(sm_scr)
        xpad_scr[TAIL - keep:TAIL, :] = cprev_ref[0]
        h_scr[...] = hprev_ref[0]

    xpad_scr[TAIL:TAIL + rows, :] = xbc_ref[...]
    sm_scr[0:rows, :] = sm_ref[...]
    conv = sum(xpad_scr[TAIL - keep + k:TAIL - keep + k + q, :] * cw_ref[k:k + 1, :]
               for k in range(CONV_WIDTH)) + cb_ref[...]
    new_tail = xpad_scr[rows + TAIL - keep:rows + TAIL, :]
    cnew_ref[0] = new_tail
    xpad_scr[TAIL - keep:TAIL, :] = new_tail

    xc = conv * jax.nn.sigmoid(conv)
    xs = xc[:, :D_SSD]
    bm = xc[:, D_SSD:D_SSD + SSD_GROUPS * D_STATE].astype(BF16)
    cm = xc[:, D_SSD + SSD_GROUPS * D_STATE:].astype(BF16)

    ri = lax.broadcasted_iota(I32, (q, q), 0)
    ci = lax.broadcasted_iota(I32, (q, q), 1)
    causal = ri >= ci
    x = sm_scr[:, SM_DT:SM_DT + SSD_HEADS] + dtb_ref[...]
    dt = jnp.maximum(x, 0.0) + jnp.log1p(jnp.exp(-jnp.abs(x)))
    dt = jnp.where(lax.broadcasted_iota(I32, (q, SSD_HEADS), 0) < rows, dt, 0.0)
    a = -jnp.exp(alog_ref[...])
    acum = _select_dot(causal.astype(BF16), dt * a)
    acum_t = jnp.concatenate([acum, jnp.zeros((q, LANES - SSD_HEADS), F32)], axis=1).T

    expand = (lax.broadcasted_iota(I32, (SSD_HEADS, D_SSD), 1) // SSD_HEAD_DIM
              == lax.broadcasted_iota(I32, (SSD_HEADS, D_SSD), 0)).astype(BF16)
    expand_t = (lax.broadcasted_iota(I32, (D_SSD, SSD_HEADS), 0) // SSD_HEAD_DIM
                == lax.broadcasted_iota(I32, (D_SSD, SSD_HEADS), 1)).astype(BF16)
    acum_x = _select_dot(acum, expand)
    xd = xs * _select_dot(dt, expand)
    xdw_t = (xd * jnp.exp(acum_x[q - 1:q, :] - acum_x)).T.astype(BF16)
    xd = xd.astype(BF16)
    state_decay = jnp.exp(_select_dot(expand_t, jnp.broadcast_to(acum_t[0:SSD_HEADS, q - 1:q], (SSD_HEADS, LANES))))

    for g in range(SSD_GROUPS):
        ns = slice(g * D_STATE, (g + 1) * D_STATE)
        hs = slice(g * heads_per_group * SSD_HEAD_DIM, (g + 1) * heads_per_group * SSD_HEAD_DIM)
        cb = _dot_nt(cm[:, ns], bm[:, ns])
        for e in range(heads_per_group):
            h = g * heads_per_group + e
            ps = slice(h * SSD_HEAD_DIM, (h + 1) * SSD_HEAD_DIM)
            seg = acum[:, h:h + 1] - acum_t[h:h + 1, :]
            lmat = jnp.exp(jnp.where(causal, seg, -jnp.inf))
            yd_scr[:, ps] = _dot((cb * lmat).astype(BF16), xd[:, ps])
        h_old = h_scr[hs, :]
        yd_scr[:, hs] += _dot_nt(cm[:, ns], h_old.astype(BF16)) * jnp.exp(acum_x[:, hs])
        h_scr[hs, :] = h_old * state_decay[hs, :] + _dot(xdw_t[hs, :], bm[:, ns])

    hnew_ref[0] = h_scr[...]
    y = yd_scr[0:rows, :] + dvec_ref[...] * xs[0:rows, :]
    zz = z_ref[...]
    y = y * (zz * jax.nn.sigmoid(zz))
    width = D_SSD // SSD_GROUPS
    for g in range(SSD_GROUPS):
        cs = slice(g * width, (g + 1) * width)
        y_ref[:, cs] = _rms(y[:, cs], g_ref[:, cs])


def _ssd(proj, conv_prev, ssm_prev, conv_w, conv_b, a_log, dt_bias, d_vec, g_ssd, *, batch, seq):
    rows = min(CHUNK, seq)
    nc = seq // rows
    full = lambda *shape: pl.BlockSpec(shape, lambda b, c: (0,) * len(shape))
    return pl.pallas_call(
        functools.partial(_ssd_kernel, rows=rows),
        grid=(batch, nc),
        in_specs=[
            pl.BlockSpec((rows, CONV_DIM), lambda b, c: (b * nc + c, COL_XBC // CONV_DIM)),
            pl.BlockSpec((rows, D_SSD), lambda b, c: (b * nc + c, COL_Z // D_SSD)),
            pl.BlockSpec((rows, LANES), lambda b, c: (b * nc + c, COL_SMALL // LANES)),
            pl.BlockSpec((1, CONV_WIDTH - 1, CONV_DIM), lambda b, c: (b, 0, 0)),
            pl.BlockSpec((1, D_SSD, D_STATE), lambda b, c: (b, 0, 0)),
            full(CONV_WIDTH, CONV_DIM), full(1, CONV_DIM), full(1, SSD_HEADS), full(1, SSD_HEADS),
            full(1, D_SSD), full(1, D_SSD),
        ],
        out_specs=[
            pl.BlockSpec((rows, D_SSD), lambda b, c: (b * nc + c, 0)),
            pl.BlockSpec((1, CONV_WIDTH - 1, CONV_DIM), lambda b, c: (b, 0, 0)),
            pl.BlockSpec((1, D_SSD, D_STATE), lambda b, c: (b, 0, 0)),
        ],
        out_shape=[
            jax.ShapeDtypeStruct((batch * seq, D_SSD), F32),
            jax.ShapeDtypeStruct((batch, CONV_WIDTH - 1, CONV_DIM), F32),
            jax.ShapeDtypeStruct((batch, D_SSD, D_STATE), F32),
        ],
        scratch_shapes=[
            pltpu.VMEM((CHUNK + TAIL, CONV_DIM), F32),
            pltpu.VMEM((CHUNK, LANES), F32),
            pltpu.VMEM((D_SSD, D_STATE), F32),
            pltpu.VMEM((CHUNK, D_SSD), F32),
        ],
        compiler_params=_params(("parallel", "arbitrary")),
        name="ssd",
    )(proj, proj, proj, conv_prev, ssm_prev, conv_w, conv_b, a_log, dt_bias, d_vec, g_ssd)


def _reorder_w_in(w_in):
    offs = [0]
    for s in IN_SPLITS:
        offs.append(offs[-1] + s)
    q, k, v, qi, ki, wi, z, xbc, dtr = [w_in[:, offs[n]:offs[n + 1]] for n in range(len(IN_SPLITS))]
    parts = [t.astype(BF16) for t in (q, qi, z, xbc, k, v, ki, wi, dtr)]
    parts.append(jnp.zeros((w_in.shape[0], N_IN - sum(IN_SPLITS)), BF16))
    return jnp.concatenate(parts, axis=1)


def kernel(x_prompt, x_sample, cache_k, cache_v, cache_kidx, state_conv, state_ssm, page_table, rel_bias, g_ffn1, w1_ffn1, w3_ffn1, w2_ffn1, g_mix, w_in, conv_w, conv_b, a_log, dt_bias, d_skip, g_ssd, w_out, g_ffn2, w1_ffn2, w3_ffn2, w2_ffn2, g_final):
    depth = w_in.shape[0]
    assert depth == 1
    l = 0
    bp, seq, _ = x_prompt.shape
    bs, nq, _ = x_sample.shape
    n_pages = page_table.shape[1]
    row = lambda t: t.reshape(1, -1)

    w1a, w3a, w2a = w1_ffn1[l], w3_ffn1[l], w2_ffn1[l]
    w1b, w3b, w2b = w1_ffn2[l], w3_ffn2[l], w2_ffn2[l]
    w_in_r = _reorder_w_in(w_in[l])
    w_out_a = w_out[l, :D_ATTN].astype(BF16)
    w_out_s = w_out[l, D_ATTN:].astype(BF16)
    d_vec = jnp.repeat(d_skip[l], SSD_HEAD_DIM).reshape(1, D_SSD)
    btab = _bias_table(rel_bias)
    gf = row(g_final)

    tm, tf = 1024, 256

    def mix(x, attend, conv_prev, ssm_prev, batch, length):
        proj = _in_proj(x, row(g_mix[l]), w_in_r, tm=min(x.shape[0], tm))
        a_out = attend(proj)
        s_out, conv_new, ssm_new = _ssd(
            proj, conv_prev, ssm_prev.reshape(batch, D_SSD, D_STATE), conv_w[l], row(conv_b[l]), row(a_log[l]),
            row(dt_bias[l]), d_vec, row(g_ssd[l]), batch=batch, seq=length)
        x = _out_proj(x, a_out, s_out, w_out_a, w_out_s, tm=min(x.shape[0], 512))
        k = proj[:, COL_K:COL_K + D_KV].reshape(1, batch, length, N_KV_HEADS, HEAD_DIM)
        v = proj[:, COL_V:COL_V + D_KV].reshape(1, batch, length, N_KV_HEADS, HEAD_DIM)
        ki = proj[:, COL_SMALL + SM_KI:COL_SMALL + SM_KI + IDX_DIM].reshape(1, batch, length, IDX_DIM)
        ssm_new = ssm_new.reshape(1, batch, SSD_HEADS, SSD_HEAD_DIM, D_STATE)
        return x, k, v, ki, conv_new[None], ssm_new

    xp, xs = _ffn(x_prompt.reshape(bp * seq, D_MODEL), x_sample.reshape(bs * nq, D_MODEL), row(g_ffn1[l]),
                  w1a, w3a, w2a, gf, tm=tm, tf=tf, final_norm=False)

    xp, k_p, v_p, ki_p, conv_p, ssm_p = mix(
        xp, functools.partial(_dsa_prompt, btab=btab, batch=bp, seq=seq),
        jnp.zeros((bp, CONV_WIDTH - 1, CONV_DIM), F32), jnp.zeros((bp, SSD_HEADS, SSD_HEAD_DIM, D_STATE), F32),
        bp, seq)

    pool_k = cache_k[l].reshape(-1, HEAD_DIM)
    pool_v = cache_v[l].reshape(-1, HEAD_DIM)
    pool_kidx = jnp.swapaxes(cache_kidx[l], 1, 2).reshape(-1, PAGE_SIZE)

    def sample_attend(proj):
        sk_past, sk_new = _sample_score(page_table, proj, pool_kidx, batch=bs, nq=nq, n_pages=n_pages)
        thr, lim = _sample_select(sk_past.reshape(bs * nq, -1), sk_new.reshape(bs * nq, LANES),
                                  rows_per_step=128)
        return _sample_attn(page_table, proj, sk_past, sk_new, thr, lim, btab, pool_k, pool_v,
                            batch=bs, nq=nq, n_pages=n_pages)

    xs, k_s, v_s, ki_s, conv_s, ssm_s = mix(xs, sample_attend, state_conv[l], state_ssm[l], bs, nq)

    y_p, y_s = _ffn(xp, xs, row(g_ffn2[l]), w1b, w3b, w2b, gf, tm=tm, tf=tf, final_norm=True)
    y_p = y_p.reshape(bp, seq, D_MODEL)
    y_s = y_s.reshape(bs, nq, D_MODEL)
    return (y_p, y_s, k_p, v_p, ki_p, conv_p, ssm_p, k_s, v_s, ki_s, conv_s, ssm_s)
```

```python
import functools
import math

import jax
import jax.numpy as jnp
from jax import lax
from jax.experimental import pallas as pl
from jax.experimental.pallas import tpu as pltpu

F32 = jnp.float32
BF16 = jnp.bfloat16
I32 = jnp.int32

D_MODEL = 2048
PAGE_SIZE = 128
N_HEADS = 8
HEAD_DIM = 128
N_KV_HEADS = 2
GQA_GROUP = N_HEADS // N_KV_HEADS
IDX_HEADS = 16
IDX_DIM = 64
TOPK = 256
N_BUCKETS = 32
MAX_DISTANCE = 128
SSD_HEADS = 16
SSD_HEAD_DIM = 64
SSD_GROUPS = 2
D_STATE = 128
CONV_WIDTH = 4
CHUNK = 128
D_ATTN = N_HEADS * HEAD_DIM
D_SSD = SSD_HEADS * SSD_HEAD_DIM
D_KV = N_KV_HEADS * HEAD_DIM
CONV_DIM = D_SSD + 2 * SSD_GROUPS * D_STATE
D_FF = 5632
EPS = 1e-6
IN_SPLITS = (D_ATTN, D_KV, D_KV, IDX_HEADS * IDX_DIM, IDX_DIM, IDX_HEADS, D_SSD, CONV_DIM, SSD_HEADS)

LANES = 128
COL_Q = 0
COL_QI = COL_Q + D_ATTN
COL_Z = COL_QI + IDX_HEADS * IDX_DIM
COL_XBC = COL_Z + D_SSD
COL_K = COL_XBC + CONV_DIM
COL_V = COL_K + D_KV
COL_SMALL = COL_V + D_KV
SM_KI = 0
SM_WI = SM_KI + IDX_DIM
SM_DT = SM_WI + IDX_HEADS
IN_TILE = 768
N_IN = ((COL_SMALL + LANES + IN_TILE - 1) // IN_TILE) * IN_TILE

INT_MIN = -(2 ** 31)
NEG_BIG = -1e30
VMEM_LIMIT = 60000 * 1024


def _params(sem):
    return pltpu.CompilerParams(dimension_semantics=sem, vmem_limit_bytes=VMEM_LIMIT)


def _rms(x, g):
    return x * lax.rsqrt(jnp.mean(x * x, axis=-1, keepdims=True) + EPS) * g


def _dot(a, b):
    return jnp.dot(a, b, preferred_element_type=F32)


def _dot_nt(a, b):
    return lax.dot_general(a, b, (((1,), (1,)), ((), ())), preferred_element_type=F32)


def _split3(x):
    hi = x.astype(BF16)
    rest = x - hi.astype(F32)
    mid = rest.astype(BF16)
    return hi, mid, (rest - mid.astype(F32)).astype(BF16)


def _select_dot(a, b):
    if a.dtype == BF16:
        return sum(_dot(a, part) for part in _split3(b))
    return sum(_dot(part, b) for part in _split3(a))


def _sort_key(x):
    bits = lax.bitcast_convert_type(x + 0.0, I32)
    return bits ^ ((bits >> 31) & 0x7FFFFFFF)


def _ffn_kernel(x_ref, xs_ref, g_ref, gf_ref, w1_hbm, w3_hbm, w2_hbm, o_ref, os_ref, h_scr, w1_buf, w3_buf,
                w2_buf, sem, *, tf, final_norm):
    i = pl.program_id(0)
    nf = D_FF // tf
    total = pl.num_programs(0) * nf
    tm = x_ref.shape[0]

    def copies(f, slot):
        c0 = pl.multiple_of(f * tf, tf)
        return (pltpu.make_async_copy(w1_hbm.at[:, pl.ds(c0, tf)], w1_buf.at[slot], sem.at[0, slot]),
                pltpu.make_async_copy(w3_hbm.at[:, pl.ds(c0, tf)], w3_buf.at[slot], sem.at[1, slot]),
                pltpu.make_async_copy(w2_hbm.at[pl.ds(c0, tf), :], w2_buf.at[slot], sem.at[2, slot]))

    @pl.when(i == 0)
    def _():
        for c in copies(0, 0):
            c.start()

    h_scr[0:tm, :] = _rms(x_ref[...], g_ref[...]).astype(BF16)
    h_scr[tm:, :] = _rms(xs_ref[...], g_ref[...]).astype(BF16)
    o_ref[...] = jnp.zeros_like(o_ref)
    os_ref[...] = jnp.zeros_like(os_ref)

    def body(f, carry):
        t = i * nf + f
        slot = lax.rem(t, 2)

        @pl.when(t + 1 < total)
        def _():
            for c in copies(jnp.where(f + 1 == nf, 0, f + 1), 1 - slot):
                c.start()

        for c in copies(f, slot):
            c.wait()
        h = h_scr[...]
        a = _dot(h, w1_buf[slot].astype(BF16))
        b = _dot(h, w3_buf[slot].astype(BF16))
        u = (a * jax.nn.sigmoid(a) * b).astype(BF16)
        y = _dot(u, w2_buf[slot].astype(BF16))
        o_ref[...] += y[0:tm, :]
        os_ref[...] += y[tm:, :]
        return carry

    lax.fori_loop(0, nf, body, 0)
    for src, dst in ((x_ref, o_ref), (xs_ref, os_ref)):
        y = src[...] + 0.5 * dst[...]
        if final_norm:
            y = _rms(y, gf_ref[...])
        dst[...] = y


def _ffn(x, xs, g, w1, w3, w2, gf, *, tm, tf, final_norm):
    t = x.shape[0]
    ts = xs.shape[0] // (t // tm)
    assert ts * (t // tm) == xs.shape[0] and ts % 16 == 0
    return pl.pallas_call(
        functools.partial(_ffn_kernel, tf=tf, final_norm=final_norm),
        grid=(t // tm,),
        in_specs=[
            pl.BlockSpec((tm, D_MODEL), lambda i: (i, 0)),
            pl.BlockSpec((ts, D_MODEL), lambda i: (i, 0)),
            pl.BlockSpec((1, D_MODEL), lambda i: (0, 0)),
            pl.BlockSpec((1, D_MODEL), lambda i: (0, 0)),
            pl.BlockSpec(memory_space=pl.ANY),
            pl.BlockSpec(memory_space=pl.ANY),
            pl.BlockSpec(memory_space=pl.ANY),
        ],
        out_specs=[pl.BlockSpec((tm, D_MODEL), lambda i: (i, 0)), pl.BlockSpec((ts, D_MODEL), lambda i: (i, 0))],
        out_shape=[jax.ShapeDtypeStruct((t, D_MODEL), F32), jax.ShapeDtypeStruct(xs.shape, F32)],
        scratch_shapes=[
            pltpu.VMEM((tm + ts, D_MODEL), BF16),
            pltpu.VMEM((2, D_MODEL, tf), F32),
            pltpu.VMEM((2, D_MODEL, tf), F32),
            pltpu.VMEM((2, tf, D_MODEL), F32),
            pltpu.SemaphoreType.DMA((3, 2)),
        ],
        compiler_params=_params(("arbitrary",)),
        name="ffn",
    )(x, xs, g, gf, w1, w3, w2)


def _in_proj_kernel(x_ref, g_ref, w_ref, o_ref, h_scr):
    @pl.when(pl.program_id(1) == 0)
    def _():
        h_scr[...] = _rms(x_ref[...], g_ref[...]).astype(BF16)

    o_ref[...] = _dot(h_scr[...], w_ref[...])


def _in_proj(x, g, w, *, tm):
    t = x.shape[0]
    return pl.pallas_call(
        _in_proj_kernel,
        grid=(t // tm, N_IN // IN_TILE),
        in_specs=[
            pl.BlockSpec((tm, D_MODEL), lambda i, j: (i, 0)),
            pl.BlockSpec((1, D_MODEL), lambda i, j: (0, 0)),
            pl.BlockSpec((D_MODEL, IN_TILE), lambda i, j: (0, j)),
        ],
        out_specs=pl.BlockSpec((tm, IN_TILE), lambda i, j: (i, j)),
        out_shape=jax.ShapeDtypeStruct((t, N_IN), F32),
        scratch_shapes=[pltpu.VMEM((tm, D_MODEL), BF16)],
        compiler_params=_params(("parallel", "arbitrary")),
        name="in_proj",
    )(x, g, w)


def _out_proj_kernel(x_ref, a_ref, s_ref, wa_ref, ws_ref, o_ref):
    acc = _dot(a_ref[...].astype(BF16), wa_ref[...])
    acc += _dot(s_ref[...].astype(BF16), ws_ref[...])
    o_ref[...] = x_ref[...] + acc


def _out_proj(x, a, s, wa, ws, *, tm):
    t = x.shape[0]
    return pl.pallas_call(
        _out_proj_kernel,
        grid=(t // tm,),
        in_specs=[
            pl.BlockSpec((tm, D_MODEL), lambda i: (i, 0)),
            pl.BlockSpec((tm, D_ATTN), lambda i: (i, 0)),
            pl.BlockSpec((tm, D_SSD), lambda i: (i, 0)),
            pl.BlockSpec((D_ATTN, D_MODEL), lambda i: (0, 0)),
            pl.BlockSpec((D_SSD, D_MODEL), lambda i: (0, 0)),
        ],
        out_specs=pl.BlockSpec((tm, D_MODEL), lambda i: (i, 0)),
        out_shape=jax.ShapeDtypeStruct((t, D_MODEL), F32),
        compiler_params=_params(("parallel",)),
        name="out_proj",
    )(x, a, s, wa, ws)


NEAR = 2 * LANES


def _bias_kernel(rb_ref, o_ref):
    r = lax.broadcasted_iota(I32, (LANES, NEAR), 0)
    c = lax.broadcasted_iota(I32, (LANES, NEAR), 1)
    n = jnp.maximum(r + LANES - c, 0)
    max_exact = N_BUCKETS // 2
    nf = jnp.maximum(n, 1).astype(F32)
    large = max_exact + (jnp.log(nf / max_exact) / math.log(MAX_DISTANCE / max_exact)
                         * (N_BUCKETS - max_exact)).astype(I32)
    bucket = jnp.where(n < max_exact, n, jnp.minimum(large, N_BUCKETS - 1))
    for h in range(N_HEADS):
        acc = jnp.zeros((LANES, NEAR), F32)
        for b in range(N_BUCKETS):
            acc = jnp.where(bucket == b, rb_ref[b, h], acc)
        o_ref[h] = acc


def _bias_table(rel_bias):
    return pl.pallas_call(
        _bias_kernel,
        in_specs=[pl.BlockSpec(memory_space=pltpu.SMEM)],
        out_specs=pl.BlockSpec(memory_space=pltpu.VMEM),
        out_shape=jax.ShapeDtypeStruct((N_HEADS, LANES, NEAR), F32),
        name="bias_table",
    )(rel_bias)


def _count(mask, axis=1):
    x = mask.astype(F32)
    if axis == 0:
        x = jnp.sum(x.reshape(x.shape[0] // 64, 64, x.shape[1]), axis=0)
    return jnp.sum(x, axis=axis, keepdims=True)


def _kth_largest(count_ge, shape):
    t = jnp.full(shape, INT_MIN, I32)
    t = jnp.where(count_ge(jnp.zeros(shape, I32)) >= TOPK, 0, t)

    def body(it, t):
        cand = t + lax.shift_left(jnp.int32(1), 30 - it)
        return jnp.where(count_ge(cand) >= TOPK, cand, t)

    return lax.fori_loop(0, 31, body, t)


def _tie_limit(count_eq_below, need, shape, idx_bits):
    def body(it, m):
        cand = m + lax.shift_left(jnp.int32(1), idx_bits - 1 - it)
        return jnp.where(count_eq_below(cand) < need, cand, m)

    return lax.fori_loop(0, idx_bits, body, jnp.zeros(shape, I32))


CAUSAL_VARIANTS = 4


def _dsa_prompt_block(width, near_tiles, i, q_ref, qi_ref, k_ref, v_ref, sm_ref, bt_ref, o_ref, sk_scr, neg_scr,
                      kb_scr, vt_scr):
    tq = LANES
    q0 = pl.multiple_of(i * LANES, LANES)
    shape = (1, tq)

    wi_t = sm_ref[pl.ds(q0, tq), :].T[SM_WI:SM_WI + IDX_HEADS, :] * (IDX_HEADS * IDX_DIM) ** -0.5
    head_group = 4
    qi_groups = [
        jnp.concatenate([qi_ref[:, h * IDX_DIM:(h + 1) * IDX_DIM] for h in range(hg, hg + head_group)],
                        axis=0).astype(BF16)
        for hg in range(0, IDX_HEADS, head_group)]
    chunk = 2 * LANES
    pos = lax.broadcasted_iota(I32, (chunk, tq), 1) + q0
    for c in range(0, width, chunk):
        ki = sm_ref[c:c + chunk, SM_KI:SM_KI + IDX_DIM].astype(BF16)
        score = jnp.zeros((chunk, tq), F32)
        for n, hg in enumerate(range(0, IDX_HEADS, head_group)):
            rel = _dot_nt(ki, qi_groups[n])
            for e in range(head_group):
                score = score + wi_t[hg + e:hg + e + 1, :] * jnp.maximum(rel[:, e * tq:(e + 1) * tq], 0.0)
        key_c = lax.broadcasted_iota(I32, (chunk, tq), 0) + c
        sk_scr[c:c + chunk, :] = jnp.where(key_c <= pos, _sort_key(score), INT_MIN)

    key = lax.broadcasted_iota(I32, (width, tq), 0)

    def sk():
        return sk_scr[0:width, :]

    thr = _kth_largest(lambda c: _count(sk() >= c, 0), shape)
    excess = (_count(sk() >= thr, 0) > TOPK) & (thr > INT_MIN)
    any_excess = jnp.max(excess.astype(F32)) > 0.0

    @pl.when(jnp.logical_not(any_excess))
    def _():
        neg_scr[0:width, :] = jnp.where(sk() >= jnp.maximum(thr, INT_MIN + 1), 0.0, -jnp.inf)

    @pl.when(any_excess)
    def _():
        need = TOPK - _count(sk() > thr, 0)
        lim = _tie_limit(lambda m: _count((sk() == thr) & (key < m), 0), need, shape, (width - 1).bit_length())
        take = (sk() > thr) | ((sk() == thr) & (key <= lim) & (sk() > INT_MIN))
        neg_scr[0:width, :] = jnp.where(take, 0.0, -jnp.inf)

    log2e = math.log2(math.e)
    scale = HEAD_DIM ** -0.5 * log2e
    tiles = width // LANES
    for g in range(N_KV_HEADS):
        heads = range(g * GQA_GROUP, (g + 1) * GQA_GROUP)
        kg = kb_scr[g, 0:width, :]
        vg_t = vt_scr[g, :, 0:width]
        qg = jnp.concatenate([q_ref[:, h * HEAD_DIM:(h + 1) * HEAD_DIM] for h in heads], axis=0).astype(BF16)
        diag_t = [((bt_ref[h, :, LANES:NEAR] - bt_ref[h, 0:1, 0:1]) * log2e).T for h in heads]
        prev_t = [((bt_ref[h, :, 0:LANES] - bt_ref[h, 0:1, 0:1]) * log2e).T for h in heads]
        qk = _dot_nt(kg, qg)
        logits = []
        for t in range(tiles):
            rows = slice(t * LANES, (t + 1) * LANES)
            per_head = []
            for e in range(GQA_GROUP):
                lt = qk[rows, e * tq:(e + 1) * tq] * scale + neg_scr[rows, :]
                if t >= tiles - near_tiles:
                    lt = lt + jnp.where(i == t, diag_t[e], jnp.where(i - 1 == t, prev_t[e], 0.0))
                per_head.append(lt)
            logits.append(per_head)
        p_t = []
        inv_l = []
        for e in range(GQA_GROUP):
            m = jnp.max(functools.reduce(jnp.maximum, [lt[e] for lt in logits]), axis=0, keepdims=True)
            probs = [jnp.exp2(lt[e] - m) for lt in logits]
            inv_l.append(1.0 / jnp.sum(functools.reduce(jnp.add, probs), axis=0, keepdims=True))
            p_t.append(jnp.concatenate([p.astype(BF16) for p in probs], axis=0))
        out_t = _dot(vg_t, jnp.concatenate(p_t, axis=1))
        for e, h in enumerate(heads):
            o_ref[:, h * HEAD_DIM:(h + 1) * HEAD_DIM] = (out_t[:, e * tq:(e + 1) * tq] * inv_l[e]).T


def _dsa_prompt_kernel(*refs, seq):
    i = pl.program_id(1)
    per = seq // LANES // CAUSAL_VARIANTS
    k_ref, v_ref = refs[2], refs[3]
    kb_scr, vt_scr = refs[-2], refs[-1]

    @pl.when(i == 0)
    def _():
        for g in range(N_KV_HEADS):
            cols = slice(g * HEAD_DIM, (g + 1) * HEAD_DIM)
            kb_scr[g] = k_ref[:, cols].astype(BF16)
            vt_scr[g] = v_ref[:, cols].T.astype(BF16)

    for v in range(CAUSAL_VARIANTS):
        @pl.when(i // per == v)
        def _(v=v):
            _dsa_prompt_block((v + 1) * per * LANES, per + 1, i, *refs)


def _dsa_prompt(proj, btab, *, batch, seq):
    nq = seq // LANES
    return pl.pallas_call(
        functools.partial(_dsa_prompt_kernel, seq=seq),
        grid=(batch, nq),
        in_specs=[
            pl.BlockSpec((LANES, D_ATTN), lambda b, i: (b * nq + i, COL_Q // D_ATTN)),
            pl.BlockSpec((LANES, IDX_HEADS * IDX_DIM), lambda b, i: (b * nq + i, COL_QI // (IDX_HEADS * IDX_DIM))),
            pl.BlockSpec((seq, D_KV), lambda b, i: (b, COL_K // D_KV)),
            pl.BlockSpec((seq, D_KV), lambda b, i: (b, COL_V // D_KV)),
            pl.BlockSpec((seq, LANES), lambda b, i: (b, COL_SMALL // LANES)),
            pl.BlockSpec((N_HEADS, LANES, NEAR), lambda b, i: (0, 0, 0)),
        ],
        out_specs=pl.BlockSpec((LANES, D_ATTN), lambda b, i: (b * nq + i, 0)),
        out_shape=jax.ShapeDtypeStruct((batch * seq, D_ATTN), F32),
        scratch_shapes=[pltpu.VMEM((seq, LANES), I32), pltpu.VMEM((seq, LANES), F32),
                        pltpu.VMEM((N_KV_HEADS, seq, HEAD_DIM), BF16), pltpu.VMEM((N_KV_HEADS, HEAD_DIM, seq), BF16)],
        compiler_params=_params(("arbitrary", "arbitrary")),
        name="dsa_prompt",
    )(proj, proj, proj, proj, proj, btab)


SCORE_PAGES = 64
ATTN_PAGES = 32


def _sample_queries(qi_ref):
    return jnp.concatenate(
        [qi_ref[:, h * IDX_DIM:(h + 1) * IDX_DIM] for h in range(IDX_HEADS)], axis=0).astype(BF16)


def _sample_scores(rel, wi, nq):
    sc = jnp.zeros((nq, rel.shape[1]), F32)
    for h in range(IDX_HEADS):
        sc = sc + wi[:, h:h + 1] * jnp.maximum(rel[h * nq:(h + 1) * nq, :], 0.0)
    return sc


def _page_copies(pt_ref, pools, bufs, sems, b, s, slot, pages, page_rows):
    copies = []
    for r in range(pages):
        row0 = pl.multiple_of(pt_ref[b, s * pages + r] * page_rows, page_rows)
        for pool, buf, sem in zip(pools, bufs, sems):
            copies.append(pltpu.make_async_copy(pool.at[pl.ds(row0, page_rows), :], buf.at[slot, r], sem.at[slot]))
    return copies


def _page_pipeline(pt_ref, pools, bufs, sems, pages, page_rows):
    b, s = pl.program_id(0), pl.program_id(1)
    steps = pl.num_programs(1)
    t = b * steps + s
    slot = lax.rem(t, 2)
    copies = functools.partial(_page_copies, pt_ref, pools, bufs, sems, pages=pages, page_rows=page_rows)

    @pl.when(t == 0)
    def _():
        for c in copies(b, s, slot):
            c.start()

    @pl.when(t + 1 < pl.num_programs(0) * steps)
    def _():
        wrap = s + 1 == steps
        for c in copies(jnp.where(wrap, b + 1, b), jnp.where(wrap, 0, s + 1), 1 - slot):
            c.start()

    for c in copies(b, s, slot):
        c.wait()
    return slot


def _sample_score_kernel(pt_ref, qi_ref, sm_ref, pool_ref, past_ref, new_ref, pages, sem, kbuf, *, nq):
    s = pl.program_id(1)
    slot = _page_pipeline(pt_ref, [pool_ref], [pages], [sem], SCORE_PAGES, IDX_DIM)
    qx = _sample_queries(qi_ref)
    wi = sm_ref[:, SM_WI:SM_WI + IDX_HEADS] * (IDX_HEADS * IDX_DIM) ** -0.5
    for r in range(SCORE_PAGES):
        kbuf[:, r * PAGE_SIZE:(r + 1) * PAGE_SIZE] = pages[slot, r].astype(BF16)
    past_ref[...] = _sort_key(_sample_scores(_dot(qx, kbuf[...]), wi, nq))

    @pl.when(s == pl.num_programs(1) - 1)
    def _():
        ki_new = jnp.concatenate(
            [sm_ref[:, SM_KI:SM_KI + IDX_DIM], jnp.zeros((LANES - nq, IDX_DIM), F32)], axis=0).astype(BF16)
        sc = _sample_scores(_dot_nt(qx, ki_new), wi, nq)
        j = lax.broadcasted_iota(I32, (nq, LANES), 1)
        t = lax.broadcasted_iota(I32, (nq, LANES), 0)
        new_ref[...] = jnp.where(j <= t, _sort_key(sc), INT_MIN)


def _sample_score(page_table, proj_s, pool_kidx_t, *, batch, nq, n_pages):
    steps = n_pages // SCORE_PAGES
    grid_spec = pltpu.PrefetchScalarGridSpec(
        num_scalar_prefetch=1,
        grid=(batch, steps),
        in_specs=[
            pl.BlockSpec((nq, IDX_HEADS * IDX_DIM), lambda b, s, pt: (b, COL_QI // (IDX_HEADS * IDX_DIM))),
            pl.BlockSpec((nq, LANES), lambda b, s, pt: (b, COL_SMALL // LANES)),
            pl.BlockSpec(memory_space=pl.ANY),
        ],
        out_specs=[
            pl.BlockSpec((None, nq, SCORE_PAGES * PAGE_SIZE), lambda b, s, pt: (b, 0, s)),
            pl.BlockSpec((None, nq, LANES), lambda b, s, pt: (b, 0, 0)),
        ],
        scratch_shapes=[
            pltpu.VMEM((2, SCORE_PAGES, IDX_DIM, PAGE_SIZE), F32),
            pltpu.SemaphoreType.DMA((2,)),
            pltpu.VMEM((IDX_DIM, SCORE_PAGES * PAGE_SIZE), BF16),
        ],
    )
    return pl.pallas_call(
        functools.partial(_sample_score_kernel, nq=nq),
        grid_spec=grid_spec,
        out_shape=[
            jax.ShapeDtypeStruct((batch, nq, n_pages * PAGE_SIZE), I32),
            jax.ShapeDtypeStruct((batch, nq, LANES), I32),
        ],
        compiler_params=_params(("arbitrary", "arbitrary")),
        name="sample_score",
    )(page_table, proj_s, proj_s, pool_kidx_t)


def _sample_select_kernel(past_ref, new_ref, thr_ref, lim_ref, *, past):
    rows = past_ref.shape[0]
    colp = lax.broadcasted_iota(I32, (rows, past), 1)
    coln = lax.broadcasted_iota(I32, (rows, LANES), 1) + past

    def count_ge(c):
        return _count(past_ref[...] >= c) + _count(new_ref[...] >= c)

    thr = _kth_largest(count_ge, (rows, 1))
    thr_ref[...] = jnp.broadcast_to(thr, (rows, LANES))
    lim_ref[...] = jnp.full((rows, LANES), 2 ** 31 - 1, I32)
    any_excess = jnp.max((count_ge(thr) > TOPK).astype(F32)) > 0.0

    @pl.when(any_excess)
    def _():
        need = TOPK - (_count(past_ref[...] > thr) + _count(new_ref[...] > thr))

        def count_eq_below(m):
            return (_count((past_ref[...] == thr) & (colp < m))
                    + _count((new_ref[...] == thr) & (coln < m)))

        lim = _tie_limit(count_eq_below, need, (rows, 1), (past + LANES - 1).bit_length())
        lim_ref[...] = jnp.broadcast_to(lim, (rows, LANES))


def _sample_select(sk_past, sk_new, *, rows_per_step):
    rows, past = sk_past.shape
    return pl.pallas_call(
        functools.partial(_sample_select_kernel, past=past),
        grid=(rows // rows_per_step,),
        in_specs=[
            pl.BlockSpec((rows_per_step, past), lambda i: (i, 0)),
            pl.BlockSpec((rows_per_step, LANES), lambda i: (i, 0)),
        ],
        out_specs=[
            pl.BlockSpec((rows_per_step, LANES), lambda i: (i, 0)),
            pl.BlockSpec((rows_per_step, LANES), lambda i: (i, 0)),
        ],
        out_shape=[jax.ShapeDtypeStruct((rows, LANES), I32)] * 2,
        compiler_params=_params(("parallel",)),
        name="sample_select",
    )(sk_past, sk_new)


def _sample_attn_kernel(pt_ref, q_ref, kn_ref, vn_ref, skp_ref, skn_ref, thr_ref, lim_ref, bt_ref, poolk_ref,
                        poolv_ref, o_ref, kpages, vpages, ksem, vsem, kbuf, vbuf, m_scr, l_scr, acc_scr, *,
                        nq, past):
    slot = _page_pipeline(pt_ref, [poolk_ref, poolv_ref], [kpages, vpages], [ksem, vsem], ATTN_PAGES,
                          PAGE_SIZE * N_KV_HEADS)
    s = pl.program_id(1)
    last = pl.num_programs(1) - 1
    width = ATTN_PAGES * PAGE_SIZE
    grows = GQA_GROUP * nq
    scale = HEAD_DIM ** -0.5

    @pl.when(s == 0)
    def _():
        m_scr[...] = jnp.full_like(m_scr, NEG_BIG)
        l_scr[...] = jnp.zeros_like(l_scr)
        acc_scr[...] = jnp.zeros_like(acc_scr)

    thr = thr_ref[:, 0:1]
    lim = lim_ref[:, 0:1]

    def select(sk, idx):
        sel = (sk > thr) | ((sk == thr) & (idx <= lim))
        return jnp.concatenate([sel] * GQA_GROUP, axis=0)

    def far_bias(g):
        return jnp.concatenate(
            [jnp.broadcast_to(bt_ref[g * GQA_GROUP + hh, 0:1, 0:1], (nq, 1)) for hh in range(GQA_GROUP)], axis=0)

    def near_bias(g, lo, hi):
        return jnp.concatenate([bt_ref[g * GQA_GROUP + hh, 0:nq, lo:hi] for hh in range(GQA_GROUP)], axis=0)

    def group_queries(g):
        return jnp.concatenate(
            [q_ref[:, (g * GQA_GROUP + hh) * HEAD_DIM:(g * GQA_GROUP + hh + 1) * HEAD_DIM]
             for hh in range(GQA_GROUP)], axis=0).astype(BF16)

    def accumulate(g, lg, sel, vals):
        rs = slice(g * grows, (g + 1) * grows)
        m_old = m_scr[rs, :]
        m_new = jnp.maximum(m_old, jnp.max(jnp.where(sel, lg, NEG_BIG), axis=1, keepdims=True))
        p = jnp.where(sel, jnp.exp(lg - m_new), 0.0)
        alpha = jnp.exp(m_old - m_new)
        l_scr[rs, :] = alpha * l_scr[rs, :] + jnp.sum(p, axis=1, keepdims=True)
        acc_scr[rs, :] = alpha * acc_scr[rs, :] + _dot(p.astype(BF16), vals)
        m_scr[rs, :] = m_new

    for r in range(ATTN_PAGES):
        for g in range(N_KV_HEADS):
            rows = pl.ds(g, PAGE_SIZE, stride=N_KV_HEADS)
            kbuf[g, r * PAGE_SIZE:(r + 1) * PAGE_SIZE, :] = kpages[slot, r, rows, :].astype(BF16)
            vbuf[g, r * PAGE_SIZE:(r + 1) * PAGE_SIZE, :] = vpages[slot, r, rows, :].astype(BF16)

    col = lax.broadcasted_iota(I32, (nq, width), 1)
    sel = select(skp_ref[...], col + s * width)
    colg = lax.broadcasted_iota(I32, (grows, width), 1)
    in_near = (s == last) & (colg >= width - PAGE_SIZE)
    for g in range(N_KV_HEADS):
        near = jnp.tile(near_bias(g, 0, LANES), (1, ATTN_PAGES))
        bias = jnp.where(in_near, near, far_bias(g))
        lg = _dot_nt(group_queries(g), kbuf[g]) * scale + bias
        accumulate(g, lg, sel, vbuf[g])

    @pl.when(s == last)
    def _():
        pad = jnp.zeros((LANES - nq, D_KV), F32)
        kn = jnp.concatenate([kn_ref[...], pad], axis=0).astype(BF16)
        vn = jnp.concatenate([vn_ref[...], pad], axis=0).astype(BF16)
        coln = lax.broadcasted_iota(I32, (nq, LANES), 1) + past
        seln = select(skn_ref[...], coln)
        for g in range(N_KV_HEADS):
            cs = slice(g * HEAD_DIM, (g + 1) * HEAD_DIM)
            lg = _dot_nt(group_queries(g), kn[:, cs]) * scale + near_bias(g, LANES, NEAR)
            accumulate(g, lg, seln, vn[:, cs])
        out = acc_scr[...] / l_scr[...]
        for h in range(N_HEADS):
            o_ref[:, h * HEAD_DIM:(h + 1) * HEAD_DIM] = out[h * nq:(h + 1) * nq, :]


def _sample_attn(page_table, proj_s, sk_past, sk_new, thr, lim, btab, pool_k, pool_v, *, batch, nq, n_pages):
    steps = n_pages // ATTN_PAGES
    width = ATTN_PAGES * PAGE_SIZE
    past = n_pages * PAGE_SIZE
    page_buf = pltpu.VMEM((2, ATTN_PAGES, PAGE_SIZE * N_KV_HEADS, HEAD_DIM), F32)
    grid_spec = pltpu.PrefetchScalarGridSpec(
        num_scalar_prefetch=1,
        grid=(batch, steps),
        in_specs=[
            pl.BlockSpec((nq, D_ATTN), lambda b, s, pt: (b, COL_Q // D_ATTN)),
            pl.BlockSpec((nq, D_KV), lambda b, s, pt: (b, COL_K // D_KV)),
            pl.BlockSpec((nq, D_KV), lambda b, s, pt: (b, COL_V // D_KV)),
            pl.BlockSpec((None, nq, width), lambda b, s, pt: (b, 0, s)),
            pl.BlockSpec((None, nq, LANES), lambda b, s, pt: (b, 0, 0)),
            pl.BlockSpec((nq, LANES), lambda b, s, pt: (b, 0)),
            pl.BlockSpec((nq, LANES), lambda b, s, pt: (b, 0)),
            pl.BlockSpec((N_HEADS, LANES, NEAR), lambda b, s, pt: (0, 0, 0)),
            pl.BlockSpec(memory_space=pl.ANY),
            pl.BlockSpec(memory_space=pl.ANY),
        ],
        out_specs=pl.BlockSpec((nq, D_ATTN), lambda b, s, pt: (b, 0)),
        scratch_shapes=[
            page_buf, page_buf, pltpu.SemaphoreType.DMA((2,)), pltpu.SemaphoreType.DMA((2,)),
            pltpu.VMEM((N_KV_HEADS, width, HEAD_DIM), BF16),
            pltpu.VMEM((N_KV_HEADS, width, HEAD_DIM), BF16),
            pltpu.VMEM((N_HEADS * nq, 1), F32),
            pltpu.VMEM((N_HEADS * nq, 1), F32),
            pltpu.VMEM((N_HEADS * nq, HEAD_DIM), F32),
        ],
    )
    return pl.pallas_call(
        functools.partial(_sample_attn_kernel, nq=nq, past=past),
        grid_spec=grid_spec,
        out_shape=jax.ShapeDtypeStruct((batch * nq, D_ATTN), F32),
        compiler_params=_params(("arbitrary", "arbitrary")),
        name="sample_attn",
    )(page_table, proj_s, proj_s, proj_s, sk_past, sk_new, thr, lim, btab, pool_k, pool_v)


TAIL = 8
SHORT_ROWS = 16


def _ssd_kernel(xbc_ref, z_ref, sm_ref, cprev_ref, hprev_ref, cw_ref, cb_ref, alog_ref, dtb_ref, dvec_ref,
                g_ref, y_ref, cnew_ref, hnew_ref, xpad_scr, sm_scr, h_scr, yd_scr, *, rows):
    c = pl.program_id(1)
    q = CHUNK
    r = q if rows == q else SHORT_ROWS
    assert rows <= r
    keep = CONV_WIDTH - 1
    heads_per_group = SSD_HEADS // SSD_GROUPS

    def pad_rows(x):
        return x if r == q else jnp.concatenate([x, jnp.zeros((q - r, x.shape[1]), x.dtype)], axis=0)

    @pl.when(c == 0)
    def _():
        xpad_scr[...] = jnp.zeros_like(xpad_scr)
        sm_scr[...] = jnp.zeros_like(sm_scr)
        xpad_scr[TAIL - keep:TAIL, :] = cprev_ref[0]
        h_scr[...] = hprev_ref[0]

    xpad_scr[TAIL:TAIL + rows, :] = xbc_ref[...]
    sm_scr[0:rows, :] = sm_ref[...]
    conv = sum(xpad_scr[TAIL - keep + k:TAIL - keep + k + r, :] * cw_ref[k:k + 1, :]
               for k in range(CONV_WIDTH)) + cb_ref[...]
    new_tail = xpad_scr[rows + TAIL - keep:rows + TAIL, :]
    cnew_ref[0] = new_tail
    xpad_scr[TAIL - keep:TAIL, :] = new_tail

    xc = conv * jax.nn.sigmoid(conv)
    xs = xc[:, :D_SSD]
    bm = pad_rows(xc[:, D_SSD:D_SSD + SSD_GROUPS * D_STATE]).astype(BF16)
    cm = xc[:, D_SSD + SSD_GROUPS * D_STATE:].astype(BF16)

    ri = lax.broadcasted_iota(I32, (q, q), 0)
    ci = lax.broadcasted_iota(I32, (q, q), 1)
    causal = ri >= ci
    causal_r = causal[0:r, :]
    x = sm_scr[:, SM_DT:SM_DT + SSD_HEADS] + dtb_ref[...]
    dt = jnp.maximum(x, 0.0) + jnp.log1p(jnp.exp(-jnp.abs(x)))
    dt = jnp.where(lax.broadcasted_iota(I32, (q, SSD_HEADS), 0) < rows, dt, 0.0)
    a = -jnp.exp(alog_ref[...])
    acum = _select_dot(causal.astype(BF16), dt * a)
    acum_t = jnp.concatenate([acum, jnp.zeros((q, LANES - SSD_HEADS), F32)], axis=1).T

    expand = (lax.broadcasted_iota(I32, (SSD_HEADS, D_SSD), 1) // SSD_HEAD_DIM
              == lax.broadcasted_iota(I32, (SSD_HEADS, D_SSD), 0)).astype(BF16)
    expand_t = (lax.broadcasted_iota(I32, (D_SSD, SSD_HEADS), 0) // SSD_HEAD_DIM
                == lax.broadcasted_iota(I32, (D_SSD, SSD_HEADS), 1)).astype(BF16)
    acum_x = _select_dot(acum[0:r, :], expand)
    xd = xs * _select_dot(dt[0:r, :], expand)
    xdw_t = pad_rows(xd * jnp.exp(acum_x[r - 1:r, :] - acum_x)).T.astype(BF16)
    xd = pad_rows(xd).astype(BF16)
    state_decay = jnp.exp(_select_dot(expand_t, jnp.broadcast_to(acum_t[0:SSD_HEADS, q - 1:q], (SSD_HEADS, LANES))))

    for g in range(SSD_GROUPS):
        ns = slice(g * D_STATE, (g + 1) * D_STATE)
        hs = slice(g * heads_per_group * SSD_HEAD_DIM, (g + 1) * heads_per_group * SSD_HEAD_DIM)
        cb = _dot_nt(cm[:, ns], bm[:, ns])
        for e in range(heads_per_group):
            h = g * heads_per_group + e
            ps = slice(h * SSD_HEAD_DIM, (h + 1) * SSD_HEAD_DIM)
            seg = acum[0:r, h:h + 1] - acum_t[h:h + 1, :]
            lmat = jnp.exp(jnp.where(causal_r, seg, -jnp.inf))
            yd_scr[0:r, ps] = _dot((cb * lmat).astype(BF16), xd[:, ps])
        h_old = h_scr[hs, :]
        yd_scr[0:r, hs] += _dot_nt(cm[:, ns], h_old.astype(BF16)) * jnp.exp(acum_x[:, hs])
        h_scr[hs, :] = h_old * state_decay[hs, :] + _dot(xdw_t[hs, :], bm[:, ns])

    hnew_ref[0] = h_scr[...]
    y = yd_scr[0:rows, :] + dvec_ref[...] * xs[0:rows, :]
    zz = z_ref[...]
    y = y * (zz * jax.nn.sigmoid(zz))
    width = D_SSD // SSD_GROUPS
    for g in range(SSD_GROUPS):
        cs = slice(g * width, (g + 1) * width)
        y_ref[:, cs] = _rms(y[:, cs], g_ref[:, cs])


def _ssd(proj, conv_prev, ssm_prev, conv_w, conv_b, a_log, dt_bias, d_vec, g_ssd, *, batch, seq):
    rows = min(CHUNK, seq)
    nc = seq // rows
    full = lambda *shape: pl.BlockSpec(shape, lambda b, c: (0,) * len(shape))
    return pl.pallas_call(
        functools.partial(_ssd_kernel, rows=rows),
        grid=(batch, nc),
        in_specs=[
            pl.BlockSpec((rows, CONV_DIM), lambda b, c: (b * nc + c, COL_XBC // CONV_DIM)),
            pl.BlockSpec((rows, D_SSD), lambda b, c: (b * nc + c, COL_Z // D_SSD)),
            pl.BlockSpec((rows, LANES), lambda b, c: (b * nc + c, COL_SMALL // LANES)),
            pl.BlockSpec((1, CONV_WIDTH - 1, CONV_DIM), lambda b, c: (b, 0, 0)),
            pl.BlockSpec((1, D_SSD, D_STATE), lambda b, c: (b, 0, 0)),
            full(CONV_WIDTH, CONV_DIM), full(1, CONV_DIM), full(1, SSD_HEADS), full(1, SSD_HEADS),
            full(1, D_SSD), full(1, D_SSD),
        ],
        out_specs=[
            pl.BlockSpec((rows, D_SSD), lambda b, c: (b * nc + c, 0)),
            pl.BlockSpec((1, CONV_WIDTH - 1, CONV_DIM), lambda b, c: (b, 0, 0)),
            pl.BlockSpec((1, D_SSD, D_STATE), lambda b, c: (b, 0, 0)),
        ],
        out_shape=[
            jax.ShapeDtypeStruct((batch * seq, D_SSD), F32),
            jax.ShapeDtypeStruct((batch, CONV_WIDTH - 1, CONV_DIM), F32),
            jax.ShapeDtypeStruct((batch, D_SSD, D_STATE), F32),
        ],
        scratch_shapes=[
            pltpu.VMEM((CHUNK + TAIL, CONV_DIM), F32),
            pltpu.VMEM((CHUNK, LANES), F32),
            pltpu.VMEM((D_SSD, D_STATE), F32),
            pltpu.VMEM((CHUNK, D_SSD), F32),
        ],
        compiler_params=_params(("parallel", "arbitrary")),
        name="ssd",
    )(proj, proj, proj, conv_prev, ssm_prev, conv_w, conv_b, a_log, dt_bias, d_vec, g_ssd)


def _reorder_w_in(w_in):
    offs = [0]
    for s in IN_SPLITS:
        offs.append(offs[-1] + s)
    q, k, v, qi, ki, wi, z, xbc, dtr = [w_in[:, offs[n]:offs[n + 1]] for n in range(len(IN_SPLITS))]
    parts = [t.astype(BF16) for t in (q, qi, z, xbc, k, v, ki, wi, dtr)]
    parts.append(jnp.zeros((w_in.shape[0], N_IN - sum(IN_SPLITS)), BF16))
    return jnp.concatenate(parts, axis=1)


def kernel(x_prompt, x_sample, cache_k, cache_v, cache_kidx, state_conv, state_ssm, page_table, rel_bias, g_ffn1, w1_ffn1, w3_ffn1, w2_ffn1, g_mix, w_in, conv_w, conv_b, a_log, dt_bias, d_skip, g_ssd, w_out, g_ffn2, w1_ffn2, w3_ffn2, w2_ffn2, g_final):
    depth = w_in.shape[0]
    assert depth == 1
    l = 0
    bp, seq, _ = x_prompt.shape
    bs, nq, _ = x_sample.shape
    n_pages = page_table.shape[1]
    row = lambda t: t.reshape(1, -1)

    w1a, w3a, w2a = w1_ffn1[l], w3_ffn1[l], w2_ffn1[l]
    w1b, w3b, w2b = w1_ffn2[l], w3_ffn2[l], w2_ffn2[l]
    w_in_r = _reorder_w_in(w_in[l])
    w_out_a = w_out[l, :D_ATTN].astype(BF16)
    w_out_s = w_out[l, D_ATTN:].astype(BF16)
    d_vec = jnp.repeat(d_skip[l], SSD_HEAD_DIM).reshape(1, D_SSD)
    btab = _bias_table(rel_bias)
    gf = row(g_final)

    tm, tf = 1024, 256

    def mix(x, attend, conv_prev, ssm_prev, batch, length):
        proj = _in_proj(x, row(g_mix[l]), w_in_r, tm=min(x.shape[0], tm))
        a_out = attend(proj)
        s_out, conv_new, ssm_new = _ssd(
            proj, conv_prev, ssm_prev.reshape(batch, D_SSD, D_STATE), conv_w[l], row(conv_b[l]), row(a_log[l]),
            row(dt_bias[l]), d_vec, row(g_ssd[l]), batch=batch, seq=length)
        x = _out_proj(x, a_out, s_out, w_out_a, w_out_s, tm=min(x.shape[0], 512))
        k = proj[:, COL_K:COL_K + D_KV].reshape(1, batch, length, N_KV_HEADS, HEAD_DIM)
        v = proj[:, COL_V:COL_V + D_KV].reshape(1, batch, length, N_KV_HEADS, HEAD_DIM)
        ki = proj[:, COL_SMALL + SM_KI:COL_SMALL + SM_KI + IDX_DIM].reshape(1, batch, length, IDX_DIM)
        ssm_new = ssm_new.reshape(1, batch, SSD_HEADS, SSD_HEAD_DIM, D_STATE)
        return x, k, v, ki, conv_new[None], ssm_new

    xp, xs = _ffn(x_prompt.reshape(bp * seq, D_MODEL), x_sample.reshape(bs * nq, D_MODEL), row(g_ffn1[l]),
                  w1a, w3a, w2a, gf, tm=tm, tf=tf, final_norm=False)

    xp, k_p, v_p, ki_p, conv_p, ssm_p = mix(
        xp, functools.partial(_dsa_prompt, btab=btab, batch=bp, seq=seq),
        jnp.zeros((bp, CONV_WIDTH - 1, CONV_DIM), F32), jnp.zeros((bp, SSD_HEADS, SSD_HEAD_DIM, D_STATE), F32),
        bp, seq)

    pool_k = cache_k[l].reshape(-1, HEAD_DIM)
    pool_v = cache_v[l].reshape(-1, HEAD_DIM)
    pool_kidx = jnp.swapaxes(cache_kidx[l], 1, 2).reshape(-1, PAGE_SIZE)

    def sample_attend(proj):
        sk_past, sk_new = _sample_score(page_table, proj, pool_kidx, batch=bs, nq=nq, n_pages=n_pages)
        thr, lim = _sample_select(sk_past.reshape(bs * nq, -1), sk_new.reshape(bs * nq, LANES),
                                  rows_per_step=128)
        return _sample_attn(page_table, proj, sk_past, sk_new, thr, lim, btab, pool_k, pool_v,
                            batch=bs, nq=nq, n_pages=n_pages)

    xs, k_s, v_s, ki_s, conv_s, ssm_s = mix(xs, sample_attend, state_conv[l], state_ssm[l], bs, nq)

    y_p, y_s = _ffn(xp, xs, row(g_ffn2[l]), w1b, w3b, w2b, gf, tm=tm, tf=tf, final_norm=True)
    y_p = y_p.reshape(bp, seq, D_MODEL)
    y_s = y_s.reshape(bs, nq, D_MODEL)
    return (y_p, y_s, k_p, v_p, ki_p, conv_p, ssm_p, k_s, v_s, ki_s, conv_s, ssm_s)
```

```python
import functools
import math

import jax
import jax.numpy as jnp
from jax import lax
from jax.experimental import pallas as pl
from jax.experimental.pallas import tpu as pltpu

F32 = jnp.float32
BF16 = jnp.bfloat16
I32 = jnp.int32

D_MODEL = 2048
PAGE_SIZE = 128
N_HEADS = 8
HEAD_DIM = 128
N_KV_HEADS = 2
GQA_GROUP = N_HEADS // N_KV_HEADS
IDX_HEADS = 16
IDX_DIM = 64
TOPK = 256
N_BUCKETS = 32
MAX_DISTANCE = 128
SSD_HEADS = 16
SSD_HEAD_DIM = 64
SSD_GROUPS = 2
D_STATE = 128
CONV_WIDTH = 4
CHUNK = 128
D_ATTN = N_HEADS * HEAD_DIM
D_SSD = SSD_HEADS * SSD_HEAD_DIM
D_KV = N_KV_HEADS * HEAD_DIM
CONV_DIM = D_SSD + 2 * SSD_GROUPS * D_STATE
D_FF = 5632
EPS = 1e-6
IN_SPLITS = (D_ATTN, D_KV, D_KV, IDX_HEADS * IDX_DIM, IDX_DIM, IDX_HEADS, D_SSD, CONV_DIM, SSD_HEADS)

LANES = 128
COL_Q = 0
COL_QI = COL_Q + D_ATTN
COL_Z = COL_QI + IDX_HEADS * IDX_DIM
COL_XBC = COL_Z + D_SSD
COL_K = COL_XBC + CONV_DIM
COL_V = COL_K + D_KV
COL_SMALL = COL_V + D_KV
SM_KI = 0
SM_WI = SM_KI + IDX_DIM
SM_DT = SM_WI + IDX_HEADS
IN_TILE = 768
N_IN = ((COL_SMALL + LANES + IN_TILE - 1) // IN_TILE) * IN_TILE

INT_MIN = -(2 ** 31)
NEG_BIG = -1e30
VMEM_LIMIT = 60000 * 1024


def _params(sem):
    return pltpu.CompilerParams(dimension_semantics=sem, vmem_limit_bytes=VMEM_LIMIT)


def _rms(x, g):
    return x * lax.rsqrt(jnp.mean(x * x, axis=-1, keepdims=True) + EPS) * g


def _dot(a, b):
    return jnp.dot(a, b, preferred_element_type=F32)


def _dot_nt(a, b):
    return lax.dot_general(a, b, (((1,), (1,)), ((), ())), preferred_element_type=F32)


def _split3(x):
    hi = x.astype(BF16)
    rest = x - hi.astype(F32)
    mid = rest.astype(BF16)
    return hi, mid, (rest - mid.astype(F32)).astype(BF16)


def _select_dot(a, b):
    if a.dtype == BF16:
        return sum(_dot(a, part) for part in _split3(b))
    return sum(_dot(part, b) for part in _split3(a))


def _sort_key(x):
    bits = lax.bitcast_convert_type(x + 0.0, I32)
    return bits ^ ((bits >> 31) & 0x7FFFFFFF)


def _ffn_kernel(x_ref, xs_ref, g_ref, gf_ref, w1_hbm, w3_hbm, w2_hbm, o_ref, os_ref, h_scr, w1_buf, w3_buf,
                w2_buf, sem, *, tf, final_norm):
    i = pl.program_id(0)
    nf = D_FF // tf
    total = pl.num_programs(0) * nf
    tm = x_ref.shape[0]

    def copies(f, slot):
        c0 = pl.multiple_of(f * tf, tf)
        return (pltpu.make_async_copy(w1_hbm.at[:, pl.ds(c0, tf)], w1_buf.at[slot], sem.at[0, slot]),
                pltpu.make_async_copy(w3_hbm.at[:, pl.ds(c0, tf)], w3_buf.at[slot], sem.at[1, slot]),
                pltpu.make_async_copy(w2_hbm.at[pl.ds(c0, tf), :], w2_buf.at[slot], sem.at[2, slot]))

    @pl.when(i == 0)
    def _():
        for c in copies(0, 0):
            c.start()

    h_scr[0:tm, :] = _rms(x_ref[...], g_ref[...]).astype(BF16)
    h_scr[tm:, :] = _rms(xs_ref[...], g_ref[...]).astype(BF16)
    o_ref[...] = jnp.zeros_like(o_ref)
    os_ref[...] = jnp.zeros_like(os_ref)

    def body(f, carry):
        t = i * nf + f
        slot = lax.rem(t, 2)

        @pl.when(t + 1 < total)
        def _():
            for c in copies(jnp.where(f + 1 == nf, 0, f + 1), 1 - slot):
                c.start()

        for c in copies(f, slot):
            c.wait()
        h = h_scr[...]
        a = _dot(h, w1_buf[slot].astype(BF16))
        b = _dot(h, w3_buf[slot].astype(BF16))
        u = (a * jax.nn.sigmoid(a) * b).astype(BF16)
        y = _dot(u, w2_buf[slot].astype(BF16))
        o_ref[...] += y[0:tm, :]
        os_ref[...] += y[tm:, :]
        return carry

    lax.fori_loop(0, nf, body, 0)
    for src, dst in ((x_ref, o_ref), (xs_ref, os_ref)):
        y = src[...] + 0.5 * dst[...]
        if final_norm:
            y = _rms(y, gf_ref[...])
        dst[...] = y


def _ffn(x, xs, g, w1, w3, w2, gf, *, tm, tf, final_norm):
    t = x.shape[0]
    ts = xs.shape[0] // (t // tm)
    assert ts * (t // tm) == xs.shape[0] and ts % 16 == 0
    return pl.pallas_call(
        functools.partial(_ffn_kernel, tf=tf, final_norm=final_norm),
        grid=(t // tm,),
        in_specs=[
            pl.BlockSpec((tm, D_MODEL), lambda i: (i, 0)),
            pl.BlockSpec((ts, D_MODEL), lambda i: (i, 0)),
            pl.BlockSpec((1, D_MODEL), lambda i: (0, 0)),
            pl.BlockSpec((1, D_MODEL), lambda i: (0, 0)),
            pl.BlockSpec(memory_space=pl.ANY),
            pl.BlockSpec(memory_space=pl.ANY),
            pl.BlockSpec(memory_space=pl.ANY),
        ],
        out_specs=[pl.BlockSpec((tm, D_MODEL), lambda i: (i, 0)), pl.BlockSpec((ts, D_MODEL), lambda i: (i, 0))],
        out_shape=[jax.ShapeDtypeStruct((t, D_MODEL), F32), jax.ShapeDtypeStruct(xs.shape, F32)],
        scratch_shapes=[
            pltpu.VMEM((tm + ts, D_MODEL), BF16),
            pltpu.VMEM((2, D_MODEL, tf), F32),
            pltpu.VMEM((2, D_MODEL, tf), F32),
            pltpu.VMEM((2, tf, D_MODEL), F32),
            pltpu.SemaphoreType.DMA((3, 2)),
        ],
        compiler_params=_params(("arbitrary",)),
        name="ffn",
    )(x, xs, g, gf, w1, w3, w2)


KV_TILE = COL_K // IN_TILE
assert KV_TILE * IN_TILE == COL_K and COL_SMALL + LANES <= (KV_TILE + 1) * IN_TILE


def _in_proj_kernel(x_ref, g_ref, w_ref, o_ref, k_ref, v_ref, h_scr):
    j = pl.program_id(1)

    @pl.when(j == 0)
    def _():
        h_scr[...] = _rms(x_ref[...], g_ref[...]).astype(BF16)

    y = _dot(h_scr[...], w_ref[...])
    o_ref[...] = y

    @pl.when(j == KV_TILE)
    def _():
        tm = x_ref.shape[0]
        for dst, col in ((k_ref, COL_K), (v_ref, COL_V)):
            for g in range(N_KV_HEADS):
                c0 = col - COL_K + g * HEAD_DIM
                dst[pl.ds(g, tm, stride=N_KV_HEADS), :] = y[:, c0:c0 + HEAD_DIM]


def _in_proj(x, g, w, *, tm):
    t = x.shape[0]
    kv_spec = pl.BlockSpec((tm * N_KV_HEADS, HEAD_DIM), lambda i, j: (i, 0))
    kv_shape = jax.ShapeDtypeStruct((t * N_KV_HEADS, HEAD_DIM), F32)
    return pl.pallas_call(
        _in_proj_kernel,
        grid=(t // tm, N_IN // IN_TILE),
        in_specs=[
            pl.BlockSpec((tm, D_MODEL), lambda i, j: (i, 0)),
            pl.BlockSpec((1, D_MODEL), lambda i, j: (0, 0)),
            pl.BlockSpec((D_MODEL, IN_TILE), lambda i, j: (0, j)),
        ],
        out_specs=[pl.BlockSpec((tm, IN_TILE), lambda i, j: (i, j)), kv_spec, kv_spec],
        out_shape=[jax.ShapeDtypeStruct((t, N_IN), F32), kv_shape, kv_shape],
        scratch_shapes=[pltpu.VMEM((tm, D_MODEL), BF16)],
        compiler_params=_params(("parallel", "arbitrary")),
        name="in_proj",
    )(x, g, w)


def _out_proj_kernel(x_ref, a_ref, s_ref, wa_ref, ws_ref, o_ref):
    acc = _dot(a_ref[...].astype(BF16), wa_ref[...])
    acc += _dot(s_ref[...].astype(BF16), ws_ref[...])
    o_ref[...] = x_ref[...] + acc


def _out_proj(x, a, s, wa, ws, *, tm):
    t = x.shape[0]
    return pl.pallas_call(
        _out_proj_kernel,
        grid=(t // tm,),
        in_specs=[
            pl.BlockSpec((tm, D_MODEL), lambda i: (i, 0)),
            pl.BlockSpec((tm, D_ATTN), lambda i: (i, 0)),
            pl.BlockSpec((tm, D_SSD), lambda i: (i, 0)),
            pl.BlockSpec((D_ATTN, D_MODEL), lambda i: (0, 0)),
            pl.BlockSpec((D_SSD, D_MODEL), lambda i: (0, 0)),
        ],
        out_specs=pl.BlockSpec((tm, D_MODEL), lambda i: (i, 0)),
        out_shape=jax.ShapeDtypeStruct((t, D_MODEL), F32),
        compiler_params=_params(("parallel",)),
        name="out_proj",
    )(x, a, s, wa, ws)


NEAR = 2 * LANES


def _bias_kernel(rb_ref, o_ref):
    r = lax.broadcasted_iota(I32, (LANES, NEAR), 0)
    c = lax.broadcasted_iota(I32, (LANES, NEAR), 1)
    n = jnp.maximum(r + LANES - c, 0)
    max_exact = N_BUCKETS // 2
    nf = jnp.maximum(n, 1).astype(F32)
    large = max_exact + (jnp.log(nf / max_exact) / math.log(MAX_DISTANCE / max_exact)
                         * (N_BUCKETS - max_exact)).astype(I32)
    bucket = jnp.where(n < max_exact, n, jnp.minimum(large, N_BUCKETS - 1))
    for h in range(N_HEADS):
        acc = jnp.zeros((LANES, NEAR), F32)
        for b in range(N_BUCKETS):
            acc = jnp.where(bucket == b, rb_ref[b, h], acc)
        o_ref[h] = acc


def _bias_table(rel_bias):
    return pl.pallas_call(
        _bias_kernel,
        in_specs=[pl.BlockSpec(memory_space=pltpu.SMEM)],
        out_specs=pl.BlockSpec(memory_space=pltpu.VMEM),
        out_shape=jax.ShapeDtypeStruct((N_HEADS, LANES, NEAR), F32),
        name="bias_table",
    )(rel_bias)


def _count(mask, axis=1):
    x = mask.astype(F32)
    if axis == 0:
        x = jnp.sum(x.reshape(x.shape[0] // 64, 64, x.shape[1]), axis=0)
    return jnp.sum(x, axis=axis, keepdims=True)


def _kth_largest(count_ge, shape):
    t = jnp.full(shape, INT_MIN, I32)
    t = jnp.where(count_ge(jnp.zeros(shape, I32)) >= TOPK, 0, t)

    def body(it, t):
        cand = t + lax.shift_left(jnp.int32(1), 30 - it)
        return jnp.where(count_ge(cand) >= TOPK, cand, t)

    return lax.fori_loop(0, 31, body, t)


def _tie_limit(count_eq_below, need, shape, idx_bits):
    def body(it, m):
        cand = m + lax.shift_left(jnp.int32(1), idx_bits - 1 - it)
        return jnp.where(count_eq_below(cand) < need, cand, m)

    return lax.fori_loop(0, idx_bits, body, jnp.zeros(shape, I32))


CAUSAL_VARIANTS = 4


def _dsa_prompt_block(width, near_tiles, i, q_ref, qi_ref, k_ref, v_ref, sm_ref, bt_ref, o_ref, sk_scr, neg_scr,
                      kb_scr, vt_scr):
    tq = LANES
    q0 = pl.multiple_of(i * LANES, LANES)
    shape = (1, tq)

    wi_t = sm_ref[pl.ds(q0, tq), :].T[SM_WI:SM_WI + IDX_HEADS, :] * (IDX_HEADS * IDX_DIM) ** -0.5
    head_group = 4
    qi_groups = [
        jnp.concatenate([qi_ref[:, h * IDX_DIM:(h + 1) * IDX_DIM] for h in range(hg, hg + head_group)],
                        axis=0).astype(BF16)
        for hg in range(0, IDX_HEADS, head_group)]
    chunk = 2 * LANES
    pos = lax.broadcasted_iota(I32, (chunk, tq), 1) + q0
    for c in range(0, width, chunk):
        ki = sm_ref[c:c + chunk, SM_KI:SM_KI + IDX_DIM].astype(BF16)
        score = jnp.zeros((chunk, tq), F32)
        for n, hg in enumerate(range(0, IDX_HEADS, head_group)):
            rel = _dot_nt(ki, qi_groups[n])
            for e in range(head_group):
                score = score + wi_t[hg + e:hg + e + 1, :] * jnp.maximum(rel[:, e * tq:(e + 1) * tq], 0.0)
        key_c = lax.broadcasted_iota(I32, (chunk, tq), 0) + c
        sk_scr[c:c + chunk, :] = jnp.where(key_c <= pos, _sort_key(score), INT_MIN)

    key = lax.broadcasted_iota(I32, (width, tq), 0)

    def sk():
        return sk_scr[0:width, :]

    thr = _kth_largest(lambda c: _count(sk() >= c, 0), shape)
    excess = (_count(sk() >= thr, 0) > TOPK) & (thr > INT_MIN)
    any_excess = jnp.max(excess.astype(F32)) > 0.0

    @pl.when(jnp.logical_not(any_excess))
    def _():
        neg_scr[0:width, :] = jnp.where(sk() >= jnp.maximum(thr, INT_MIN + 1), 0.0, -jnp.inf)

    @pl.when(any_excess)
    def _():
        need = TOPK - _count(sk() > thr, 0)
        lim = _tie_limit(lambda m: _count((sk() == thr) & (key < m), 0), need, shape, (width - 1).bit_length())
        take = (sk() > thr) | ((sk() == thr) & (key <= lim) & (sk() > INT_MIN))
        neg_scr[0:width, :] = jnp.where(take, 0.0, -jnp.inf)

    log2e = math.log2(math.e)
    scale = HEAD_DIM ** -0.5 * log2e
    tiles = width // LANES
    for g in range(N_KV_HEADS):
        heads = range(g * GQA_GROUP, (g + 1) * GQA_GROUP)
        kg = kb_scr[g, 0:width, :]
        vg_t = vt_scr[g, :, 0:width]
        qg = jnp.concatenate([q_ref[:, h * HEAD_DIM:(h + 1) * HEAD_DIM] for h in heads], axis=0).astype(BF16)
        diag_t = [((bt_ref[h, :, LANES:NEAR] - bt_ref[h, 0:1, 0:1]) * log2e).T for h in heads]
        prev_t = [((bt_ref[h, :, 0:LANES] - bt_ref[h, 0:1, 0:1]) * log2e).T for h in heads]
        qk = _dot_nt(kg, qg)
        logits = []
        for t in range(tiles):
            rows = slice(t * LANES, (t + 1) * LANES)
            per_head = []
            for e in range(GQA_GROUP):
                lt = qk[rows, e * tq:(e + 1) * tq] * scale + neg_scr[rows, :]
                if t >= tiles - near_tiles:
                    lt = lt + jnp.where(i == t, diag_t[e], jnp.where(i - 1 == t, prev_t[e], 0.0))
                per_head.append(lt)
            logits.append(per_head)
        p_t = []
        inv_l = []
        for e in range(GQA_GROUP):
            m = jnp.max(functools.reduce(jnp.maximum, [lt[e] for lt in logits]), axis=0, keepdims=True)
            probs = [jnp.exp2(lt[e] - m) for lt in logits]
            inv_l.append(1.0 / jnp.sum(functools.reduce(jnp.add, probs), axis=0, keepdims=True))
            p_t.append(jnp.concatenate([p.astype(BF16) for p in probs], axis=0))
        out_t = _dot(vg_t, jnp.concatenate(p_t, axis=1))
        for e, h in enumerate(heads):
            o_ref[:, h * HEAD_DIM:(h + 1) * HEAD_DIM] = (out_t[:, e * tq:(e + 1) * tq] * inv_l[e]).T


def _dsa_prompt_kernel(*refs, seq):
    i = pl.program_id(1)
    per = seq // LANES // CAUSAL_VARIANTS
    k_ref, v_ref = refs[2], refs[3]
    kb_scr, vt_scr = refs[-2], refs[-1]

    @pl.when(i == 0)
    def _():
        for g in range(N_KV_HEADS):
            cols = slice(g * HEAD_DIM, (g + 1) * HEAD_DIM)
            kb_scr[g] = k_ref[:, cols].astype(BF16)
            vt_scr[g] = v_ref[:, cols].T.astype(BF16)

    for v in range(CAUSAL_VARIANTS):
        @pl.when(i // per == v)
        def _(v=v):
            _dsa_prompt_block((v + 1) * per * LANES, per + 1, i, *refs)


def _dsa_prompt(proj, btab, *, batch, seq):
    nq = seq // LANES
    return pl.pallas_call(
        functools.partial(_dsa_prompt_kernel, seq=seq),
        grid=(batch, nq),
        in_specs=[
            pl.BlockSpec((LANES, D_ATTN), lambda b, i: (b * nq + i, COL_Q // D_ATTN)),
            pl.BlockSpec((LANES, IDX_HEADS * IDX_DIM), lambda b, i: (b * nq + i, COL_QI // (IDX_HEADS * IDX_DIM))),
            pl.BlockSpec((seq, D_KV), lambda b, i: (b, COL_K // D_KV)),
            pl.BlockSpec((seq, D_KV), lambda b, i: (b, COL_V // D_KV)),
            pl.BlockSpec((seq, LANES), lambda b, i: (b, COL_SMALL // LANES)),
            pl.BlockSpec((N_HEADS, LANES, NEAR), lambda b, i: (0, 0, 0)),
        ],
        out_specs=pl.BlockSpec((LANES, D_ATTN), lambda b, i: (b * nq + i, 0)),
        out_shape=jax.ShapeDtypeStruct((batch * seq, D_ATTN), F32),
        scratch_shapes=[pltpu.VMEM((seq, LANES), I32), pltpu.VMEM((seq, LANES), F32),
                        pltpu.VMEM((N_KV_HEADS, seq, HEAD_DIM), BF16), pltpu.VMEM((N_KV_HEADS, HEAD_DIM, seq), BF16)],
        compiler_params=_params(("arbitrary", "arbitrary")),
        name="dsa_prompt",
    )(proj, proj, proj, proj, proj, btab)


SCORE_PAGES = 64
ATTN_PAGES = 32


def _sample_queries(qi_ref):
    return jnp.concatenate(
        [qi_ref[:, h * IDX_DIM:(h + 1) * IDX_DIM] for h in range(IDX_HEADS)], axis=0).astype(BF16)


def _sample_scores(rel, wi, nq):
    sc = jnp.zeros((nq, rel.shape[1]), F32)
    for h in range(IDX_HEADS):
        sc = sc + wi[:, h:h + 1] * jnp.maximum(rel[h * nq:(h + 1) * nq, :], 0.0)
    return sc


def _page_copies(pt_ref, pools, bufs, sems, b, s, slot, pages, page_rows):
    copies = []
    for r in range(pages):
        row0 = pl.multiple_of(pt_ref[b, s * pages + r] * page_rows, page_rows)
        for pool, buf, sem in zip(pools, bufs, sems):
            copies.append(pltpu.make_async_copy(pool.at[pl.ds(row0, page_rows), :], buf.at[slot, r], sem.at[slot]))
    return copies


def _page_pipeline(pt_ref, pools, bufs, sems, pages, page_rows):
    b, s = pl.program_id(0), pl.program_id(1)
    steps = pl.num_programs(1)
    t = b * steps + s
    slot = lax.rem(t, 2)
    copies = functools.partial(_page_copies, pt_ref, pools, bufs, sems, pages=pages, page_rows=page_rows)

    @pl.when(t == 0)
    def _():
        for c in copies(b, s, slot):
            c.start()

    @pl.when(t + 1 < pl.num_programs(0) * steps)
    def _():
        wrap = s + 1 == steps
        for c in copies(jnp.where(wrap, b + 1, b), jnp.where(wrap, 0, s + 1), 1 - slot):
            c.start()

    for c in copies(b, s, slot):
        c.wait()
    return slot


def _sample_score_kernel(pt_ref, qi_ref, sm_ref, pool_ref, past_ref, new_ref, pages, sem, kbuf, *, nq):
    s = pl.program_id(1)
    slot = _page_pipeline(pt_ref, [pool_ref], [pages], [sem], SCORE_PAGES, IDX_DIM)
    qx = _sample_queries(qi_ref)
    wi = sm_ref[:, SM_WI:SM_WI + IDX_HEADS] * (IDX_HEADS * IDX_DIM) ** -0.5
    for r in range(SCORE_PAGES):
        kbuf[:, r * PAGE_SIZE:(r + 1) * PAGE_SIZE] = pages[slot, r].astype(BF16)
    past_ref[...] = _sort_key(_sample_scores(_dot(qx, kbuf[...]), wi, nq))

    @pl.when(s == pl.num_programs(1) - 1)
    def _():
        ki_new = jnp.concatenate(
            [sm_ref[:, SM_KI:SM_KI + IDX_DIM], jnp.zeros((LANES - nq, IDX_DIM), F32)], axis=0).astype(BF16)
        sc = _sample_scores(_dot_nt(qx, ki_new), wi, nq)
        j = lax.broadcasted_iota(I32, (nq, LANES), 1)
        t = lax.broadcasted_iota(I32, (nq, LANES), 0)
        new_ref[...] = jnp.where(j <= t, _sort_key(sc), INT_MIN)


def _sample_score(page_table, proj_s, pool_kidx_t, *, batch, nq, n_pages):
    steps = n_pages // SCORE_PAGES
    grid_spec = pltpu.PrefetchScalarGridSpec(
        num_scalar_prefetch=1,
        grid=(batch, steps),
        in_specs=[
            pl.BlockSpec((nq, IDX_HEADS * IDX_DIM), lambda b, s, pt: (b, COL_QI // (IDX_HEADS * IDX_DIM))),
            pl.BlockSpec((nq, LANES), lambda b, s, pt: (b, COL_SMALL // LANES)),
            pl.BlockSpec(memory_space=pl.ANY),
        ],
        out_specs=[
            pl.BlockSpec((None, nq, SCORE_PAGES * PAGE_SIZE), lambda b, s, pt: (b, 0, s)),
            pl.BlockSpec((None, nq, LANES), lambda b, s, pt: (b, 0, 0)),
        ],
        scratch_shapes=[
            pltpu.VMEM((2, SCORE_PAGES, IDX_DIM, PAGE_SIZE), F32),
            pltpu.SemaphoreType.DMA((2,)),
            pltpu.VMEM((IDX_DIM, SCORE_PAGES * PAGE_SIZE), BF16),
        ],
    )
    return pl.pallas_call(
        functools.partial(_sample_score_kernel, nq=nq),
        grid_spec=grid_spec,
        out_shape=[
            jax.ShapeDtypeStruct((batch, nq, n_pages * PAGE_SIZE), I32),
            jax.ShapeDtypeStruct((batch, nq, LANES), I32),
        ],
        compiler_params=_params(("arbitrary", "arbitrary")),
        name="sample_score",
    )(page_table, proj_s, proj_s, pool_kidx_t)


def _sample_select_kernel(past_ref, new_ref, thr_ref, lim_ref, *, past):
    rows = past_ref.shape[0]
    colp = lax.broadcasted_iota(I32, (rows, past), 1)
    coln = lax.broadcasted_iota(I32, (rows, LANES), 1) + past

    def count_ge(c):
        return _count(past_ref[...] >= c) + _count(new_ref[...] >= c)

    thr = _kth_largest(count_ge, (rows, 1))
    thr_ref[...] = jnp.broadcast_to(thr, (rows, LANES))
    lim_ref[...] = jnp.full((rows, LANES), 2 ** 31 - 1, I32)
    any_excess = jnp.max((count_ge(thr) > TOPK).astype(F32)) > 0.0

    @pl.when(any_excess)
    def _():
        need = TOPK - (_count(past_ref[...] > thr) + _count(new_ref[...] > thr))

        def count_eq_below(m):
            return (_count((past_ref[...] == thr) & (colp < m))
                    + _count((new_ref[...] == thr) & (coln < m)))

        lim = _tie_limit(count_eq_below, need, (rows, 1), (past + LANES - 1).bit_length())
        lim_ref[...] = jnp.broadcast_to(lim, (rows, LANES))


def _sample_select(sk_past, sk_new, *, rows_per_step):
    rows, past = sk_past.shape
    return pl.pallas_call(
        functools.partial(_sample_select_kernel, past=past),
        grid=(rows // rows_per_step,),
        in_specs=[
            pl.BlockSpec((rows_per_step, past), lambda i: (i, 0)),
            pl.BlockSpec((rows_per_step, LANES), lambda i: (i, 0)),
        ],
        out_specs=[
            pl.BlockSpec((rows_per_step, LANES), lambda i: (i, 0)),
            pl.BlockSpec((rows_per_step, LANES), lambda i: (i, 0)),
        ],
        out_shape=[jax.ShapeDtypeStruct((rows, LANES), I32)] * 2,
        compiler_params=_params(("parallel",)),
        name="sample_select",
    )(sk_past, sk_new)


def _sample_attn_kernel(pt_ref, q_ref, kn_ref, vn_ref, skp_ref, skn_ref, thr_ref, lim_ref, bt_ref, poolk_ref,
                        poolv_ref, o_ref, kpages, vpages, ksem, vsem, kbuf, vbuf, m_scr, l_scr, acc_scr, *,
                        nq, past):
    slot = _page_pipeline(pt_ref, [poolk_ref, poolv_ref], [kpages, vpages], [ksem, vsem], ATTN_PAGES,
                          PAGE_SIZE * N_KV_HEADS)
    s = pl.program_id(1)
    last = pl.num_programs(1) - 1
    width = ATTN_PAGES * PAGE_SIZE
    grows = GQA_GROUP * nq
    scale = HEAD_DIM ** -0.5

    @pl.when(s == 0)
    def _():
        m_scr[...] = jnp.full_like(m_scr, NEG_BIG)
        l_scr[...] = jnp.zeros_like(l_scr)
        acc_scr[...] = jnp.zeros_like(acc_scr)

    thr = thr_ref[:, 0:1]
    lim = lim_ref[:, 0:1]

    def select(sk, idx):
        sel = (sk > thr) | ((sk == thr) & (idx <= lim))
        return jnp.concatenate([sel] * GQA_GROUP, axis=0)

    def far_bias(g):
        return jnp.concatenate(
            [jnp.broadcast_to(bt_ref[g * GQA_GROUP + hh, 0:1, 0:1], (nq, 1)) for hh in range(GQA_GROUP)], axis=0)

    def near_bias(g, lo, hi):
        return jnp.concatenate([bt_ref[g * GQA_GROUP + hh, 0:nq, lo:hi] for hh in range(GQA_GROUP)], axis=0)

    def group_queries(g):
        return jnp.concatenate(
            [q_ref[:, (g * GQA_GROUP + hh) * HEAD_DIM:(g * GQA_GROUP + hh + 1) * HEAD_DIM]
             for hh in range(GQA_GROUP)], axis=0).astype(BF16)

    def accumulate(g, lg, sel, vals):
        rs = slice(g * grows, (g + 1) * grows)
        m_old = m_scr[rs, :]
        m_new = jnp.maximum(m_old, jnp.max(jnp.where(sel, lg, NEG_BIG), axis=1, keepdims=True))
        p = jnp.where(sel, jnp.exp(lg - m_new), 0.0)
        alpha = jnp.exp(m_old - m_new)
        l_scr[rs, :] = alpha * l_scr[rs, :] + jnp.sum(p, axis=1, keepdims=True)
        acc_scr[rs, :] = alpha * acc_scr[rs, :] + _dot(p.astype(BF16), vals)
        m_scr[rs, :] = m_new

    for r in range(ATTN_PAGES):
        for g in range(N_KV_HEADS):
            rows = pl.ds(g, PAGE_SIZE, stride=N_KV_HEADS)
            kbuf[g, r * PAGE_SIZE:(r + 1) * PAGE_SIZE, :] = kpages[slot, r, rows, :].astype(BF16)
            vbuf[g, r * PAGE_SIZE:(r + 1) * PAGE_SIZE, :] = vpages[slot, r, rows, :].astype(BF16)

    col = lax.broadcasted_iota(I32, (nq, width), 1)
    sel = select(skp_ref[...], col + s * width)
    colg = lax.broadcasted_iota(I32, (grows, width), 1)
    in_near = (s == last) & (colg >= width - PAGE_SIZE)
    for g in range(N_KV_HEADS):
        near = jnp.tile(near_bias(g, 0, LANES), (1, ATTN_PAGES))
        bias = jnp.where(in_near, near, far_bias(g))
        lg = _dot_nt(group_queries(g), kbuf[g]) * scale + bias
        accumulate(g, lg, sel, vbuf[g])

    @pl.when(s == last)
    def _():
        pad = jnp.zeros((LANES - nq, D_KV), F32)
        kn = jnp.concatenate([kn_ref[...], pad], axis=0).astype(BF16)
        vn = jnp.concatenate([vn_ref[...], pad], axis=0).astype(BF16)
        coln = lax.broadcasted_iota(I32, (nq, LANES), 1) + past
        seln = select(skn_ref[...], coln)
        for g in range(N_KV_HEADS):
            cs = slice(g * HEAD_DIM, (g + 1) * HEAD_DIM)
            lg = _dot_nt(group_queries(g), kn[:, cs]) * scale + near_bias(g, LANES, NEAR)
            accumulate(g, lg, seln, vn[:, cs])
        out = acc_scr[...] / l_scr[...]
        for h in range(N_HEADS):
            o_ref[:, h * HEAD_DIM:(h + 1) * HEAD_DIM] = out[h * nq:(h + 1) * nq, :]


def _sample_attn(page_table, proj_s, sk_past, sk_new, thr, lim, btab, pool_k, pool_v, *, batch, nq, n_pages):
    steps = n_pages // ATTN_PAGES
    width = ATTN_PAGES * PAGE_SIZE
    past = n_pages * PAGE_SIZE
    page_buf = pltpu.VMEM((2, ATTN_PAGES, PAGE_SIZE * N_KV_HEADS, HEAD_DIM), F32)
    grid_spec = pltpu.PrefetchScalarGridSpec(
        num_scalar_prefetch=1,
        grid=(batch, steps),
        in_specs=[
            pl.BlockSpec((nq, D_ATTN), lambda b, s, pt: (b, COL_Q // D_ATTN)),
            pl.BlockSpec((nq, D_KV), lambda b, s, pt: (b, COL_K // D_KV)),
            pl.BlockSpec((nq, D_KV), lambda b, s, pt: (b, COL_V // D_KV)),
            pl.BlockSpec((None, nq, width), lambda b, s, pt: (b, 0, s)),
            pl.BlockSpec((None, nq, LANES), lambda b, s, pt: (b, 0, 0)),
            pl.BlockSpec((nq, LANES), lambda b, s, pt: (b, 0)),
            pl.BlockSpec((nq, LANES), lambda b, s, pt: (b, 0)),
            pl.BlockSpec((N_HEADS, LANES, NEAR), lambda b, s, pt: (0, 0, 0)),
            pl.BlockSpec(memory_space=pl.ANY),
            pl.BlockSpec(memory_space=pl.ANY),
        ],
        out_specs=pl.BlockSpec((nq, D_ATTN), lambda b, s, pt: (b, 0)),
        scratch_shapes=[
            page_buf, page_buf, pltpu.SemaphoreType.DMA((2,)), pltpu.SemaphoreType.DMA((2,)),
            pltpu.VMEM((N_KV_HEADS, width, HEAD_DIM), BF16),
            pltpu.VMEM((N_KV_HEADS, width, HEAD_DIM), BF16),
            pltpu.VMEM((N_HEADS * nq, 1), F32),
            pltpu.VMEM((N_HEADS * nq, 1), F32),
            pltpu.VMEM((N_HEADS * nq, HEAD_DIM), F32),
        ],
    )
    return pl.pallas_call(
        functools.partial(_sample_attn_kernel, nq=nq, past=past),
        grid_spec=grid_spec,
        out_shape=jax.ShapeDtypeStruct((batch * nq, D_ATTN), F32),
        compiler_params=_params(("arbitrary", "arbitrary")),
        name="sample_attn",
    )(page_table, proj_s, proj_s, proj_s, sk_past, sk_new, thr, lim, btab, pool_k, pool_v)


TAIL = 8
SHORT_ROWS = 16


def _ssd_kernel(xbc_ref, z_ref, sm_ref, cprev_ref, hprev_ref, cw_ref, cb_ref, alog_ref, dtb_ref, dvec_ref,
                g_ref, y_ref, cnew_ref, hnew_ref, xpad_scr, sm_scr, h_scr, yd_scr, *, rows):
    c = pl.program_id(1)
    q = CHUNK
    r = q if rows == q else SHORT_ROWS
    assert rows <= r
    keep = CONV_WIDTH - 1
    heads_per_group = SSD_HEADS // SSD_GROUPS

    def pad_rows(x):
        return x if r == q else jnp.concatenate([x, jnp.zeros((q - r, x.shape[1]), x.dtype)], axis=0)

    @pl.when(c == 0)
    def _():
        xpad_scr[...] = jnp.zeros_like(xpad_scr)
        sm_scr[...] = jnp.zeros_like(sm_scr)
        xpad_scr[TAIL - keep:TAIL, :] = cprev_ref[0]
        h_scr[...] = hprev_ref[0]

    xpad_scr[TAIL:TAIL + rows, :] = xbc_ref[...]
    sm_scr[0:rows, :] = sm_ref[...]
    conv = sum(xpad_scr[TAIL - keep + k:TAIL - keep + k + r, :] * cw_ref[k:k + 1, :]
               for k in range(CONV_WIDTH)) + cb_ref[...]
    new_tail = xpad_scr[rows + TAIL - keep:rows + TAIL, :]
    cnew_ref[0] = new_tail
    xpad_scr[TAIL - keep:TAIL, :] = new_tail

    xc = conv * jax.nn.sigmoid(conv)
    xs = xc[:, :D_SSD]
    bm = pad_rows(xc[:, D_SSD:D_SSD + SSD_GROUPS * D_STATE]).astype(BF16)
    cm = xc[:, D_SSD + SSD_GROUPS * D_STATE:].astype(BF16)

    ri = lax.broadcasted_iota(I32, (q, q), 0)
    ci = lax.broadcasted_iota(I32, (q, q), 1)
    causal = ri >= ci
    causal_r = causal[0:r, :]
    x = sm_scr[:, SM_DT:SM_DT + SSD_HEADS] + dtb_ref[...]
    dt = jnp.maximum(x, 0.0) + jnp.log1p(jnp.exp(-jnp.abs(x)))
    dt = jnp.where(lax.broadcasted_iota(I32, (q, SSD_HEADS), 0) < rows, dt, 0.0)
    a = -jnp.exp(alog_ref[...])
    acum = _select_dot(causal.astype(BF16), dt * a)
    acum_t = jnp.concatenate([acum, jnp.zeros((q, LANES - SSD_HEADS), F32)], axis=1).T

    expand = (lax.broadcasted_iota(I32, (SSD_HEADS, D_SSD), 1) // SSD_HEAD_DIM
              == lax.broadcasted_iota(I32, (SSD_HEADS, D_SSD), 0)).astype(BF16)
    expand_t = (lax.broadcasted_iota(I32, (D_SSD, SSD_HEADS), 0) // SSD_HEAD_DIM
                == lax.broadcasted_iota(I32, (D_SSD, SSD_HEADS), 1)).astype(BF16)
    acum_x = _select_dot(acum[0:r, :], expand)
    xd = xs * _select_dot(dt[0:r, :], expand)
    xdw_t = pad_rows(xd * jnp.exp(acum_x[r - 1:r, :] - acum_x)).T.astype(BF16)
    xd = pad_rows(xd).astype(BF16)
    state_decay = jnp.exp(_select_dot(expand_t, jnp.broadcast_to(acum_t[0:SSD_HEADS, q - 1:q], (SSD_HEADS, LANES))))

    for g in range(SSD_GROUPS):
        ns = slice(g * D_STATE, (g + 1) * D_STATE)
        hs = slice(g * heads_per_group * SSD_HEAD_DIM, (g + 1) * heads_per_group * SSD_HEAD_DIM)
        cb = _dot_nt(cm[:, ns], bm[:, ns])
        for e in range(heads_per_group):
            h = g * heads_per_group + e
            ps = slice(h * SSD_HEAD_DIM, (h + 1) * SSD_HEAD_DIM)
            seg = acum[0:r, h:h + 1] - acum_t[h:h + 1, :]
            lmat = jnp.exp(jnp.where(causal_r, seg, -jnp.inf))
            yd_scr[0:r, ps] = _dot((cb * lmat).astype(BF16), xd[:, ps])
        h_old = h_scr[hs, :]
        yd_scr[0:r, hs] += _dot_nt(cm[:, ns], h_old.astype(BF16)) * jnp.exp(acum_x[:, hs])
        h_scr[hs, :] = h_old * state_decay[hs, :] + _dot(xdw_t[hs, :], bm[:, ns])

    hnew_ref[0] = h_scr[...]
    y = yd_scr[0:rows, :] + dvec_ref[...] * xs[0:rows, :]
    zz = z_ref[...]
    y = y * (zz * jax.nn.sigmoid(zz))
    width = D_SSD // SSD_GROUPS
    for g in range(SSD_GROUPS):
        cs = slice(g * width, (g + 1) * width)
        y_ref[:, cs] = _rms(y[:, cs], g_ref[:, cs])


def _ssd(proj, conv_prev, ssm_prev, conv_w, conv_b, a_log, dt_bias, d_vec, g_ssd, *, batch, seq):
    rows = min(CHUNK, seq)
    nc = seq // rows
    full = lambda *shape: pl.BlockSpec(shape, lambda b, c: (0,) * len(shape))
    return pl.pallas_call(
        functools.partial(_ssd_kernel, rows=rows),
        grid=(batch, nc),
        in_specs=[
            pl.BlockSpec((rows, CONV_DIM), lambda b, c: (b * nc + c, COL_XBC // CONV_DIM)),
            pl.BlockSpec((rows, D_SSD), lambda b, c: (b * nc + c, COL_Z // D_SSD)),
            pl.BlockSpec((rows, LANES), lambda b, c: (b * nc + c, COL_SMALL // LANES)),
            pl.BlockSpec((1, CONV_WIDTH - 1, CONV_DIM), lambda b, c: (b, 0, 0)),
            pl.BlockSpec((1, D_SSD, D_STATE), lambda b, c: (b, 0, 0)),
            full(CONV_WIDTH, CONV_DIM), full(1, CONV_DIM), full(1, SSD_HEADS), full(1, SSD_HEADS),
            full(1, D_SSD), full(1, D_SSD),
        ],
        out_specs=[
            pl.BlockSpec((rows, D_SSD), lambda b, c: (b * nc + c, 0)),
            pl.BlockSpec((1, CONV_WIDTH - 1, CONV_DIM), lambda b, c: (b, 0, 0)),
            pl.BlockSpec((1, D_SSD, D_STATE), lambda b, c: (b, 0, 0)),
        ],
        out_shape=[
            jax.ShapeDtypeStruct((batch * seq, D_SSD), F32),
            jax.ShapeDtypeStruct((batch, CONV_WIDTH - 1, CONV_DIM), F32),
            jax.ShapeDtypeStruct((batch, D_SSD, D_STATE), F32),
        ],
        scratch_shapes=[
            pltpu.VMEM((CHUNK + TAIL, CONV_DIM), F32),
            pltpu.VMEM((CHUNK, LANES), F32),
            pltpu.VMEM((D_SSD, D_STATE), F32),
            pltpu.VMEM((CHUNK, D_SSD), F32),
        ],
        compiler_params=_params(("parallel", "arbitrary")),
        name="ssd",
    )(proj, proj, proj, conv_prev, ssm_prev, conv_w, conv_b, a_log, dt_bias, d_vec, g_ssd)


def _reorder_w_in(w_in):
    offs = [0]
    for s in IN_SPLITS:
        offs.append(offs[-1] + s)
    q, k, v, qi, ki, wi, z, xbc, dtr = [w_in[:, offs[n]:offs[n + 1]] for n in range(len(IN_SPLITS))]
    parts = [t.astype(BF16) for t in (q, qi, z, xbc, k, v, ki, wi, dtr)]
    parts.append(jnp.zeros((w_in.shape[0], N_IN - sum(IN_SPLITS)), BF16))
    return jnp.concatenate(parts, axis=1)


def kernel(x_prompt, x_sample, cache_k, cache_v, cache_kidx, state_conv, state_ssm, page_table, rel_bias, g_ffn1, w1_ffn1, w3_ffn1, w2_ffn1, g_mix, w_in, conv_w, conv_b, a_log, dt_bias, d_skip, g_ssd, w_out, g_ffn2, w1_ffn2, w3_ffn2, w2_ffn2, g_final):
    depth = w_in.shape[0]
    assert depth == 1
    l = 0
    bp, seq, _ = x_prompt.shape
    bs, nq, _ = x_sample.shape
    n_pages = page_table.shape[1]
    row = lambda t: t.reshape(1, -1)

    w1a, w3a, w2a = w1_ffn1[l], w3_ffn1[l], w2_ffn1[l]
    w1b, w3b, w2b = w1_ffn2[l], w3_ffn2[l], w2_ffn2[l]
    w_in_r = _reorder_w_in(w_in[l])
    w_out_a = w_out[l, :D_ATTN].astype(BF16)
    w_out_s = w_out[l, D_ATTN:].astype(BF16)
    d_vec = jnp.repeat(d_skip[l], SSD_HEAD_DIM).reshape(1, D_SSD)
    btab = _bias_table(rel_bias)
    gf = row(g_final)

    tm, tf = 1024, 256

    def mix(x, attend, conv_prev, ssm_prev, batch, length):
        proj, k, v = _in_proj(x, row(g_mix[l]), w_in_r, tm=min(x.shape[0], tm))
        a_out = attend(proj)
        s_out, conv_new, ssm_new = _ssd(
            proj, conv_prev, ssm_prev.reshape(batch, D_SSD, D_STATE), conv_w[l], row(conv_b[l]), row(a_log[l]),
            row(dt_bias[l]), d_vec, row(g_ssd[l]), batch=batch, seq=length)
        x = _out_proj(x, a_out, s_out, w_out_a, w_out_s, tm=min(x.shape[0], 512))
        k = k.reshape(1, batch, length, N_KV_HEADS, HEAD_DIM)
        v = v.reshape(1, batch, length, N_KV_HEADS, HEAD_DIM)
        ki = proj[:, COL_SMALL + SM_KI:COL_SMALL + SM_KI + IDX_DIM].reshape(1, batch, length, IDX_DIM)
        ssm_new = ssm_new.reshape(1, batch, SSD_HEADS, SSD_HEAD_DIM, D_STATE)
        return x, k, v, ki, conv_new[None], ssm_new

    xp, xs = _ffn(x_prompt.reshape(bp * seq, D_MODEL), x_sample.reshape(bs * nq, D_MODEL), row(g_ffn1[l]),
                  w1a, w3a, w2a, gf, tm=tm, tf=tf, final_norm=False)

    xp, k_p, v_p, ki_p, conv_p, ssm_p = mix(
        xp, functools.partial(_dsa_prompt, btab=btab, batch=bp, seq=seq),
        jnp.zeros((bp, CONV_WIDTH - 1, CONV_DIM), F32), jnp.zeros((bp, SSD_HEADS, SSD_HEAD_DIM, D_STATE), F32),
        bp, seq)

    pool_k = cache_k[l].reshape(-1, HEAD_DIM)
    pool_v = cache_v[l].reshape(-1, HEAD_DIM)
    pool_kidx = jnp.swapaxes(cache_kidx[l], 1, 2).reshape(-1, PAGE_SIZE)

    def sample_attend(proj):
        sk_past, sk_new = _sample_score(page_table, proj, pool_kidx, batch=bs, nq=nq, n_pages=n_pages)
        thr, lim = _sample_select(sk_past.reshape(bs * nq, -1), sk_new.reshape(bs * nq, LANES),
                                  rows_per_step=128)
        return _sample_attn(page_table, proj, sk_past, sk_new, thr, lim, btab, pool_k, pool_v,
                            batch=bs, nq=nq, n_pages=n_pages)

    xs, k_s, v_s, ki_s, conv_s, ssm_s = mix(xs, sample_attend, state_conv[l], state_ssm[l], bs, nq)

    y_p, y_s = _ffn(xp, xs, row(g_ffn2[l]), w1b, w3b, w2b, gf, tm=tm, tf=tf, final_norm=True)
    y_p = y_p.reshape(bp, seq, D_MODEL)
    y_s = y_s.reshape(bs, nq, D_MODEL)
    return (y_p, y_s, k_p, v_p, ki_p, conv_p, ssm_p, k_s, v_s, ki_s, conv_s, ssm_s)
```

```python
import functools
import math

import jax
import jax.numpy as jnp
from jax import lax
from jax.experimental import pallas as pl
from jax.experimental.pallas import tpu as pltpu

F32 = jnp.float32
BF16 = jnp.bfloat16
I32 = jnp.int32

D_MODEL = 2048
PAGE_SIZE = 128
N_HEADS = 8
HEAD_DIM = 128
N_KV_HEADS = 2
GQA_GROUP = N_HEADS // N_KV_HEADS
IDX_HEADS = 16
IDX_DIM = 64
TOPK = 256
N_BUCKETS = 32
MAX_DISTANCE = 128
SSD_HEADS = 16
SSD_HEAD_DIM = 64
SSD_GROUPS = 2
D_STATE = 128
CONV_WIDTH = 4
CHUNK = 128
D_ATTN = N_HEADS * HEAD_DIM
D_SSD = SSD_HEADS * SSD_HEAD_DIM
D_KV = N_KV_HEADS * HEAD_DIM
CONV_DIM = D_SSD + 2 * SSD_GROUPS * D_STATE
D_FF = 5632
EPS = 1e-6
IN_SPLITS = (D_ATTN, D_KV, D_KV, IDX_HEADS * IDX_DIM, IDX_DIM, IDX_HEADS, D_SSD, CONV_DIM, SSD_HEADS)

LANES = 128
COL_Q = 0
COL_QI = COL_Q + D_ATTN
COL_Z = COL_QI + IDX_HEADS * IDX_DIM
COL_XBC = COL_Z + D_SSD
COL_K = COL_XBC + CONV_DIM
COL_V = COL_K + D_KV
COL_SMALL = COL_V + D_KV
SM_KI = 0
SM_WI = SM_KI + IDX_DIM
SM_DT = SM_WI + IDX_HEADS
IN_TILE = 768
N_IN = ((COL_SMALL + LANES + IN_TILE - 1) // IN_TILE) * IN_TILE

INT_MIN = -(2 ** 31)
NEG_BIG = -1e30
VMEM_LIMIT = 60000 * 1024


def _params(sem):
    return pltpu.CompilerParams(dimension_semantics=sem, vmem_limit_bytes=VMEM_LIMIT)


def _rms(x, g):
    return x * lax.rsqrt(jnp.mean(x * x, axis=-1, keepdims=True) + EPS) * g


def _dot(a, b):
    return jnp.dot(a, b, preferred_element_type=F32)


def _dot_nt(a, b):
    return lax.dot_general(a, b, (((1,), (1,)), ((), ())), preferred_element_type=F32)


def _split3(x):
    hi = x.astype(BF16)
    rest = x - hi.astype(F32)
    mid = rest.astype(BF16)
    return hi, mid, (rest - mid.astype(F32)).astype(BF16)


def _select_dot(a, b):
    if a.dtype == BF16:
        return sum(_dot(a, part) for part in _split3(b))
    return sum(_dot(part, b) for part in _split3(a))


def _sort_key(x):
    bits = lax.bitcast_convert_type(x + 0.0, I32)
    return bits ^ ((bits >> 31) & 0x7FFFFFFF)


def _ffn_kernel(x_ref, xs_ref, g_ref, gf_ref, w1_hbm, w3_hbm, w2_hbm, o_ref, os_ref, h_scr, w1_buf, w3_buf,
                w2_buf, sem, *, tf, final_norm):
    i = pl.program_id(0)
    nf = D_FF // tf
    total = pl.num_programs(0) * nf
    tm = x_ref.shape[0]

    def copies(f, slot):
        c0 = pl.multiple_of(f * tf, tf)
        return (pltpu.make_async_copy(w1_hbm.at[:, pl.ds(c0, tf)], w1_buf.at[slot], sem.at[0, slot]),
                pltpu.make_async_copy(w3_hbm.at[:, pl.ds(c0, tf)], w3_buf.at[slot], sem.at[1, slot]),
                pltpu.make_async_copy(w2_hbm.at[pl.ds(c0, tf), :], w2_buf.at[slot], sem.at[2, slot]))

    @pl.when(i == 0)
    def _():
        for c in copies(0, 0):
            c.start()

    h_scr[0:tm, :] = _rms(x_ref[...], g_ref[...]).astype(BF16)
    h_scr[tm:, :] = _rms(xs_ref[...], g_ref[...]).astype(BF16)
    o_ref[...] = jnp.zeros_like(o_ref)
    os_ref[...] = jnp.zeros_like(os_ref)

    def body(f, carry):
        t = i * nf + f
        slot = lax.rem(t, 2)

        @pl.when(t + 1 < total)
        def _():
            for c in copies(jnp.where(f + 1 == nf, 0, f + 1), 1 - slot):
                c.start()

        for c in copies(f, slot):
            c.wait()
        h = h_scr[...]
        a = _dot(h, w1_buf[slot].astype(BF16))
        b = _dot(h, w3_buf[slot].astype(BF16))
        u = (a * jax.nn.sigmoid(a) * b).astype(BF16)
        y = _dot(u, w2_buf[slot].astype(BF16))
        o_ref[...] += y[0:tm, :]
        os_ref[...] += y[tm:, :]
        return carry

    lax.fori_loop(0, nf, body, 0)
    for src, dst in ((x_ref, o_ref), (xs_ref, os_ref)):
        y = src[...] + 0.5 * dst[...]
        if final_norm:
            y = _rms(y, gf_ref[...])
        dst[...] = y


def _ffn(x, xs, g, w1, w3, w2, gf, *, tm, tf, final_norm):
    t = x.shape[0]
    ts = xs.shape[0] // (t // tm)
    assert ts * (t // tm) == xs.shape[0] and ts % 16 == 0
    return pl.pallas_call(
        functools.partial(_ffn_kernel, tf=tf, final_norm=final_norm),
        grid=(t // tm,),
        in_specs=[
            pl.BlockSpec((tm, D_MODEL), lambda i: (i, 0)),
            pl.BlockSpec((ts, D_MODEL), lambda i: (i, 0)),
            pl.BlockSpec((1, D_MODEL), lambda i: (0, 0)),
            pl.BlockSpec((1, D_MODEL), lambda i: (0, 0)),
            pl.BlockSpec(memory_space=pl.ANY),
            pl.BlockSpec(memory_space=pl.ANY),
            pl.BlockSpec(memory_space=pl.ANY),
        ],
        out_specs=[pl.BlockSpec((tm, D_MODEL), lambda i: (i, 0)), pl.BlockSpec((ts, D_MODEL), lambda i: (i, 0))],
        out_shape=[jax.ShapeDtypeStruct((t, D_MODEL), F32), jax.ShapeDtypeStruct(xs.shape, F32)],
        scratch_shapes=[
            pltpu.VMEM((tm + ts, D_MODEL), BF16),
            pltpu.VMEM((2, D_MODEL, tf), F32),
            pltpu.VMEM((2, D_MODEL, tf), F32),
            pltpu.VMEM((2, tf, D_MODEL), F32),
            pltpu.SemaphoreType.DMA((3, 2)),
        ],
        compiler_params=_params(("arbitrary",)),
        name="ffn",
    )(x, xs, g, gf, w1, w3, w2)


KV_TILE = COL_K // IN_TILE
assert KV_TILE * IN_TILE == COL_K and COL_SMALL + LANES <= (KV_TILE + 1) * IN_TILE


def _in_tile_segments():
    src = {}
    off = 0
    for name, width in zip(("q", "k", "v", "qi", "ki", "wi", "z", "xbc", "dt"), IN_SPLITS):
        src[name] = (off, width)
        off += width
    runs, out = [], 0
    for name in ("q", "qi", "z", "xbc", "k", "v", "ki", "wi", "dt"):
        runs.append((out, src[name][0], src[name][1]))
        out += src[name][1]
    tiles = []
    for lo in range(0, N_IN, IN_TILE):
        segs = []
        for o, s0, n in runs:
            a, b = max(o, lo), min(o + n, lo + IN_TILE)
            if a < b:
                seg = (a - lo, s0 + a - o, b - a)
                if segs and segs[-1][0] + segs[-1][2] == seg[0] and segs[-1][1] + segs[-1][2] == seg[1]:
                    seg = (segs[-1][0], segs[-1][1], segs[-1][2] + seg[2])
                    segs.pop()
                segs.append(seg)
        assert all(v % 8 == 0 for seg in segs for v in seg)
        tiles.append(segs)
    return tiles


IN_TILE_SEGMENTS = _in_tile_segments()


def _in_proj_kernel(x_ref, g_ref, wt_hbm, o_ref, k_ref, v_ref, h_scr, w_buf, sem):
    i, j = pl.program_id(0), pl.program_id(1)
    nj = pl.num_programs(1)
    t = i * nj + j
    slot = lax.rem(t, 2)

    def copies(tile, slot):
        return [pltpu.make_async_copy(wt_hbm.at[pl.ds(s0, n), :], w_buf.at[slot, pl.ds(d0, n), :], sem.at[slot])
                for d0, s0, n in IN_TILE_SEGMENTS[tile]]

    def for_tile(tile, fn):
        for static_tile in range(len(IN_TILE_SEGMENTS)):
            pl.when(tile == static_tile)(functools.partial(fn, static_tile))

    def start(tile, slot):
        filled = sum(n for _, _, n in IN_TILE_SEGMENTS[tile])
        if filled < IN_TILE:
            w_buf[slot, filled:IN_TILE, :] = jnp.zeros((IN_TILE - filled, D_MODEL), F32)
        for c in copies(tile, slot):
            c.start()

    def wait(tile, slot):
        for c in copies(tile, slot):
            c.wait()

    @pl.when(t == 0)
    def _():
        start(0, 0)

    @pl.when(t + 1 < pl.num_programs(0) * nj)
    def _():
        for_tile(jnp.where(j + 1 == nj, 0, j + 1), functools.partial(start, slot=1 - slot))

    for_tile(j, functools.partial(wait, slot=slot))

    @pl.when(j == 0)
    def _():
        h_scr[...] = _rms(x_ref[...], g_ref[...]).astype(BF16)

    y = _dot_nt(h_scr[...], w_buf[slot].astype(BF16))
    o_ref[...] = y

    @pl.when(j == KV_TILE)
    def _():
        tm = x_ref.shape[0]
        for dst, col in ((k_ref, COL_K), (v_ref, COL_V)):
            for g in range(N_KV_HEADS):
                c0 = col - COL_K + g * HEAD_DIM
                dst[pl.ds(g, tm, stride=N_KV_HEADS), :] = y[:, c0:c0 + HEAD_DIM]


def _in_proj(x, g, w_t, *, tm):
    t = x.shape[0]
    kv_spec = pl.BlockSpec((tm * N_KV_HEADS, HEAD_DIM), lambda i, j: (i, 0))
    kv_shape = jax.ShapeDtypeStruct((t * N_KV_HEADS, HEAD_DIM), F32)
    return pl.pallas_call(
        _in_proj_kernel,
        grid=(t // tm, N_IN // IN_TILE),
        in_specs=[
            pl.BlockSpec((tm, D_MODEL), lambda i, j: (i, 0)),
            pl.BlockSpec((1, D_MODEL), lambda i, j: (0, 0)),
            pl.BlockSpec(memory_space=pl.ANY),
        ],
        out_specs=[pl.BlockSpec((tm, IN_TILE), lambda i, j: (i, j)), kv_spec, kv_spec],
        out_shape=[jax.ShapeDtypeStruct((t, N_IN), F32), kv_shape, kv_shape],
        scratch_shapes=[pltpu.VMEM((tm, D_MODEL), BF16), pltpu.VMEM((2, IN_TILE, D_MODEL), F32),
                        pltpu.SemaphoreType.DMA((2,))],
        compiler_params=_params(("arbitrary", "arbitrary")),
        name="in_proj",
    )(x, g, w_t)


def _out_proj_kernel(x_ref, a_ref, s_ref, wa_ref, ws_ref, o_ref):
    acc = _dot(a_ref[...].astype(BF16), wa_ref[...])
    acc += _dot(s_ref[...].astype(BF16), ws_ref[...])
    o_ref[...] = x_ref[...] + acc


def _out_proj(x, a, s, wa, ws, *, tm):
    t = x.shape[0]
    return pl.pallas_call(
        _out_proj_kernel,
        grid=(t // tm,),
        in_specs=[
            pl.BlockSpec((tm, D_MODEL), lambda i: (i, 0)),
            pl.BlockSpec((tm, D_ATTN), lambda i: (i, 0)),
            pl.BlockSpec((tm, D_SSD), lambda i: (i, 0)),
            pl.BlockSpec((D_ATTN, D_MODEL), lambda i: (0, 0)),
            pl.BlockSpec((D_SSD, D_MODEL), lambda i: (0, 0)),
        ],
        out_specs=pl.BlockSpec((tm, D_MODEL), lambda i: (i, 0)),
        out_shape=jax.ShapeDtypeStruct((t, D_MODEL), F32),
        compiler_params=_params(("parallel",)),
        name="out_proj",
    )(x, a, s, wa, ws)


NEAR = 2 * LANES


def _bias_kernel(rb_ref, o_ref):
    r = lax.broadcasted_iota(I32, (LANES, NEAR), 0)
    c = lax.broadcasted_iota(I32, (LANES, NEAR), 1)
    n = jnp.maximum(r + LANES - c, 0)
    max_exact = N_BUCKETS // 2
    nf = jnp.maximum(n, 1).astype(F32)
    large = max_exact + (jnp.log(nf / max_exact) / math.log(MAX_DISTANCE / max_exact)
                         * (N_BUCKETS - max_exact)).astype(I32)
    bucket = jnp.where(n < max_exact, n, jnp.minimum(large, N_BUCKETS - 1))
    for h in range(N_HEADS):
        acc = jnp.zeros((LANES, NEAR), F32)
        for b in range(N_BUCKETS):
            acc = jnp.where(bucket == b, rb_ref[b, h], acc)
        o_ref[h] = acc


def _bias_table(rel_bias):
    return pl.pallas_call(
        _bias_kernel,
        in_specs=[pl.BlockSpec(memory_space=pltpu.SMEM)],
        out_specs=pl.BlockSpec(memory_space=pltpu.VMEM),
        out_shape=jax.ShapeDtypeStruct((N_HEADS, LANES, NEAR), F32),
        name="bias_table",
    )(rel_bias)


def _count(mask, axis=1):
    x = mask.astype(F32)
    if axis == 0:
        x = jnp.sum(x.reshape(x.shape[0] // 64, 64, x.shape[1]), axis=0)
    return jnp.sum(x, axis=axis, keepdims=True)


def _kth_largest(count_ge, shape):
    t = jnp.full(shape, INT_MIN, I32)
    t = jnp.where(count_ge(jnp.zeros(shape, I32)) >= TOPK, 0, t)

    def body(it, t):
        cand = t + lax.shift_left(jnp.int32(1), 30 - it)
        return jnp.where(count_ge(cand) >= TOPK, cand, t)

    return lax.fori_loop(0, 31, body, t)


def _tie_limit(count_eq_below, need, shape, idx_bits):
    def body(it, m):
        cand = m + lax.shift_left(jnp.int32(1), idx_bits - 1 - it)
        return jnp.where(count_eq_below(cand) < need, cand, m)

    return lax.fori_loop(0, idx_bits, body, jnp.zeros(shape, I32))


CAUSAL_VARIANTS = 4


def _dsa_prompt_block(width, near_tiles, i, q_ref, qi_ref, k_ref, v_ref, sm_ref, bt_ref, o_ref, sk_scr, neg_scr,
                      kb_scr, vt_scr):
    tq = LANES
    q0 = pl.multiple_of(i * LANES, LANES)
    shape = (1, tq)

    wi_t = sm_ref[pl.ds(q0, tq), :].T[SM_WI:SM_WI + IDX_HEADS, :] * (IDX_HEADS * IDX_DIM) ** -0.5
    head_group = 4
    qi_groups = [
        jnp.concatenate([qi_ref[:, h * IDX_DIM:(h + 1) * IDX_DIM] for h in range(hg, hg + head_group)],
                        axis=0).astype(BF16)
        for hg in range(0, IDX_HEADS, head_group)]
    chunk = 2 * LANES
    pos = lax.broadcasted_iota(I32, (chunk, tq), 1) + q0
    for c in range(0, width, chunk):
        ki = sm_ref[c:c + chunk, SM_KI:SM_KI + IDX_DIM].astype(BF16)
        score = jnp.zeros((chunk, tq), F32)
        for n, hg in enumerate(range(0, IDX_HEADS, head_group)):
            rel = _dot_nt(ki, qi_groups[n])
            for e in range(head_group):
                score = score + wi_t[hg + e:hg + e + 1, :] * jnp.maximum(rel[:, e * tq:(e + 1) * tq], 0.0)
        key_c = lax.broadcasted_iota(I32, (chunk, tq), 0) + c
        sk_scr[c:c + chunk, :] = jnp.where(key_c <= pos, _sort_key(score), INT_MIN)

    key = lax.broadcasted_iota(I32, (width, tq), 0)

    def sk():
        return sk_scr[0:width, :]

    thr = _kth_largest(lambda c: _count(sk() >= c, 0), shape)
    excess = (_count(sk() >= thr, 0) > TOPK) & (thr > INT_MIN)
    any_excess = jnp.max(excess.astype(F32)) > 0.0

    @pl.when(jnp.logical_not(any_excess))
    def _():
        neg_scr[0:width, :] = jnp.where(sk() >= jnp.maximum(thr, INT_MIN + 1), 0.0, -jnp.inf)

    @pl.when(any_excess)
    def _():
        need = TOPK - _count(sk() > thr, 0)
        lim = _tie_limit(lambda m: _count((sk() == thr) & (key < m), 0), need, shape, (width - 1).bit_length())
        take = (sk() > thr) | ((sk() == thr) & (key <= lim) & (sk() > INT_MIN))
        neg_scr[0:width, :] = jnp.where(take, 0.0, -jnp.inf)

    log2e = math.log2(math.e)
    scale = HEAD_DIM ** -0.5 * log2e
    tiles = width // LANES
    for g in range(N_KV_HEADS):
        heads = range(g * GQA_GROUP, (g + 1) * GQA_GROUP)
        kg = kb_scr[g, 0:width, :]
        vg_t = vt_scr[g, :, 0:width]
        qg = jnp.concatenate([q_ref[:, h * HEAD_DIM:(h + 1) * HEAD_DIM] for h in heads], axis=0).astype(BF16)
        diag_t = [((bt_ref[h, :, LANES:NEAR] - bt_ref[h, 0:1, 0:1]) * log2e).T for h in heads]
        prev_t = [((bt_ref[h, :, 0:LANES] - bt_ref[h, 0:1, 0:1]) * log2e).T for h in heads]
        qk = _dot_nt(kg, qg)
        logits = []
        for t in range(tiles):
            rows = slice(t * LANES, (t + 1) * LANES)
            per_head = []
            for e in range(GQA_GROUP):
                lt = qk[rows, e * tq:(e + 1) * tq] * scale + neg_scr[rows, :]
                if t >= tiles - near_tiles:
                    lt = lt + jnp.where(i == t, diag_t[e], jnp.where(i - 1 == t, prev_t[e], 0.0))
                per_head.append(lt)
            logits.append(per_head)
        p_t = []
        inv_l = []
        for e in range(GQA_GROUP):
            m = jnp.max(functools.reduce(jnp.maximum, [lt[e] for lt in logits]), axis=0, keepdims=True)
            probs = [jnp.exp2(lt[e] - m) for lt in logits]
            inv_l.append(1.0 / jnp.sum(functools.reduce(jnp.add, probs), axis=0, keepdims=True))
            p_t.append(jnp.concatenate([p.astype(BF16) for p in probs], axis=0))
        out_t = _dot(vg_t, jnp.concatenate(p_t, axis=1))
        for e, h in enumerate(heads):
            o_ref[:, h * HEAD_DIM:(h + 1) * HEAD_DIM] = (out_t[:, e * tq:(e + 1) * tq] * inv_l[e]).T


def _dsa_prompt_kernel(*refs, seq):
    i = pl.program_id(1)
    per = seq // LANES // CAUSAL_VARIANTS
    k_ref, v_ref = refs[2], refs[3]
    kb_scr, vt_scr = refs[-2], refs[-1]

    @pl.when(i == 0)
    def _():
        for g in range(N_KV_HEADS):
            cols = slice(g * HEAD_DIM, (g + 1) * HEAD_DIM)
            kb_scr[g] = k_ref[:, cols].astype(BF16)
            vt_scr[g] = v_ref[:, cols].T.astype(BF16)

    for v in range(CAUSAL_VARIANTS):
        @pl.when(i // per == v)
        def _(v=v):
            _dsa_prompt_block((v + 1) * per * LANES, per + 1, i, *refs)


def _dsa_prompt(proj, btab, *, batch, seq):
    nq = seq // LANES
    return pl.pallas_call(
        functools.partial(_dsa_prompt_kernel, seq=seq),
        grid=(batch, nq),
        in_specs=[
            pl.BlockSpec((LANES, D_ATTN), lambda b, i: (b * nq + i, COL_Q // D_ATTN)),
            pl.BlockSpec((LANES, IDX_HEADS * IDX_DIM), lambda b, i: (b * nq + i, COL_QI // (IDX_HEADS * IDX_DIM))),
            pl.BlockSpec((seq, D_KV), lambda b, i: (b, COL_K // D_KV)),
            pl.BlockSpec((seq, D_KV), lambda b, i: (b, COL_V // D_KV)),
            pl.BlockSpec((seq, LANES), lambda b, i: (b, COL_SMALL // LANES)),
            pl.BlockSpec((N_HEADS, LANES, NEAR), lambda b, i: (0, 0, 0)),
        ],
        out_specs=pl.BlockSpec((LANES, D_ATTN), lambda b, i: (b * nq + i, 0)),
        out_shape=jax.ShapeDtypeStruct((batch * seq, D_ATTN), F32),
        scratch_shapes=[pltpu.VMEM((seq, LANES), I32), pltpu.VMEM((seq, LANES), F32),
                        pltpu.VMEM((N_KV_HEADS, seq, HEAD_DIM), BF16), pltpu.VMEM((N_KV_HEADS, HEAD_DIM, seq), BF16)],
        compiler_params=_params(("arbitrary", "arbitrary")),
        name="dsa_prompt",
    )(proj, proj, proj, proj, proj, btab)


SCORE_PAGES = 64
ATTN_PAGES = 32


def _sample_queries(qi_ref):
    return jnp.concatenate(
        [qi_ref[:, h * IDX_DIM:(h + 1) * IDX_DIM] for h in range(IDX_HEADS)], axis=0).astype(BF16)


def _sample_scores(rel, wi, nq):
    sc = jnp.zeros((nq, rel.shape[1]), F32)
    for h in range(IDX_HEADS):
        sc = sc + wi[:, h:h + 1] * jnp.maximum(rel[h * nq:(h + 1) * nq, :], 0.0)
    return sc


def _page_copies(pt_ref, pools, bufs, sems, b, s, slot, pages, page_rows):
    copies = []
    for r in range(pages):
        row0 = pl.multiple_of(pt_ref[b, s * pages + r] * page_rows, page_rows)
        for pool, buf, sem in zip(pools, bufs, sems):
            copies.append(pltpu.make_async_copy(pool.at[pl.ds(row0, page_rows), :], buf.at[slot, r], sem.at[slot]))
    return copies


def _page_pipeline(pt_ref, pools, bufs, sems, pages, page_rows):
    b, s = pl.program_id(0), pl.program_id(1)
    steps = pl.num_programs(1)
    t = b * steps + s
    slot = lax.rem(t, 2)
    copies = functools.partial(_page_copies, pt_ref, pools, bufs, sems, pages=pages, page_rows=page_rows)

    @pl.when(t == 0)
    def _():
        for c in copies(b, s, slot):
            c.start()

    @pl.when(t + 1 < pl.num_programs(0) * steps)
    def _():
        wrap = s + 1 == steps
        for c in copies(jnp.where(wrap, b + 1, b), jnp.where(wrap, 0, s + 1), 1 - slot):
            c.start()

    for c in copies(b, s, slot):
        c.wait()
    return slot


def _sample_score_kernel(pt_ref, qi_ref, sm_ref, pool_ref, past_ref, new_ref, pages, sem, kbuf, *, nq):
    s = pl.program_id(1)
    slot = _page_pipeline(pt_ref, [pool_ref], [pages], [sem], SCORE_PAGES, IDX_DIM)
    qx = _sample_queries(qi_ref)
    wi = sm_ref[:, SM_WI:SM_WI + IDX_HEADS] * (IDX_HEADS * IDX_DIM) ** -0.5
    for r in range(SCORE_PAGES):
        kbuf[:, r * PAGE_SIZE:(r + 1) * PAGE_SIZE] = pages[slot, r].astype(BF16)
    past_ref[...] = _sort_key(_sample_scores(_dot(qx, kbuf[...]), wi, nq))

    @pl.when(s == pl.num_programs(1) - 1)
    def _():
        ki_new = jnp.concatenate(
            [sm_ref[:, SM_KI:SM_KI + IDX_DIM], jnp.zeros((LANES - nq, IDX_DIM), F32)], axis=0).astype(BF16)
        sc = _sample_scores(_dot_nt(qx, ki_new), wi, nq)
        j = lax.broadcasted_iota(I32, (nq, LANES), 1)
        t = lax.broadcasted_iota(I32, (nq, LANES), 0)
        new_ref[...] = jnp.where(j <= t, _sort_key(sc), INT_MIN)


def _sample_score(page_table, proj_s, pool_kidx_t, *, batch, nq, n_pages):
    steps = n_pages // SCORE_PAGES
    grid_spec = pltpu.PrefetchScalarGridSpec(
        num_scalar_prefetch=1,
        grid=(batch, steps),
        in_specs=[
            pl.BlockSpec((nq, IDX_HEADS * IDX_DIM), lambda b, s, pt: (b, COL_QI // (IDX_HEADS * IDX_DIM))),
            pl.BlockSpec((nq, LANES), lambda b, s, pt: (b, COL_SMALL // LANES)),
            pl.BlockSpec(memory_space=pl.ANY),
        ],
        out_specs=[
            pl.BlockSpec((None, nq, SCORE_PAGES * PAGE_SIZE), lambda b, s, pt: (b, 0, s)),
            pl.BlockSpec((None, nq, LANES), lambda b, s, pt: (b, 0, 0)),
        ],
        scratch_shapes=[
            pltpu.VMEM((2, SCORE_PAGES, IDX_DIM, PAGE_SIZE), F32),
            pltpu.SemaphoreType.DMA((2,)),
            pltpu.VMEM((IDX_DIM, SCORE_PAGES * PAGE_SIZE), BF16),
        ],
    )
    return pl.pallas_call(
        functools.partial(_sample_score_kernel, nq=nq),
        grid_spec=grid_spec,
        out_shape=[
            jax.ShapeDtypeStruct((batch, nq, n_pages * PAGE_SIZE), I32),
            jax.ShapeDtypeStruct((batch, nq, LANES), I32),
        ],
        compiler_params=_params(("arbitrary", "arbitrary")),
        name="sample_score",
    )(page_table, proj_s, proj_s, pool_kidx_t)


def _sample_select_kernel(past_ref, new_ref, thr_ref, lim_ref, *, past):
    rows = past_ref.shape[0]
    colp = lax.broadcasted_iota(I32, (rows, past), 1)
    coln = lax.broadcasted_iota(I32, (rows, LANES), 1) + past

    def count_ge(c):
        return _count(past_ref[...] >= c) + _count(new_ref[...] >= c)

    thr = _kth_largest(count_ge, (rows, 1))
    thr_ref[...] = jnp.broadcast_to(thr, (rows, LANES))
    lim_ref[...] = jnp.full((rows, LANES), 2 ** 31 - 1, I32)
    any_excess = jnp.max((count_ge(thr) > TOPK).astype(F32)) > 0.0

    @pl.when(any_excess)
    def _():
        need = TOPK - (_count(past_ref[...] > thr) + _count(new_ref[...] > thr))

        def count_eq_below(m):
            return (_count((past_ref[...] == thr) & (colp < m))
                    + _count((new_ref[...] == thr) & (coln < m)))

        lim = _tie_limit(count_eq_below, need, (rows, 1), (past + LANES - 1).bit_length())
        lim_ref[...] = jnp.broadcast_to(lim, (rows, LANES))


def _sample_select(sk_past, sk_new, *, rows_per_step):
    rows, past = sk_past.shape
    return pl.pallas_call(
        functools.partial(_sample_select_kernel, past=past),
        grid=(rows // rows_per_step,),
        in_specs=[
            pl.BlockSpec((rows_per_step, past), lambda i: (i, 0)),
            pl.BlockSpec((rows_per_step, LANES), lambda i: (i, 0)),
        ],
        out_specs=[
            pl.BlockSpec((rows_per_step, LANES), lambda i: (i, 0)),
            pl.BlockSpec((rows_per_step, LANES), lambda i: (i, 0)),
        ],
        out_shape=[jax.ShapeDtypeStruct((rows, LANES), I32)] * 2,
        compiler_params=_params(("parallel",)),
        name="sample_select",
    )(sk_past, sk_new)


def _sample_attn_kernel(pt_ref, q_ref, kn_ref, vn_ref, skp_ref, skn_ref, thr_ref, lim_ref, bt_ref, poolk_ref,
                        poolv_ref, o_ref, kpages, vpages, ksem, vsem, kbuf, vbuf, m_scr, l_scr, acc_scr, *,
                        nq, past):
    slot = _page_pipeline(pt_ref, [poolk_ref, poolv_ref], [kpages, vpages], [ksem, vsem], ATTN_PAGES,
                          PAGE_SIZE * N_KV_HEADS)
    s = pl.program_id(1)
    last = pl.num_programs(1) - 1
    width = ATTN_PAGES * PAGE_SIZE
    grows = GQA_GROUP * nq
    scale = HEAD_DIM ** -0.5

    @pl.when(s == 0)
    def _():
        m_scr[...] = jnp.full_like(m_scr, NEG_BIG)
        l_scr[...] = jnp.zeros_like(l_scr)
        acc_scr[...] = jnp.zeros_like(acc_scr)

    thr = thr_ref[:, 0:1]
    lim = lim_ref[:, 0:1]

    def select(sk, idx):
        sel = (sk > thr) | ((sk == thr) & (idx <= lim))
        return jnp.concatenate([sel] * GQA_GROUP, axis=0)

    def far_bias(g):
        return jnp.concatenate(
            [jnp.broadcast_to(bt_ref[g * GQA_GROUP + hh, 0:1, 0:1], (nq, 1)) for hh in range(GQA_GROUP)], axis=0)

    def near_bias(g, lo, hi):
        return jnp.concatenate([bt_ref[g * GQA_GROUP + hh, 0:nq, lo:hi] for hh in range(GQA_GROUP)], axis=0)

    def group_queries(g):
        return jnp.concatenate(
            [q_ref[:, (g * GQA_GROUP + hh) * HEAD_DIM:(g * GQA_GROUP + hh + 1) * HEAD_DIM]
             for hh in range(GQA_GROUP)], axis=0).astype(BF16)

    def accumulate(g, lg, sel, vals):
        rs = slice(g * grows, (g + 1) * grows)
        m_old = m_scr[rs, :]
        m_new = jnp.maximum(m_old, jnp.max(jnp.where(sel, lg, NEG_BIG), axis=1, keepdims=True))
        p = jnp.where(sel, jnp.exp(lg - m_new), 0.0)
        alpha = jnp.exp(m_old - m_new)
        l_scr[rs, :] = alpha * l_scr[rs, :] + jnp.sum(p, axis=1, keepdims=True)
        acc_scr[rs, :] = alpha * acc_scr[rs, :] + _dot(p.astype(BF16), vals)
        m_scr[rs, :] = m_new

    for r in range(ATTN_PAGES):
        for g in range(N_KV_HEADS):
            rows = pl.ds(g, PAGE_SIZE, stride=N_KV_HEADS)
            kbuf[g, r * PAGE_SIZE:(r + 1) * PAGE_SIZE, :] = kpages[slot, r, rows, :].astype(BF16)
            vbuf[g, r * PAGE_SIZE:(r + 1) * PAGE_SIZE, :] = vpages[slot, r, rows, :].astype(BF16)

    col = lax.broadcasted_iota(I32, (nq, width), 1)
    sel = select(skp_ref[...], col + s * width)
    colg = lax.broadcasted_iota(I32, (grows, width), 1)
    in_near = (s == last) & (colg >= width - PAGE_SIZE)
    for g in range(N_KV_HEADS):
        near = jnp.tile(near_bias(g, 0, LANES), (1, ATTN_PAGES))
        bias = jnp.where(in_near, near, far_bias(g))
        lg = _dot_nt(group_queries(g), kbuf[g]) * scale + bias
        accumulate(g, lg, sel, vbuf[g])

    @pl.when(s == last)
    def _():
        pad = jnp.zeros((LANES - nq, D_KV), F32)
        kn = jnp.concatenate([kn_ref[...], pad], axis=0).astype(BF16)
        vn = jnp.concatenate([vn_ref[...], pad], axis=0).astype(BF16)
        coln = lax.broadcasted_iota(I32, (nq, LANES), 1) + past
        seln = select(skn_ref[...], coln)
        for g in range(N_KV_HEADS):
            cs = slice(g * HEAD_DIM, (g + 1) * HEAD_DIM)
            lg = _dot_nt(group_queries(g), kn[:, cs]) * scale + near_bias(g, LANES, NEAR)
            accumulate(g, lg, seln, vn[:, cs])
        out = acc_scr[...] / l_scr[...]
        for h in range(N_HEADS):
            o_ref[:, h * HEAD_DIM:(h + 1) * HEAD_DIM] = out[h * nq:(h + 1) * nq, :]


def _sample_attn(page_table, proj_s, sk_past, sk_new, thr, lim, btab, pool_k, pool_v, *, batch, nq, n_pages):
    steps = n_pages // ATTN_PAGES
    width = ATTN_PAGES * PAGE_SIZE
    past = n_pages * PAGE_SIZE
    page_buf = pltpu.VMEM((2, ATTN_PAGES, PAGE_SIZE * N_KV_HEADS, HEAD_DIM), F32)
    grid_spec = pltpu.PrefetchScalarGridSpec(
        num_scalar_prefetch=1,
        grid=(batch, steps),
        in_specs=[
            pl.BlockSpec((nq, D_ATTN), lambda b, s, pt: (b, COL_Q // D_ATTN)),
            pl.BlockSpec((nq, D_KV), lambda b, s, pt: (b, COL_K // D_KV)),
            pl.BlockSpec((nq, D_KV), lambda b, s, pt: (b, COL_V // D_KV)),
            pl.BlockSpec((None, nq, width), lambda b, s, pt: (b, 0, s)),
            pl.BlockSpec((None, nq, LANES), lambda b, s, pt: (b, 0, 0)),
            pl.BlockSpec((nq, LANES), lambda b, s, pt: (b, 0)),
            pl.BlockSpec((nq, LANES), lambda b, s, pt: (b, 0)),
            pl.BlockSpec((N_HEADS, LANES, NEAR), lambda b, s, pt: (0, 0, 0)),
            pl.BlockSpec(memory_space=pl.ANY),
            pl.BlockSpec(memory_space=pl.ANY),
        ],
        out_specs=pl.BlockSpec((nq, D_ATTN), lambda b, s, pt: (b, 0)),
        scratch_shapes=[
            page_buf, page_buf, pltpu.SemaphoreType.DMA((2,)), pltpu.SemaphoreType.DMA((2,)),
            pltpu.VMEM((N_KV_HEADS, width, HEAD_DIM), BF16),
            pltpu.VMEM((N_KV_HEADS, width, HEAD_DIM), BF16),
            pltpu.VMEM((N_HEADS * nq, 1), F32),
            pltpu.VMEM((N_HEADS * nq, 1), F32),
            pltpu.VMEM((N_HEADS * nq, HEAD_DIM), F32),
        ],
    )
    return pl.pallas_call(
        functools.partial(_sample_attn_kernel, nq=nq, past=past),
        grid_spec=grid_spec,
        out_shape=jax.ShapeDtypeStruct((batch * nq, D_ATTN), F32),
        compiler_params=_params(("arbitrary", "arbitrary")),
        name="sample_attn",
    )(page_table, proj_s, proj_s, proj_s, sk_past, sk_new, thr, lim, btab, pool_k, pool_v)


TAIL = 8
SHORT_ROWS = 16


def _ssd_kernel(xbc_ref, z_ref, sm_ref, cprev_ref, hprev_ref, cw_ref, cb_ref, alog_ref, dtb_ref, dvec_ref,
                g_ref, y_ref, cnew_ref, hnew_ref, xpad_scr, sm_scr, h_scr, yd_scr, *, rows):
    c = pl.program_id(1)
    q = CHUNK
    r = q if rows == q else SHORT_ROWS
    assert rows <= r
    keep = CONV_WIDTH - 1
    heads_per_group = SSD_HEADS // SSD_GROUPS

    def pad_rows(x):
        return x if r == q else jnp.concatenate([x, jnp.zeros((q - r, x.shape[1]), x.dtype)], axis=0)

    @pl.when(c == 0)
    def _():
        xpad_scr[...] = jnp.zeros_like(xpad_scr)
        sm_scr[...] = jnp.zeros_like(sm_scr)
        xpad_scr[TAIL - keep:TAIL, :] = cprev_ref[0]
        h_scr[...] = hprev_ref[0]

    xpad_scr[TAIL:TAIL + rows, :] = xbc_ref[...]
    sm_scr[0:rows, :] = sm_ref[...]
    conv = sum(xpad_scr[TAIL - keep + k:TAIL - keep + k + r, :] * cw_ref[k:k + 1, :]
               for k in range(CONV_WIDTH)) + cb_ref[...]
    new_tail = xpad_scr[rows + TAIL - keep:rows + TAIL, :]
    cnew_ref[0] = new_tail
    xpad_scr[TAIL - keep:TAIL, :] = new_tail

    xc = conv * jax.nn.sigmoid(conv)
    xs = xc[:, :D_SSD]
    bm = pad_rows(xc[:, D_SSD:D_SSD + SSD_GROUPS * D_STATE]).astype(BF16)
    cm = xc[:, D_SSD + SSD_GROUPS * D_STATE:].astype(BF16)

    ri = lax.broadcasted_iota(I32, (q, q), 0)
    ci = lax.broadcasted_iota(I32, (q, q), 1)
    causal = ri >= ci
    causal_r = causal[0:r, :]
    x = sm_scr[:, SM_DT:SM_DT + SSD_HEADS] + dtb_ref[...]
    dt = jnp.maximum(x, 0.0) + jnp.log1p(jnp.exp(-jnp.abs(x)))
    dt = jnp.where(lax.broadcasted_iota(I32, (q, SSD_HEADS), 0) < rows, dt, 0.0)
    a = -jnp.exp(alog_ref[...])
    acum = _select_dot(causal.astype(BF16), dt * a)
    acum_t = jnp.concatenate([acum, jnp.zeros((q, LANES - SSD_HEADS), F32)], axis=1).T

    expand = (lax.broadcasted_iota(I32, (SSD_HEADS, D_SSD), 1) // SSD_HEAD_DIM
              == lax.broadcasted_iota(I32, (SSD_HEADS, D_SSD), 0)).astype(BF16)
    expand_t = (lax.broadcasted_iota(I32, (D_SSD, SSD_HEADS), 0) // SSD_HEAD_DIM
                == lax.broadcasted_iota(I32, (D_SSD, SSD_HEADS), 1)).astype(BF16)
    acum_x = _select_dot(acum[0:r, :], expand)
    xd = xs * _select_dot(dt[0:r, :], expand)
    xdw_t = pad_rows(xd * jnp.exp(acum_x[r - 1:r, :] - acum_x)).T.astype(BF16)
    xd = pad_rows(xd).astype(BF16)
    state_decay = jnp.exp(_select_dot(expand_t, jnp.broadcast_to(acum_t[0:SSD_HEADS, q - 1:q], (SSD_HEADS, LANES))))

    for g in range(SSD_GROUPS):
        ns = slice(g * D_STATE, (g + 1) * D_STATE)
        hs = slice(g * heads_per_group * SSD_HEAD_DIM, (g + 1) * heads_per_group * SSD_HEAD_DIM)
        cb = _dot_nt(cm[:, ns], bm[:, ns])
        for e in range(heads_per_group):
            h = g * heads_per_group + e
            ps = slice(h * SSD_HEAD_DIM, (h + 1) * SSD_HEAD_DIM)
            seg = acum[0:r, h:h + 1] - acum_t[h:h + 1, :]
            lmat = jnp.exp(jnp.where(causal_r, seg, -jnp.inf))
            yd_scr[0:r, ps] = _dot((cb * lmat).astype(BF16), xd[:, ps])
        h_old = h_scr[hs, :]
        yd_scr[0:r, hs] += _dot_nt(cm[:, ns], h_old.astype(BF16)) * jnp.exp(acum_x[:, hs])
        h_scr[hs, :] = h_old * state_decay[hs, :] + _dot(xdw_t[hs, :], bm[:, ns])

    hnew_ref[0] = h_scr[...]
    y = yd_scr[0:rows, :] + dvec_ref[...] * xs[0:rows, :]
    zz = z_ref[...]
    y = y * (zz * jax.nn.sigmoid(zz))
    width = D_SSD // SSD_GROUPS
    for g in range(SSD_GROUPS):
        cs = slice(g * width, (g + 1) * width)
        y_ref[:, cs] = _rms(y[:, cs], g_ref[:, cs])


def _ssd(proj, conv_prev, ssm_prev, conv_w, conv_b, a_log, dt_bias, d_vec, g_ssd, *, batch, seq):
    rows = min(CHUNK, seq)
    nc = seq // rows
    full = lambda *shape: pl.BlockSpec(shape, lambda b, c: (0,) * len(shape))
    return pl.pallas_call(
        functools.partial(_ssd_kernel, rows=rows),
        grid=(batch, nc),
        in_specs=[
            pl.BlockSpec((rows, CONV_DIM), lambda b, c: (b * nc + c, COL_XBC // CONV_DIM)),
            pl.BlockSpec((rows, D_SSD), lambda b, c: (b * nc + c, COL_Z // D_SSD)),
            pl.BlockSpec((rows, LANES), lambda b, c: (b * nc + c, COL_SMALL // LANES)),
            pl.BlockSpec((1, CONV_WIDTH - 1, CONV_DIM), lambda b, c: (b, 0, 0)),
            pl.BlockSpec((1, D_SSD, D_STATE), lambda b, c: (b, 0, 0)),
            full(CONV_WIDTH, CONV_DIM), full(1, CONV_DIM), full(1, SSD_HEADS), full(1, SSD_HEADS),
            full(1, D_SSD), full(1, D_SSD),
        ],
        out_specs=[
            pl.BlockSpec((rows, D_SSD), lambda b, c: (b * nc + c, 0)),
            pl.BlockSpec((1, CONV_WIDTH - 1, CONV_DIM), lambda b, c: (b, 0, 0)),
            pl.BlockSpec((1, D_SSD, D_STATE), lambda b, c: (b, 0, 0)),
        ],
        out_shape=[
            jax.ShapeDtypeStruct((batch * seq, D_SSD), F32),
            jax.ShapeDtypeStruct((batch, CONV_WIDTH - 1, CONV_DIM), F32),
            jax.ShapeDtypeStruct((batch, D_SSD, D_STATE), F32),
        ],
        scratch_shapes=[
            pltpu.VMEM((CHUNK + TAIL, CONV_DIM), F32),
            pltpu.VMEM((CHUNK, LANES), F32),
            pltpu.VMEM((D_SSD, D_STATE), F32),
            pltpu.VMEM((CHUNK, D_SSD), F32),
        ],
        compiler_params=_params(("parallel", "arbitrary")),
        name="ssd",
    )(proj, proj, proj, conv_prev, ssm_prev, conv_w, conv_b, a_log, dt_bias, d_vec, g_ssd)


def kernel(x_prompt, x_sample, cache_k, cache_v, cache_kidx, state_conv, state_ssm, page_table, rel_bias, g_ffn1, w1_ffn1, w3_ffn1, w2_ffn1, g_mix, w_in, conv_w, conv_b, a_log, dt_bias, d_skip, g_ssd, w_out, g_ffn2, w1_ffn2, w3_ffn2, w2_ffn2, g_final):
    depth = w_in.shape[0]
    assert depth == 1
    l = 0
    bp, seq, _ = x_prompt.shape
    bs, nq, _ = x_sample.shape
    n_pages = page_table.shape[1]
    row = lambda t: t.reshape(1, -1)

    w1a, w3a, w2a = w1_ffn1[l], w3_ffn1[l], w2_ffn1[l]
    w1b, w3b, w2b = w1_ffn2[l], w3_ffn2[l], w2_ffn2[l]
    w_in_t = jnp.swapaxes(w_in[l], 0, 1)
    w_out_a = w_out[l, :D_ATTN].astype(BF16)
    w_out_s = w_out[l, D_ATTN:].astype(BF16)
    d_vec = jnp.repeat(d_skip[l], SSD_HEAD_DIM).reshape(1, D_SSD)
    btab = _bias_table(rel_bias)
    gf = row(g_final)

    tm, tf = 1024, 256

    def mix(x, attend, conv_prev, ssm_prev, batch, length):
        proj, k, v = _in_proj(x, row(g_mix[l]), w_in_t, tm=min(x.shape[0], tm))
        a_out = attend(proj)
        s_out, conv_new, ssm_new = _ssd(
            proj, conv_prev, ssm_prev.reshape(batch, D_SSD, D_STATE), conv_w[l], row(conv_b[l]), row(a_log[l]),
            row(dt_bias[l]), d_vec, row(g_ssd[l]), batch=batch, seq=length)
        x = _out_proj(x, a_out, s_out, w_out_a, w_out_s, tm=min(x.shape[0], 512))
        k = k.reshape(1, batch, length, N_KV_HEADS, HEAD_DIM)
        v = v.reshape(1, batch, length, N_KV_HEADS, HEAD_DIM)
        ki = proj[:, COL_SMALL + SM_KI:COL_SMALL + SM_KI + IDX_DIM].reshape(1, batch, length, IDX_DIM)
        ssm_new = ssm_new.reshape(1, batch, SSD_HEADS, SSD_HEAD_DIM, D_STATE)
        return x, k, v, ki, conv_new[None], ssm_new

    xp, xs = _ffn(x_prompt.reshape(bp * seq, D_MODEL), x_sample.reshape(bs * nq, D_MODEL), row(g_ffn1[l]),
                  w1a, w3a, w2a, gf, tm=tm, tf=tf, final_norm=False)

    xp, k_p, v_p, ki_p, conv_p, ssm_p = mix(
        xp, functools.partial(_dsa_prompt, btab=btab, batch=bp, seq=seq),
        jnp.zeros((bp, CONV_WIDTH - 1, CONV_DIM), F32), jnp.zeros((bp, SSD_HEADS, SSD_HEAD_DIM, D_STATE), F32),
        bp, seq)

    pool_k = cache_k[l].reshape(-1, HEAD_DIM)
    pool_v = cache_v[l].reshape(-1, HEAD_DIM)
    pool_kidx = jnp.swapaxes(cache_kidx[l], 1, 2).reshape(-1, PAGE_SIZE)

    def sample_attend(proj):
        sk_past, sk_new = _sample_score(page_table, proj, pool_kidx, batch=bs, nq=nq, n_pages=n_pages)
        thr, lim = _sample_select(sk_past.reshape(bs * nq, -1), sk_new.reshape(bs * nq, LANES),
                                  rows_per_step=128)
        return _sample_attn(page_table, proj, sk_past, sk_new, thr, lim, btab, pool_k, pool_v,
                            batch=bs, nq=nq, n_pages=n_pages)

    xs, k_s, v_s, ki_s, conv_s, ssm_s = mix(xs, sample_attend, state_conv[l], state_ssm[l], bs, nq)

    y_p, y_s = _ffn(xp, xs, row(g_ffn2[l]), w1b, w3b, w2b, gf, tm=tm, tf=tf, final_norm=True)
    y_p = y_p.reshape(bp, seq, D_MODEL)
    y_s = y_s.reshape(bs, nq, D_MODEL)
    return (y_p, y_s, k_p, v_p, ki_p, conv_p, ssm_p, k_s, v_s, ki_s, conv_s, ssm_s)
```

```python
import functools
import math

import jax
import jax.numpy as jnp
from jax import lax
from jax.experimental import pallas as pl
from jax.experimental.pallas import tpu as pltpu

F32 = jnp.float32
BF16 = jnp.bfloat16
I32 = jnp.int32

D_MODEL = 2048
PAGE_SIZE = 128
N_HEADS = 8
HEAD_DIM = 128
N_KV_HEADS = 2
GQA_GROUP = N_HEADS // N_KV_HEADS
IDX_HEADS = 16
IDX_DIM = 64
TOPK = 256
N_BUCKETS = 32
MAX_DISTANCE = 128
SSD_HEADS = 16
SSD_HEAD_DIM = 64
SSD_GROUPS = 2
D_STATE = 128
CONV_WIDTH = 4
CHUNK = 128
D_ATTN = N_HEADS * HEAD_DIM
D_SSD = SSD_HEADS * SSD_HEAD_DIM
D_KV = N_KV_HEADS * HEAD_DIM
CONV_DIM = D_SSD + 2 * SSD_GROUPS * D_STATE
D_FF = 5632
EPS = 1e-6
IN_SPLITS = (D_ATTN, D_KV, D_KV, IDX_HEADS * IDX_DIM, IDX_DIM, IDX_HEADS, D_SSD, CONV_DIM, SSD_HEADS)

LANES = 128
COL_Q = 0
COL_QI = COL_Q + D_ATTN
COL_Z = COL_QI + IDX_HEADS * IDX_DIM
COL_XBC = COL_Z + D_SSD
COL_K = COL_XBC + CONV_DIM
COL_V = COL_K + D_KV
COL_SMALL = COL_V + D_KV
SM_KI = 0
SM_WI = SM_KI + IDX_DIM
SM_DT = SM_WI + IDX_HEADS
IN_TILE = 768
N_IN = ((COL_SMALL + LANES + IN_TILE - 1) // IN_TILE) * IN_TILE

INT_MIN = -(2 ** 31)
NEG_BIG = -1e30
VMEM_LIMIT = 60000 * 1024


def _params(sem):
    return pltpu.CompilerParams(dimension_semantics=sem, vmem_limit_bytes=VMEM_LIMIT)


def _rms(x, g):
    return x * lax.rsqrt(jnp.mean(x * x, axis=-1, keepdims=True) + EPS) * g


def _dot(a, b):
    return jnp.dot(a, b, preferred_element_type=F32)


def _dot_nt(a, b):
    return lax.dot_general(a, b, (((1,), (1,)), ((), ())), preferred_element_type=F32)


def _split3(x):
    hi = x.astype(BF16)
    rest = x - hi.astype(F32)
    mid = rest.astype(BF16)
    return hi, mid, (rest - mid.astype(F32)).astype(BF16)


def _select_dot(a, b):
    if a.dtype == BF16:
        return sum(_dot(a, part) for part in _split3(b))
    return sum(_dot(part, b) for part in _split3(a))


def _sort_key(x):
    bits = lax.bitcast_convert_type(x + 0.0, I32)
    return bits ^ ((bits >> 31) & 0x7FFFFFFF)


def _ffn_kernel(x_ref, xs_ref, g_ref, gf_ref, w1_hbm, w3_hbm, w2_hbm, o_ref, os_ref, h_scr, w1_buf, w3_buf,
                w2_buf, sem, *, tf, final_norm):
    i = pl.program_id(0)
    nf = D_FF // tf
    total = pl.num_programs(0) * nf
    tm = x_ref.shape[0]

    def copies(f, slot):
        c0 = pl.multiple_of(f * tf, tf)
        return (pltpu.make_async_copy(w1_hbm.at[:, pl.ds(c0, tf)], w1_buf.at[slot], sem.at[0, slot]),
                pltpu.make_async_copy(w3_hbm.at[:, pl.ds(c0, tf)], w3_buf.at[slot], sem.at[1, slot]),
                pltpu.make_async_copy(w2_hbm.at[pl.ds(c0, tf), :], w2_buf.at[slot], sem.at[2, slot]))

    @pl.when(i == 0)
    def _():
        for c in copies(0, 0):
            c.start()

    h_scr[0:tm, :] = _rms(x_ref[...], g_ref[...]).astype(BF16)
    h_scr[tm:, :] = _rms(xs_ref[...], g_ref[...]).astype(BF16)
    o_ref[...] = jnp.zeros_like(o_ref)
    os_ref[...] = jnp.zeros_like(os_ref)

    def body(f, carry):
        t = i * nf + f
        slot = lax.rem(t, 2)

        @pl.when(t + 1 < total)
        def _():
            for c in copies(jnp.where(f + 1 == nf, 0, f + 1), 1 - slot):
                c.start()

        for c in copies(f, slot):
            c.wait()
        h = h_scr[...]
        a = _dot(h, w1_buf[slot].astype(BF16))
        b = _dot(h, w3_buf[slot].astype(BF16))
        u = (a * jax.nn.sigmoid(a) * b).astype(BF16)
        y = _dot(u, w2_buf[slot].astype(BF16))
        o_ref[...] += y[0:tm, :]
        os_ref[...] += y[tm:, :]
        return carry

    lax.fori_loop(0, nf, body, 0)
    for src, dst in ((x_ref, o_ref), (xs_ref, os_ref)):
        y = src[...] + 0.5 * dst[...]
        if final_norm:
            y = _rms(y, gf_ref[...])
        dst[...] = y


def _ffn(x, xs, g, w1, w3, w2, gf, *, tm, tf, final_norm):
    t = x.shape[0]
    ts = xs.shape[0] // (t // tm)
    assert ts * (t // tm) == xs.shape[0] and ts % 16 == 0
    return pl.pallas_call(
        functools.partial(_ffn_kernel, tf=tf, final_norm=final_norm),
        grid=(t // tm,),
        in_specs=[
            pl.BlockSpec((tm, D_MODEL), lambda i: (i, 0)),
            pl.BlockSpec((ts, D_MODEL), lambda i: (i, 0)),
            pl.BlockSpec((1, D_MODEL), lambda i: (0, 0)),
            pl.BlockSpec((1, D_MODEL), lambda i: (0, 0)),
            pl.BlockSpec(memory_space=pl.ANY),
            pl.BlockSpec(memory_space=pl.ANY),
            pl.BlockSpec(memory_space=pl.ANY),
        ],
        out_specs=[pl.BlockSpec((tm, D_MODEL), lambda i: (i, 0)), pl.BlockSpec((ts, D_MODEL), lambda i: (i, 0))],
        out_shape=[jax.ShapeDtypeStruct((t, D_MODEL), F32), jax.ShapeDtypeStruct(xs.shape, F32)],
        scratch_shapes=[
            pltpu.VMEM((tm + ts, D_MODEL), BF16),
            pltpu.VMEM((2, D_MODEL, tf), F32),
            pltpu.VMEM((2, D_MODEL, tf), F32),
            pltpu.VMEM((2, tf, D_MODEL), F32),
            pltpu.SemaphoreType.DMA((3, 2)),
        ],
        compiler_params=_params(("arbitrary",)),
        name="ffn",
    )(x, xs, g, gf, w1, w3, w2)


KV_TILE = COL_K // IN_TILE
assert KV_TILE * IN_TILE == COL_K and COL_SMALL + LANES <= (KV_TILE + 1) * IN_TILE


def _in_tile_segments():
    src = {}
    off = 0
    for name, width in zip(("q", "k", "v", "qi", "ki", "wi", "z", "xbc", "dt"), IN_SPLITS):
        src[name] = (off, width)
        off += width
    runs, out = [], 0
    for name in ("q", "qi", "z", "xbc", "k", "v", "ki", "wi", "dt"):
        runs.append((out, src[name][0], src[name][1]))
        out += src[name][1]
    tiles = []
    for lo in range(0, N_IN, IN_TILE):
        segs = []
        for o, s0, n in runs:
            a, b = max(o, lo), min(o + n, lo + IN_TILE)
            if a < b:
                seg = (a - lo, s0 + a - o, b - a)
                if segs and segs[-1][0] + segs[-1][2] == seg[0] and segs[-1][1] + segs[-1][2] == seg[1]:
                    seg = (segs[-1][0], segs[-1][1], segs[-1][2] + seg[2])
                    segs.pop()
                segs.append(seg)
        assert all(v % 8 == 0 for seg in segs for v in seg)
        tiles.append(segs)
    return tiles


IN_TILE_SEGMENTS = _in_tile_segments()


def _in_proj_kernel(x_ref, xs_ref, g_ref, wt_hbm, o_ref, os_ref, k_ref, v_ref, ks_ref, vs_ref, h_scr, w_buf, sem):
    tm, ts = x_ref.shape[0], xs_ref.shape[0]
    i, j = pl.program_id(0), pl.program_id(1)
    nj = pl.num_programs(1)
    t = i * nj + j
    slot = lax.rem(t, 2)

    def copies(tile, slot):
        return [pltpu.make_async_copy(wt_hbm.at[pl.ds(s0, n), :], w_buf.at[slot, pl.ds(d0, n), :], sem.at[slot])
                for d0, s0, n in IN_TILE_SEGMENTS[tile]]

    def for_tile(tile, fn):
        for static_tile in range(len(IN_TILE_SEGMENTS)):
            pl.when(tile == static_tile)(functools.partial(fn, static_tile))

    def start(tile, slot):
        filled = sum(n for _, _, n in IN_TILE_SEGMENTS[tile])
        if filled < IN_TILE:
            w_buf[slot, filled:IN_TILE, :] = jnp.zeros((IN_TILE - filled, D_MODEL), F32)
        for c in copies(tile, slot):
            c.start()

    def wait(tile, slot):
        for c in copies(tile, slot):
            c.wait()

    @pl.when(t == 0)
    def _():
        start(0, 0)

    @pl.when(t + 1 < pl.num_programs(0) * nj)
    def _():
        for_tile(jnp.where(j + 1 == nj, 0, j + 1), functools.partial(start, slot=1 - slot))

    for_tile(j, functools.partial(wait, slot=slot))

    @pl.when(j == 0)
    def _():
        h_scr[0:tm, :] = _rms(x_ref[...], g_ref[...]).astype(BF16)
        h_scr[tm:, :] = _rms(xs_ref[...], g_ref[...]).astype(BF16)

    y = _dot_nt(h_scr[...], w_buf[slot].astype(BF16))
    o_ref[...] = y[0:tm, :]
    os_ref[...] = y[tm:, :]

    @pl.when(j == KV_TILE)
    def _():
        for r0, n, k_dst, v_dst in ((0, tm, k_ref, v_ref), (tm, ts, ks_ref, vs_ref)):
            for dst, col in ((k_dst, COL_K), (v_dst, COL_V)):
                for g in range(N_KV_HEADS):
                    c0 = col - COL_K + g * HEAD_DIM
                    dst[pl.ds(g, n, stride=N_KV_HEADS), :] = y[r0:r0 + n, c0:c0 + HEAD_DIM]


def _in_proj(x, xs, g, w_t, *, tm):
    t, tiles = x.shape[0], x.shape[0] // tm
    ts = xs.shape[0] // tiles
    assert ts * tiles == xs.shape[0] and ts % 16 == 0

    def group(rows, n):
        kv_spec = pl.BlockSpec((rows * N_KV_HEADS, HEAD_DIM), lambda i, j: (i, 0))
        kv_shape = jax.ShapeDtypeStruct((n * N_KV_HEADS, HEAD_DIM), F32)
        return ([pl.BlockSpec((rows, IN_TILE), lambda i, j: (i, j)), kv_spec, kv_spec],
                [jax.ShapeDtypeStruct((n, N_IN), F32), kv_shape, kv_shape])

    (po, pk, pv), (pos, pks, pvs) = group(tm, t)
    (so, sk, sv), (sos, sks, svs) = group(ts, xs.shape[0])
    proj, proj_s, k, v, k_s, v_s = pl.pallas_call(
        _in_proj_kernel,
        grid=(tiles, N_IN // IN_TILE),
        in_specs=[
            pl.BlockSpec((tm, D_MODEL), lambda i, j: (i, 0)),
            pl.BlockSpec((ts, D_MODEL), lambda i, j: (i, 0)),
            pl.BlockSpec((1, D_MODEL), lambda i, j: (0, 0)),
            pl.BlockSpec(memory_space=pl.ANY),
        ],
        out_specs=[po, so, pk, pv, sk, sv],
        out_shape=[pos, sos, pks, pvs, sks, svs],
        scratch_shapes=[pltpu.VMEM((tm + ts, D_MODEL), BF16), pltpu.VMEM((2, IN_TILE, D_MODEL), F32),
                        pltpu.SemaphoreType.DMA((2,))],
        compiler_params=_params(("arbitrary", "arbitrary")),
        name="in_proj",
    )(x, xs, g, w_t)
    return (proj, k, v), (proj_s, k_s, v_s)


def _out_proj_kernel(x_ref, a_ref, s_ref, xs_ref, as_ref, ss_ref, wa_ref, ws_ref, o_ref, os_ref):
    tm = x_ref.shape[0]
    a = jnp.concatenate([a_ref[...], as_ref[...]], axis=0).astype(BF16)
    s = jnp.concatenate([s_ref[...], ss_ref[...]], axis=0).astype(BF16)
    acc = _dot(a, wa_ref[...])
    acc += _dot(s, ws_ref[...])
    o_ref[...] = x_ref[...] + acc[0:tm, :]
    os_ref[...] = xs_ref[...] + acc[tm:, :]


def _out_proj(x, a, s, xs, a_s, s_s, wa, ws, *, tm):
    t, tiles = x.shape[0], x.shape[0] // tm
    ts = xs.shape[0] // tiles
    assert ts * tiles == xs.shape[0] and ts % 16 == 0

    def rows(n, width):
        return pl.BlockSpec((n, width), lambda i: (i, 0))

    return pl.pallas_call(
        _out_proj_kernel,
        grid=(tiles,),
        in_specs=[
            rows(tm, D_MODEL), rows(tm, D_ATTN), rows(tm, D_SSD),
            rows(ts, D_MODEL), rows(ts, D_ATTN), rows(ts, D_SSD),
            pl.BlockSpec((D_ATTN, D_MODEL), lambda i: (0, 0)),
            pl.BlockSpec((D_SSD, D_MODEL), lambda i: (0, 0)),
        ],
        out_specs=[rows(tm, D_MODEL), rows(ts, D_MODEL)],
        out_shape=[jax.ShapeDtypeStruct((t, D_MODEL), F32), jax.ShapeDtypeStruct(xs.shape, F32)],
        compiler_params=_params(("parallel",)),
        name="out_proj",
    )(x, a, s, xs, a_s, s_s, wa, ws)


NEAR = 2 * LANES


def _bias_kernel(rb_ref, o_ref):
    r = lax.broadcasted_iota(I32, (LANES, NEAR), 0)
    c = lax.broadcasted_iota(I32, (LANES, NEAR), 1)
    n = jnp.maximum(r + LANES - c, 0)
    max_exact = N_BUCKETS // 2
    nf = jnp.maximum(n, 1).astype(F32)
    large = max_exact + (jnp.log(nf / max_exact) / math.log(MAX_DISTANCE / max_exact)
                         * (N_BUCKETS - max_exact)).astype(I32)
    bucket = jnp.where(n < max_exact, n, jnp.minimum(large, N_BUCKETS - 1))
    for h in range(N_HEADS):
        acc = jnp.zeros((LANES, NEAR), F32)
        for b in range(N_BUCKETS):
            acc = jnp.where(bucket == b, rb_ref[b, h], acc)
        o_ref[h] = acc


def _bias_table(rel_bias):
    return pl.pallas_call(
        _bias_kernel,
        in_specs=[pl.BlockSpec(memory_space=pltpu.SMEM)],
        out_specs=pl.BlockSpec(memory_space=pltpu.VMEM),
        out_shape=jax.ShapeDtypeStruct((N_HEADS, LANES, NEAR), F32),
        name="bias_table",
    )(rel_bias)


def _count(mask, axis=1):
    x = mask.astype(F32)
    if axis == 0:
        x = jnp.sum(x.reshape(x.shape[0] // 64, 64, x.shape[1]), axis=0)
    return jnp.sum(x, axis=axis, keepdims=True)


def _kth_largest(count_ge, shape):
    t = jnp.full(shape, INT_MIN, I32)
    t = jnp.where(count_ge(jnp.zeros(shape, I32)) >= TOPK, 0, t)

    def body(it, t):
        cand = t + lax.shift_left(jnp.int32(1), 30 - it)
        return jnp.where(count_ge(cand) >= TOPK, cand, t)

    return lax.fori_loop(0, 31, body, t)


def _tie_limit(count_eq_below, need, shape, idx_bits):
    def body(it, m):
        cand = m + lax.shift_left(jnp.int32(1), idx_bits - 1 - it)
        return jnp.where(count_eq_below(cand) < need, cand, m)

    return lax.fori_loop(0, idx_bits, body, jnp.zeros(shape, I32))


CAUSAL_VARIANTS = 4


def _dsa_prompt_block(width, near_tiles, i, q_ref, qi_ref, k_ref, v_ref, sm_ref, bt_ref, o_ref, sk_scr, neg_scr,
                      kb_scr, vt_scr):
    tq = LANES
    q0 = pl.multiple_of(i * LANES, LANES)
    shape = (1, tq)

    wi_t = sm_ref[pl.ds(q0, tq), :].T[SM_WI:SM_WI + IDX_HEADS, :] * (IDX_HEADS * IDX_DIM) ** -0.5
    head_group = 4
    qi_groups = [
        jnp.concatenate([qi_ref[:, h * IDX_DIM:(h + 1) * IDX_DIM] for h in range(hg, hg + head_group)],
                        axis=0).astype(BF16)
        for hg in range(0, IDX_HEADS, head_group)]
    chunk = 2 * LANES
    pos = lax.broadcasted_iota(I32, (chunk, tq), 1) + q0
    for c in range(0, width, chunk):
        ki = sm_ref[c:c + chunk, SM_KI:SM_KI + IDX_DIM].astype(BF16)
        score = jnp.zeros((chunk, tq), F32)
        for n, hg in enumerate(range(0, IDX_HEADS, head_group)):
            rel = _dot_nt(ki, qi_groups[n])
            for e in range(head_group):
                score = score + wi_t[hg + e:hg + e + 1, :] * jnp.maximum(rel[:, e * tq:(e + 1) * tq], 0.0)
        key_c = lax.broadcasted_iota(I32, (chunk, tq), 0) + c
        sk_scr[c:c + chunk, :] = jnp.where(key_c <= pos, _sort_key(score), INT_MIN)

    key = lax.broadcasted_iota(I32, (width, tq), 0)

    def sk():
        return sk_scr[0:width, :]

    thr = _kth_largest(lambda c: _count(sk() >= c, 0), shape)
    excess = (_count(sk() >= thr, 0) > TOPK) & (thr > INT_MIN)
    any_excess = jnp.max(excess.astype(F32)) > 0.0

    @pl.when(jnp.logical_not(any_excess))
    def _():
        neg_scr[0:width, :] = jnp.where(sk() >= jnp.maximum(thr, INT_MIN + 1), 0.0, -jnp.inf)

    @pl.when(any_excess)
    def _():
        need = TOPK - _count(sk() > thr, 0)
        lim = _tie_limit(lambda m: _count((sk() == thr) & (key < m), 0), need, shape, (width - 1).bit_length())
        take = (sk() > thr) | ((sk() == thr) & (key <= lim) & (sk() > INT_MIN))
        neg_scr[0:width, :] = jnp.where(take, 0.0, -jnp.inf)

    log2e = math.log2(math.e)
    scale = HEAD_DIM ** -0.5 * log2e
    tiles = width // LANES
    for g in range(N_KV_HEADS):
        heads = range(g * GQA_GROUP, (g + 1) * GQA_GROUP)
        kg = kb_scr[g, 0:width, :]
        vg_t = vt_scr[g, :, 0:width]
        qg = jnp.concatenate([q_ref[:, h * HEAD_DIM:(h + 1) * HEAD_DIM] for h in heads], axis=0).astype(BF16)
        diag_t = [((bt_ref[h, :, LANES:NEAR] - bt_ref[h, 0:1, 0:1]) * log2e).T for h in heads]
        prev_t = [((bt_ref[h, :, 0:LANES] - bt_ref[h, 0:1, 0:1]) * log2e).T for h in heads]
        qk = _dot_nt(kg, qg)
        logits = []
        for t in range(tiles):
            rows = slice(t * LANES, (t + 1) * LANES)
            per_head = []
            for e in range(GQA_GROUP):
                lt = qk[rows, e * tq:(e + 1) * tq] * scale + neg_scr[rows, :]
                if t >= tiles - near_tiles:
                    lt = lt + jnp.where(i == t, diag_t[e], jnp.where(i - 1 == t, prev_t[e], 0.0))
                per_head.append(lt)
            logits.append(per_head)
        p_t = []
        inv_l = []
        for e in range(GQA_GROUP):
            m = jnp.max(functools.reduce(jnp.maximum, [lt[e] for lt in logits]), axis=0, keepdims=True)
            probs = [jnp.exp2(lt[e] - m) for lt in logits]
            inv_l.append(1.0 / jnp.sum(functools.reduce(jnp.add, probs), axis=0, keepdims=True))
            p_t.append(jnp.concatenate([p.astype(BF16) for p in probs], axis=0))
        out_t = _dot(vg_t, jnp.concatenate(p_t, axis=1))
        for e, h in enumerate(heads):
            o_ref[:, h * HEAD_DIM:(h + 1) * HEAD_DIM] = (out_t[:, e * tq:(e + 1) * tq] * inv_l[e]).T


def _dsa_prompt_kernel(*refs, seq):
    i = pl.program_id(1)
    per = seq // LANES // CAUSAL_VARIANTS
    k_ref, v_ref = refs[2], refs[3]
    kb_scr, vt_scr = refs[-2], refs[-1]

    @pl.when(i == 0)
    def _():
        for g in range(N_KV_HEADS):
            cols = slice(g * HEAD_DIM, (g + 1) * HEAD_DIM)
            kb_scr[g] = k_ref[:, cols].astype(BF16)
            vt_scr[g] = v_ref[:, cols].T.astype(BF16)

    for v in range(CAUSAL_VARIANTS):
        @pl.when(i // per == v)
        def _(v=v):
            _dsa_prompt_block((v + 1) * per * LANES, per + 1, i, *refs)


def _dsa_prompt(proj, btab, *, batch, seq):
    nq = seq // LANES
    return pl.pallas_call(
        functools.partial(_dsa_prompt_kernel, seq=seq),
        grid=(batch, nq),
        in_specs=[
            pl.BlockSpec((LANES, D_ATTN), lambda b, i: (b * nq + i, COL_Q // D_ATTN)),
            pl.BlockSpec((LANES, IDX_HEADS * IDX_DIM), lambda b, i: (b * nq + i, COL_QI // (IDX_HEADS * IDX_DIM))),
            pl.BlockSpec((seq, D_KV), lambda b, i: (b, COL_K // D_KV)),
            pl.BlockSpec((seq, D_KV), lambda b, i: (b, COL_V // D_KV)),
            pl.BlockSpec((seq, LANES), lambda b, i: (b, COL_SMALL // LANES)),
            pl.BlockSpec((N_HEADS, LANES, NEAR), lambda b, i: (0, 0, 0)),
        ],
        out_specs=pl.BlockSpec((LANES, D_ATTN), lambda b, i: (b * nq + i, 0)),
        out_shape=jax.ShapeDtypeStruct((batch * seq, D_ATTN), F32),
        scratch_shapes=[pltpu.VMEM((seq, LANES), I32), pltpu.VMEM((seq, LANES), F32),
                        pltpu.VMEM((N_KV_HEADS, seq, HEAD_DIM), BF16), pltpu.VMEM((N_KV_HEADS, HEAD_DIM, seq), BF16)],
        compiler_params=_params(("arbitrary", "arbitrary")),
        name="dsa_prompt",
    )(proj, proj, proj, proj, proj, btab)


SCORE_PAGES = 64
ATTN_PAGES = 32


def _sample_queries(qi_ref):
    return jnp.concatenate(
        [qi_ref[:, h * IDX_DIM:(h + 1) * IDX_DIM] for h in range(IDX_HEADS)], axis=0).astype(BF16)


def _sample_scores(rel, wi, nq):
    sc = jnp.zeros((nq, rel.shape[1]), F32)
    for h in range(IDX_HEADS):
        sc = sc + wi[:, h:h + 1] * jnp.maximum(rel[h * nq:(h + 1) * nq, :], 0.0)
    return sc


def _page_copies(pt_ref, pools, bufs, sems, b, s, slot, pages, page_rows):
    copies = []
    for r in range(pages):
        row0 = pl.multiple_of(pt_ref[b, s * pages + r] * page_rows, page_rows)
        for pool, buf, sem in zip(pools, bufs, sems):
            copies.append(pltpu.make_async_copy(pool.at[pl.ds(row0, page_rows), :], buf.at[slot, r], sem.at[slot]))
    return copies


def _page_pipeline(pt_ref, pools, bufs, sems, pages, page_rows):
    b, s = pl.program_id(0), pl.program_id(1)
    steps = pl.num_programs(1)
    t = b * steps + s
    slot = lax.rem(t, 2)
    copies = functools.partial(_page_copies, pt_ref, pools, bufs, sems, pages=pages, page_rows=page_rows)

    @pl.when(t == 0)
    def _():
        for c in copies(b, s, slot):
            c.start()

    @pl.when(t + 1 < pl.num_programs(0) * steps)
    def _():
        wrap = s + 1 == steps
        for c in copies(jnp.where(wrap, b + 1, b), jnp.where(wrap, 0, s + 1), 1 - slot):
            c.start()

    for c in copies(b, s, slot):
        c.wait()
    return slot


def _sample_score_kernel(pt_ref, qi_ref, sm_ref, pool_ref, past_ref, new_ref, pages, sem, kbuf, *, nq):
    s = pl.program_id(1)
    slot = _page_pipeline(pt_ref, [pool_ref], [pages], [sem], SCORE_PAGES, IDX_DIM)
    qx = _sample_queries(qi_ref)
    wi = sm_ref[:, SM_WI:SM_WI + IDX_HEADS] * (IDX_HEADS * IDX_DIM) ** -0.5
    for r in range(SCORE_PAGES):
        kbuf[:, r * PAGE_SIZE:(r + 1) * PAGE_SIZE] = pages[slot, r].astype(BF16)
    past_ref[...] = _sort_key(_sample_scores(_dot(qx, kbuf[...]), wi, nq))

    @pl.when(s == pl.num_programs(1) - 1)
    def _():
        ki_new = jnp.concatenate(
            [sm_ref[:, SM_KI:SM_KI + IDX_DIM], jnp.zeros((LANES - nq, IDX_DIM), F32)], axis=0).astype(BF16)
        sc = _sample_scores(_dot_nt(qx, ki_new), wi, nq)
        j = lax.broadcasted_iota(I32, (nq, LANES), 1)
        t = lax.broadcasted_iota(I32, (nq, LANES), 0)
        new_ref[...] = jnp.where(j <= t, _sort_key(sc), INT_MIN)


def _sample_score(page_table, proj_s, pool_kidx_t, *, batch, nq, n_pages):
    steps = n_pages // SCORE_PAGES
    grid_spec = pltpu.PrefetchScalarGridSpec(
        num_scalar_prefetch=1,
        grid=(batch, steps),
        in_specs=[
            pl.BlockSpec((nq, IDX_HEADS * IDX_DIM), lambda b, s, pt: (b, COL_QI // (IDX_HEADS * IDX_DIM))),
            pl.BlockSpec((nq, LANES), lambda b, s, pt: (b, COL_SMALL // LANES)),
            pl.BlockSpec(memory_space=pl.ANY),
        ],
        out_specs=[
            pl.BlockSpec((None, nq, SCORE_PAGES * PAGE_SIZE), lambda b, s, pt: (b, 0, s)),
            pl.BlockSpec((None, nq, LANES), lambda b, s, pt: (b, 0, 0)),
        ],
        scratch_shapes=[
            pltpu.VMEM((2, SCORE_PAGES, IDX_DIM, PAGE_SIZE), F32),
            pltpu.SemaphoreType.DMA((2,)),
            pltpu.VMEM((IDX_DIM, SCORE_PAGES * PAGE_SIZE), BF16),
        ],
    )
    return pl.pallas_call(
        functools.partial(_sample_score_kernel, nq=nq),
        grid_spec=grid_spec,
        out_shape=[
            jax.ShapeDtypeStruct((batch, nq, n_pages * PAGE_SIZE), I32),
            jax.ShapeDtypeStruct((batch, nq, LANES), I32),
        ],
        compiler_params=_params(("arbitrary", "arbitrary")),
        name="sample_score",
    )(page_table, proj_s, proj_s, pool_kidx_t)


def _sample_select_kernel(past_ref, new_ref, thr_ref, lim_ref, *, past):
    rows = past_ref.shape[0]
    colp = lax.broadcasted_iota(I32, (rows, past), 1)
    coln = lax.broadcasted_iota(I32, (rows, LANES), 1) + past

    def count_ge(c):
        return _count(past_ref[...] >= c) + _count(new_ref[...] >= c)

    thr = _kth_largest(count_ge, (rows, 1))
    thr_ref[...] = jnp.broadcast_to(thr, (rows, LANES))
    lim_ref[...] = jnp.full((rows, LANES), 2 ** 31 - 1, I32)
    any_excess = jnp.max((count_ge(thr) > TOPK).astype(F32)) > 0.0

    @pl.when(any_excess)
    def _():
        need = TOPK - (_count(past_ref[...] > thr) + _count(new_ref[...] > thr))

        def count_eq_below(m):
            return (_count((past_ref[...] == thr) & (colp < m))
                    + _count((new_ref[...] == thr) & (coln < m)))

        lim = _tie_limit(count_eq_below, need, (rows, 1), (past + LANES - 1).bit_length())
        lim_ref[...] = jnp.broadcast_to(lim, (rows, LANES))


def _sample_select(sk_past, sk_new, *, rows_per_step):
    rows, past = sk_past.shape
    return pl.pallas_call(
        functools.partial(_sample_select_kernel, past=past),
        grid=(rows // rows_per_step,),
        in_specs=[
            pl.BlockSpec((rows_per_step, past), lambda i: (i, 0)),
            pl.BlockSpec((rows_per_step, LANES), lambda i: (i, 0)),
        ],
        out_specs=[
            pl.BlockSpec((rows_per_step, LANES), lambda i: (i, 0)),
            pl.BlockSpec((rows_per_step, LANES), lambda i: (i, 0)),
        ],
        out_shape=[jax.ShapeDtypeStruct((rows, LANES), I32)] * 2,
        compiler_params=_params(("parallel",)),
        name="sample_select",
    )(sk_past, sk_new)


def _sample_attn_kernel(pt_ref, q_ref, kn_ref, vn_ref, skp_ref, skn_ref, thr_ref, lim_ref, bt_ref, poolk_ref,
                        poolv_ref, o_ref, kpages, vpages, ksem, vsem, kbuf, vbuf, m_scr, l_scr, acc_scr, *,
                        nq, past):
    slot = _page_pipeline(pt_ref, [poolk_ref, poolv_ref], [kpages, vpages], [ksem, vsem], ATTN_PAGES,
                          PAGE_SIZE * N_KV_HEADS)
    s = pl.program_id(1)
    last = pl.num_programs(1) - 1
    width = ATTN_PAGES * PAGE_SIZE
    grows = GQA_GROUP * nq
    scale = HEAD_DIM ** -0.5

    @pl.when(s == 0)
    def _():
        m_scr[...] = jnp.full_like(m_scr, NEG_BIG)
        l_scr[...] = jnp.zeros_like(l_scr)
        acc_scr[...] = jnp.zeros_like(acc_scr)

    thr = thr_ref[:, 0:1]
    lim = lim_ref[:, 0:1]

    def select(sk, idx):
        sel = (sk > thr) | ((sk == thr) & (idx <= lim))
        return jnp.concatenate([sel] * GQA_GROUP, axis=0)

    def far_bias(g):
        return jnp.concatenate(
            [jnp.broadcast_to(bt_ref[g * GQA_GROUP + hh, 0:1, 0:1], (nq, 1)) for hh in range(GQA_GROUP)], axis=0)

    def near_bias(g, lo, hi):
        return jnp.concatenate([bt_ref[g * GQA_GROUP + hh, 0:nq, lo:hi] for hh in range(GQA_GROUP)], axis=0)

    def group_queries(g):
        return jnp.concatenate(
            [q_ref[:, (g * GQA_GROUP + hh) * HEAD_DIM:(g * GQA_GROUP + hh + 1) * HEAD_DIM]
             for hh in range(GQA_GROUP)], axis=0).astype(BF16)

    def accumulate(g, lg, sel, vals):
        rs = slice(g * grows, (g + 1) * grows)
        m_old = m_scr[rs, :]
        m_new = jnp.maximum(m_old, jnp.max(jnp.where(sel, lg, NEG_BIG), axis=1, keepdims=True))
        p = jnp.where(sel, jnp.exp(lg - m_new), 0.0)
        alpha = jnp.exp(m_old - m_new)
        l_scr[rs, :] = alpha * l_scr[rs, :] + jnp.sum(p, axis=1, keepdims=True)
        acc_scr[rs, :] = alpha * acc_scr[rs, :] + _dot(p.astype(BF16), vals)
        m_scr[rs, :] = m_new

    for r in range(ATTN_PAGES):
        for g in range(N_KV_HEADS):
            rows = pl.ds(g, PAGE_SIZE, stride=N_KV_HEADS)
            kbuf[g, r * PAGE_SIZE:(r + 1) * PAGE_SIZE, :] = kpages[slot, r, rows, :].astype(BF16)
            vbuf[g, r * PAGE_SIZE:(r + 1) * PAGE_SIZE, :] = vpages[slot, r, rows, :].astype(BF16)

    col = lax.broadcasted_iota(I32, (nq, width), 1)
    sel = select(skp_ref[...], col + s * width)
    colg = lax.broadcasted_iota(I32, (grows, width), 1)
    in_near = (s == last) & (colg >= width - PAGE_SIZE)
    for g in range(N_KV_HEADS):
        near = jnp.tile(near_bias(g, 0, LANES), (1, ATTN_PAGES))
        bias = jnp.where(in_near, near, far_bias(g))
        lg = _dot_nt(group_queries(g), kbuf[g]) * scale + bias
        accumulate(g, lg, sel, vbuf[g])

    @pl.when(s == last)
    def _():
        pad = jnp.zeros((LANES - nq, D_KV), F32)
        kn = jnp.concatenate([kn_ref[...], pad], axis=0).astype(BF16)
        vn = jnp.concatenate([vn_ref[...], pad], axis=0).astype(BF16)
        coln = lax.broadcasted_iota(I32, (nq, LANES), 1) + past
        seln = select(skn_ref[...], coln)
        for g in range(N_KV_HEADS):
            cs = slice(g * HEAD_DIM, (g + 1) * HEAD_DIM)
            lg = _dot_nt(group_queries(g), kn[:, cs]) * scale + near_bias(g, LANES, NEAR)
            accumulate(g, lg, seln, vn[:, cs])
        out = acc_scr[...] / l_scr[...]
        for h in range(N_HEADS):
            o_ref[:, h * HEAD_DIM:(h + 1) * HEAD_DIM] = out[h * nq:(h + 1) * nq, :]


def _sample_attn(page_table, proj_s, sk_past, sk_new, thr, lim, btab, pool_k, pool_v, *, batch, nq, n_pages):
    steps = n_pages // ATTN_PAGES
    width = ATTN_PAGES * PAGE_SIZE
    past = n_pages * PAGE_SIZE
    page_buf = pltpu.VMEM((2, ATTN_PAGES, PAGE_SIZE * N_KV_HEADS, HEAD_DIM), F32)
    grid_spec = pltpu.PrefetchScalarGridSpec(
        num_scalar_prefetch=1,
        grid=(batch, steps),
        in_specs=[
            pl.BlockSpec((nq, D_ATTN), lambda b, s, pt: (b, COL_Q // D_ATTN)),
            pl.BlockSpec((nq, D_KV), lambda b, s, pt: (b, COL_K // D_KV)),
            pl.BlockSpec((nq, D_KV), lambda b, s, pt: (b, COL_V // D_KV)),
            pl.BlockSpec((None, nq, width), lambda b, s, pt: (b, 0, s)),
            pl.BlockSpec((None, nq, LANES), lambda b, s, pt: (b, 0, 0)),
            pl.BlockSpec((nq, LANES), lambda b, s, pt: (b, 0)),
            pl.BlockSpec((nq, LANES), lambda b, s, pt: (b, 0)),
            pl.BlockSpec((N_HEADS, LANES, NEAR), lambda b, s, pt: (0, 0, 0)),
            pl.BlockSpec(memory_space=pl.ANY),
            pl.BlockSpec(memory_space=pl.ANY),
        ],
        out_specs=pl.BlockSpec((nq, D_ATTN), lambda b, s, pt: (b, 0)),
        scratch_shapes=[
            page_buf, page_buf, pltpu.SemaphoreType.DMA((2,)), pltpu.SemaphoreType.DMA((2,)),
            pltpu.VMEM((N_KV_HEADS, width, HEAD_DIM), BF16),
            pltpu.VMEM((N_KV_HEADS, width, HEAD_DIM), BF16),
            pltpu.VMEM((N_HEADS * nq, 1), F32),
            pltpu.VMEM((N_HEADS * nq, 1), F32),
            pltpu.VMEM((N_HEADS * nq, HEAD_DIM), F32),
        ],
    )
    return pl.pallas_call(
        functools.partial(_sample_attn_kernel, nq=nq, past=past),
        grid_spec=grid_spec,
        out_shape=jax.ShapeDtypeStruct((batch * nq, D_ATTN), F32),
        compiler_params=_params(("arbitrary", "arbitrary")),
        name="sample_attn",
    )(page_table, proj_s, proj_s, proj_s, sk_past, sk_new, thr, lim, btab, pool_k, pool_v)


TAIL = 8
SHORT_ROWS = 16


def _ssd_kernel(xbc_ref, z_ref, sm_ref, cprev_ref, hprev_ref, cw_ref, cb_ref, alog_ref, dtb_ref, dvec_ref,
                g_ref, y_ref, cnew_ref, hnew_ref, xpad_scr, sm_scr, h_scr, yd_scr, *, rows):
    c = pl.program_id(1)
    q = CHUNK
    r = q if rows == q else SHORT_ROWS
    assert rows <= r
    keep = CONV_WIDTH - 1
    heads_per_group = SSD_HEADS // SSD_GROUPS

    def pad_rows(x):
        return x if r == q else jnp.concatenate([x, jnp.zeros((q - r, x.shape[1]), x.dtype)], axis=0)

    @pl.when(c == 0)
    def _():
        xpad_scr[...] = jnp.zeros_like(xpad_scr)
        sm_scr[...] = jnp.zeros_like(sm_scr)
        xpad_scr[TAIL - keep:TAIL, :] = cprev_ref[0]
        h_scr[...] = hprev_ref[0]

    xpad_scr[TAIL:TAIL + rows, :] = xbc_ref[...]
    sm_scr[0:rows, :] = sm_ref[...]
    conv = sum(xpad_scr[TAIL - keep + k:TAIL - keep + k + r, :] * cw_ref[k:k + 1, :]
               for k in range(CONV_WIDTH)) + cb_ref[...]
    new_tail = xpad_scr[rows + TAIL - keep:rows + TAIL, :]
    cnew_ref[0] = new_tail
    xpad_scr[TAIL - keep:TAIL, :] = new_tail

    xc = conv * jax.nn.sigmoid(conv)
    xs = xc[:, :D_SSD]
    bm = pad_rows(xc[:, D_SSD:D_SSD + SSD_GROUPS * D_STATE]).astype(BF16)
    cm = xc[:, D_SSD + SSD_GROUPS * D_STATE:].astype(BF16)

    ri = lax.broadcasted_iota(I32, (q, q), 0)
    ci = lax.broadcasted_iota(I32, (q, q), 1)
    causal = ri >= ci
    causal_r = causal[0:r, :]
    x = sm_scr[:, SM_DT:SM_DT + SSD_HEADS] + dtb_ref[...]
    dt = jnp.maximum(x, 0.0) + jnp.log1p(jnp.exp(-jnp.abs(x)))
    dt = jnp.where(lax.broadcasted_iota(I32, (q, SSD_HEADS), 0) < rows, dt, 0.0)
    a = -jnp.exp(alog_ref[...])
    acum = _select_dot(causal.astype(BF16), dt * a)
    acum_t = jnp.concatenate([acum, jnp.zeros((q, LANES - SSD_HEADS), F32)], axis=1).T

    expand = (lax.broadcasted_iota(I32, (SSD_HEADS, D_SSD), 1) // SSD_HEAD_DIM
              == lax.broadcasted_iota(I32, (SSD_HEADS, D_SSD), 0)).astype(BF16)
    expand_t = (lax.broadcasted_iota(I32, (D_SSD, SSD_HEADS), 0) // SSD_HEAD_DIM
                == lax.broadcasted_iota(I32, (D_SSD, SSD_HEADS), 1)).astype(BF16)
    acum_x = _select_dot(acum[0:r, :], expand)
    xd = xs * _select_dot(dt[0:r, :], expand)
    xdw_t = pad_rows(xd * jnp.exp(acum_x[r - 1:r, :] - acum_x)).T.astype(BF16)
    xd = pad_rows(xd).astype(BF16)
    state_decay = jnp.exp(_select_dot(expand_t, jnp.broadcast_to(acum_t[0:SSD_HEADS, q - 1:q], (SSD_HEADS, LANES))))

    for g in range(SSD_GROUPS):
        ns = slice(g * D_STATE, (g + 1) * D_STATE)
        hs = slice(g * heads_per_group * SSD_HEAD_DIM, (g + 1) * heads_per_group * SSD_HEAD_DIM)
        cb = _dot_nt(cm[:, ns], bm[:, ns])
        for e in range(heads_per_group):
            h = g * heads_per_group + e
            ps = slice(h * SSD_HEAD_DIM, (h + 1) * SSD_HEAD_DIM)
            seg = acum[0:r, h:h + 1] - acum_t[h:h + 1, :]
            lmat = jnp.exp(jnp.where(causal_r, seg, -jnp.inf))
            yd_scr[0:r, ps] = _dot((cb * lmat).astype(BF16), xd[:, ps])
        h_old = h_scr[hs, :]
        yd_scr[0:r, hs] += _dot_nt(cm[:, ns], h_old.astype(BF16)) * jnp.exp(acum_x[:, hs])
        h_scr[hs, :] = h_old * state_decay[hs, :] + _dot(xdw_t[hs, :], bm[:, ns])

    hnew_ref[0] = h_scr[...]
    y = yd_scr[0:rows, :] + dvec_ref[...] * xs[0:rows, :]
    zz = z_ref[...]
    y = y * (zz * jax.nn.sigmoid(zz))
    width = D_SSD // SSD_GROUPS
    for g in range(SSD_GROUPS):
        cs = slice(g * width, (g + 1) * width)
        y_ref[:, cs] = _rms(y[:, cs], g_ref[:, cs])


def _ssd(proj, conv_prev, ssm_prev, conv_w, conv_b, a_log, dt_bias, d_vec, g_ssd, *, batch, seq):
    rows = min(CHUNK, seq)
    nc = seq // rows
    full = lambda *shape: pl.BlockSpec(shape, lambda b, c: (0,) * len(shape))
    return pl.pallas_call(
        functools.partial(_ssd_kernel, rows=rows),
        grid=(batch, nc),
        in_specs=[
            pl.BlockSpec((rows, CONV_DIM), lambda b, c: (b * nc + c, COL_XBC // CONV_DIM)),
            pl.BlockSpec((rows, D_SSD), lambda b, c: (b * nc + c, COL_Z // D_SSD)),
            pl.BlockSpec((rows, LANES), lambda b, c: (b * nc + c, COL_SMALL // LANES)),
            pl.BlockSpec((1, CONV_WIDTH - 1, CONV_DIM), lambda b, c: (b, 0, 0)),
            pl.BlockSpec((1, D_SSD, D_STATE), lambda b, c: (b, 0, 0)),
            full(CONV_WIDTH, CONV_DIM), full(1, CONV_DIM), full(1, SSD_HEADS), full(1, SSD_HEADS),
            full(1, D_SSD), full(1, D_SSD),
        ],
        out_specs=[
            pl.BlockSpec((rows, D_SSD), lambda b, c: (b * nc + c, 0)),
            pl.BlockSpec((1, CONV_WIDTH - 1, CONV_DIM), lambda b, c: (b, 0, 0)),
            pl.BlockSpec((1, D_SSD, D_STATE), lambda b, c: (b, 0, 0)),
        ],
        out_shape=[
            jax.ShapeDtypeStruct((batch * seq, D_SSD), F32),
            jax.ShapeDtypeStruct((batch, CONV_WIDTH - 1, CONV_DIM), F32),
            jax.ShapeDtypeStruct((batch, D_SSD, D_STATE), F32),
        ],
        scratch_shapes=[
            pltpu.VMEM((CHUNK + TAIL, CONV_DIM), F32),
            pltpu.VMEM((CHUNK, LANES), F32),
            pltpu.VMEM((D_SSD, D_STATE), F32),
            pltpu.VMEM((CHUNK, D_SSD), F32),
        ],
        compiler_params=_params(("parallel", "arbitrary")),
        name="ssd",
    )(proj, proj, proj, conv_prev, ssm_prev, conv_w, conv_b, a_log, dt_bias, d_vec, g_ssd)


def kernel(x_prompt, x_sample, cache_k, cache_v, cache_kidx, state_conv, state_ssm, page_table, rel_bias, g_ffn1, w1_ffn1, w3_ffn1, w2_ffn1, g_mix, w_in, conv_w, conv_b, a_log, dt_bias, d_skip, g_ssd, w_out, g_ffn2, w1_ffn2, w3_ffn2, w2_ffn2, g_final):
    depth = w_in.shape[0]
    assert depth == 1
    l = 0
    bp, seq, _ = x_prompt.shape
    bs, nq, _ = x_sample.shape
    n_pages = page_table.shape[1]
    row = lambda t: t.reshape(1, -1)

    w1a, w3a, w2a = w1_ffn1[l], w3_ffn1[l], w2_ffn1[l]
    w1b, w3b, w2b = w1_ffn2[l], w3_ffn2[l], w2_ffn2[l]
    w_in_t = jnp.swapaxes(w_in[l], 0, 1)
    w_out_a = w_out[l, :D_ATTN].astype(BF16)
    w_out_s = w_out[l, D_ATTN:].astype(BF16)
    d_vec = jnp.repeat(d_skip[l], SSD_HEAD_DIM).reshape(1, D_SSD)
    btab = _bias_table(rel_bias)
    gf = row(g_final)

    tm, tf = 1024, 256

    def mixers(proj, k, v, attend, conv_prev, ssm_prev, batch, length):
        a_out = attend(proj)
        s_out, conv_new, ssm_new = _ssd(
            proj, conv_prev, ssm_prev.reshape(batch, D_SSD, D_STATE), conv_w[l], row(conv_b[l]), row(a_log[l]),
            row(dt_bias[l]), d_vec, row(g_ssd[l]), batch=batch, seq=length)
        k = k.reshape(1, batch, length, N_KV_HEADS, HEAD_DIM)
        v = v.reshape(1, batch, length, N_KV_HEADS, HEAD_DIM)
        ki = proj[:, COL_SMALL + SM_KI:COL_SMALL + SM_KI + IDX_DIM].reshape(1, batch, length, IDX_DIM)
        ssm_new = ssm_new.reshape(1, batch, SSD_HEADS, SSD_HEAD_DIM, D_STATE)
        return a_out, s_out, k, v, ki, conv_new[None], ssm_new

    xp, xs = _ffn(x_prompt.reshape(bp * seq, D_MODEL), x_sample.reshape(bs * nq, D_MODEL), row(g_ffn1[l]),
                  w1a, w3a, w2a, gf, tm=tm, tf=tf, final_norm=False)
    in_p, in_s = _in_proj(xp, xs, row(g_mix[l]), w_in_t, tm=tm)

    a_p, s_p, k_p, v_p, ki_p, conv_p, ssm_p = mixers(
        *in_p, functools.partial(_dsa_prompt, btab=btab, batch=bp, seq=seq),
        jnp.zeros((bp, CONV_WIDTH - 1, CONV_DIM), F32), jnp.zeros((bp, SSD_HEADS, SSD_HEAD_DIM, D_STATE), F32),
        bp, seq)

    pool_k = cache_k[l].reshape(-1, HEAD_DIM)
    pool_v = cache_v[l].reshape(-1, HEAD_DIM)
    pool_kidx = jnp.swapaxes(cache_kidx[l], 1, 2).reshape(-1, PAGE_SIZE)

    def sample_attend(proj):
        sk_past, sk_new = _sample_score(page_table, proj, pool_kidx, batch=bs, nq=nq, n_pages=n_pages)
        thr, lim = _sample_select(sk_past.reshape(bs * nq, -1), sk_new.reshape(bs * nq, LANES),
                                  rows_per_step=128)
        return _sample_attn(page_table, proj, sk_past, sk_new, thr, lim, btab, pool_k, pool_v,
                            batch=bs, nq=nq, n_pages=n_pages)

    a_s, s_s, k_s, v_s, ki_s, conv_s, ssm_s = mixers(*in_s, sample_attend, state_conv[l], state_ssm[l], bs, nq)

    xp, xs = _out_proj(xp, a_p, s_p, xs, a_s, s_s, w_out_a, w_out_s, tm=512)
    y_p, y_s = _ffn(xp, xs, row(g_ffn2[l]), w1b, w3b, w2b, gf, tm=tm, tf=tf, final_norm=True)
    y_p = y_p.reshape(bp, seq, D_MODEL)
    y_s = y_s.reshape(bs, nq, D_MODEL)
    return (y_p, y_s, k_p, v_p, ki_p, conv_p, ssm_p, k_s, v_s, ki_s, conv_s, ssm_s)
```

```python
import functools
import math

import jax
import jax.numpy as jnp
from jax import lax
from jax.experimental import pallas as pl
from jax.experimental.pallas import tpu as pltpu

F32 = jnp.float32
BF16 = jnp.bfloat16
I32 = jnp.int32

D_MODEL = 2048
PAGE_SIZE = 128
N_HEADS = 8
HEAD_DIM = 128
N_KV_HEADS = 2
GQA_GROUP = N_HEADS // N_KV_HEADS
IDX_HEADS = 16
IDX_DIM = 64
TOPK = 256
N_BUCKETS = 32
MAX_DISTANCE = 128
SSD_HEADS = 16
SSD_HEAD_DIM = 64
SSD_GROUPS = 2
D_STATE = 128
CONV_WIDTH = 4
CHUNK = 128
D_ATTN = N_HEADS * HEAD_DIM
D_SSD = SSD_HEADS * SSD_HEAD_DIM
D_KV = N_KV_HEADS * HEAD_DIM
CONV_DIM = D_SSD + 2 * SSD_GROUPS * D_STATE
D_FF = 5632
EPS = 1e-6
IN_SPLITS = (D_ATTN, D_KV, D_KV, IDX_HEADS * IDX_DIM, IDX_DIM, IDX_HEADS, D_SSD, CONV_DIM, SSD_HEADS)

LANES = 128
COL_Q = 0
COL_QI = COL_Q + D_ATTN
COL_Z = COL_QI + IDX_HEADS * IDX_DIM
COL_XBC = COL_Z + D_SSD
COL_K = COL_XBC + CONV_DIM
COL_V = COL_K + D_KV
COL_SMALL = COL_V + D_KV
SM_KI = 0
SM_WI = SM_KI + IDX_DIM
SM_DT = SM_WI + IDX_HEADS
IN_TILE = 768
N_IN = ((COL_SMALL + LANES + IN_TILE - 1) // IN_TILE) * IN_TILE

INT_MIN = -(2 ** 31)
NEG_BIG = -1e30
VMEM_LIMIT = 60000 * 1024


def _params(sem):
    return pltpu.CompilerParams(dimension_semantics=sem, vmem_limit_bytes=VMEM_LIMIT)


def _rms(x, g):
    return x * lax.rsqrt(jnp.mean(x * x, axis=-1, keepdims=True) + EPS) * g


def _dot(a, b):
    return jnp.dot(a, b, preferred_element_type=F32)


def _dot_nt(a, b):
    return lax.dot_general(a, b, (((1,), (1,)), ((), ())), preferred_element_type=F32)


def _split3(x):
    hi = x.astype(BF16)
    rest = x - hi.astype(F32)
    mid = rest.astype(BF16)
    return hi, mid, (rest - mid.astype(F32)).astype(BF16)


def _select_dot(a, b):
    if a.dtype == BF16:
        return sum(_dot(a, part) for part in _split3(b))
    return sum(_dot(part, b) for part in _split3(a))


def _sort_key(x):
    bits = lax.bitcast_convert_type(x + 0.0, I32)
    return bits ^ ((bits >> 31) & 0x7FFFFFFF)


def _ffn_kernel(x_ref, xs_ref, g_ref, gf_ref, w1_hbm, w3_hbm, w2_hbm, o_ref, os_ref, h_scr, w1_buf, w3_buf,
                w2_buf, sem, *, tf, final_norm):
    i = pl.program_id(0)
    nf = D_FF // tf
    total = pl.num_programs(0) * nf
    tm = x_ref.shape[0]

    def copies(f, slot):
        c0 = pl.multiple_of(f * tf, tf)
        return (pltpu.make_async_copy(w1_hbm.at[:, pl.ds(c0, tf)], w1_buf.at[slot], sem.at[0, slot]),
                pltpu.make_async_copy(w3_hbm.at[:, pl.ds(c0, tf)], w3_buf.at[slot], sem.at[1, slot]),
                pltpu.make_async_copy(w2_hbm.at[pl.ds(c0, tf), :], w2_buf.at[slot], sem.at[2, slot]))

    @pl.when(i == 0)
    def _():
        for c in copies(0, 0):
            c.start()

    h_scr[0:tm, :] = _rms(x_ref[...], g_ref[...]).astype(BF16)
    h_scr[tm:, :] = _rms(xs_ref[...], g_ref[...]).astype(BF16)
    o_ref[...] = jnp.zeros_like(o_ref)
    os_ref[...] = jnp.zeros_like(os_ref)

    def body(f, carry):
        t = i * nf + f
        slot = lax.rem(t, 2)

        @pl.when(t + 1 < total)
        def _():
            for c in copies(jnp.where(f + 1 == nf, 0, f + 1), 1 - slot):
                c.start()

        for c in copies(f, slot):
            c.wait()
        h = h_scr[...]
        a = _dot(h, w1_buf[slot].astype(BF16))
        b = _dot(h, w3_buf[slot].astype(BF16))
        u = (a * jax.nn.sigmoid(a) * b).astype(BF16)
        y = _dot(u, w2_buf[slot].astype(BF16))
        o_ref[...] += y[0:tm, :]
        os_ref[...] += y[tm:, :]
        return carry

    lax.fori_loop(0, nf, body, 0)
    for src, dst in ((x_ref, o_ref), (xs_ref, os_ref)):
        y = src[...] + 0.5 * dst[...]
        if final_norm:
            y = _rms(y, gf_ref[...])
        dst[...] = y


def _ffn(x, xs, g, w1, w3, w2, gf, *, tm, tf, final_norm):
    t = x.shape[0]
    ts = xs.shape[0] // (t // tm)
    assert ts * (t // tm) == xs.shape[0] and ts % 16 == 0
    return pl.pallas_call(
        functools.partial(_ffn_kernel, tf=tf, final_norm=final_norm),
        grid=(t // tm,),
        in_specs=[
            pl.BlockSpec((tm, D_MODEL), lambda i: (i, 0)),
            pl.BlockSpec((ts, D_MODEL), lambda i: (i, 0)),
            pl.BlockSpec((1, D_MODEL), lambda i: (0, 0)),
            pl.BlockSpec((1, D_MODEL), lambda i: (0, 0)),
            pl.BlockSpec(memory_space=pl.ANY),
            pl.BlockSpec(memory_space=pl.ANY),
            pl.BlockSpec(memory_space=pl.ANY),
        ],
        out_specs=[pl.BlockSpec((tm, D_MODEL), lambda i: (i, 0)), pl.BlockSpec((ts, D_MODEL), lambda i: (i, 0))],
        out_shape=[jax.ShapeDtypeStruct((t, D_MODEL), F32), jax.ShapeDtypeStruct(xs.shape, F32)],
        scratch_shapes=[
            pltpu.VMEM((tm + ts, D_MODEL), BF16),
            pltpu.VMEM((2, D_MODEL, tf), F32),
            pltpu.VMEM((2, D_MODEL, tf), F32),
            pltpu.VMEM((2, tf, D_MODEL), F32),
            pltpu.SemaphoreType.DMA((3, 2)),
        ],
        compiler_params=_params(("arbitrary",)),
        name="ffn",
    )(x, xs, g, gf, w1, w3, w2)


KV_TILE = COL_K // IN_TILE
assert KV_TILE * IN_TILE == COL_K and COL_SMALL + LANES <= (KV_TILE + 1) * IN_TILE


def _in_tile_segments():
    src = {}
    off = 0
    for name, width in zip(("q", "k", "v", "qi", "ki", "wi", "z", "xbc", "dt"), IN_SPLITS):
        src[name] = (off, width)
        off += width
    runs, out = [], 0
    for name in ("q", "qi", "z", "xbc", "k", "v", "ki", "wi", "dt"):
        runs.append((out, src[name][0], src[name][1]))
        out += src[name][1]
    tiles = []
    for lo in range(0, N_IN, IN_TILE):
        segs = []
        for o, s0, n in runs:
            a, b = max(o, lo), min(o + n, lo + IN_TILE)
            if a < b:
                seg = (a - lo, s0 + a - o, b - a)
                if segs and segs[-1][0] + segs[-1][2] == seg[0] and segs[-1][1] + segs[-1][2] == seg[1]:
                    seg = (segs[-1][0], segs[-1][1], segs[-1][2] + seg[2])
                    segs.pop()
                segs.append(seg)
        assert all(v % 8 == 0 for seg in segs for v in seg)
        tiles.append(segs)
    return tiles


IN_TILE_SEGMENTS = _in_tile_segments()


def _in_proj_kernel(x_ref, xs_ref, g_ref, wt_hbm, o_ref, os_ref, k_ref, v_ref, ks_ref, vs_ref, h_scr, w_buf, sem):
    tm, ts = x_ref.shape[0], xs_ref.shape[0]
    i, j = pl.program_id(0), pl.program_id(1)
    nj = pl.num_programs(1)
    t = i * nj + j
    slot = lax.rem(t, 2)

    def copies(tile, slot):
        return [pltpu.make_async_copy(wt_hbm.at[pl.ds(s0, n), :], w_buf.at[slot, pl.ds(d0, n), :], sem.at[slot])
                for d0, s0, n in IN_TILE_SEGMENTS[tile]]

    def for_tile(tile, fn):
        for static_tile in range(len(IN_TILE_SEGMENTS)):
            pl.when(tile == static_tile)(functools.partial(fn, static_tile))

    def start(tile, slot):
        filled = sum(n for _, _, n in IN_TILE_SEGMENTS[tile])
        if filled < IN_TILE:
            w_buf[slot, filled:IN_TILE, :] = jnp.zeros((IN_TILE - filled, D_MODEL), F32)
        for c in copies(tile, slot):
            c.start()

    def wait(tile, slot):
        for c in copies(tile, slot):
            c.wait()

    @pl.when(t == 0)
    def _():
        start(0, 0)

    @pl.when(t + 1 < pl.num_programs(0) * nj)
    def _():
        for_tile(jnp.where(j + 1 == nj, 0, j + 1), functools.partial(start, slot=1 - slot))

    for_tile(j, functools.partial(wait, slot=slot))

    @pl.when(j == 0)
    def _():
        h_scr[0:tm, :] = _rms(x_ref[...], g_ref[...]).astype(BF16)
        h_scr[tm:, :] = _rms(xs_ref[...], g_ref[...]).astype(BF16)

    y = _dot_nt(h_scr[...], w_buf[slot].astype(BF16))
    o_ref[...] = y[0:tm, :]
    os_ref[...] = y[tm:, :]

    @pl.when(j == KV_TILE)
    def _():
        for r0, n, k_dst, v_dst in ((0, tm, k_ref, v_ref), (tm, ts, ks_ref, vs_ref)):
            for dst, col in ((k_dst, COL_K), (v_dst, COL_V)):
                for g in range(N_KV_HEADS):
                    c0 = col - COL_K + g * HEAD_DIM
                    dst[pl.ds(g, n, stride=N_KV_HEADS), :] = y[r0:r0 + n, c0:c0 + HEAD_DIM]


def _in_proj(x, xs, g, w_t, *, tm):
    t, tiles = x.shape[0], x.shape[0] // tm
    ts = xs.shape[0] // tiles
    assert ts * tiles == xs.shape[0] and ts % 16 == 0

    def group(rows, n):
        kv_spec = pl.BlockSpec((rows * N_KV_HEADS, HEAD_DIM), lambda i, j: (i, 0))
        kv_shape = jax.ShapeDtypeStruct((n * N_KV_HEADS, HEAD_DIM), F32)
        return ([pl.BlockSpec((rows, IN_TILE), lambda i, j: (i, j)), kv_spec, kv_spec],
                [jax.ShapeDtypeStruct((n, N_IN), F32), kv_shape, kv_shape])

    (po, pk, pv), (pos, pks, pvs) = group(tm, t)
    (so, sk, sv), (sos, sks, svs) = group(ts, xs.shape[0])
    proj, proj_s, k, v, k_s, v_s = pl.pallas_call(
        _in_proj_kernel,
        grid=(tiles, N_IN // IN_TILE),
        in_specs=[
            pl.BlockSpec((tm, D_MODEL), lambda i, j: (i, 0)),
            pl.BlockSpec((ts, D_MODEL), lambda i, j: (i, 0)),
            pl.BlockSpec((1, D_MODEL), lambda i, j: (0, 0)),
            pl.BlockSpec(memory_space=pl.ANY),
        ],
        out_specs=[po, so, pk, pv, sk, sv],
        out_shape=[pos, sos, pks, pvs, sks, svs],
        scratch_shapes=[pltpu.VMEM((tm + ts, D_MODEL), BF16), pltpu.VMEM((2, IN_TILE, D_MODEL), F32),
                        pltpu.SemaphoreType.DMA((2,))],
        compiler_params=_params(("arbitrary", "arbitrary")),
        name="in_proj",
    )(x, xs, g, w_t)
    return (proj, k, v), (proj_s, k_s, v_s)


def _out_proj_kernel(x_ref, a_ref, s_ref, xs_ref, as_ref, ss_ref, wa_ref, ws_ref, o_ref, os_ref):
    tm = x_ref.shape[0]
    a = jnp.concatenate([a_ref[...], as_ref[...]], axis=0).astype(BF16)
    s = jnp.concatenate([s_ref[...], ss_ref[...]], axis=0).astype(BF16)
    acc = _dot(a, wa_ref[...])
    acc += _dot(s, ws_ref[...])
    o_ref[...] = x_ref[...] + acc[0:tm, :]
    os_ref[...] = xs_ref[...] + acc[tm:, :]


def _out_proj(x, a, s, xs, a_s, s_s, wa, ws, *, tm):
    t, tiles = x.shape[0], x.shape[0] // tm
    ts = xs.shape[0] // tiles
    assert ts * tiles == xs.shape[0] and ts % 16 == 0

    def rows(n, width):
        return pl.BlockSpec((n, width), lambda i: (i, 0))

    return pl.pallas_call(
        _out_proj_kernel,
        grid=(tiles,),
        in_specs=[
            rows(tm, D_MODEL), rows(tm, D_ATTN), rows(tm, D_SSD),
            rows(ts, D_MODEL), rows(ts, D_ATTN), rows(ts, D_SSD),
            pl.BlockSpec((D_ATTN, D_MODEL), lambda i: (0, 0)),
            pl.BlockSpec((D_SSD, D_MODEL), lambda i: (0, 0)),
        ],
        out_specs=[rows(tm, D_MODEL), rows(ts, D_MODEL)],
        out_shape=[jax.ShapeDtypeStruct((t, D_MODEL), F32), jax.ShapeDtypeStruct(xs.shape, F32)],
        compiler_params=_params(("parallel",)),
        name="out_proj",
    )(x, a, s, xs, a_s, s_s, wa, ws)


NEAR = 2 * LANES


def _bias_kernel(rb_ref, o_ref):
    r = lax.broadcasted_iota(I32, (LANES, NEAR), 0)
    c = lax.broadcasted_iota(I32, (LANES, NEAR), 1)
    n = jnp.maximum(r + LANES - c, 0)
    max_exact = N_BUCKETS // 2
    nf = jnp.maximum(n, 1).astype(F32)
    large = max_exact + (jnp.log(nf / max_exact) / math.log(MAX_DISTANCE / max_exact)
                         * (N_BUCKETS - max_exact)).astype(I32)
    bucket = jnp.where(n < max_exact, n, jnp.minimum(large, N_BUCKETS - 1))
    for h in range(N_HEADS):
        acc = jnp.zeros((LANES, NEAR), F32)
        for b in range(N_BUCKETS):
            acc = jnp.where(bucket == b, rb_ref[b, h], acc)
        o_ref[h] = acc


def _bias_table(rel_bias):
    return pl.pallas_call(
        _bias_kernel,
        in_specs=[pl.BlockSpec(memory_space=pltpu.SMEM)],
        out_specs=pl.BlockSpec(memory_space=pltpu.VMEM),
        out_shape=jax.ShapeDtypeStruct((N_HEADS, LANES, NEAR), F32),
        name="bias_table",
    )(rel_bias)


def _count(mask, axis=1):
    x = mask.astype(F32)
    if axis == 0:
        x = jnp.sum(x.reshape(x.shape[0] // 64, 64, x.shape[1]), axis=0)
    return jnp.sum(x, axis=axis, keepdims=True)


def _kth_largest(count_ge, shape, optional_bits=0):
    t = jnp.full(shape, INT_MIN, I32)
    held = jnp.full(shape, 2.0 ** 30, F32)
    first = count_ge(jnp.zeros(shape, I32))
    t = jnp.where(first >= TOPK, 0, t)
    held = jnp.where(first >= TOPK, first, held)

    def body(it, carry):
        t, held = carry
        cand = t + lax.shift_left(jnp.int32(1), 30 - it)
        count = count_ge(cand)
        return jnp.where(count >= TOPK, cand, t), jnp.where(count >= TOPK, count, held)

    split = 31 - optional_bits
    carry = lax.fori_loop(0, split, body, (t, held))
    if optional_bits:
        settled = jnp.min((carry[1] == TOPK).astype(F32)) > 0.0
        carry = lax.cond(settled, lambda c: c, lambda c: lax.fori_loop(split, 31, body, c), carry)
    return carry[0]


def _tie_limit(count_eq_below, need, shape, idx_bits):
    def body(it, m):
        cand = m + lax.shift_left(jnp.int32(1), idx_bits - 1 - it)
        return jnp.where(count_eq_below(cand) < need, cand, m)

    return lax.fori_loop(0, idx_bits, body, jnp.zeros(shape, I32))


CAUSAL_VARIANTS = 4


def _dsa_prompt_block(width, near_tiles, i, q_ref, qi_ref, k_ref, v_ref, sm_ref, bt_ref, o_ref, sk_scr, neg_scr,
                      kb_scr, vt_scr):
    tq = LANES
    q0 = pl.multiple_of(i * LANES, LANES)
    shape = (1, tq)

    wi_t = sm_ref[pl.ds(q0, tq), :].T[SM_WI:SM_WI + IDX_HEADS, :] * (IDX_HEADS * IDX_DIM) ** -0.5
    head_group = 4
    qi_groups = [
        jnp.concatenate([qi_ref[:, h * IDX_DIM:(h + 1) * IDX_DIM] for h in range(hg, hg + head_group)],
                        axis=0).astype(BF16)
        for hg in range(0, IDX_HEADS, head_group)]
    chunk = 2 * LANES
    pos = lax.broadcasted_iota(I32, (chunk, tq), 1) + q0
    for c in range(0, width, chunk):
        ki = sm_ref[c:c + chunk, SM_KI:SM_KI + IDX_DIM].astype(BF16)
        score = jnp.zeros((chunk, tq), F32)
        for n, hg in enumerate(range(0, IDX_HEADS, head_group)):
            rel = _dot_nt(ki, qi_groups[n])
            for e in range(head_group):
                score = score + wi_t[hg + e:hg + e + 1, :] * jnp.maximum(rel[:, e * tq:(e + 1) * tq], 0.0)
        key_c = lax.broadcasted_iota(I32, (chunk, tq), 0) + c
        sk_scr[c:c + chunk, :] = jnp.where(key_c <= pos, _sort_key(score), INT_MIN)

    key = lax.broadcasted_iota(I32, (width, tq), 0)

    def sk():
        return sk_scr[0:width, :]

    thr = _kth_largest(lambda c: _count(sk() >= c, 0), shape, optional_bits=5)
    excess = (_count(sk() >= thr, 0) > TOPK) & (thr > INT_MIN)
    any_excess = jnp.max(excess.astype(F32)) > 0.0

    @pl.when(jnp.logical_not(any_excess))
    def _():
        neg_scr[0:width, :] = jnp.where(sk() >= jnp.maximum(thr, INT_MIN + 1), 0.0, -jnp.inf)

    @pl.when(any_excess)
    def _():
        need = TOPK - _count(sk() > thr, 0)
        lim = _tie_limit(lambda m: _count((sk() == thr) & (key < m), 0), need, shape, (width - 1).bit_length())
        take = (sk() > thr) | ((sk() == thr) & (key <= lim) & (sk() > INT_MIN))
        neg_scr[0:width, :] = jnp.where(take, 0.0, -jnp.inf)

    log2e = math.log2(math.e)
    scale = HEAD_DIM ** -0.5 * log2e
    tiles = width // LANES
    for g in range(N_KV_HEADS):
        heads = range(g * GQA_GROUP, (g + 1) * GQA_GROUP)
        kg = kb_scr[g, 0:width, :]
        vg_t = vt_scr[g, :, 0:width]
        qg = jnp.concatenate([q_ref[:, h * HEAD_DIM:(h + 1) * HEAD_DIM] for h in heads], axis=0).astype(BF16)
        diag_t = [((bt_ref[h, :, LANES:NEAR] - bt_ref[h, 0:1, 0:1]) * log2e).T for h in heads]
        prev_t = [((bt_ref[h, :, 0:LANES] - bt_ref[h, 0:1, 0:1]) * log2e).T for h in heads]
        qk = _dot_nt(kg, qg)
        logits = []
        for t in range(tiles):
            rows = slice(t * LANES, (t + 1) * LANES)
            per_head = []
            for e in range(GQA_GROUP):
                lt = qk[rows, e * tq:(e + 1) * tq] * scale + neg_scr[rows, :]
                if t >= tiles - near_tiles:
                    lt = lt + jnp.where(i == t, diag_t[e], jnp.where(i - 1 == t, prev_t[e], 0.0))
                per_head.append(lt)
            logits.append(per_head)
        p_t = []
        inv_l = []
        for e in range(GQA_GROUP):
            m = jnp.max(functools.reduce(jnp.maximum, [lt[e] for lt in logits]), axis=0, keepdims=True)
            probs = [jnp.exp2(lt[e] - m) for lt in logits]
            inv_l.append(1.0 / jnp.sum(functools.reduce(jnp.add, probs), axis=0, keepdims=True))
            p_t.append(jnp.concatenate([p.astype(BF16) for p in probs], axis=0))
        out_t = _dot(vg_t, jnp.concatenate(p_t, axis=1))
        for e, h in enumerate(heads):
            o_ref[:, h * HEAD_DIM:(h + 1) * HEAD_DIM] = (out_t[:, e * tq:(e + 1) * tq] * inv_l[e]).T


def _dsa_prompt_kernel(*refs, seq):
    i = pl.program_id(1)
    per = seq // LANES // CAUSAL_VARIANTS
    k_ref, v_ref = refs[2], refs[3]
    kb_scr, vt_scr = refs[-2], refs[-1]

    @pl.when(i == 0)
    def _():
        for g in range(N_KV_HEADS):
            cols = slice(g * HEAD_DIM, (g + 1) * HEAD_DIM)
            kb_scr[g] = k_ref[:, cols].astype(BF16)
            vt_scr[g] = v_ref[:, cols].T.astype(BF16)

    for v in range(CAUSAL_VARIANTS):
        @pl.when(i // per == v)
        def _(v=v):
            _dsa_prompt_block((v + 1) * per * LANES, per + 1, i, *refs)


def _dsa_prompt(proj, btab, *, batch, seq):
    nq = seq // LANES
    return pl.pallas_call(
        functools.partial(_dsa_prompt_kernel, seq=seq),
        grid=(batch, nq),
        in_specs=[
            pl.BlockSpec((LANES, D_ATTN), lambda b, i: (b * nq + i, COL_Q // D_ATTN)),
            pl.BlockSpec((LANES, IDX_HEADS * IDX_DIM), lambda b, i: (b * nq + i, COL_QI // (IDX_HEADS * IDX_DIM))),
            pl.BlockSpec((seq, D_KV), lambda b, i: (b, COL_K // D_KV)),
            pl.BlockSpec((seq, D_KV), lambda b, i: (b, COL_V // D_KV)),
            pl.BlockSpec((seq, LANES), lambda b, i: (b, COL_SMALL // LANES)),
            pl.BlockSpec((N_HEADS, LANES, NEAR), lambda b, i: (0, 0, 0)),
        ],
        out_specs=pl.BlockSpec((LANES, D_ATTN), lambda b, i: (b * nq + i, 0)),
        out_shape=jax.ShapeDtypeStruct((batch * seq, D_ATTN), F32),
        scratch_shapes=[pltpu.VMEM((seq, LANES), I32), pltpu.VMEM((seq, LANES), F32),
                        pltpu.VMEM((N_KV_HEADS, seq, HEAD_DIM), BF16), pltpu.VMEM((N_KV_HEADS, HEAD_DIM, seq), BF16)],
        compiler_params=_params(("arbitrary", "arbitrary")),
        name="dsa_prompt",
    )(proj, proj, proj, proj, proj, btab)


SCORE_PAGES = 64
ATTN_PAGES = 32


def _sample_queries(qi_ref):
    return jnp.concatenate(
        [qi_ref[:, h * IDX_DIM:(h + 1) * IDX_DIM] for h in range(IDX_HEADS)], axis=0).astype(BF16)


def _sample_scores(rel, wi, nq):
    sc = jnp.zeros((nq, rel.shape[1]), F32)
    for h in range(IDX_HEADS):
        sc = sc + wi[:, h:h + 1] * jnp.maximum(rel[h * nq:(h + 1) * nq, :], 0.0)
    return sc


def _page_copies(pt_ref, pools, bufs, sems, b, s, slot, pages, page_rows):
    copies = []
    for r in range(pages):
        row0 = pl.multiple_of(pt_ref[b, s * pages + r] * page_rows, page_rows)
        for pool, buf, sem in zip(pools, bufs, sems):
            copies.append(pltpu.make_async_copy(pool.at[pl.ds(row0, page_rows), :], buf.at[slot, r], sem.at[slot]))
    return copies


def _page_pipeline(pt_ref, pools, bufs, sems, pages, page_rows):
    b, s = pl.program_id(0), pl.program_id(1)
    steps = pl.num_programs(1)
    t = b * steps + s
    slot = lax.rem(t, 2)
    copies = functools.partial(_page_copies, pt_ref, pools, bufs, sems, pages=pages, page_rows=page_rows)

    @pl.when(t == 0)
    def _():
        for c in copies(b, s, slot):
            c.start()

    @pl.when(t + 1 < pl.num_programs(0) * steps)
    def _():
        wrap = s + 1 == steps
        for c in copies(jnp.where(wrap, b + 1, b), jnp.where(wrap, 0, s + 1), 1 - slot):
            c.start()

    for c in copies(b, s, slot):
        c.wait()
    return slot


def _sample_score_kernel(pt_ref, qi_ref, sm_ref, pool_ref, past_ref, new_ref, pages, sem, kbuf, *, nq):
    s = pl.program_id(1)
    slot = _page_pipeline(pt_ref, [pool_ref], [pages], [sem], SCORE_PAGES, IDX_DIM)
    qx = _sample_queries(qi_ref)
    wi = sm_ref[:, SM_WI:SM_WI + IDX_HEADS] * (IDX_HEADS * IDX_DIM) ** -0.5
    for r in range(SCORE_PAGES):
        kbuf[:, r * PAGE_SIZE:(r + 1) * PAGE_SIZE] = pages[slot, r].astype(BF16)
    past_ref[...] = _sort_key(_sample_scores(_dot(qx, kbuf[...]), wi, nq))

    @pl.when(s == pl.num_programs(1) - 1)
    def _():
        ki_new = jnp.concatenate(
            [sm_ref[:, SM_KI:SM_KI + IDX_DIM], jnp.zeros((LANES - nq, IDX_DIM), F32)], axis=0).astype(BF16)
        sc = _sample_scores(_dot_nt(qx, ki_new), wi, nq)
        j = lax.broadcasted_iota(I32, (nq, LANES), 1)
        t = lax.broadcasted_iota(I32, (nq, LANES), 0)
        new_ref[...] = jnp.where(j <= t, _sort_key(sc), INT_MIN)


def _sample_score(page_table, proj_s, pool_kidx_t, *, batch, nq, n_pages):
    steps = n_pages // SCORE_PAGES
    grid_spec = pltpu.PrefetchScalarGridSpec(
        num_scalar_prefetch=1,
        grid=(batch, steps),
        in_specs=[
            pl.BlockSpec((nq, IDX_HEADS * IDX_DIM), lambda b, s, pt: (b, COL_QI // (IDX_HEADS * IDX_DIM))),
            pl.BlockSpec((nq, LANES), lambda b, s, pt: (b, COL_SMALL // LANES)),
            pl.BlockSpec(memory_space=pl.ANY),
        ],
        out_specs=[
            pl.BlockSpec((None, nq, SCORE_PAGES * PAGE_SIZE), lambda b, s, pt: (b, 0, s)),
            pl.BlockSpec((None, nq, LANES), lambda b, s, pt: (b, 0, 0)),
        ],
        scratch_shapes=[
            pltpu.VMEM((2, SCORE_PAGES, IDX_DIM, PAGE_SIZE), F32),
            pltpu.SemaphoreType.DMA((2,)),
            pltpu.VMEM((IDX_DIM, SCORE_PAGES * PAGE_SIZE), BF16),
        ],
    )
    return pl.pallas_call(
        functools.partial(_sample_score_kernel, nq=nq),
        grid_spec=grid_spec,
        out_shape=[
            jax.ShapeDtypeStruct((batch, nq, n_pages * PAGE_SIZE), I32),
            jax.ShapeDtypeStruct((batch, nq, LANES), I32),
        ],
        compiler_params=_params(("arbitrary", "arbitrary")),
        name="sample_score",
    )(page_table, proj_s, proj_s, pool_kidx_t)


def _sample_select_kernel(past_ref, new_ref, thr_ref, lim_ref, *, past):
    rows = past_ref.shape[0]
    colp = lax.broadcasted_iota(I32, (rows, past), 1)
    coln = lax.broadcasted_iota(I32, (rows, LANES), 1) + past

    def count_ge(c):
        return _count(past_ref[...] >= c) + _count(new_ref[...] >= c)

    thr = _kth_largest(count_ge, (rows, 1))
    thr_ref[...] = jnp.broadcast_to(thr, (rows, LANES))
    lim_ref[...] = jnp.full((rows, LANES), 2 ** 31 - 1, I32)
    any_excess = jnp.max((count_ge(thr) > TOPK).astype(F32)) > 0.0

    @pl.when(any_excess)
    def _():
        need = TOPK - (_count(past_ref[...] > thr) + _count(new_ref[...] > thr))

        def count_eq_below(m):
            return (_count((past_ref[...] == thr) & (colp < m))
                    + _count((new_ref[...] == thr) & (coln < m)))

        lim = _tie_limit(count_eq_below, need, (rows, 1), (past + LANES - 1).bit_length())
        lim_ref[...] = jnp.broadcast_to(lim, (rows, LANES))


def _sample_select(sk_past, sk_new, *, rows_per_step):
    rows, past = sk_past.shape
    return pl.pallas_call(
        functools.partial(_sample_select_kernel, past=past),
        grid=(rows // rows_per_step,),
        in_specs=[
            pl.BlockSpec((rows_per_step, past), lambda i: (i, 0)),
            pl.BlockSpec((rows_per_step, LANES), lambda i: (i, 0)),
        ],
        out_specs=[
            pl.BlockSpec((rows_per_step, LANES), lambda i: (i, 0)),
            pl.BlockSpec((rows_per_step, LANES), lambda i: (i, 0)),
        ],
        out_shape=[jax.ShapeDtypeStruct((rows, LANES), I32)] * 2,
        compiler_params=_params(("parallel",)),
        name="sample_select",
    )(sk_past, sk_new)


def _sample_attn_kernel(pt_ref, q_ref, kn_ref, vn_ref, skp_ref, skn_ref, thr_ref, lim_ref, bt_ref, poolk_ref,
                        poolv_ref, o_ref, kpages, vpages, ksem, vsem, kbuf, vbuf, m_scr, l_scr, acc_scr, *,
                        nq, past):
    slot = _page_pipeline(pt_ref, [poolk_ref, poolv_ref], [kpages, vpages], [ksem, vsem], ATTN_PAGES,
                          PAGE_SIZE * N_KV_HEADS)
    s = pl.program_id(1)
    last = pl.num_programs(1) - 1
    width = ATTN_PAGES * PAGE_SIZE
    grows = GQA_GROUP * nq
    scale = HEAD_DIM ** -0.5

    @pl.when(s == 0)
    def _():
        m_scr[...] = jnp.full_like(m_scr, NEG_BIG)
        l_scr[...] = jnp.zeros_like(l_scr)
        acc_scr[...] = jnp.zeros_like(acc_scr)

    thr = thr_ref[:, 0:1]
    lim = lim_ref[:, 0:1]

    def select(sk, idx):
        sel = (sk > thr) | ((sk == thr) & (idx <= lim))
        return jnp.concatenate([sel] * GQA_GROUP, axis=0)

    def far_bias(g):
        return jnp.concatenate(
            [jnp.broadcast_to(bt_ref[g * GQA_GROUP + hh, 0:1, 0:1], (nq, 1)) for hh in range(GQA_GROUP)], axis=0)

    def near_bias(g, lo, hi):
        return jnp.concatenate([bt_ref[g * GQA_GROUP + hh, 0:nq, lo:hi] for hh in range(GQA_GROUP)], axis=0)

    def group_queries(g):
        return jnp.concatenate(
            [q_ref[:, (g * GQA_GROUP + hh) * HEAD_DIM:(g * GQA_GROUP + hh + 1) * HEAD_DIM]
             for hh in range(GQA_GROUP)], axis=0).astype(BF16)

    def accumulate(g, lg, sel, vals):
        rs = slice(g * grows, (g + 1) * grows)
        m_old = m_scr[rs, :]
        m_new = jnp.maximum(m_old, jnp.max(jnp.where(sel, lg, NEG_BIG), axis=1, keepdims=True))
        p = jnp.where(sel, jnp.exp(lg - m_new), 0.0)
        alpha = jnp.exp(m_old - m_new)
        l_scr[rs, :] = alpha * l_scr[rs, :] + jnp.sum(p, axis=1, keepdims=True)
        acc_scr[rs, :] = alpha * acc_scr[rs, :] + _dot(p.astype(BF16), vals)
        m_scr[rs, :] = m_new

    for r in range(ATTN_PAGES):
        for g in range(N_KV_HEADS):
            rows = pl.ds(g, PAGE_SIZE, stride=N_KV_HEADS)
            kbuf[g, r * PAGE_SIZE:(r + 1) * PAGE_SIZE, :] = kpages[slot, r, rows, :].astype(BF16)
            vbuf[g, r * PAGE_SIZE:(r + 1) * PAGE_SIZE, :] = vpages[slot, r, rows, :].astype(BF16)

    col = lax.broadcasted_iota(I32, (nq, width), 1)
    sel = select(skp_ref[...], col + s * width)
    colg = lax.broadcasted_iota(I32, (grows, width), 1)
    in_near = (s == last) & (colg >= width - PAGE_SIZE)
    for g in range(N_KV_HEADS):
        near = jnp.tile(near_bias(g, 0, LANES), (1, ATTN_PAGES))
        bias = jnp.where(in_near, near, far_bias(g))
        lg = _dot_nt(group_queries(g), kbuf[g]) * scale + bias
        accumulate(g, lg, sel, vbuf[g])

    @pl.when(s == last)
    def _():
        pad = jnp.zeros((LANES - nq, D_KV), F32)
        kn = jnp.concatenate([kn_ref[...], pad], axis=0).astype(BF16)
        vn = jnp.concatenate([vn_ref[...], pad], axis=0).astype(BF16)
        coln = lax.broadcasted_iota(I32, (nq, LANES), 1) + past
        seln = select(skn_ref[...], coln)
        for g in range(N_KV_HEADS):
            cs = slice(g * HEAD_DIM, (g + 1) * HEAD_DIM)
            lg = _dot_nt(group_queries(g), kn[:, cs]) * scale + near_bias(g, LANES, NEAR)
            accumulate(g, lg, seln, vn[:, cs])
        out = acc_scr[...] / l_scr[...]
        for h in range(N_HEADS):
            o_ref[:, h * HEAD_DIM:(h + 1) * HEAD_DIM] = out[h * nq:(h + 1) * nq, :]


def _sample_attn(page_table, proj_s, sk_past, sk_new, thr, lim, btab, pool_k, pool_v, *, batch, nq, n_pages):
    steps = n_pages // ATTN_PAGES
    width = ATTN_PAGES * PAGE_SIZE
    past = n_pages * PAGE_SIZE
    page_buf = pltpu.VMEM((2, ATTN_PAGES, PAGE_SIZE * N_KV_HEADS, HEAD_DIM), F32)
    grid_spec = pltpu.PrefetchScalarGridSpec(
        num_scalar_prefetch=1,
        grid=(batch, steps),
        in_specs=[
            pl.BlockSpec((nq, D_ATTN), lambda b, s, pt: (b, COL_Q // D_ATTN)),
            pl.BlockSpec((nq, D_KV), lambda b, s, pt: (b, COL_K // D_KV)),
            pl.BlockSpec((nq, D_KV), lambda b, s, pt: (b, COL_V // D_KV)),
            pl.BlockSpec((None, nq, width), lambda b, s, pt: (b, 0, s)),
            pl.BlockSpec((None, nq, LANES), lambda b, s, pt: (b, 0, 0)),
            pl.BlockSpec((nq, LANES), lambda b, s, pt: (b, 0)),
            pl.BlockSpec((nq, LANES), lambda b, s, pt: (b, 0)),
            pl.BlockSpec((N_HEADS, LANES, NEAR), lambda b, s, pt: (0, 0, 0)),
            pl.BlockSpec(memory_space=pl.ANY),
            pl.BlockSpec(memory_space=pl.ANY),
        ],
        out_specs=pl.BlockSpec((nq, D_ATTN), lambda b, s, pt: (b, 0)),
        scratch_shapes=[
            page_buf, page_buf, pltpu.SemaphoreType.DMA((2,)), pltpu.SemaphoreType.DMA((2,)),
            pltpu.VMEM((N_KV_HEADS, width, HEAD_DIM), BF16),
            pltpu.VMEM((N_KV_HEADS, width, HEAD_DIM), BF16),
            pltpu.VMEM((N_HEADS * nq, 1), F32),
            pltpu.VMEM((N_HEADS * nq, 1), F32),
            pltpu.VMEM((N_HEADS * nq, HEAD_DIM), F32),
        ],
    )
    return pl.pallas_call(
        functools.partial(_sample_attn_kernel, nq=nq, past=past),
        grid_spec=grid_spec,
        out_shape=jax.ShapeDtypeStruct((batch * nq, D_ATTN), F32),
        compiler_params=_params(("arbitrary", "arbitrary")),
        name="sample_attn",
    )(page_table, proj_s, proj_s, proj_s, sk_past, sk_new, thr, lim, btab, pool_k, pool_v)


TAIL = 8
SHORT_ROWS = 16


def _ssd_kernel(xbc_ref, z_ref, sm_ref, cprev_ref, hprev_ref, cw_ref, cb_ref, alog_ref, dtb_ref, dvec_ref,
                g_ref, y_ref, cnew_ref, hnew_ref, xpad_scr, sm_scr, h_scr, yd_scr, *, rows):
    c = pl.program_id(1)
    q = CHUNK
    r = q if rows == q else SHORT_ROWS
    assert rows <= r
    keep = CONV_WIDTH - 1
    heads_per_group = SSD_HEADS // SSD_GROUPS

    def pad_rows(x):
        return x if r == q else jnp.concatenate([x, jnp.zeros((q - r, x.shape[1]), x.dtype)], axis=0)

    @pl.when(c == 0)
    def _():
        xpad_scr[...] = jnp.zeros_like(xpad_scr)
        sm_scr[...] = jnp.zeros_like(sm_scr)
        xpad_scr[TAIL - keep:TAIL, :] = cprev_ref[0]
        h_scr[...] = hprev_ref[0]

    xpad_scr[TAIL:TAIL + rows, :] = xbc_ref[...]
    sm_scr[0:rows, :] = sm_ref[...]
    conv = sum(xpad_scr[TAIL - keep + k:TAIL - keep + k + r, :] * cw_ref[k:k + 1, :]
               for k in range(CONV_WIDTH)) + cb_ref[...]
    new_tail = xpad_scr[rows + TAIL - keep:rows + TAIL, :]
    cnew_ref[0] = new_tail
    xpad_scr[TAIL - keep:TAIL, :] = new_tail

    xc = conv * jax.nn.sigmoid(conv)
    xs = xc[:, :D_SSD]
    bm = pad_rows(xc[:, D_SSD:D_SSD + SSD_GROUPS * D_STATE]).astype(BF16)
    cm = xc[:, D_SSD + SSD_GROUPS * D_STATE:].astype(BF16)

    ri = lax.broadcasted_iota(I32, (q, q), 0)
    ci = lax.broadcasted_iota(I32, (q, q), 1)
    causal = ri >= ci
    causal_r = causal[0:r, :]
    x = sm_scr[:, SM_DT:SM_DT + SSD_HEADS] + dtb_ref[...]
    dt = jnp.maximum(x, 0.0) + jnp.log1p(jnp.exp(-jnp.abs(x)))
    dt = jnp.where(lax.broadcasted_iota(I32, (q, SSD_HEADS), 0) < rows, dt, 0.0)
    a = -jnp.exp(alog_ref[...])
    acum = _select_dot(causal.astype(BF16), dt * a)
    acum_t = jnp.concatenate([acum, jnp.zeros((q, LANES - SSD_HEADS), F32)], axis=1).T

    expand = (lax.broadcasted_iota(I32, (SSD_HEADS, D_SSD), 1) // SSD_HEAD_DIM
              == lax.broadcasted_iota(I32, (SSD_HEADS, D_SSD), 0)).astype(BF16)
    expand_t = (lax.broadcasted_iota(I32, (D_SSD, SSD_HEADS), 0) // SSD_HEAD_DIM
                == lax.broadcasted_iota(I32, (D_SSD, SSD_HEADS), 1)).astype(BF16)
    acum_x = _select_dot(acum[0:r, :], expand)
    xd = xs * _select_dot(dt[0:r, :], expand)
    xdw_t = pad_rows(xd * jnp.exp(acum_x[r - 1:r, :] - acum_x)).T.astype(BF16)
    xd = pad_rows(xd).astype(BF16)
    state_decay = jnp.exp(_select_dot(expand_t, jnp.broadcast_to(acum_t[0:SSD_HEADS, q - 1:q], (SSD_HEADS, LANES))))

    for g in range(SSD_GROUPS):
        ns = slice(g * D_STATE, (g + 1) * D_STATE)
        hs = slice(g * heads_per_group * SSD_HEAD_DIM, (g + 1) * heads_per_group * SSD_HEAD_DIM)
        cb = _dot_nt(cm[:, ns], bm[:, ns])
        for e in range(heads_per_group):
            h = g * heads_per_group + e
            ps = slice(h * SSD_HEAD_DIM, (h + 1) * SSD_HEAD_DIM)
            seg = acum[0:r, h:h + 1] - acum_t[h:h + 1, :]
            lmat = jnp.exp(jnp.where(causal_r, seg, -jnp.inf))
            yd_scr[0:r, ps] = _dot((cb * lmat).astype(BF16), xd[:, ps])
        h_old = h_scr[hs, :]
        yd_scr[0:r, hs] += _dot_nt(cm[:, ns], h_old.astype(BF16)) * jnp.exp(acum_x[:, hs])
        h_scr[hs, :] = h_old * state_decay[hs, :] + _dot(xdw_t[hs, :], bm[:, ns])

    hnew_ref[0] = h_scr[...]
    y = yd_scr[0:rows, :] + dvec_ref[...] * xs[0:rows, :]
    zz = z_ref[...]
    y = y * (zz * jax.nn.sigmoid(zz))
    width = D_SSD // SSD_GROUPS
    for g in range(SSD_GROUPS):
        cs = slice(g * width, (g + 1) * width)
        y_ref[:, cs] = _rms(y[:, cs], g_ref[:, cs])


def _ssd(proj, conv_prev, ssm_prev, conv_w, conv_b, a_log, dt_bias, d_vec, g_ssd, *, batch, seq):
    rows = min(CHUNK, seq)
    nc = seq // rows
    full = lambda *shape: pl.BlockSpec(shape, lambda b, c: (0,) * len(shape))
    return pl.pallas_call(
        functools.partial(_ssd_kernel, rows=rows),
        grid=(batch, nc),
        in_specs=[
            pl.BlockSpec((rows, CONV_DIM), lambda b, c: (b * nc + c, COL_XBC // CONV_DIM)),
            pl.BlockSpec((rows, D_SSD), lambda b, c: (b * nc + c, COL_Z // D_SSD)),
            pl.BlockSpec((rows, LANES), lambda b, c: (b * nc + c, COL_SMALL // LANES)),
            pl.BlockSpec((1, CONV_WIDTH - 1, CONV_DIM), lambda b, c: (b, 0, 0)),
            pl.BlockSpec((1, D_SSD, D_STATE), lambda b, c: (b, 0, 0)),
            full(CONV_WIDTH, CONV_DIM), full(1, CONV_DIM), full(1, SSD_HEADS), full(1, SSD_HEADS),
            full(1, D_SSD), full(1, D_SSD),
        ],
        out_specs=[
            pl.BlockSpec((rows, D_SSD), lambda b, c: (b * nc + c, 0)),
            pl.BlockSpec((1, CONV_WIDTH - 1, CONV_DIM), lambda b, c: (b, 0, 0)),
            pl.BlockSpec((1, D_SSD, D_STATE), lambda b, c: (b, 0, 0)),
        ],
        out_shape=[
            jax.ShapeDtypeStruct((batch * seq, D_SSD), F32),
            jax.ShapeDtypeStruct((batch, CONV_WIDTH - 1, CONV_DIM), F32),
            jax.ShapeDtypeStruct((batch, D_SSD, D_STATE), F32),
        ],
        scratch_shapes=[
            pltpu.VMEM((CHUNK + TAIL, CONV_DIM), F32),
            pltpu.VMEM((CHUNK, LANES), F32),
            pltpu.VMEM((D_SSD, D_STATE), F32),
            pltpu.VMEM((CHUNK, D_SSD), F32),
        ],
        compiler_params=_params(("parallel", "arbitrary")),
        name="ssd",
    )(proj, proj, proj, conv_prev, ssm_prev, conv_w, conv_b, a_log, dt_bias, d_vec, g_ssd)


def kernel(x_prompt, x_sample, cache_k, cache_v, cache_kidx, state_conv, state_ssm, page_table, rel_bias, g_ffn1, w1_ffn1, w3_ffn1, w2_ffn1, g_mix, w_in, conv_w, conv_b, a_log, dt_bias, d_skip, g_ssd, w_out, g_ffn2, w1_ffn2, w3_ffn2, w2_ffn2, g_final):
    depth = w_in.shape[0]
    assert depth == 1
    l = 0
    bp, seq, _ = x_prompt.shape
    bs, nq, _ = x_sample.shape
    n_pages = page_table.shape[1]
    row = lambda t: t.reshape(1, -1)

    w1a, w3a, w2a = w1_ffn1[l], w3_ffn1[l], w2_ffn1[l]
    w1b, w3b, w2b = w1_ffn2[l], w3_ffn2[l], w2_ffn2[l]
    w_in_t = jnp.swapaxes(w_in[l], 0, 1)
    w_out_a = w_out[l, :D_ATTN].astype(BF16)
    w_out_s = w_out[l, D_ATTN:].astype(BF16)
    d_vec = jnp.repeat(d_skip[l], SSD_HEAD_DIM).reshape(1, D_SSD)
    btab = _bias_table(rel_bias)
    gf = row(g_final)

    tm, tf = 1024, 256

    def mixers(proj, k, v, attend, conv_prev, ssm_prev, batch, length):
        a_out = attend(proj)
        s_out, conv_new, ssm_new = _ssd(
            proj, conv_prev, ssm_prev.reshape(batch, D_SSD, D_STATE), conv_w[l], row(conv_b[l]), row(a_log[l]),
            row(dt_bias[l]), d_vec, row(g_ssd[l]), batch=batch, seq=length)
        k = k.reshape(1, batch, length, N_KV_HEADS, HEAD_DIM)
        v = v.reshape(1, batch, length, N_KV_HEADS, HEAD_DIM)
        ki = proj[:, COL_SMALL + SM_KI:COL_SMALL + SM_KI + IDX_DIM].reshape(1, batch, length, IDX_DIM)
        ssm_new = ssm_new.reshape(1, batch, SSD_HEADS, SSD_HEAD_DIM, D_STATE)
        return a_out, s_out, k, v, ki, conv_new[None], ssm_new

    xp, xs = _ffn(x_prompt.reshape(bp * seq, D_MODEL), x_sample.reshape(bs * nq, D_MODEL), row(g_ffn1[l]),
                  w1a, w3a, w2a, gf, tm=tm, tf=tf, final_norm=False)
    in_p, in_s = _in_proj(xp, xs, row(g_mix[l]), w_in_t, tm=tm)

    a_p, s_p, k_p, v_p, ki_p, conv_p, ssm_p = mixers(
        *in_p, functools.partial(_dsa_prompt, btab=btab, batch=bp, seq=seq),
        jnp.zeros((bp, CONV_WIDTH - 1, CONV_DIM), F32), jnp.zeros((bp, SSD_HEADS, SSD_HEAD_DIM, D_STATE), F32),
        bp, seq)

    pool_k = cache_k[l].reshape(-1, HEAD_DIM)
    pool_v = cache_v[l].reshape(-1, HEAD_DIM)
    pool_kidx = jnp.swapaxes(cache_kidx[l], 1, 2).reshape(-1, PAGE_SIZE)

    def sample_attend(proj):
        sk_past, sk_new = _sample_score(page_table, proj, pool_kidx, batch=bs, nq=nq, n_pages=n_pages)
        thr, lim = _sample_select(sk_past.reshape(bs * nq, -1), sk_new.reshape(bs * nq, LANES),
                                  rows_per_step=128)
        return _sample_attn(page_table, proj, sk_past, sk_new, thr, lim, btab, pool_k, pool_v,
                            batch=bs, nq=nq, n_pages=n_pages)

    a_s, s_s, k_s, v_s, ki_s, conv_s, ssm_s = mixers(*in_s, sample_attend, state_conv[l], state_ssm[l], bs, nq)

    xp, xs = _out_proj(xp, a_p, s_p, xs, a_s, s_s, w_out_a, w_out_s, tm=512)
    y_p, y_s = _ffn(xp, xs, row(g_ffn2[l]), w1b, w3b, w2b, gf, tm=tm, tf=tf, final_norm=True)
    y_p = y_p.reshape(bp, seq, D_MODEL)
    y_s = y_s.reshape(bs, nq, D_MODEL)
    return (y_p, y_s, k_p, v_p, ki_p, conv_p, ssm_p, k_s, v_s, ki_s, conv_s, ssm_s)
```

```python
import functools
import math

import jax
import jax.numpy as jnp
from jax import lax
from jax.experimental import pallas as pl
from jax.experimental.pallas import tpu as pltpu

F32 = jnp.float32
BF16 = jnp.bfloat16
I32 = jnp.int32

D_MODEL = 2048
PAGE_SIZE = 128
N_HEADS = 8
HEAD_DIM = 128
N_KV_HEADS = 2
GQA_GROUP = N_HEADS // N_KV_HEADS
IDX_HEADS = 16
IDX_DIM = 64
TOPK = 256
N_BUCKETS = 32
MAX_DISTANCE = 128
SSD_HEADS = 16
SSD_HEAD_DIM = 64
SSD_GROUPS = 2
D_STATE = 128
CONV_WIDTH = 4
CHUNK = 128
D_ATTN = N_HEADS * HEAD_DIM
D_SSD = SSD_HEADS * SSD_HEAD_DIM
D_KV = N_KV_HEADS * HEAD_DIM
CONV_DIM = D_SSD + 2 * SSD_GROUPS * D_STATE
D_FF = 5632
EPS = 1e-6
IN_SPLITS = (D_ATTN, D_KV, D_KV, IDX_HEADS * IDX_DIM, IDX_DIM, IDX_HEADS, D_SSD, CONV_DIM, SSD_HEADS)

LANES = 128
COL_Q = 0
COL_QI = COL_Q + D_ATTN
COL_Z = COL_QI + IDX_HEADS * IDX_DIM
COL_XBC = COL_Z + D_SSD
COL_K = COL_XBC + CONV_DIM
COL_V = COL_K + D_KV
COL_SMALL = COL_V + D_KV
SM_KI = 0
SM_WI = SM_KI + IDX_DIM
SM_DT = SM_WI + IDX_HEADS
IN_TILE = 768
N_IN = ((COL_SMALL + LANES + IN_TILE - 1) // IN_TILE) * IN_TILE

INT_MIN = -(2 ** 31)
NEG_BIG = -1e30
VMEM_LIMIT = 60000 * 1024


def _params(sem):
    return pltpu.CompilerParams(dimension_semantics=sem, vmem_limit_bytes=VMEM_LIMIT)


def _rms(x, g):
    return x * lax.rsqrt(jnp.mean(x * x, axis=-1, keepdims=True) + EPS) * g


def _dot(a, b):
    return jnp.dot(a, b, preferred_element_type=F32)


def _dot_nt(a, b):
    return lax.dot_general(a, b, (((1,), (1,)), ((), ())), preferred_element_type=F32)


def _split3(x):
    hi = x.astype(BF16)
    rest = x - hi.astype(F32)
    mid = rest.astype(BF16)
    return hi, mid, (rest - mid.astype(F32)).astype(BF16)


def _select_dot(a, b):
    if a.dtype == BF16:
        return sum(_dot(a, part) for part in _split3(b))
    return sum(_dot(part, b) for part in _split3(a))


def _sort_key(x):
    bits = lax.bitcast_convert_type(x + 0.0, I32)
    return bits ^ ((bits >> 31) & 0x7FFFFFFF)


def _ffn_kernel(x_ref, xs_ref, g_ref, gf_ref, w1_hbm, w3_hbm, w2_hbm, o_ref, os_ref, h_scr, w1_buf, w3_buf,
                w2_buf, sem, *, tf, final_norm):
    i = pl.program_id(0)
    nf = D_FF // tf
    total = pl.num_programs(0) * nf
    tm = x_ref.shape[0]

    def copies(f, slot):
        c0 = pl.multiple_of(f * tf, tf)
        return (pltpu.make_async_copy(w1_hbm.at[:, pl.ds(c0, tf)], w1_buf.at[slot], sem.at[0, slot]),
                pltpu.make_async_copy(w3_hbm.at[:, pl.ds(c0, tf)], w3_buf.at[slot], sem.at[1, slot]),
                pltpu.make_async_copy(w2_hbm.at[pl.ds(c0, tf), :], w2_buf.at[slot], sem.at[2, slot]))

    @pl.when(i == 0)
    def _():
        for c in copies(0, 0):
            c.start()

    h_scr[0:tm, :] = _rms(x_ref[...], g_ref[...]).astype(BF16)
    h_scr[tm:, :] = _rms(xs_ref[...], g_ref[...]).astype(BF16)
    o_ref[...] = jnp.zeros_like(o_ref)
    os_ref[...] = jnp.zeros_like(os_ref)

    def body(f, carry):
        t = i * nf + f
        slot = lax.rem(t, 2)

        @pl.when(t + 1 < total)
        def _():
            for c in copies(jnp.where(f + 1 == nf, 0, f + 1), 1 - slot):
                c.start()

        for c in copies(f, slot):
            c.wait()
        h = h_scr[...]
        a = _dot(h, w1_buf[slot].astype(BF16))
        b = _dot(h, w3_buf[slot].astype(BF16))
        u = (a * jax.nn.sigmoid(a) * b).astype(BF16)
        y = _dot(u, w2_buf[slot].astype(BF16))
        o_ref[...] += y[0:tm, :]
        os_ref[...] += y[tm:, :]
        return carry

    lax.fori_loop(0, nf, body, 0)
    for src, dst in ((x_ref, o_ref), (xs_ref, os_ref)):
        y = src[...] + 0.5 * dst[...]
        if final_norm:
            y = _rms(y, gf_ref[...])
        dst[...] = y


def _ffn(x, xs, g, w1, w3, w2, gf, *, tm, tf, final_norm):
    t = x.shape[0]
    ts = xs.shape[0] // (t // tm)
    assert ts * (t // tm) == xs.shape[0] and ts % 16 == 0
    return pl.pallas_call(
        functools.partial(_ffn_kernel, tf=tf, final_norm=final_norm),
        grid=(t // tm,),
        in_specs=[
            pl.BlockSpec((tm, D_MODEL), lambda i: (i, 0)),
            pl.BlockSpec((ts, D_MODEL), lambda i: (i, 0)),
            pl.BlockSpec((1, D_MODEL), lambda i: (0, 0)),
            pl.BlockSpec((1, D_MODEL), lambda i: (0, 0)),
            pl.BlockSpec(memory_space=pl.ANY),
            pl.BlockSpec(memory_space=pl.ANY),
            pl.BlockSpec(memory_space=pl.ANY),
        ],
        out_specs=[pl.BlockSpec((tm, D_MODEL), lambda i: (i, 0)), pl.BlockSpec((ts, D_MODEL), lambda i: (i, 0))],
        out_shape=[jax.ShapeDtypeStruct((t, D_MODEL), F32), jax.ShapeDtypeStruct(xs.shape, F32)],
        scratch_shapes=[
            pltpu.VMEM((tm + ts, D_MODEL), BF16),
            pltpu.VMEM((2, D_MODEL, tf), F32),
            pltpu.VMEM((2, D_MODEL, tf), F32),
            pltpu.VMEM((2, tf, D_MODEL), F32),
            pltpu.SemaphoreType.DMA((3, 2)),
        ],
        compiler_params=_params(("arbitrary",)),
        name="ffn",
    )(x, xs, g, gf, w1, w3, w2)


KV_TILE = COL_K // IN_TILE
assert KV_TILE * IN_TILE == COL_K and COL_SMALL + LANES <= (KV_TILE + 1) * IN_TILE


def _in_tile_segments():
    src = {}
    off = 0
    for name, width in zip(("q", "k", "v", "qi", "ki", "wi", "z", "xbc", "dt"), IN_SPLITS):
        src[name] = (off, width)
        off += width
    runs, out = [], 0
    for name in ("q", "qi", "z", "xbc", "k", "v", "ki", "wi", "dt"):
        runs.append((out, src[name][0], src[name][1]))
        out += src[name][1]
    tiles = []
    for lo in range(0, N_IN, IN_TILE):
        segs = []
        for o, s0, n in runs:
            a, b = max(o, lo), min(o + n, lo + IN_TILE)
            if a < b:
                seg = (a - lo, s0 + a - o, b - a)
                if segs and segs[-1][0] + segs[-1][2] == seg[0] and segs[-1][1] + segs[-1][2] == seg[1]:
                    seg = (segs[-1][0], segs[-1][1], segs[-1][2] + seg[2])
                    segs.pop()
                segs.append(seg)
        assert all(v % 8 == 0 for seg in segs for v in seg)
        tiles.append(segs)
    return tiles


IN_TILE_SEGMENTS = _in_tile_segments()


def _in_proj_kernel(x_ref, xs_ref, g_ref, wt_hbm, o_ref, os_ref, k_ref, v_ref, ks_ref, vs_ref, h_scr, w_buf, sem):
    tm, ts = x_ref.shape[0], xs_ref.shape[0]
    i, j = pl.program_id(0), pl.program_id(1)
    nj = pl.num_programs(1)
    t = i * nj + j
    slot = lax.rem(t, 2)

    def copies(tile, slot):
        return [pltpu.make_async_copy(wt_hbm.at[pl.ds(s0, n), :], w_buf.at[slot, pl.ds(d0, n), :], sem.at[slot])
                for d0, s0, n in IN_TILE_SEGMENTS[tile]]

    def for_tile(tile, fn):
        for static_tile in range(len(IN_TILE_SEGMENTS)):
            pl.when(tile == static_tile)(functools.partial(fn, static_tile))

    def start(tile, slot):
        filled = sum(n for _, _, n in IN_TILE_SEGMENTS[tile])
        if filled < IN_TILE:
            w_buf[slot, filled:IN_TILE, :] = jnp.zeros((IN_TILE - filled, D_MODEL), F32)
        for c in copies(tile, slot):
            c.start()

    def wait(tile, slot):
        for c in copies(tile, slot):
            c.wait()

    @pl.when(t == 0)
    def _():
        start(0, 0)

    @pl.when(t + 1 < pl.num_programs(0) * nj)
    def _():
        for_tile(jnp.where(j + 1 == nj, 0, j + 1), functools.partial(start, slot=1 - slot))

    for_tile(j, functools.partial(wait, slot=slot))

    @pl.when(j == 0)
    def _():
        h_scr[0:tm, :] = _rms(x_ref[...], g_ref[...]).astype(BF16)
        h_scr[tm:, :] = _rms(xs_ref[...], g_ref[...]).astype(BF16)

    y = _dot_nt(h_scr[...], w_buf[slot].astype(BF16))
    o_ref[...] = y[0:tm, :]
    os_ref[...] = y[tm:, :]

    @pl.when(j == KV_TILE)
    def _():
        for r0, n, k_dst, v_dst in ((0, tm, k_ref, v_ref), (tm, ts, ks_ref, vs_ref)):
            for dst, col in ((k_dst, COL_K), (v_dst, COL_V)):
                for g in range(N_KV_HEADS):
                    c0 = col - COL_K + g * HEAD_DIM
                    dst[pl.ds(g, n, stride=N_KV_HEADS), :] = y[r0:r0 + n, c0:c0 + HEAD_DIM]


def _in_proj(x, xs, g, w_t, *, tm):
    t, tiles = x.shape[0], x.shape[0] // tm
    ts = xs.shape[0] // tiles
    assert ts * tiles == xs.shape[0] and ts % 16 == 0

    def group(rows, n):
        kv_spec = pl.BlockSpec((rows * N_KV_HEADS, HEAD_DIM), lambda i, j: (i, 0))
        kv_shape = jax.ShapeDtypeStruct((n * N_KV_HEADS, HEAD_DIM), F32)
        return ([pl.BlockSpec((rows, IN_TILE), lambda i, j: (i, j)), kv_spec, kv_spec],
                [jax.ShapeDtypeStruct((n, N_IN), F32), kv_shape, kv_shape])

    (po, pk, pv), (pos, pks, pvs) = group(tm, t)
    (so, sk, sv), (sos, sks, svs) = group(ts, xs.shape[0])
    proj, proj_s, k, v, k_s, v_s = pl.pallas_call(
        _in_proj_kernel,
        grid=(tiles, N_IN // IN_TILE),
        in_specs=[
            pl.BlockSpec((tm, D_MODEL), lambda i, j: (i, 0)),
            pl.BlockSpec((ts, D_MODEL), lambda i, j: (i, 0)),
            pl.BlockSpec((1, D_MODEL), lambda i, j: (0, 0)),
            pl.BlockSpec(memory_space=pl.ANY),
        ],
        out_specs=[po, so, pk, pv, sk, sv],
        out_shape=[pos, sos, pks, pvs, sks, svs],
        scratch_shapes=[pltpu.VMEM((tm + ts, D_MODEL), BF16), pltpu.VMEM((2, IN_TILE, D_MODEL), F32),
                        pltpu.SemaphoreType.DMA((2,))],
        compiler_params=_params(("arbitrary", "arbitrary")),
        name="in_proj",
    )(x, xs, g, w_t)
    return (proj, k, v), (proj_s, k_s, v_s)


def _out_proj_kernel(x_ref, a_ref, s_ref, xs_ref, as_ref, ss_ref, wa_ref, ws_ref, o_ref, os_ref):
    tm = x_ref.shape[0]
    a = jnp.concatenate([a_ref[...], as_ref[...]], axis=0).astype(BF16)
    s = jnp.concatenate([s_ref[...], ss_ref[...]], axis=0).astype(BF16)
    acc = _dot(a, wa_ref[...])
    acc += _dot(s, ws_ref[...])
    o_ref[...] = x_ref[...] + acc[0:tm, :]
    os_ref[...] = xs_ref[...] + acc[tm:, :]


def _out_proj(x, a, s, xs, a_s, s_s, wa, ws, *, tm):
    t, tiles = x.shape[0], x.shape[0] // tm
    ts = xs.shape[0] // tiles
    assert ts * tiles == xs.shape[0] and ts % 16 == 0

    def rows(n, width):
        return pl.BlockSpec((n, width), lambda i: (i, 0))

    return pl.pallas_call(
        _out_proj_kernel,
        grid=(tiles,),
        in_specs=[
            rows(tm, D_MODEL), rows(tm, D_ATTN), rows(tm, D_SSD),
            rows(ts, D_MODEL), rows(ts, D_ATTN), rows(ts, D_SSD),
            pl.BlockSpec((D_ATTN, D_MODEL), lambda i: (0, 0)),
            pl.BlockSpec((D_SSD, D_MODEL), lambda i: (0, 0)),
        ],
        out_specs=[rows(tm, D_MODEL), rows(ts, D_MODEL)],
        out_shape=[jax.ShapeDtypeStruct((t, D_MODEL), F32), jax.ShapeDtypeStruct(xs.shape, F32)],
        compiler_params=_params(("parallel",)),
        name="out_proj",
    )(x, a, s, xs, a_s, s_s, wa, ws)


NEAR = 2 * LANES


def _bias_kernel(rb_ref, o_ref):
    r = lax.broadcasted_iota(I32, (LANES, NEAR), 0)
    c = lax.broadcasted_iota(I32, (LANES, NEAR), 1)
    n = jnp.maximum(r + LANES - c, 0)
    max_exact = N_BUCKETS // 2
    nf = jnp.maximum(n, 1).astype(F32)
    large = max_exact + (jnp.log(nf / max_exact) / math.log(MAX_DISTANCE / max_exact)
                         * (N_BUCKETS - max_exact)).astype(I32)
    bucket = jnp.where(n < max_exact, n, jnp.minimum(large, N_BUCKETS - 1))
    for h in range(N_HEADS):
        acc = jnp.zeros((LANES, NEAR), F32)
        for b in range(N_BUCKETS):
            acc = jnp.where(bucket == b, rb_ref[b, h], acc)
        o_ref[h] = acc


def _bias_table(rel_bias):
    return pl.pallas_call(
        _bias_kernel,
        in_specs=[pl.BlockSpec(memory_space=pltpu.SMEM)],
        out_specs=pl.BlockSpec(memory_space=pltpu.VMEM),
        out_shape=jax.ShapeDtypeStruct((N_HEADS, LANES, NEAR), F32),
        name="bias_table",
    )(rel_bias)


def _count(mask, axis=1):
    x = mask.astype(F32)
    if axis == 0:
        x = jnp.sum(x.reshape(x.shape[0] // 64, 64, x.shape[1]), axis=0)
    return jnp.sum(x, axis=axis, keepdims=True)


def _kth_largest(count_ge, shape, optional_bits=()):
    t = jnp.full(shape, INT_MIN, I32)
    held = jnp.full(shape, 2.0 ** 30, F32)
    first = count_ge(jnp.zeros(shape, I32))
    t = jnp.where(first >= TOPK, 0, t)
    held = jnp.where(first >= TOPK, first, held)

    def body(it, carry):
        t, held = carry
        cand = t + lax.shift_left(jnp.int32(1), 30 - it)
        count = count_ge(cand)
        return jnp.where(count >= TOPK, cand, t), jnp.where(count >= TOPK, count, held)

    def finish(first_bit, groups, carry):
        if not groups:
            return carry
        stop = first_bit + groups[0]
        settled = jnp.min((carry[1] == TOPK).astype(F32)) > 0.0
        return lax.cond(settled, lambda c: c,
                        lambda c: finish(stop, groups[1:], lax.fori_loop(first_bit, stop, body, c)), carry)

    split = 31 - sum(optional_bits)
    return finish(split, tuple(optional_bits), lax.fori_loop(0, split, body, (t, held)))[0]


def _tie_limit(count_eq_below, need, shape, idx_bits):
    def body(it, m):
        cand = m + lax.shift_left(jnp.int32(1), idx_bits - 1 - it)
        return jnp.where(count_eq_below(cand) < need, cand, m)

    return lax.fori_loop(0, idx_bits, body, jnp.zeros(shape, I32))


CAUSAL_VARIANTS = 4


def _dsa_prompt_block(width, near_tiles, i, q_ref, qi_ref, k_ref, v_ref, sm_ref, bt_ref, o_ref, sk_scr, neg_scr,
                      kb_scr, vt_scr):
    tq = LANES
    q0 = pl.multiple_of(i * LANES, LANES)
    shape = (1, tq)

    wi_t = sm_ref[pl.ds(q0, tq), :].T[SM_WI:SM_WI + IDX_HEADS, :] * (IDX_HEADS * IDX_DIM) ** -0.5
    head_group = 4
    qi_groups = [
        jnp.concatenate([qi_ref[:, h * IDX_DIM:(h + 1) * IDX_DIM] for h in range(hg, hg + head_group)],
                        axis=0).astype(BF16)
        for hg in range(0, IDX_HEADS, head_group)]
    chunk = 2 * LANES
    pos = lax.broadcasted_iota(I32, (chunk, tq), 1) + q0
    for c in range(0, width, chunk):
        ki = sm_ref[c:c + chunk, SM_KI:SM_KI + IDX_DIM].astype(BF16)
        score = jnp.zeros((chunk, tq), F32)
        for n, hg in enumerate(range(0, IDX_HEADS, head_group)):
            rel = _dot_nt(ki, qi_groups[n])
            for e in range(head_group):
                score = score + wi_t[hg + e:hg + e + 1, :] * jnp.maximum(rel[:, e * tq:(e + 1) * tq], 0.0)
        key_c = lax.broadcasted_iota(I32, (chunk, tq), 0) + c
        sk_scr[c:c + chunk, :] = jnp.where(key_c <= pos, _sort_key(score), INT_MIN)

    key = lax.broadcasted_iota(I32, (width, tq), 0)

    def sk():
        return sk_scr[0:width, :]

    thr = _kth_largest(lambda c: _count(sk() >= c, 0), shape, optional_bits=(4, 4))
    excess = (_count(sk() >= thr, 0) > TOPK) & (thr > INT_MIN)
    any_excess = jnp.max(excess.astype(F32)) > 0.0

    @pl.when(jnp.logical_not(any_excess))
    def _():
        neg_scr[0:width, :] = jnp.where(sk() >= jnp.maximum(thr, INT_MIN + 1), 0.0, -jnp.inf)

    @pl.when(any_excess)
    def _():
        need = TOPK - _count(sk() > thr, 0)
        lim = _tie_limit(lambda m: _count((sk() == thr) & (key < m), 0), need, shape, (width - 1).bit_length())
        take = (sk() > thr) | ((sk() == thr) & (key <= lim) & (sk() > INT_MIN))
        neg_scr[0:width, :] = jnp.where(take, 0.0, -jnp.inf)

    log2e = math.log2(math.e)
    scale = HEAD_DIM ** -0.5 * log2e
    tiles = width // LANES
    for g in range(N_KV_HEADS):
        heads = range(g * GQA_GROUP, (g + 1) * GQA_GROUP)
        kg = kb_scr[g, 0:width, :]
        vg_t = vt_scr[g, :, 0:width]
        qg = jnp.concatenate([q_ref[:, h * HEAD_DIM:(h + 1) * HEAD_DIM] for h in heads], axis=0).astype(BF16)
        diag_t = [((bt_ref[h, :, LANES:NEAR] - bt_ref[h, 0:1, 0:1]) * log2e).T for h in heads]
        prev_t = [((bt_ref[h, :, 0:LANES] - bt_ref[h, 0:1, 0:1]) * log2e).T for h in heads]
        qk = _dot_nt(kg, qg)
        logits = []
        for t in range(tiles):
            rows = slice(t * LANES, (t + 1) * LANES)
            per_head = []
            for e in range(GQA_GROUP):
                lt = qk[rows, e * tq:(e + 1) * tq] * scale + neg_scr[rows, :]
                if t >= tiles - near_tiles:
                    lt = lt + jnp.where(i == t, diag_t[e], jnp.where(i - 1 == t, prev_t[e], 0.0))
                per_head.append(lt)
            logits.append(per_head)
        p_t = []
        inv_l = []
        for e in range(GQA_GROUP):
            m = jnp.max(functools.reduce(jnp.maximum, [lt[e] for lt in logits]), axis=0, keepdims=True)
            probs = [jnp.exp2(lt[e] - m) for lt in logits]
            inv_l.append(1.0 / jnp.sum(functools.reduce(jnp.add, probs), axis=0, keepdims=True))
            p_t.append(jnp.concatenate([p.astype(BF16) for p in probs], axis=0))
        out_t = _dot(vg_t, jnp.concatenate(p_t, axis=1))
        for e, h in enumerate(heads):
            o_ref[:, h * HEAD_DIM:(h + 1) * HEAD_DIM] = (out_t[:, e * tq:(e + 1) * tq] * inv_l[e]).T


def _dsa_prompt_kernel(*refs, seq):
    i = pl.program_id(1)
    per = seq // LANES // CAUSAL_VARIANTS
    k_ref, v_ref = refs[2], refs[3]
    kb_scr, vt_scr = refs[-2], refs[-1]

    @pl.when(i == 0)
    def _():
        for g in range(N_KV_HEADS):
            cols = slice(g * HEAD_DIM, (g + 1) * HEAD_DIM)
            kb_scr[g] = k_ref[:, cols].astype(BF16)
            vt_scr[g] = v_ref[:, cols].T.astype(BF16)

    for v in range(CAUSAL_VARIANTS):
        @pl.when(i // per == v)
        def _(v=v):
            _dsa_prompt_block((v + 1) * per * LANES, per + 1, i, *refs)


def _dsa_prompt(proj, btab, *, batch, seq):
    nq = seq // LANES
    return pl.pallas_call(
        functools.partial(_dsa_prompt_kernel, seq=seq),
        grid=(batch, nq),
        in_specs=[
            pl.BlockSpec((LANES, D_ATTN), lambda b, i: (b * nq + i, COL_Q // D_ATTN)),
            pl.BlockSpec((LANES, IDX_HEADS * IDX_DIM), lambda b, i: (b * nq + i, COL_QI // (IDX_HEADS * IDX_DIM))),
            pl.BlockSpec((seq, D_KV), lambda b, i: (b, COL_K // D_KV)),
            pl.BlockSpec((seq, D_KV), lambda b, i: (b, COL_V // D_KV)),
            pl.BlockSpec((seq, LANES), lambda b, i: (b, COL_SMALL // LANES)),
            pl.BlockSpec((N_HEADS, LANES, NEAR), lambda b, i: (0, 0, 0)),
        ],
        out_specs=pl.BlockSpec((LANES, D_ATTN), lambda b, i: (b * nq + i, 0)),
        out_shape=jax.ShapeDtypeStruct((batch * seq, D_ATTN), F32),
        scratch_shapes=[pltpu.VMEM((seq, LANES), I32), pltpu.VMEM((seq, LANES), F32),
                        pltpu.VMEM((N_KV_HEADS, seq, HEAD_DIM), BF16), pltpu.VMEM((N_KV_HEADS, HEAD_DIM, seq), BF16)],
        compiler_params=_params(("arbitrary", "arbitrary")),
        name="dsa_prompt",
    )(proj, proj, proj, proj, proj, btab)


SCORE_PAGES = 64
ATTN_PAGES = 32


def _sample_queries(qi_ref):
    return jnp.concatenate(
        [qi_ref[:, h * IDX_DIM:(h + 1) * IDX_DIM] for h in range(IDX_HEADS)], axis=0).astype(BF16)


def _sample_scores(rel, wi, nq):
    sc = jnp.zeros((nq, rel.shape[1]), F32)
    for h in range(IDX_HEADS):
        sc = sc + wi[:, h:h + 1] * jnp.maximum(rel[h * nq:(h + 1) * nq, :], 0.0)
    return sc


def _page_copies(pt_ref, pools, bufs, sems, b, s, slot, pages, page_rows):
    copies = []
    for r in range(pages):
        row0 = pl.multiple_of(pt_ref[b, s * pages + r] * page_rows, page_rows)
        for pool, buf, sem in zip(pools, bufs, sems):
            copies.append(pltpu.make_async_copy(pool.at[pl.ds(row0, page_rows), :], buf.at[slot, r], sem.at[slot]))
    return copies


def _page_pipeline(pt_ref, pools, bufs, sems, pages, page_rows):
    b, s = pl.program_id(0), pl.program_id(1)
    steps = pl.num_programs(1)
    t = b * steps + s
    slot = lax.rem(t, 2)
    copies = functools.partial(_page_copies, pt_ref, pools, bufs, sems, pages=pages, page_rows=page_rows)

    @pl.when(t == 0)
    def _():
        for c in copies(b, s, slot):
            c.start()

    @pl.when(t + 1 < pl.num_programs(0) * steps)
    def _():
        wrap = s + 1 == steps
        for c in copies(jnp.where(wrap, b + 1, b), jnp.where(wrap, 0, s + 1), 1 - slot):
            c.start()

    for c in copies(b, s, slot):
        c.wait()
    return slot


def _sample_score_kernel(pt_ref, qi_ref, sm_ref, pool_ref, past_ref, new_ref, pages, sem, kbuf, *, nq):
    s = pl.program_id(1)
    slot = _page_pipeline(pt_ref, [pool_ref], [pages], [sem], SCORE_PAGES, IDX_DIM)
    qx = _sample_queries(qi_ref)
    wi = sm_ref[:, SM_WI:SM_WI + IDX_HEADS] * (IDX_HEADS * IDX_DIM) ** -0.5
    for r in range(SCORE_PAGES):
        kbuf[:, r * PAGE_SIZE:(r + 1) * PAGE_SIZE] = pages[slot, r].astype(BF16)
    past_ref[...] = _sort_key(_sample_scores(_dot(qx, kbuf[...]), wi, nq))

    @pl.when(s == pl.num_programs(1) - 1)
    def _():
        ki_new = jnp.concatenate(
            [sm_ref[:, SM_KI:SM_KI + IDX_DIM], jnp.zeros((LANES - nq, IDX_DIM), F32)], axis=0).astype(BF16)
        sc = _sample_scores(_dot_nt(qx, ki_new), wi, nq)
        j = lax.broadcasted_iota(I32, (nq, LANES), 1)
        t = lax.broadcasted_iota(I32, (nq, LANES), 0)
        new_ref[...] = jnp.where(j <= t, _sort_key(sc), INT_MIN)


def _sample_score(page_table, proj_s, pool_kidx_t, *, batch, nq, n_pages):
    steps = n_pages // SCORE_PAGES
    grid_spec = pltpu.PrefetchScalarGridSpec(
        num_scalar_prefetch=1,
        grid=(batch, steps),
        in_specs=[
            pl.BlockSpec((nq, IDX_HEADS * IDX_DIM), lambda b, s, pt: (b, COL_QI // (IDX_HEADS * IDX_DIM))),
            pl.BlockSpec((nq, LANES), lambda b, s, pt: (b, COL_SMALL // LANES)),
            pl.BlockSpec(memory_space=pl.ANY),
        ],
        out_specs=[
            pl.BlockSpec((None, nq, SCORE_PAGES * PAGE_SIZE), lambda b, s, pt: (b, 0, s)),
            pl.BlockSpec((None, nq, LANES), lambda b, s, pt: (b, 0, 0)),
        ],
        scratch_shapes=[
            pltpu.VMEM((2, SCORE_PAGES, IDX_DIM, PAGE_SIZE), F32),
            pltpu.SemaphoreType.DMA((2,)),
            pltpu.VMEM((IDX_DIM, SCORE_PAGES * PAGE_SIZE), BF16),
        ],
    )
    return pl.pallas_call(
        functools.partial(_sample_score_kernel, nq=nq),
        grid_spec=grid_spec,
        out_shape=[
            jax.ShapeDtypeStruct((batch, nq, n_pages * PAGE_SIZE), I32),
            jax.ShapeDtypeStruct((batch, nq, LANES), I32),
        ],
        compiler_params=_params(("arbitrary", "arbitrary")),
        name="sample_score",
    )(page_table, proj_s, proj_s, pool_kidx_t)


def _sample_select_kernel(past_ref, new_ref, thr_ref, lim_ref, *, past):
    rows = past_ref.shape[0]
    colp = lax.broadcasted_iota(I32, (rows, past), 1)
    coln = lax.broadcasted_iota(I32, (rows, LANES), 1) + past

    def count_ge(c):
        return _count(past_ref[...] >= c) + _count(new_ref[...] >= c)

    thr = _kth_largest(count_ge, (rows, 1))
    thr_ref[...] = jnp.broadcast_to(thr, (rows, LANES))
    lim_ref[...] = jnp.full((rows, LANES), 2 ** 31 - 1, I32)
    any_excess = jnp.max((count_ge(thr) > TOPK).astype(F32)) > 0.0

    @pl.when(any_excess)
    def _():
        need = TOPK - (_count(past_ref[...] > thr) + _count(new_ref[...] > thr))

        def count_eq_below(m):
            return (_count((past_ref[...] == thr) & (colp < m))
                    + _count((new_ref[...] == thr) & (coln < m)))

        lim = _tie_limit(count_eq_below, need, (rows, 1), (past + LANES - 1).bit_length())
        lim_ref[...] = jnp.broadcast_to(lim, (rows, LANES))


def _sample_select(sk_past, sk_new, *, rows_per_step):
    rows, past = sk_past.shape
    return pl.pallas_call(
        functools.partial(_sample_select_kernel, past=past),
        grid=(rows // rows_per_step,),
        in_specs=[
            pl.BlockSpec((rows_per_step, past), lambda i: (i, 0)),
            pl.BlockSpec((rows_per_step, LANES), lambda i: (i, 0)),
        ],
        out_specs=[
            pl.BlockSpec((rows_per_step, LANES), lambda i: (i, 0)),
            pl.BlockSpec((rows_per_step, LANES), lambda i: (i, 0)),
        ],
        out_shape=[jax.ShapeDtypeStruct((rows, LANES), I32)] * 2,
        compiler_params=_params(("parallel",)),
        name="sample_select",
    )(sk_past, sk_new)


def _sample_attn_kernel(pt_ref, q_ref, kn_ref, vn_ref, skp_ref, skn_ref, thr_ref, lim_ref, bt_ref, poolk_ref,
                        poolv_ref, o_ref, kpages, vpages, ksem, vsem, kbuf, vbuf, m_scr, l_scr, acc_scr, *,
                        nq, past):
    slot = _page_pipeline(pt_ref, [poolk_ref, poolv_ref], [kpages, vpages], [ksem, vsem], ATTN_PAGES,
                          PAGE_SIZE * N_KV_HEADS)
    s = pl.program_id(1)
    last = pl.num_programs(1) - 1
    width = ATTN_PAGES * PAGE_SIZE
    grows = GQA_GROUP * nq
    scale = HEAD_DIM ** -0.5

    @pl.when(s == 0)
    def _():
        m_scr[...] = jnp.full_like(m_scr, NEG_BIG)
        l_scr[...] = jnp.zeros_like(l_scr)
        acc_scr[...] = jnp.zeros_like(acc_scr)

    thr = thr_ref[:, 0:1]
    lim = lim_ref[:, 0:1]

    def select(sk, idx):
        sel = (sk > thr) | ((sk == thr) & (idx <= lim))
        return jnp.concatenate([sel] * GQA_GROUP, axis=0)

    def far_bias(g):
        return jnp.concatenate(
            [jnp.broadcast_to(bt_ref[g * GQA_GROUP + hh, 0:1, 0:1], (nq, 1)) for hh in range(GQA_GROUP)], axis=0)

    def near_bias(g, lo, hi):
        return jnp.concatenate([bt_ref[g * GQA_GROUP + hh, 0:nq, lo:hi] for hh in range(GQA_GROUP)], axis=0)

    def group_queries(g):
        return jnp.concatenate(
            [q_ref[:, (g * GQA_GROUP + hh) * HEAD_DIM:(g * GQA_GROUP + hh + 1) * HEAD_DIM]
             for hh in range(GQA_GROUP)], axis=0).astype(BF16)

    def accumulate(g, lg, sel, vals):
        rs = slice(g * grows, (g + 1) * grows)
        m_old = m_scr[rs, :]
        m_new = jnp.maximum(m_old, jnp.max(jnp.where(sel, lg, NEG_BIG), axis=1, keepdims=True))
        p = jnp.where(sel, jnp.exp(lg - m_new), 0.0)
        alpha = jnp.exp(m_old - m_new)
        l_scr[rs, :] = alpha * l_scr[rs, :] + jnp.sum(p, axis=1, keepdims=True)
        acc_scr[rs, :] = alpha * acc_scr[rs, :] + _dot(p.astype(BF16), vals)
        m_scr[rs, :] = m_new

    for r in range(ATTN_PAGES):
        for g in range(N_KV_HEADS):
            rows = pl.ds(g, PAGE_SIZE, stride=N_KV_HEADS)
            kbuf[g, r * PAGE_SIZE:(r + 1) * PAGE_SIZE, :] = kpages[slot, r, rows, :].astype(BF16)
            vbuf[g, r * PAGE_SIZE:(r + 1) * PAGE_SIZE, :] = vpages[slot, r, rows, :].astype(BF16)

    col = lax.broadcasted_iota(I32, (nq, width), 1)
    sel = select(skp_ref[...], col + s * width)
    colg = lax.broadcasted_iota(I32, (grows, width), 1)
    in_near = (s == last) & (colg >= width - PAGE_SIZE)
    for g in range(N_KV_HEADS):
        near = jnp.tile(near_bias(g, 0, LANES), (1, ATTN_PAGES))
        bias = jnp.where(in_near, near, far_bias(g))
        lg = _dot_nt(group_queries(g), kbuf[g]) * scale + bias
        accumulate(g, lg, sel, vbuf[g])

    @pl.when(s == last)
    def _():
        pad = jnp.zeros((LANES - nq, D_KV), F32)
        kn = jnp.concatenate([kn_ref[...], pad], axis=0).astype(BF16)
        vn = jnp.concatenate([vn_ref[...], pad], axis=0).astype(BF16)
        coln = lax.broadcasted_iota(I32, (nq, LANES), 1) + past
        seln = select(skn_ref[...], coln)
        for g in range(N_KV_HEADS):
            cs = slice(g * HEAD_DIM, (g + 1) * HEAD_DIM)
            lg = _dot_nt(group_queries(g), kn[:, cs]) * scale + near_bias(g, LANES, NEAR)
            accumulate(g, lg, seln, vn[:, cs])
        out = acc_scr[...] / l_scr[...]
        for h in range(N_HEADS):
            o_ref[:, h * HEAD_DIM:(h + 1) * HEAD_DIM] = out[h * nq:(h + 1) * nq, :]


def _sample_attn(page_table, proj_s, sk_past, sk_new, thr, lim, btab, pool_k, pool_v, *, batch, nq, n_pages):
    steps = n_pages // ATTN_PAGES
    width = ATTN_PAGES * PAGE_SIZE
    past = n_pages * PAGE_SIZE
    page_buf = pltpu.VMEM((2, ATTN_PAGES, PAGE_SIZE * N_KV_HEADS, HEAD_DIM), F32)
    grid_spec = pltpu.PrefetchScalarGridSpec(
        num_scalar_prefetch=1,
        grid=(batch, steps),
        in_specs=[
            pl.BlockSpec((nq, D_ATTN), lambda b, s, pt: (b, COL_Q // D_ATTN)),
            pl.BlockSpec((nq, D_KV), lambda b, s, pt: (b, COL_K // D_KV)),
            pl.BlockSpec((nq, D_KV), lambda b, s, pt: (b, COL_V // D_KV)),
            pl.BlockSpec((None, nq, width), lambda b, s, pt: (b, 0, s)),
            pl.BlockSpec((None, nq, LANES), lambda b, s, pt: (b, 0, 0)),
            pl.BlockSpec((nq, LANES), lambda b, s, pt: (b, 0)),
            pl.BlockSpec((nq, LANES), lambda b, s, pt: (b, 0)),
            pl.BlockSpec((N_HEADS, LANES, NEAR), lambda b, s, pt: (0, 0, 0)),
            pl.BlockSpec(memory_space=pl.ANY),
            pl.BlockSpec(memory_space=pl.ANY),
        ],
        out_specs=pl.BlockSpec((nq, D_ATTN), lambda b, s, pt: (b, 0)),
        scratch_shapes=[
            page_buf, page_buf, pltpu.SemaphoreType.DMA((2,)), pltpu.SemaphoreType.DMA((2,)),
            pltpu.VMEM((N_KV_HEADS, width, HEAD_DIM), BF16),
            pltpu.VMEM((N_KV_HEADS, width, HEAD_DIM), BF16),
            pltpu.VMEM((N_HEADS * nq, 1), F32),
            pltpu.VMEM((N_HEADS * nq, 1), F32),
            pltpu.VMEM((N_HEADS * nq, HEAD_DIM), F32),
        ],
    )
    return pl.pallas_call(
        functools.partial(_sample_attn_kernel, nq=nq, past=past),
        grid_spec=grid_spec,
        out_shape=jax.ShapeDtypeStruct((batch * nq, D_ATTN), F32),
        compiler_params=_params(("arbitrary", "arbitrary")),
        name="sample_attn",
    )(page_table, proj_s, proj_s, proj_s, sk_past, sk_new, thr, lim, btab, pool_k, pool_v)


TAIL = 8
SHORT_ROWS = 16


def _ssd_kernel(xbc_ref, z_ref, sm_ref, cprev_ref, hprev_ref, cw_ref, cb_ref, alog_ref, dtb_ref, dvec_ref,
                g_ref, y_ref, cnew_ref, hnew_ref, xpad_scr, sm_scr, h_scr, yd_scr, *, rows):
    c = pl.program_id(1)
    q = CHUNK
    r = q if rows == q else SHORT_ROWS
    assert rows <= r
    keep = CONV_WIDTH - 1
    heads_per_group = SSD_HEADS // SSD_GROUPS

    def pad_rows(x):
        return x if r == q else jnp.concatenate([x, jnp.zeros((q - r, x.shape[1]), x.dtype)], axis=0)

    @pl.when(c == 0)
    def _():
        xpad_scr[...] = jnp.zeros_like(xpad_scr)
        sm_scr[...] = jnp.zeros_like(sm_scr)
        xpad_scr[TAIL - keep:TAIL, :] = cprev_ref[0]
        h_scr[...] = hprev_ref[0]

    xpad_scr[TAIL:TAIL + rows, :] = xbc_ref[...]
    sm_scr[0:rows, :] = sm_ref[...]
    conv = sum(xpad_scr[TAIL - keep + k:TAIL - keep + k + r, :] * cw_ref[k:k + 1, :]
               for k in range(CONV_WIDTH)) + cb_ref[...]
    new_tail = xpad_scr[rows + TAIL - keep:rows + TAIL, :]
    cnew_ref[0] = new_tail
    xpad_scr[TAIL - keep:TAIL, :] = new_tail

    xc = conv * jax.nn.sigmoid(conv)
    xs = xc[:, :D_SSD]
    bm = pad_rows(xc[:, D_SSD:D_SSD + SSD_GROUPS * D_STATE]).astype(BF16)
    cm = xc[:, D_SSD + SSD_GROUPS * D_STATE:].astype(BF16)

    ri = lax.broadcasted_iota(I32, (q, q), 0)
    ci = lax.broadcasted_iota(I32, (q, q), 1)
    causal = ri >= ci
    causal_r = causal[0:r, :]
    x = sm_scr[:, SM_DT:SM_DT + SSD_HEADS] + dtb_ref[...]
    dt = jnp.maximum(x, 0.0) + jnp.log1p(jnp.exp(-jnp.abs(x)))
    dt = jnp.where(lax.broadcasted_iota(I32, (q, SSD_HEADS), 0) < rows, dt, 0.0)
    a = -jnp.exp(alog_ref[...])
    acum = _select_dot(causal.astype(BF16), dt * a)
    acum_t = jnp.concatenate([acum, jnp.zeros((q, LANES - SSD_HEADS), F32)], axis=1).T

    expand = (lax.broadcasted_iota(I32, (SSD_HEADS, D_SSD), 1) // SSD_HEAD_DIM
              == lax.broadcasted_iota(I32, (SSD_HEADS, D_SSD), 0)).astype(BF16)
    expand_t = (lax.broadcasted_iota(I32, (D_SSD, SSD_HEADS), 0) // SSD_HEAD_DIM
                == lax.broadcasted_iota(I32, (D_SSD, SSD_HEADS), 1)).astype(BF16)
    acum_x = _select_dot(acum[0:r, :], expand)
    xd = xs * _select_dot(dt[0:r, :], expand)
    xdw_t = pad_rows(xd * jnp.exp(acum_x[r - 1:r, :] - acum_x)).T.astype(BF16)
    xd = pad_rows(xd).astype(BF16)
    state_decay = jnp.exp(_select_dot(expand_t, jnp.broadcast_to(acum_t[0:SSD_HEADS, q - 1:q], (SSD_HEADS, LANES))))

    for g in range(SSD_GROUPS):
        ns = slice(g * D_STATE, (g + 1) * D_STATE)
        hs = slice(g * heads_per_group * SSD_HEAD_DIM, (g + 1) * heads_per_group * SSD_HEAD_DIM)
        cb = _dot_nt(cm[:, ns], bm[:, ns])
        for e in range(heads_per_group):
            h = g * heads_per_group + e
            ps = slice(h * SSD_HEAD_DIM, (h + 1) * SSD_HEAD_DIM)
            seg = acum[0:r, h:h + 1] - acum_t[h:h + 1, :]
            lmat = jnp.exp(jnp.where(causal_r, seg, -jnp.inf))
            yd_scr[0:r, ps] = _dot((cb * lmat).astype(BF16), xd[:, ps])
        h_old = h_scr[hs, :]
        yd_scr[0:r, hs] += _dot_nt(cm[:, ns], h_old.astype(BF16)) * jnp.exp(acum_x[:, hs])
        h_scr[hs, :] = h_old * state_decay[hs, :] + _dot(xdw_t[hs, :], bm[:, ns])

    hnew_ref[0] = h_scr[...]
    y = yd_scr[0:rows, :] + dvec_ref[...] * xs[0:rows, :]
    zz = z_ref[...]
    y = y * (zz * jax.nn.sigmoid(zz))
    width = D_SSD // SSD_GROUPS
    for g in range(SSD_GROUPS):
        cs = slice(g * width, (g + 1) * width)
        y_ref[:, cs] = _rms(y[:, cs], g_ref[:, cs])


def _ssd(proj, conv_prev, ssm_prev, conv_w, conv_b, a_log, dt_bias, d_vec, g_ssd, *, batch, seq):
    rows = min(CHUNK, seq)
    nc = seq // rows
    full = lambda *shape: pl.BlockSpec(shape, lambda b, c: (0,) * len(shape))
    return pl.pallas_call(
        functools.partial(_ssd_kernel, rows=rows),
        grid=(batch, nc),
        in_specs=[
            pl.BlockSpec((rows, CONV_DIM), lambda b, c: (b * nc + c, COL_XBC // CONV_DIM)),
            pl.BlockSpec((rows, D_SSD), lambda b, c: (b * nc + c, COL_Z // D_SSD)),
            pl.BlockSpec((rows, LANES), lambda b, c: (b * nc + c, COL_SMALL // LANES)),
            pl.BlockSpec((1, CONV_WIDTH - 1, CONV_DIM), lambda b, c: (b, 0, 0)),
            pl.BlockSpec((1, D_SSD, D_STATE), lambda b, c: (b, 0, 0)),
            full(CONV_WIDTH, CONV_DIM), full(1, CONV_DIM), full(1, SSD_HEADS), full(1, SSD_HEADS),
            full(1, D_SSD), full(1, D_SSD),
        ],
        out_specs=[
            pl.BlockSpec((rows, D_SSD), lambda b, c: (b * nc + c, 0)),
            pl.BlockSpec((1, CONV_WIDTH - 1, CONV_DIM), lambda b, c: (b, 0, 0)),
            pl.BlockSpec((1, D_SSD, D_STATE), lambda b, c: (b, 0, 0)),
        ],
        out_shape=[
            jax.ShapeDtypeStruct((batch * seq, D_SSD), F32),
            jax.ShapeDtypeStruct((batch, CONV_WIDTH - 1, CONV_DIM), F32),
            jax.ShapeDtypeStruct((batch, D_SSD, D_STATE), F32),
        ],
        scratch_shapes=[
            pltpu.VMEM((CHUNK + TAIL, CONV_DIM), F32),
            pltpu.VMEM((CHUNK, LANES), F32),
            pltpu.VMEM((D_SSD, D_STATE), F32),
            pltpu.VMEM((CHUNK, D_SSD), F32),
        ],
        compiler_params=_params(("parallel", "arbitrary")),
        name="ssd",
    )(proj, proj, proj, conv_prev, ssm_prev, conv_w, conv_b, a_log, dt_bias, d_vec, g_ssd)


def kernel(x_prompt, x_sample, cache_k, cache_v, cache_kidx, state_conv, state_ssm, page_table, rel_bias, g_ffn1, w1_ffn1, w3_ffn1, w2_ffn1, g_mix, w_in, conv_w, conv_b, a_log, dt_bias, d_skip, g_ssd, w_out, g_ffn2, w1_ffn2, w3_ffn2, w2_ffn2, g_final):
    depth = w_in.shape[0]
    assert depth == 1
    l = 0
    bp, seq, _ = x_prompt.shape
    bs, nq, _ = x_sample.shape
    n_pages = page_table.shape[1]
    row = lambda t: t.reshape(1, -1)

    w1a, w3a, w2a = w1_ffn1[l], w3_ffn1[l], w2_ffn1[l]
    w1b, w3b, w2b = w1_ffn2[l], w3_ffn2[l], w2_ffn2[l]
    w_in_t = jnp.swapaxes(w_in[l], 0, 1)
    w_out_a = w_out[l, :D_ATTN].astype(BF16)
    w_out_s = w_out[l, D_ATTN:].astype(BF16)
    d_vec = jnp.repeat(d_skip[l], SSD_HEAD_DIM).reshape(1, D_SSD)
    btab = _bias_table(rel_bias)
    gf = row(g_final)

    tm, tf = 1024, 256

    def mixers(proj, k, v, attend, conv_prev, ssm_prev, batch, length):
        a_out = attend(proj)
        s_out, conv_new, ssm_new = _ssd(
            proj, conv_prev, ssm_prev.reshape(batch, D_SSD, D_STATE), conv_w[l], row(conv_b[l]), row(a_log[l]),
            row(dt_bias[l]), d_vec, row(g_ssd[l]), batch=batch, seq=length)
        k = k.reshape(1, batch, length, N_KV_HEADS, HEAD_DIM)
        v = v.reshape(1, batch, length, N_KV_HEADS, HEAD_DIM)
        ki = proj[:, COL_SMALL + SM_KI:COL_SMALL + SM_KI + IDX_DIM].reshape(1, batch, length, IDX_DIM)
        ssm_new = ssm_new.reshape(1, batch, SSD_HEADS, SSD_HEAD_DIM, D_STATE)
        return a_out, s_out, k, v, ki, conv_new[None], ssm_new

    xp, xs = _ffn(x_prompt.reshape(bp * seq, D_MODEL), x_sample.reshape(bs * nq, D_MODEL), row(g_ffn1[l]),
                  w1a, w3a, w2a, gf, tm=tm, tf=tf, final_norm=False)
    in_p, in_s = _in_proj(xp, xs, row(g_mix[l]), w_in_t, tm=tm)

    a_p, s_p, k_p, v_p, ki_p, conv_p, ssm_p = mixers(
        *in_p, functools.partial(_dsa_prompt, btab=btab, batch=bp, seq=seq),
        jnp.zeros((bp, CONV_WIDTH - 1, CONV_DIM), F32), jnp.zeros((bp, SSD_HEADS, SSD_HEAD_DIM, D_STATE), F32),
        bp, seq)

    pool_k = cache_k[l].reshape(-1, HEAD_DIM)
    pool_v = cache_v[l].reshape(-1, HEAD_DIM)
    pool_kidx = jnp.swapaxes(cache_kidx[l], 1, 2).reshape(-1, PAGE_SIZE)

    def sample_attend(proj):
        sk_past, sk_new = _sample_score(page_table, proj, pool_kidx, batch=bs, nq=nq, n_pages=n_pages)
        thr, lim = _sample_select(sk_past.reshape(bs * nq, -1), sk_new.reshape(bs * nq, LANES),
                                  rows_per_step=128)
        return _sample_attn(page_table, proj, sk_past, sk_new, thr, lim, btab, pool_k, pool_v,
                            batch=bs, nq=nq, n_pages=n_pages)

    a_s, s_s, k_s, v_s, ki_s, conv_s, ssm_s = mixers(*in_s, sample_attend, state_conv[l], state_ssm[l], bs, nq)

    xp, xs = _out_proj(xp, a_p, s_p, xs, a_s, s_s, w_out_a, w_out_s, tm=512)
    y_p, y_s = _ffn(xp, xs, row(g_ffn2[l]), w1b, w3b, w2b, gf, tm=tm, tf=tf, final_norm=True)
    y_p = y_p.reshape(bp, seq, D_MODEL)
    y_s = y_s.reshape(bs, nq, D_MODEL)
    return (y_p, y_s, k_p, v_p, ki_p, conv_p, ssm_p, k_s, v_s, ki_s, conv_s, ssm_s)
```

```python
import functools
import math

import jax
import jax.numpy as jnp
from jax import lax
from jax.experimental import pallas as pl
from jax.experimental.pallas import tpu as pltpu

F32 = jnp.float32
BF16 = jnp.bfloat16
I32 = jnp.int32

D_MODEL = 2048
PAGE_SIZE = 128
N_HEADS = 8
HEAD_DIM = 128
N_KV_HEADS = 2
GQA_GROUP = N_HEADS // N_KV_HEADS
IDX_HEADS = 16
IDX_DIM = 64
TOPK = 256
N_BUCKETS = 32
MAX_DISTANCE = 128
SSD_HEADS = 16
SSD_HEAD_DIM = 64
SSD_GROUPS = 2
D_STATE = 128
CONV_WIDTH = 4
CHUNK = 128
D_ATTN = N_HEADS * HEAD_DIM
D_SSD = SSD_HEADS * SSD_HEAD_DIM
D_KV = N_KV_HEADS * HEAD_DIM
CONV_DIM = D_SSD + 2 * SSD_GROUPS * D_STATE
D_FF = 5632
EPS = 1e-6
IN_SPLITS = (D_ATTN, D_KV, D_KV, IDX_HEADS * IDX_DIM, IDX_DIM, IDX_HEADS, D_SSD, CONV_DIM, SSD_HEADS)

LANES = 128
COL_Q = 0
COL_QI = COL_Q + D_ATTN
COL_Z = COL_QI + IDX_HEADS * IDX_DIM
COL_XBC = COL_Z + D_SSD
COL_K = COL_XBC + CONV_DIM
COL_V = COL_K + D_KV
COL_SMALL = COL_V + D_KV
SM_KI = 0
SM_WI = SM_KI + IDX_DIM
SM_DT = SM_WI + IDX_HEADS
IN_TILE = 768
N_IN = ((COL_SMALL + LANES + IN_TILE - 1) // IN_TILE) * IN_TILE

INT_MIN = -(2 ** 31)
NEG_BIG = -1e30
VMEM_LIMIT = 60000 * 1024


def _params(sem):
    return pltpu.CompilerParams(dimension_semantics=sem, vmem_limit_bytes=VMEM_LIMIT)


def _rms(x, g):
    return x * lax.rsqrt(jnp.mean(x * x, axis=-1, keepdims=True) + EPS) * g


def _dot(a, b):
    return jnp.dot(a, b, preferred_element_type=F32)


def _dot_nt(a, b):
    return lax.dot_general(a, b, (((1,), (1,)), ((), ())), preferred_element_type=F32)


def _split3(x):
    hi = x.astype(BF16)
    rest = x - hi.astype(F32)
    mid = rest.astype(BF16)
    return hi, mid, (rest - mid.astype(F32)).astype(BF16)


def _select_dot(a, b):
    if a.dtype == BF16:
        return sum(_dot(a, part) for part in _split3(b))
    return sum(_dot(part, b) for part in _split3(a))


def _sort_key(x):
    bits = lax.bitcast_convert_type(x + 0.0, I32)
    return bits ^ ((bits >> 31) & 0x7FFFFFFF)


def _ffn_kernel(x_ref, xs_ref, g_ref, gf_ref, w1_hbm, w3_hbm, w2_hbm, o_ref, os_ref, h_scr, w1_buf, w3_buf,
                w2_buf, sem, *, tf, final_norm):
    i = pl.program_id(0)
    nf = D_FF // tf
    total = pl.num_programs(0) * nf
    tm = x_ref.shape[0]

    def copies(f, slot):
        c0 = pl.multiple_of(f * tf, tf)
        return (pltpu.make_async_copy(w1_hbm.at[:, pl.ds(c0, tf)], w1_buf.at[slot], sem.at[0, slot]),
                pltpu.make_async_copy(w3_hbm.at[:, pl.ds(c0, tf)], w3_buf.at[slot], sem.at[1, slot]),
                pltpu.make_async_copy(w2_hbm.at[pl.ds(c0, tf), :], w2_buf.at[slot], sem.at[2, slot]))

    @pl.when(i == 0)
    def _():
        for c in copies(0, 0):
            c.start()

    h_scr[0:tm, :] = _rms(x_ref[...], g_ref[...]).astype(BF16)
    h_scr[tm:, :] = _rms(xs_ref[...], g_ref[...]).astype(BF16)
    o_ref[...] = jnp.zeros_like(o_ref)
    os_ref[...] = jnp.zeros_like(os_ref)

    def body(f, carry):
        t = i * nf + f
        slot = lax.rem(t, 2)

        @pl.when(t + 1 < total)
        def _():
            for c in copies(jnp.where(f + 1 == nf, 0, f + 1), 1 - slot):
                c.start()

        for c in copies(f, slot):
            c.wait()
        h = h_scr[...]
        a = _dot(h, w1_buf[slot].astype(BF16))
        b = _dot(h, w3_buf[slot].astype(BF16))
        u = (a * jax.nn.sigmoid(a) * b).astype(BF16)
        y = _dot(u, w2_buf[slot].astype(BF16))
        o_ref[...] += y[0:tm, :]
        os_ref[...] += y[tm:, :]
        return carry

    lax.fori_loop(0, nf, body, 0)
    for src, dst in ((x_ref, o_ref), (xs_ref, os_ref)):
        y = src[...] + 0.5 * dst[...]
        if final_norm:
            y = _rms(y, gf_ref[...])
        dst[...] = y


def _ffn(x, xs, g, w1, w3, w2, gf, *, tm, tf, final_norm):
    t = x.shape[0]
    ts = xs.shape[0] // (t // tm)
    assert ts * (t // tm) == xs.shape[0] and ts % 16 == 0
    return pl.pallas_call(
        functools.partial(_ffn_kernel, tf=tf, final_norm=final_norm),
        grid=(t // tm,),
        in_specs=[
            pl.BlockSpec((tm, D_MODEL), lambda i: (i, 0)),
            pl.BlockSpec((ts, D_MODEL), lambda i: (i, 0)),
            pl.BlockSpec((1, D_MODEL), lambda i: (0, 0)),
            pl.BlockSpec((1, D_MODEL), lambda i: (0, 0)),
            pl.BlockSpec(memory_space=pl.ANY),
            pl.BlockSpec(memory_space=pl.ANY),
            pl.BlockSpec(memory_space=pl.ANY),
        ],
        out_specs=[pl.BlockSpec((tm, D_MODEL), lambda i: (i, 0)), pl.BlockSpec((ts, D_MODEL), lambda i: (i, 0))],
        out_shape=[jax.ShapeDtypeStruct((t, D_MODEL), F32), jax.ShapeDtypeStruct(xs.shape, F32)],
        scratch_shapes=[
            pltpu.VMEM((tm + ts, D_MODEL), BF16),
            pltpu.VMEM((2, D_MODEL, tf), F32),
            pltpu.VMEM((2, D_MODEL, tf), F32),
            pltpu.VMEM((2, tf, D_MODEL), F32),
            pltpu.SemaphoreType.DMA((3, 2)),
        ],
        compiler_params=_params(("arbitrary",)),
        name="ffn",
    )(x, xs, g, gf, w1, w3, w2)


KV_TILE = COL_K // IN_TILE
assert KV_TILE * IN_TILE == COL_K and COL_SMALL + LANES <= (KV_TILE + 1) * IN_TILE


def _in_tile_segments():
    src = {}
    off = 0
    for name, width in zip(("q", "k", "v", "qi", "ki", "wi", "z", "xbc", "dt"), IN_SPLITS):
        src[name] = (off, width)
        off += width
    runs, out = [], 0
    for name in ("q", "qi", "z", "xbc", "k", "v", "ki", "wi", "dt"):
        runs.append((out, src[name][0], src[name][1]))
        out += src[name][1]
    tiles = []
    for lo in range(0, N_IN, IN_TILE):
        segs = []
        for o, s0, n in runs:
            a, b = max(o, lo), min(o + n, lo + IN_TILE)
            if a < b:
                seg = (a - lo, s0 + a - o, b - a)
                if segs and segs[-1][0] + segs[-1][2] == seg[0] and segs[-1][1] + segs[-1][2] == seg[1]:
                    seg = (segs[-1][0], segs[-1][1], segs[-1][2] + seg[2])
                    segs.pop()
                segs.append(seg)
        assert all(v % 8 == 0 for seg in segs for v in seg)
        tiles.append(segs)
    return tiles


IN_TILE_SEGMENTS = _in_tile_segments()


def _in_proj_kernel(x_ref, xs_ref, g_ref, wt_hbm, o_ref, os_ref, k_ref, v_ref, ks_ref, vs_ref, h_scr, w_buf, sem):
    tm, ts = x_ref.shape[0], xs_ref.shape[0]
    i, j = pl.program_id(0), pl.program_id(1)
    nj = pl.num_programs(1)
    t = i * nj + j
    slot = lax.rem(t, 2)

    def copies(tile, slot):
        return [pltpu.make_async_copy(wt_hbm.at[pl.ds(s0, n), :], w_buf.at[slot, pl.ds(d0, n), :], sem.at[slot])
                for d0, s0, n in IN_TILE_SEGMENTS[tile]]

    def for_tile(tile, fn):
        for static_tile in range(len(IN_TILE_SEGMENTS)):
            pl.when(tile == static_tile)(functools.partial(fn, static_tile))

    def start(tile, slot):
        filled = sum(n for _, _, n in IN_TILE_SEGMENTS[tile])
        if filled < IN_TILE:
            w_buf[slot, filled:IN_TILE, :] = jnp.zeros((IN_TILE - filled, D_MODEL), F32)
        for c in copies(tile, slot):
            c.start()

    def wait(tile, slot):
        for c in copies(tile, slot):
            c.wait()

    @pl.when(t == 0)
    def _():
        start(0, 0)

    @pl.when(t + 1 < pl.num_programs(0) * nj)
    def _():
        for_tile(jnp.where(j + 1 == nj, 0, j + 1), functools.partial(start, slot=1 - slot))

    for_tile(j, functools.partial(wait, slot=slot))

    @pl.when(j == 0)
    def _():
        h_scr[0:tm, :] = _rms(x_ref[...], g_ref[...]).astype(BF16)
        h_scr[tm:, :] = _rms(xs_ref[...], g_ref[...]).astype(BF16)

    y = _dot_nt(h_scr[...], w_buf[slot].astype(BF16))
    o_ref[...] = y[0:tm, :]
    os_ref[...] = y[tm:, :]

    @pl.when(j == KV_TILE)
    def _():
        for r0, n, k_dst, v_dst in ((0, tm, k_ref, v_ref), (tm, ts, ks_ref, vs_ref)):
            for dst, col in ((k_dst, COL_K), (v_dst, COL_V)):
                for g in range(N_KV_HEADS):
                    c0 = col - COL_K + g * HEAD_DIM
                    dst[pl.ds(g, n, stride=N_KV_HEADS), :] = y[r0:r0 + n, c0:c0 + HEAD_DIM]


def _in_proj(x, xs, g, w_t, *, tm):
    t, tiles = x.shape[0], x.shape[0] // tm
    ts = xs.shape[0] // tiles
    assert ts * tiles == xs.shape[0] and ts % 16 == 0

    def group(rows, n):
        kv_spec = pl.BlockSpec((rows * N_KV_HEADS, HEAD_DIM), lambda i, j: (i, 0))
        kv_shape = jax.ShapeDtypeStruct((n * N_KV_HEADS, HEAD_DIM), F32)
        return ([pl.BlockSpec((rows, IN_TILE), lambda i, j: (i, j)), kv_spec, kv_spec],
                [jax.ShapeDtypeStruct((n, N_IN), F32), kv_shape, kv_shape])

    (po, pk, pv), (pos, pks, pvs) = group(tm, t)
    (so, sk, sv), (sos, sks, svs) = group(ts, xs.shape[0])
    proj, proj_s, k, v, k_s, v_s = pl.pallas_call(
        _in_proj_kernel,
        grid=(tiles, N_IN // IN_TILE),
        in_specs=[
            pl.BlockSpec((tm, D_MODEL), lambda i, j: (i, 0)),
            pl.BlockSpec((ts, D_MODEL), lambda i, j: (i, 0)),
            pl.BlockSpec((1, D_MODEL), lambda i, j: (0, 0)),
            pl.BlockSpec(memory_space=pl.ANY),
        ],
        out_specs=[po, so, pk, pv, sk, sv],
        out_shape=[pos, sos, pks, pvs, sks, svs],
        scratch_shapes=[pltpu.VMEM((tm + ts, D_MODEL), BF16), pltpu.VMEM((2, IN_TILE, D_MODEL), F32),
                        pltpu.SemaphoreType.DMA((2,))],
        compiler_params=_params(("arbitrary", "arbitrary")),
        name="in_proj",
    )(x, xs, g, w_t)
    return (proj, k, v), (proj_s, k_s, v_s)


def _out_proj_kernel(x_ref, a_ref, s_ref, xs_ref, as_ref, ss_ref, wa_ref, ws_ref, o_ref, os_ref):
    tm = x_ref.shape[0]
    a = jnp.concatenate([a_ref[...], as_ref[...]], axis=0).astype(BF16)
    s = jnp.concatenate([s_ref[...], ss_ref[...]], axis=0).astype(BF16)
    acc = _dot(a, wa_ref[...])
    acc += _dot(s, ws_ref[...])
    o_ref[...] = x_ref[...] + acc[0:tm, :]
    os_ref[...] = xs_ref[...] + acc[tm:, :]


def _out_proj(x, a, s, xs, a_s, s_s, wa, ws, *, tm):
    t, tiles = x.shape[0], x.shape[0] // tm
    ts = xs.shape[0] // tiles
    assert ts * tiles == xs.shape[0] and ts % 16 == 0

    def rows(n, width):
        return pl.BlockSpec((n, width), lambda i: (i, 0))

    return pl.pallas_call(
        _out_proj_kernel,
        grid=(tiles,),
        in_specs=[
            rows(tm, D_MODEL), rows(tm, D_ATTN), rows(tm, D_SSD),
            rows(ts, D_MODEL), rows(ts, D_ATTN), rows(ts, D_SSD),
            pl.BlockSpec((D_ATTN, D_MODEL), lambda i: (0, 0)),
            pl.BlockSpec((D_SSD, D_MODEL), lambda i: (0, 0)),
        ],
        out_specs=[rows(tm, D_MODEL), rows(ts, D_MODEL)],
        out_shape=[jax.ShapeDtypeStruct((t, D_MODEL), F32), jax.ShapeDtypeStruct(xs.shape, F32)],
        compiler_params=_params(("parallel",)),
        name="out_proj",
    )(x, a, s, xs, a_s, s_s, wa, ws)


NEAR = 2 * LANES


def _bias_kernel(rb_ref, o_ref):
    r = lax.broadcasted_iota(I32, (LANES, NEAR), 0)
    c = lax.broadcasted_iota(I32, (LANES, NEAR), 1)
    n = jnp.maximum(r + LANES - c, 0)
    max_exact = N_BUCKETS // 2
    nf = jnp.maximum(n, 1).astype(F32)
    large = max_exact + (jnp.log(nf / max_exact) / math.log(MAX_DISTANCE / max_exact)
                         * (N_BUCKETS - max_exact)).astype(I32)
    bucket = jnp.where(n < max_exact, n, jnp.minimum(large, N_BUCKETS - 1))
    for h in range(N_HEADS):
        acc = jnp.zeros((LANES, NEAR), F32)
        for b in range(N_BUCKETS):
            acc = jnp.where(bucket == b, rb_ref[b, h], acc)
        o_ref[h] = acc


def _bias_table(rel_bias):
    return pl.pallas_call(
        _bias_kernel,
        in_specs=[pl.BlockSpec(memory_space=pltpu.SMEM)],
        out_specs=pl.BlockSpec(memory_space=pltpu.VMEM),
        out_shape=jax.ShapeDtypeStruct((N_HEADS, LANES, NEAR), F32),
        name="bias_table",
    )(rel_bias)


def _count(mask, axis=1):
    x = mask.astype(F32)
    if axis == 0:
        x = jnp.sum(x.reshape(x.shape[0] // 64, 64, x.shape[1]), axis=0)
    return jnp.sum(x, axis=axis, keepdims=True)


def _kth_largest(count_ge, shape, optional_bits=()):
    t = jnp.full(shape, INT_MIN, I32)
    held = jnp.full(shape, 2.0 ** 30, F32)
    first = count_ge(jnp.zeros(shape, I32))
    t = jnp.where(first >= TOPK, 0, t)
    held = jnp.where(first >= TOPK, first, held)

    def body(it, carry):
        t, held = carry
        cand = t + lax.shift_left(jnp.int32(1), 30 - it)
        count = count_ge(cand)
        return jnp.where(count >= TOPK, cand, t), jnp.where(count >= TOPK, count, held)

    def finish(first_bit, groups, carry):
        if not groups:
            return carry
        stop = first_bit + groups[0]
        settled = jnp.min((carry[1] == TOPK).astype(F32)) > 0.0
        return lax.cond(settled, lambda c: c,
                        lambda c: finish(stop, groups[1:], lax.fori_loop(first_bit, stop, body, c)), carry)

    split = 31 - sum(optional_bits)
    return finish(split, tuple(optional_bits), lax.fori_loop(0, split, body, (t, held)))[0]


def _tie_limit(count_eq_below, need, shape, idx_bits):
    def body(it, m):
        cand = m + lax.shift_left(jnp.int32(1), idx_bits - 1 - it)
        return jnp.where(count_eq_below(cand) < need, cand, m)

    return lax.fori_loop(0, idx_bits, body, jnp.zeros(shape, I32))


CAUSAL_VARIANTS = 4


def _dsa_prompt_block(width, near_tiles, i, q_ref, qi_ref, k_ref, v_ref, sm_ref, bt_ref, o_ref, sk_scr, neg_scr,
                      kb_scr, vt_scr):
    tq = LANES
    q0 = pl.multiple_of(i * LANES, LANES)
    shape = (1, tq)

    wi_t = sm_ref[pl.ds(q0, tq), :].T[SM_WI:SM_WI + IDX_HEADS, :] * (IDX_HEADS * IDX_DIM) ** -0.5
    head_group = 4
    qi_groups = [
        jnp.concatenate([qi_ref[:, h * IDX_DIM:(h + 1) * IDX_DIM] for h in range(hg, hg + head_group)],
                        axis=0).astype(BF16)
        for hg in range(0, IDX_HEADS, head_group)]
    chunk = 2 * LANES
    pos = lax.broadcasted_iota(I32, (chunk, tq), 1) + q0
    for c in range(0, width, chunk):
        ki = sm_ref[c:c + chunk, SM_KI:SM_KI + IDX_DIM].astype(BF16)
        score = jnp.zeros((chunk, tq), F32)
        for n, hg in enumerate(range(0, IDX_HEADS, head_group)):
            rel = _dot_nt(ki, qi_groups[n])
            for e in range(head_group):
                score = score + wi_t[hg + e:hg + e + 1, :] * jnp.maximum(rel[:, e * tq:(e + 1) * tq], 0.0)
        key_c = lax.broadcasted_iota(I32, (chunk, tq), 0) + c
        sk_scr[c:c + chunk, :] = jnp.where(key_c <= pos, _sort_key(score), INT_MIN)

    key = lax.broadcasted_iota(I32, (width, tq), 0)

    def sk():
        return sk_scr[0:width, :]

    thr = _kth_largest(lambda c: _count(sk() >= c, 0), shape, optional_bits=(4, 4))
    excess = (_count(sk() >= thr, 0) > TOPK) & (thr > INT_MIN)
    any_excess = jnp.max(excess.astype(F32)) > 0.0

    @pl.when(jnp.logical_not(any_excess))
    def _():
        neg_scr[0:width, :] = jnp.where(sk() >= jnp.maximum(thr, INT_MIN + 1), 0.0, -jnp.inf)

    @pl.when(any_excess)
    def _():
        need = TOPK - _count(sk() > thr, 0)
        lim = _tie_limit(lambda m: _count((sk() == thr) & (key < m), 0), need, shape, (width - 1).bit_length())
        take = (sk() > thr) | ((sk() == thr) & (key <= lim) & (sk() > INT_MIN))
        neg_scr[0:width, :] = jnp.where(take, 0.0, -jnp.inf)

    log2e = math.log2(math.e)
    scale = HEAD_DIM ** -0.5 * log2e
    tiles = width // LANES
    for g in range(N_KV_HEADS):
        heads = range(g * GQA_GROUP, (g + 1) * GQA_GROUP)
        kg = kb_scr[g, 0:width, :]
        vg_t = vt_scr[g, :, 0:width]
        qg = jnp.concatenate([q_ref[:, h * HEAD_DIM:(h + 1) * HEAD_DIM] for h in heads], axis=0).astype(BF16)
        diag_t = [((bt_ref[h, :, LANES:NEAR] - bt_ref[h, 0:1, 0:1]) * log2e).T for h in heads]
        prev_t = [((bt_ref[h, :, 0:LANES] - bt_ref[h, 0:1, 0:1]) * log2e).T for h in heads]
        qk = _dot_nt(kg, qg)
        logits = []
        for t in range(tiles):
            rows = slice(t * LANES, (t + 1) * LANES)
            per_head = []
            for e in range(GQA_GROUP):
                lt = qk[rows, e * tq:(e + 1) * tq] * scale + neg_scr[rows, :]
                if t >= tiles - near_tiles:
                    lt = lt + jnp.where(i == t, diag_t[e], jnp.where(i - 1 == t, prev_t[e], 0.0))
                per_head.append(lt)
            logits.append(per_head)
        p_t = []
        inv_l = []
        for e in range(GQA_GROUP):
            m = jnp.max(functools.reduce(jnp.maximum, [lt[e] for lt in logits]), axis=0, keepdims=True)
            probs = [jnp.exp2(lt[e] - m) for lt in logits]
            inv_l.append(1.0 / jnp.sum(functools.reduce(jnp.add, probs), axis=0, keepdims=True))
            p_t.append(jnp.concatenate([p.astype(BF16) for p in probs], axis=0))
        out_t = _dot(vg_t, jnp.concatenate(p_t, axis=1))
        for e, h in enumerate(heads):
            o_ref[:, h * HEAD_DIM:(h + 1) * HEAD_DIM] = (out_t[:, e * tq:(e + 1) * tq] * inv_l[e]).T


def _dsa_prompt_kernel(*refs, seq):
    i = pl.program_id(1)
    per = seq // LANES // CAUSAL_VARIANTS
    k_ref, v_ref = refs[2], refs[3]
    kb_scr, vt_scr = refs[-2], refs[-1]

    @pl.when(i == 0)
    def _():
        for g in range(N_KV_HEADS):
            cols = slice(g * HEAD_DIM, (g + 1) * HEAD_DIM)
            kb_scr[g] = k_ref[:, cols].astype(BF16)
            vt_scr[g] = v_ref[:, cols].T.astype(BF16)

    for v in range(CAUSAL_VARIANTS):
        @pl.when(i // per == v)
        def _(v=v):
            _dsa_prompt_block((v + 1) * per * LANES, per + 1, i, *refs)


def _dsa_prompt(proj, btab, *, batch, seq):
    nq = seq // LANES
    return pl.pallas_call(
        functools.partial(_dsa_prompt_kernel, seq=seq),
        grid=(batch, nq),
        in_specs=[
            pl.BlockSpec((LANES, D_ATTN), lambda b, i: (b * nq + i, COL_Q // D_ATTN)),
            pl.BlockSpec((LANES, IDX_HEADS * IDX_DIM), lambda b, i: (b * nq + i, COL_QI // (IDX_HEADS * IDX_DIM))),
            pl.BlockSpec((seq, D_KV), lambda b, i: (b, COL_K // D_KV)),
            pl.BlockSpec((seq, D_KV), lambda b, i: (b, COL_V // D_KV)),
            pl.BlockSpec((seq, LANES), lambda b, i: (b, COL_SMALL // LANES)),
            pl.BlockSpec((N_HEADS, LANES, NEAR), lambda b, i: (0, 0, 0)),
        ],
        out_specs=pl.BlockSpec((LANES, D_ATTN), lambda b, i: (b * nq + i, 0)),
        out_shape=jax.ShapeDtypeStruct((batch * seq, D_ATTN), F32),
        scratch_shapes=[pltpu.VMEM((seq, LANES), I32), pltpu.VMEM((seq, LANES), F32),
                        pltpu.VMEM((N_KV_HEADS, seq, HEAD_DIM), BF16), pltpu.VMEM((N_KV_HEADS, HEAD_DIM, seq), BF16)],
        compiler_params=_params(("arbitrary", "arbitrary")),
        name="dsa_prompt",
    )(proj, proj, proj, proj, proj, btab)


SCORE_PAGES = 64
ATTN_PAGES = 32


def _sample_queries(qi_ref):
    return jnp.concatenate(
        [qi_ref[:, h * IDX_DIM:(h + 1) * IDX_DIM] for h in range(IDX_HEADS)], axis=0).astype(BF16)


def _sample_scores(rel, wi, nq):
    sc = jnp.zeros((nq, rel.shape[1]), F32)
    for h in range(IDX_HEADS):
        sc = sc + wi[:, h:h + 1] * jnp.maximum(rel[h * nq:(h + 1) * nq, :], 0.0)
    return sc


def _page_copies(pt_ref, pools, bufs, sems, b, s, slot, pages, page_rows):
    copies = []
    for r in range(pages):
        row0 = pl.multiple_of(pt_ref[b, s * pages + r] * page_rows, page_rows)
        for pool, buf, sem in zip(pools, bufs, sems):
            copies.append(pltpu.make_async_copy(pool.at[pl.ds(row0, page_rows), :], buf.at[slot, r], sem.at[slot]))
    return copies


def _page_pipeline(pt_ref, pools, bufs, sems, pages, page_rows):
    b, s = pl.program_id(0), pl.program_id(1)
    steps = pl.num_programs(1)
    t = b * steps + s
    slot = lax.rem(t, 2)
    copies = functools.partial(_page_copies, pt_ref, pools, bufs, sems, pages=pages, page_rows=page_rows)

    def start(batch, step, into):
        for n, c in enumerate(copies(batch, step, into)):
            c.start(priority=n % 2)

    @pl.when(t == 0)
    def _():
        start(b, s, slot)

    @pl.when(t + 1 < pl.num_programs(0) * steps)
    def _():
        wrap = s + 1 == steps
        start(jnp.where(wrap, b + 1, b), jnp.where(wrap, 0, s + 1), 1 - slot)

    for c in copies(b, s, slot):
        c.wait()
    return slot


def _sample_score_kernel(pt_ref, qi_ref, sm_ref, pool_ref, past_ref, new_ref, pages, sem, kbuf, *, nq):
    s = pl.program_id(1)
    slot = _page_pipeline(pt_ref, [pool_ref], [pages], [sem], SCORE_PAGES, IDX_DIM)
    qx = _sample_queries(qi_ref)
    wi = sm_ref[:, SM_WI:SM_WI + IDX_HEADS] * (IDX_HEADS * IDX_DIM) ** -0.5
    for r in range(SCORE_PAGES):
        kbuf[:, r * PAGE_SIZE:(r + 1) * PAGE_SIZE] = pages[slot, r].astype(BF16)
    past_ref[...] = _sort_key(_sample_scores(_dot(qx, kbuf[...]), wi, nq))

    @pl.when(s == pl.num_programs(1) - 1)
    def _():
        ki_new = jnp.concatenate(
            [sm_ref[:, SM_KI:SM_KI + IDX_DIM], jnp.zeros((LANES - nq, IDX_DIM), F32)], axis=0).astype(BF16)
        sc = _sample_scores(_dot_nt(qx, ki_new), wi, nq)
        j = lax.broadcasted_iota(I32, (nq, LANES), 1)
        t = lax.broadcasted_iota(I32, (nq, LANES), 0)
        new_ref[...] = jnp.where(j <= t, _sort_key(sc), INT_MIN)


def _sample_score(page_table, proj_s, pool_kidx_t, *, batch, nq, n_pages):
    steps = n_pages // SCORE_PAGES
    grid_spec = pltpu.PrefetchScalarGridSpec(
        num_scalar_prefetch=1,
        grid=(batch, steps),
        in_specs=[
            pl.BlockSpec((nq, IDX_HEADS * IDX_DIM), lambda b, s, pt: (b, COL_QI // (IDX_HEADS * IDX_DIM))),
            pl.BlockSpec((nq, LANES), lambda b, s, pt: (b, COL_SMALL // LANES)),
            pl.BlockSpec(memory_space=pl.ANY),
        ],
        out_specs=[
            pl.BlockSpec((None, nq, SCORE_PAGES * PAGE_SIZE), lambda b, s, pt: (b, 0, s)),
            pl.BlockSpec((None, nq, LANES), lambda b, s, pt: (b, 0, 0)),
        ],
        scratch_shapes=[
            pltpu.VMEM((2, SCORE_PAGES, IDX_DIM, PAGE_SIZE), F32),
            pltpu.SemaphoreType.DMA((2,)),
            pltpu.VMEM((IDX_DIM, SCORE_PAGES * PAGE_SIZE), BF16),
        ],
    )
    return pl.pallas_call(
        functools.partial(_sample_score_kernel, nq=nq),
        grid_spec=grid_spec,
        out_shape=[
            jax.ShapeDtypeStruct((batch, nq, n_pages * PAGE_SIZE), I32),
            jax.ShapeDtypeStruct((batch, nq, LANES), I32),
        ],
        compiler_params=_params(("arbitrary", "arbitrary")),
        name="sample_score",
    )(page_table, proj_s, proj_s, pool_kidx_t)


def _sample_select_kernel(past_ref, new_ref, thr_ref, lim_ref, *, past):
    rows = past_ref.shape[0]
    colp = lax.broadcasted_iota(I32, (rows, past), 1)
    coln = lax.broadcasted_iota(I32, (rows, LANES), 1) + past

    def count_ge(c):
        return _count(past_ref[...] >= c) + _count(new_ref[...] >= c)

    thr = _kth_largest(count_ge, (rows, 1))
    thr_ref[...] = jnp.broadcast_to(thr, (rows, LANES))
    lim_ref[...] = jnp.full((rows, LANES), 2 ** 31 - 1, I32)
    any_excess = jnp.max((count_ge(thr) > TOPK).astype(F32)) > 0.0

    @pl.when(any_excess)
    def _():
        need = TOPK - (_count(past_ref[...] > thr) + _count(new_ref[...] > thr))

        def count_eq_below(m):
            return (_count((past_ref[...] == thr) & (colp < m))
                    + _count((new_ref[...] == thr) & (coln < m)))

        lim = _tie_limit(count_eq_below, need, (rows, 1), (past + LANES - 1).bit_length())
        lim_ref[...] = jnp.broadcast_to(lim, (rows, LANES))


def _sample_select(sk_past, sk_new, *, rows_per_step):
    rows, past = sk_past.shape
    return pl.pallas_call(
        functools.partial(_sample_select_kernel, past=past),
        grid=(rows // rows_per_step,),
        in_specs=[
            pl.BlockSpec((rows_per_step, past), lambda i: (i, 0)),
            pl.BlockSpec((rows_per_step, LANES), lambda i: (i, 0)),
        ],
        out_specs=[
            pl.BlockSpec((rows_per_step, LANES), lambda i: (i, 0)),
            pl.BlockSpec((rows_per_step, LANES), lambda i: (i, 0)),
        ],
        out_shape=[jax.ShapeDtypeStruct((rows, LANES), I32)] * 2,
        compiler_params=_params(("parallel",)),
        name="sample_select",
    )(sk_past, sk_new)


def _sample_attn_kernel(pt_ref, q_ref, kn_ref, vn_ref, skp_ref, skn_ref, thr_ref, lim_ref, bt_ref, poolk_ref,
                        poolv_ref, o_ref, kpages, vpages, ksem, vsem, kbuf, vbuf, m_scr, l_scr, acc_scr, *,
                        nq, past):
    slot = _page_pipeline(pt_ref, [poolk_ref, poolv_ref], [kpages, vpages], [ksem, vsem], ATTN_PAGES,
                          PAGE_SIZE * N_KV_HEADS)
    s = pl.program_id(1)
    last = pl.num_programs(1) - 1
    width = ATTN_PAGES * PAGE_SIZE
    grows = GQA_GROUP * nq
    scale = HEAD_DIM ** -0.5

    @pl.when(s == 0)
    def _():
        m_scr[...] = jnp.full_like(m_scr, NEG_BIG)
        l_scr[...] = jnp.zeros_like(l_scr)
        acc_scr[...] = jnp.zeros_like(acc_scr)

    thr = thr_ref[:, 0:1]
    lim = lim_ref[:, 0:1]

    def select(sk, idx):
        sel = (sk > thr) | ((sk == thr) & (idx <= lim))
        return jnp.concatenate([sel] * GQA_GROUP, axis=0)

    def far_bias(g):
        return jnp.concatenate(
            [jnp.broadcast_to(bt_ref[g * GQA_GROUP + hh, 0:1, 0:1], (nq, 1)) for hh in range(GQA_GROUP)], axis=0)

    def near_bias(g, lo, hi):
        return jnp.concatenate([bt_ref[g * GQA_GROUP + hh, 0:nq, lo:hi] for hh in range(GQA_GROUP)], axis=0)

    def group_queries(g):
        return jnp.concatenate(
            [q_ref[:, (g * GQA_GROUP + hh) * HEAD_DIM:(g * GQA_GROUP + hh + 1) * HEAD_DIM]
             for hh in range(GQA_GROUP)], axis=0).astype(BF16)

    def accumulate(g, lg, sel, vals):
        rs = slice(g * grows, (g + 1) * grows)
        m_old = m_scr[rs, :]
        m_new = jnp.maximum(m_old, jnp.max(jnp.where(sel, lg, NEG_BIG), axis=1, keepdims=True))
        p = jnp.where(sel, jnp.exp(lg - m_new), 0.0)
        alpha = jnp.exp(m_old - m_new)
        l_scr[rs, :] = alpha * l_scr[rs, :] + jnp.sum(p, axis=1, keepdims=True)
        acc_scr[rs, :] = alpha * acc_scr[rs, :] + _dot(p.astype(BF16), vals)
        m_scr[rs, :] = m_new

    for r in range(ATTN_PAGES):
        for g in range(N_KV_HEADS):
            rows = pl.ds(g, PAGE_SIZE, stride=N_KV_HEADS)
            kbuf[g, r * PAGE_SIZE:(r + 1) * PAGE_SIZE, :] = kpages[slot, r, rows, :].astype(BF16)
            vbuf[g, r * PAGE_SIZE:(r + 1) * PAGE_SIZE, :] = vpages[slot, r, rows, :].astype(BF16)

    col = lax.broadcasted_iota(I32, (nq, width), 1)
    sel = select(skp_ref[...], col + s * width)
    colg = lax.broadcasted_iota(I32, (grows, width), 1)
    in_near = (s == last) & (colg >= width - PAGE_SIZE)
    for g in range(N_KV_HEADS):
        near = jnp.tile(near_bias(g, 0, LANES), (1, ATTN_PAGES))
        bias = jnp.where(in_near, near, far_bias(g))
        lg = _dot_nt(group_queries(g), kbuf[g]) * scale + bias
        accumulate(g, lg, sel, vbuf[g])

    @pl.when(s == last)
    def _():
        pad = jnp.zeros((LANES - nq, D_KV), F32)
        kn = jnp.concatenate([kn_ref[...], pad], axis=0).astype(BF16)
        vn = jnp.concatenate([vn_ref[...], pad], axis=0).astype(BF16)
        coln = lax.broadcasted_iota(I32, (nq, LANES), 1) + past
        seln = select(skn_ref[...], coln)
        for g in range(N_KV_HEADS):
            cs = slice(g * HEAD_DIM, (g + 1) * HEAD_DIM)
            lg = _dot_nt(group_queries(g), kn[:, cs]) * scale + near_bias(g, LANES, NEAR)
            accumulate(g, lg, seln, vn[:, cs])
        out = acc_scr[...] / l_scr[...]
        for h in range(N_HEADS):
            o_ref[:, h * HEAD_DIM:(h + 1) * HEAD_DIM] = out[h * nq:(h + 1) * nq, :]


def _sample_attn(page_table, proj_s, sk_past, sk_new, thr, lim, btab, pool_k, pool_v, *, batch, nq, n_pages):
    steps = n_pages // ATTN_PAGES
    width = ATTN_PAGES * PAGE_SIZE
    past = n_pages * PAGE_SIZE
    page_buf = pltpu.VMEM((2, ATTN_PAGES, PAGE_SIZE * N_KV_HEADS, HEAD_DIM), F32)
    grid_spec = pltpu.PrefetchScalarGridSpec(
        num_scalar_prefetch=1,
        grid=(batch, steps),
        in_specs=[
            pl.BlockSpec((nq, D_ATTN), lambda b, s, pt: (b, COL_Q // D_ATTN)),
            pl.BlockSpec((nq, D_KV), lambda b, s, pt: (b, COL_K // D_KV)),
            pl.BlockSpec((nq, D_KV), lambda b, s, pt: (b, COL_V // D_KV)),
            pl.BlockSpec((None, nq, width), lambda b, s, pt: (b, 0, s)),
            pl.BlockSpec((None, nq, LANES), lambda b, s, pt: (b, 0, 0)),
            pl.BlockSpec((nq, LANES), lambda b, s, pt: (b, 0)),
            pl.BlockSpec((nq, LANES), lambda b, s, pt: (b, 0)),
            pl.BlockSpec((N_HEADS, LANES, NEAR), lambda b, s, pt: (0, 0, 0)),
            pl.BlockSpec(memory_space=pl.ANY),
            pl.BlockSpec(memory_space=pl.ANY),
        ],
        out_specs=pl.BlockSpec((nq, D_ATTN), lambda b, s, pt: (b, 0)),
        scratch_shapes=[
            page_buf, page_buf, pltpu.SemaphoreType.DMA((2,)), pltpu.SemaphoreType.DMA((2,)),
            pltpu.VMEM((N_KV_HEADS, width, HEAD_DIM), BF16),
            pltpu.VMEM((N_KV_HEADS, width, HEAD_DIM), BF16),
            pltpu.VMEM((N_HEADS * nq, 1), F32),
            pltpu.VMEM((N_HEADS * nq, 1), F32),
            pltpu.VMEM((N_HEADS * nq, HEAD_DIM), F32),
        ],
    )
    return pl.pallas_call(
        functools.partial(_sample_attn_kernel, nq=nq, past=past),
        grid_spec=grid_spec,
        out_shape=jax.ShapeDtypeStruct((batch * nq, D_ATTN), F32),
        compiler_params=_params(("arbitrary", "arbitrary")),
        name="sample_attn",
    )(page_table, proj_s, proj_s, proj_s, sk_past, sk_new, thr, lim, btab, pool_k, pool_v)


TAIL = 8
SHORT_ROWS = 16


def _ssd_kernel(xbc_ref, z_ref, sm_ref, cprev_ref, hprev_ref, cw_ref, cb_ref, alog_ref, dtb_ref, dvec_ref,
                g_ref, y_ref, cnew_ref, hnew_ref, xpad_scr, sm_scr, h_scr, yd_scr, *, rows):
    c = pl.program_id(1)
    q = CHUNK
    r = q if rows == q else SHORT_ROWS
    assert rows <= r
    keep = CONV_WIDTH - 1
    heads_per_group = SSD_HEADS // SSD_GROUPS

    def pad_rows(x):
        return x if r == q else jnp.concatenate([x, jnp.zeros((q - r, x.shape[1]), x.dtype)], axis=0)

    @pl.when(c == 0)
    def _():
        xpad_scr[...] = jnp.zeros_like(xpad_scr)
        sm_scr[...] = jnp.zeros_like(sm_scr)
        xpad_scr[TAIL - keep:TAIL, :] = cprev_ref[0]
        h_scr[...] = hprev_ref[0]

    xpad_scr[TAIL:TAIL + rows, :] = xbc_ref[...]
    sm_scr[0:rows, :] = sm_ref[...]
    conv = sum(xpad_scr[TAIL - keep + k:TAIL - keep + k + r, :] * cw_ref[k:k + 1, :]
               for k in range(CONV_WIDTH)) + cb_ref[...]
    new_tail = xpad_scr[rows + TAIL - keep:rows + TAIL, :]
    cnew_ref[0] = new_tail
    xpad_scr[TAIL - keep:TAIL, :] = new_tail

    xc = conv * jax.nn.sigmoid(conv)
    xs = xc[:, :D_SSD]
    bm = pad_rows(xc[:, D_SSD:D_SSD + SSD_GROUPS * D_STATE]).astype(BF16)
    cm = xc[:, D_SSD + SSD_GROUPS * D_STATE:].astype(BF16)

    ri = lax.broadcasted_iota(I32, (q, q), 0)
    ci = lax.broadcasted_iota(I32, (q, q), 1)
    causal = ri >= ci
    causal_r = causal[0:r, :]
    x = sm_scr[:, SM_DT:SM_DT + SSD_HEADS] + dtb_ref[...]
    dt = jnp.maximum(x, 0.0) + jnp.log1p(jnp.exp(-jnp.abs(x)))
    dt = jnp.where(lax.broadcasted_iota(I32, (q, SSD_HEADS), 0) < rows, dt, 0.0)
    a = -jnp.exp(alog_ref[...])
    acum = _select_dot(causal.astype(BF16), dt * a)
    acum_t = jnp.concatenate([acum, jnp.zeros((q, LANES - SSD_HEADS), F32)], axis=1).T

    expand = (lax.broadcasted_iota(I32, (SSD_HEADS, D_SSD), 1) // SSD_HEAD_DIM
              == lax.broadcasted_iota(I32, (SSD_HEADS, D_SSD), 0)).astype(BF16)
    expand_t = (lax.broadcasted_iota(I32, (D_SSD, SSD_HEADS), 0) // SSD_HEAD_DIM
                == lax.broadcasted_iota(I32, (D_SSD, SSD_HEADS), 1)).astype(BF16)
    acum_x = _select_dot(acum[0:r, :], expand)
    xd = xs * _select_dot(dt[0:r, :], expand)
    xdw_t = pad_rows(xd * jnp.exp(acum_x[r - 1:r, :] - acum_x)).T.astype(BF16)
    xd = pad_rows(xd).astype(BF16)
    state_decay = jnp.exp(_select_dot(expand_t, jnp.broadcast_to(acum_t[0:SSD_HEADS, q - 1:q], (SSD_HEADS, LANES))))

    for g in range(SSD_GROUPS):
        ns = slice(g * D_STATE, (g + 1) * D_STATE)
        hs = slice(g * heads_per_group * SSD_HEAD_DIM, (g + 1) * heads_per_group * SSD_HEAD_DIM)
        cb = _dot_nt(cm[:, ns], bm[:, ns])
        for e in range(heads_per_group):
            h = g * heads_per_group + e
            ps = slice(h * SSD_HEAD_DIM, (h + 1) * SSD_HEAD_DIM)
            seg = acum[0:r, h:h + 1] - acum_t[h:h + 1, :]
            lmat = jnp.exp(jnp.where(causal_r, seg, -jnp.inf))
            yd_scr[0:r, ps] = _dot((cb * lmat).astype(BF16), xd[:, ps])
        h_old = h_scr[hs, :]
        yd_scr[0:r, hs] += _dot_nt(cm[:, ns], h_old.astype(BF16)) * jnp.exp(acum_x[:, hs])
        h_scr[hs, :] = h_old * state_decay[hs, :] + _dot(xdw_t[hs, :], bm[:, ns])

    hnew_ref[0] = h_scr[...]
    y = yd_scr[0:rows, :] + dvec_ref[...] * xs[0:rows, :]
    zz = z_ref[...]
    y = y * (zz * jax.nn.sigmoid(zz))
    width = D_SSD // SSD_GROUPS
    for g in range(SSD_GROUPS):
        cs = slice(g * width, (g + 1) * width)
        y_ref[:, cs] = _rms(y[:, cs], g_ref[:, cs])


def _ssd(proj, conv_prev, ssm_prev, conv_w, conv_b, a_log, dt_bias, d_vec, g_ssd, *, batch, seq):
    rows = min(CHUNK, seq)
    nc = seq // rows
    full = lambda *shape: pl.BlockSpec(shape, lambda b, c: (0,) * len(shape))
    return pl.pallas_call(
        functools.partial(_ssd_kernel, rows=rows),
        grid=(batch, nc),
        in_specs=[
            pl.BlockSpec((rows, CONV_DIM), lambda b, c: (b * nc + c, COL_XBC // CONV_DIM)),
            pl.BlockSpec((rows, D_SSD), lambda b, c: (b * nc + c, COL_Z // D_SSD)),
            pl.BlockSpec((rows, LANES), lambda b, c: (b * nc + c, COL_SMALL // LANES)),
            pl.BlockSpec((1, CONV_WIDTH - 1, CONV_DIM), lambda b, c: (b, 0, 0)),
            pl.BlockSpec((1, D_SSD, D_STATE), lambda b, c: (b, 0, 0)),
            full(CONV_WIDTH, CONV_DIM), full(1, CONV_DIM), full(1, SSD_HEADS), full(1, SSD_HEADS),
            full(1, D_SSD), full(1, D_SSD),
        ],
        out_specs=[
            pl.BlockSpec((rows, D_SSD), lambda b, c: (b * nc + c, 0)),
            pl.BlockSpec((1, CONV_WIDTH - 1, CONV_DIM), lambda b, c: (b, 0, 0)),
            pl.BlockSpec((1, D_SSD, D_STATE), lambda b, c: (b, 0, 0)),
        ],
        out_shape=[
            jax.ShapeDtypeStruct((batch * seq, D_SSD), F32),
            jax.ShapeDtypeStruct((batch, CONV_WIDTH - 1, CONV_DIM), F32),
            jax.ShapeDtypeStruct((batch, D_SSD, D_STATE), F32),
        ],
        scratch_shapes=[
            pltpu.VMEM((CHUNK + TAIL, CONV_DIM), F32),
            pltpu.VMEM((CHUNK, LANES), F32),
            pltpu.VMEM((D_SSD, D_STATE), F32),
            pltpu.VMEM((CHUNK, D_SSD), F32),
        ],
        compiler_params=_params(("parallel", "arbitrary")),
        name="ssd",
    )(proj, proj, proj, conv_prev, ssm_prev, conv_w, conv_b, a_log, dt_bias, d_vec, g_ssd)


def kernel(x_prompt, x_sample, cache_k, cache_v, cache_kidx, state_conv, state_ssm, page_table, rel_bias, g_ffn1, w1_ffn1, w3_ffn1, w2_ffn1, g_mix, w_in, conv_w, conv_b, a_log, dt_bias, d_skip, g_ssd, w_out, g_ffn2, w1_ffn2, w3_ffn2, w2_ffn2, g_final):
    depth = w_in.shape[0]
    assert depth == 1
    l = 0
    bp, seq, _ = x_prompt.shape
    bs, nq, _ = x_sample.shape
    n_pages = page_table.shape[1]
    row = lambda t: t.reshape(1, -1)

    w1a, w3a, w2a = w1_ffn1[l], w3_ffn1[l], w2_ffn1[l]
    w1b, w3b, w2b = w1_ffn2[l], w3_ffn2[l], w2_ffn2[l]
    w_in_t = jnp.swapaxes(w_in[l], 0, 1)
    w_out_a = w_out[l, :D_ATTN].astype(BF16)
    w_out_s = w_out[l, D_ATTN:].astype(BF16)
    d_vec = jnp.repeat(d_skip[l], SSD_HEAD_DIM).reshape(1, D_SSD)
    btab = _bias_table(rel_bias)
    gf = row(g_final)

    tm, tf = 1024, 256

    def mixers(proj, k, v, attend, conv_prev, ssm_prev, batch, length):
        a_out = attend(proj)
        s_out, conv_new, ssm_new = _ssd(
            proj, conv_prev, ssm_prev.reshape(batch, D_SSD, D_STATE), conv_w[l], row(conv_b[l]), row(a_log[l]),
            row(dt_bias[l]), d_vec, row(g_ssd[l]), batch=batch, seq=length)
        k = k.reshape(1, batch, length, N_KV_HEADS, HEAD_DIM)
        v = v.reshape(1, batch, length, N_KV_HEADS, HEAD_DIM)
        ki = proj[:, COL_SMALL + SM_KI:COL_SMALL + SM_KI + IDX_DIM].reshape(1, batch, length, IDX_DIM)
        ssm_new = ssm_new.reshape(1, batch, SSD_HEADS, SSD_HEAD_DIM, D_STATE)
        return a_out, s_out, k, v, ki, conv_new[None], ssm_new

    xp, xs = _ffn(x_prompt.reshape(bp * seq, D_MODEL), x_sample.reshape(bs * nq, D_MODEL), row(g_ffn1[l]),
                  w1a, w3a, w2a, gf, tm=tm, tf=tf, final_norm=False)
    in_p, in_s = _in_proj(xp, xs, row(g_mix[l]), w_in_t, tm=tm)

    a_p, s_p, k_p, v_p, ki_p, conv_p, ssm_p = mixers(
        *in_p, functools.partial(_dsa_prompt, btab=btab, batch=bp, seq=seq),
        jnp.zeros((bp, CONV_WIDTH - 1, CONV_DIM), F32), jnp.zeros((bp, SSD_HEADS, SSD_HEAD_DIM, D_STATE), F32),
        bp, seq)

    pool_k = cache_k[l].reshape(-1, HEAD_DIM)
    pool_v = cache_v[l].reshape(-1, HEAD_DIM)
    pool_kidx = jnp.swapaxes(cache_kidx[l], 1, 2).reshape(-1, PAGE_SIZE)

    def sample_attend(proj):
        sk_past, sk_new = _sample_score(page_table, proj, pool_kidx, batch=bs, nq=nq, n_pages=n_pages)
        thr, lim = _sample_select(sk_past.reshape(bs * nq, -1), sk_new.reshape(bs * nq, LANES),
                                  rows_per_step=128)
        return _sample_attn(page_table, proj, sk_past, sk_new, thr, lim, btab, pool_k, pool_v,
                            batch=bs, nq=nq, n_pages=n_pages)

    a_s, s_s, k_s, v_s, ki_s, conv_s, ssm_s = mixers(*in_s, sample_attend, state_conv[l], state_ssm[l], bs, nq)

    xp, xs = _out_proj(xp, a_p, s_p, xs, a_s, s_s, w_out_a, w_out_s, tm=512)
    y_p, y_s = _ffn(xp, xs, row(g_ffn2[l]), w1b, w3b, w2b, gf, tm=tm, tf=tf, final_norm=True)
    y_p = y_p.reshape(bp, seq, D_MODEL)
    y_s = y_s.reshape(bs, nq, D_MODEL)
    return (y_p, y_s, k_p, v_p, ki_p, conv_p, ssm_p, k_s, v_s, ki_s, conv_s, ssm_s)
```
